```python
import jax, jax.numpy as jnp
from jax import lax
import numpy as np

D_MODEL = 2048
BATCH = 4
SEQ = 2048
DEPTH = 4
DEC_BATCH = 32
DEC_SEQ = 1
PAST_LEN = 16384
PAGE_SIZE = 128

N_EVEN = (DEPTH + 1) // 2
N_ODD = DEPTH // 2
EPS = 1e-6
NEG_INF = -1e30
A_HEADS = 16
A_KV_HEADS = 4
A_GROUP = A_HEADS // A_KV_HEADS
HEAD_DIM = 64
A_WIDTH = A_HEADS * HEAD_DIM
A_KV_WIDTH = A_KV_HEADS * HEAD_DIM
WINDOW = 128
SWA_BLOCK = 128
ROPE_THETA = 10000.0
CONV_DIM = D_MODEL // 2
CONV_W = 3
GLA_HEADS = 4
GLA_WIDTH = D_MODEL // 2
GLA_DV = GLA_WIDTH // GLA_HEADS
GLA_DK = GLA_DV // 2
GLA_K_WIDTH = GLA_HEADS * GLA_DK
GLA_RANK = 16
GLA_TAU = 16.0
GLA_CHUNK = 64
POOL_DIM = D_MODEL // 2
POOL_WINDOWS = (2, 4, 8, 16)
POOL_GROUPS = 4
POOL_GDIM = POOL_DIM // POOL_GROUPS
POOL_BUF = 15
D_FF = -(-8 * D_MODEL // (3 * 256)) * 256
EVEN_IN = A_WIDTH + 2 * A_KV_WIDTH + 3 * CONV_DIM
EVEN_OUT = A_WIDTH + CONV_DIM
ODD_IN = 2 * GLA_K_WIDTH + 2 * GLA_WIDTH + GLA_RANK + POOL_DIM
ODD_OUT = GLA_WIDTH + POOL_DIM

kernel_name = 'hybrid_swa_conv_gla_pool_decode_step'


def rms_norm(x, g):
    x32 = x.astype(jnp.float32)
    y = x32 * lax.rsqrt(jnp.mean(x32 * x32, axis=-1, keepdims=True) + EPS)
    return (y * g.astype(jnp.float32)).astype(x.dtype)


def rope(x, pos):
    half = HEAD_DIM // 2
    inv = jnp.power(ROPE_THETA, -jnp.arange(half, dtype=jnp.float32) / half)
    ang = pos.astype(jnp.float32)[:, None] * inv[None, :]
    cos = jnp.cos(ang)[:, None, :]
    sin = jnp.sin(ang)[:, None, :]
    x32 = x.astype(jnp.float32)
    x1, x2 = x32[..., :half], x32[..., half:]
    return jnp.concatenate([x1 * cos - x2 * sin, x2 * cos + x1 * sin], axis=-1).astype(x.dtype)


def sink_attention(q, k, v, mask, sinks):
    s = jnp.einsum('...qhgd,...khd->...hgqk', q, k).astype(jnp.float32) * (HEAD_DIM ** -0.5)
    s = jnp.where(mask[..., None, None, :, :], s, NEG_INF)
    sink = sinks.astype(jnp.float32).reshape(A_KV_HEADS, A_GROUP, 1, 1)
    m = jnp.maximum(jnp.max(s, axis=-1, keepdims=True), sink)
    p = jnp.exp(s - m)
    p = p / (jnp.sum(p, axis=-1, keepdims=True) + jnp.exp(sink - m))
    return jnp.einsum('...hgqk,...khd->...qhgd', p.astype(v.dtype), v)


def swa_prompt(q, k, v, sinks):
    B, T = q.shape[:2]
    nb = T // SWA_BLOCK
    qb = q.reshape(B, nb, SWA_BLOCK, A_KV_HEADS, A_GROUP, HEAD_DIM)

    def band(a):
        ab = a.reshape(B, nb, SWA_BLOCK, A_KV_HEADS, HEAD_DIM)
        prev = jnp.concatenate([jnp.zeros_like(ab[:, :1]), ab[:, :-1]], axis=1)
        return jnp.concatenate([prev, ab], axis=2)

    kb, vb = band(k), band(v)
    qpos = jnp.arange(nb)[:, None] * SWA_BLOCK + jnp.arange(SWA_BLOCK)[None, :]
    kpos = (jnp.arange(nb)[:, None] - 1) * SWA_BLOCK + jnp.arange(2 * SWA_BLOCK)[None, :]
    diff = qpos[:, :, None] - kpos[:, None, :]
    mask = (diff >= 0) & (diff <= WINDOW) & (kpos[:, None, :] >= 0)
    o = sink_attention(qb, kb, vb, mask, sinks)
    return o.reshape(B, T, A_WIDTH)


def swa_sample(q, k_new, v_new, k_cache, v_cache, sinks):
    B, Ts = q.shape[:2]
    Lc = k_cache.shape[1]
    keys = jnp.concatenate([k_cache.astype(k_new.dtype), k_new], axis=1)
    vals = jnp.concatenate([v_cache.astype(v_new.dtype), v_new], axis=1)
    qpos = PAST_LEN + jnp.arange(Ts)
    kpos = jnp.concatenate([PAST_LEN - Lc + jnp.arange(Lc), qpos])
    diff = qpos[:, None] - kpos[None, :]
    mask = (diff >= 0) & (diff <= WINDOW)
    qg = q.reshape(B, Ts, A_KV_HEADS, A_GROUP, HEAD_DIM)
    o = sink_attention(qg, keys, vals, mask, sinks)
    return o.reshape(B, Ts, A_WIDTH), keys[:, -Lc:], vals[:, -Lc:]


def short_conv(u, prev, w):
    T = u.shape[1]
    ext = jnp.concatenate([prev.astype(u.dtype), u], axis=1)
    y = ext[:, 0:T] * w[0]
    for j in range(1, CONV_W):
        y = y + ext[:, j:j + T] * w[j]
    return y, ext[:, -(CONV_W - 1):]


def pool_mix(u, prev, n_prev, w_pool, scale):
    B, T, _ = u.shape
    ext_raw = jnp.concatenate([prev.astype(u.dtype), u], axis=1)
    ext = ext_raw.astype(jnp.float32)
    cs = jnp.concatenate([jnp.zeros((B, 1, POOL_DIM), jnp.float32), jnp.cumsum(ext, axis=1)], axis=1)
    t_idx = jnp.arange(T)
    outs = []
    for g, w in enumerate(POOL_WINDOWS):
        lo, hi = g * POOL_GDIM, (g + 1) * POOL_GDIM
        s = cs[:, POOL_BUF + 1:POOL_BUF + 1 + T, lo:hi] - cs[:, POOL_BUF + 1 - w:POOL_BUF + 1 - w + T, lo:hi]
        cnt = jnp.minimum(w, n_prev + t_idx + 1).astype(jnp.float32)
        outs.append(s / cnt[None, :, None] - ext[:, POOL_BUF:, lo:hi])
    d = jnp.stack(outs, axis=2).astype(u.dtype)
    y = jnp.einsum('btgc,gcd->btgd', d, w_pool).reshape(B, T, POOL_DIM) * scale
    return y, ext_raw[:, -POOL_BUF:]


def gla_chunked(q, k, v, gk, s0):
    B, T, H, DK = q.shape
    DV = v.shape[-1]
    C = GLA_CHUNK
    N = T // C
    q, k, gk = (a.reshape(B, N, C, H, DK) for a in (q, k, gk))
    v = v.reshape(B, N, C, H, DV)
    b = jnp.cumsum(gk, axis=2)
    b_last = b[:, :, -1:]
    qd = q * jnp.exp(b)
    kd = k * jnp.exp(-b)
    att = jnp.einsum('bnchd,bnjhd->bnhcj', qd, kd)
    att = jnp.where(jnp.tril(jnp.ones((C, C), dtype=bool)), att, 0.0)
    o_intra = jnp.einsum('bnhcj,bnjhe->bnche', att, v)
    delta = jnp.einsum('bnjhd,bnjhe->bnhde', k * jnp.exp(b_last - b), v)
    decay = jnp.exp(b_last[:, :, 0])

    def step(S, xs):
        dec, dl = xs
        return dec[..., None] * S + dl, S

    s_final, s_starts = lax.scan(step, s0, (decay.swapaxes(0, 1), delta.swapaxes(0, 1)))
    o_inter = jnp.einsum('bnchd,bnhde->bnche', qd, s_starts.swapaxes(0, 1))
    return (o_intra + o_inter).reshape(B, T, H, DV), s_final


def gla_recurrent(q, k, v, gk, s0):
    def step(S, xs):
        qt, kt, vt, gt = xs
        S = jnp.exp(gt)[..., None] * S + kt[..., :, None] * vt[..., None, :]
        return S, jnp.einsum('bhd,bhde->bhe', qt, S)

    s_final, o = lax.scan(step, s0, tuple(a.swapaxes(0, 1) for a in (q, k, v, gk)))
    return o.swapaxes(0, 1), s_final


def gla_mixer(zq, zk, zv, zg, zr, s0, w_alpha_up, b_alpha, out_gain, chunked):
    B, T, _ = zq.shape
    q = zq.reshape(B, T, GLA_HEADS, GLA_DK).astype(jnp.float32) * (GLA_DK ** -0.5)
    k = zk.reshape(B, T, GLA_HEADS, GLA_DK).astype(jnp.float32)
    v = zv.reshape(B, T, GLA_HEADS, GLA_DV).astype(jnp.float32)
    gk = jax.nn.log_sigmoid((zr @ w_alpha_up + b_alpha).astype(jnp.float32)) / GLA_TAU
    gk = gk.reshape(B, T, GLA_HEADS, GLA_DK)
    fn = gla_chunked if chunked else gla_recurrent
    o, s = fn(q, k, v, gk, s0.astype(jnp.float32))
    o = rms_norm(o, out_gain).reshape(B, T, GLA_WIDTH).astype(zv.dtype) * jax.nn.silu(zg)
    return o, s.astype(s0.dtype)


def mixer_even(h, pos, k_cache, v_cache, conv_prev, cache_len, w_in, w_out, q_gain, k_gain, sinks, conv_w, is_prompt):
    B, T, _ = h.shape
    z = h @ w_in
    zq, zk, zv, zb, zc, zh = jnp.split(z, [A_WIDTH, A_WIDTH + A_KV_WIDTH, A_WIDTH + 2 * A_KV_WIDTH,
                                          A_WIDTH + 2 * A_KV_WIDTH + CONV_DIM,
                                          A_WIDTH + 2 * A_KV_WIDTH + 2 * CONV_DIM], axis=-1)
    q = rope(rms_norm(zq.reshape(B, T, A_HEADS, HEAD_DIM), q_gain), pos)
    k = rope(rms_norm(zk.reshape(B, T, A_KV_HEADS, HEAD_DIM), k_gain), pos)
    v = zv.reshape(B, T, A_KV_HEADS, HEAD_DIM)
    if is_prompt:
        attn = swa_prompt(q, k, v, sinks)
        new_k, new_v = k[:, -cache_len:], v[:, -cache_len:]
    else:
        attn, new_k, new_v = swa_sample(q, k, v, k_cache, v_cache, sinks)
    conv_y, new_conv = short_conv(zc * zh, conv_prev, conv_w)
    y = jnp.concatenate([attn, zb * conv_y], axis=-1) @ w_out
    return y, new_k, new_v, new_conv


def mixer_odd(h, gla_s0, pool_prev, pool_prev_valid, w_in, w_out, w_alpha_up, b_alpha, gla_gain, w_pool, pool_scale, is_prompt):
    z = h @ w_in
    o1 = GLA_K_WIDTH
    o2 = o1 + GLA_K_WIDTH
    o3 = o2 + GLA_WIDTH
    o4 = o3 + GLA_WIDTH
    o5 = o4 + GLA_RANK
    zq, zk, zv, zg, zr, zp = jnp.split(z, [o1, o2, o3, o4, o5], axis=-1)
    oc, new_s = gla_mixer(zq, zk, zv, zg, zr, gla_s0, w_alpha_up, b_alpha, gla_gain, is_prompt)
    od, new_pool = pool_mix(zp, pool_prev, pool_prev_valid, w_pool, pool_scale)
    y = jnp.concatenate([oc, od], axis=-1) @ w_out
    return y, new_s, new_pool


def swiglu(x, wg, wu, wd):
    return (jax.nn.silu(x @ wg) * (x @ wu)) @ wd


def run_trunk(x, pos, k_cache, v_cache, conv_st, gla_st, pool_st, pool_prev_valid, cache_len, prm, is_prompt):
    nk_l, nv_l, nc_l, ng_l, np_l = [], [], [], [], []
    for layer in range(DEPTH):
        i = layer // 2
        h = rms_norm(x, prm['norm_mix'][layer])
        if layer % 2 == 0:
            kc = None if is_prompt else k_cache[i]
            vc = None if is_prompt else v_cache[i]
            y, nk, nv, nc = mixer_even(h, pos, kc, vc, conv_st[i], cache_len,
                                       prm['w_in_even'][i], prm['w_out_even'][i], prm['q_norm'][i],
                                       prm['k_norm'][i], prm['attn_sinks'][i], prm['conv_w'][i], is_prompt)
            nk_l.append(nk)
            nv_l.append(nv)
            nc_l.append(nc)
        else:
            y, ns, npool = mixer_odd(h, gla_st[i], pool_st[i], pool_prev_valid,
                                     prm['w_in_odd'][i], prm['w_out_odd'][i], prm['w_alpha_up'][i],
                                     prm['b_alpha'][i], prm['gla_out_norm'][i], prm['w_pool'][i],
                                     prm['pool_scale'][i], is_prompt)
            ng_l.append(ns)
            np_l.append(npool)
        x = x + y
        x = x + swiglu(rms_norm(x, prm['norm_ffn'][layer]), prm['w_gate'][layer], prm['w_up'][layer], prm['w_down'][layer])
    return x, jnp.stack(nk_l), jnp.stack(nv_l), jnp.stack(nc_l), jnp.stack(ng_l), jnp.stack(np_l)


def setup_inputs(seed: int = 0) -> dict:
    key = jax.random.key(seed)
    ks = jax.random.split(key, 26)
    f32 = jnp.float32

    def nrm(k, shape, scale):
        return jax.random.normal(k, shape, f32) * scale

    lc = min(WINDOW, PAST_LEN)
    return {
        'x_prompt': nrm(ks[0], (BATCH, SEQ, D_MODEL), 1.0),
        'x_sample': nrm(ks[1], (DEC_BATCH, DEC_SEQ, D_MODEL), 1.0),
        'cache_swa_k': nrm(ks[2], (N_EVEN, DEC_BATCH, lc, A_KV_HEADS, HEAD_DIM), 1.0),
        'cache_swa_v': nrm(ks[3], (N_EVEN, DEC_BATCH, lc, A_KV_HEADS, HEAD_DIM), 1.0),
        'state_conv': nrm(ks[4], (N_EVEN, DEC_BATCH, CONV_W - 1, CONV_DIM), 1.0),
        'state_gla': nrm(ks[5], (N_ODD, DEC_BATCH, GLA_HEADS, GLA_DK, GLA_DV), 2.0),
        'state_pool': nrm(ks[6], (N_ODD, DEC_BATCH, POOL_BUF, POOL_DIM), 1.0),
        'norm_mix': 1.0 + nrm(ks[7], (DEPTH, D_MODEL), 0.05),
        'norm_ffn': 1.0 + nrm(ks[8], (DEPTH, D_MODEL), 0.05),
        'w_in_even': nrm(ks[9], (N_EVEN, D_MODEL, EVEN_IN), D_MODEL ** -0.5),
        'w_out_even': nrm(ks[10], (N_EVEN, EVEN_OUT, D_MODEL), EVEN_OUT ** -0.5),
        'q_norm': 1.0 + nrm(ks[11], (N_EVEN, HEAD_DIM), 0.05),
        'k_norm': 1.0 + nrm(ks[12], (N_EVEN, HEAD_DIM), 0.05),
        'attn_sinks': nrm(ks[13], (N_EVEN, A_HEADS), 0.5),
        'conv_w': nrm(ks[14], (N_EVEN, CONV_W, CONV_DIM), CONV_W ** -0.5),
        'w_in_odd': nrm(ks[15], (N_ODD, D_MODEL, ODD_IN), D_MODEL ** -0.5),
        'w_out_odd': nrm(ks[16], (N_ODD, ODD_OUT, D_MODEL), ODD_OUT ** -0.5),
        'w_alpha_up': nrm(ks[17], (N_ODD, GLA_RANK, GLA_K_WIDTH), GLA_RANK ** -0.5),
        'b_alpha': nrm(ks[18], (N_ODD, GLA_K_WIDTH), 0.1),
        'gla_out_norm': 1.0 + nrm(ks[19], (N_ODD, GLA_DV), 0.05),
        'w_pool': nrm(ks[20], (N_ODD, POOL_GROUPS, POOL_GDIM, POOL_GDIM), POOL_GDIM ** -0.5),
        'pool_scale': 1.0 + nrm(ks[21], (N_ODD, POOL_DIM), 0.1),
        'w_gate': nrm(ks[22], (DEPTH, D_MODEL, D_FF), D_MODEL ** -0.5),
        'w_up': nrm(ks[23], (DEPTH, D_MODEL, D_FF), D_MODEL ** -0.5),
        'w_down': nrm(ks[24], (DEPTH, D_FF, D_MODEL), D_FF ** -0.5),
    }


def reference(x_prompt, x_sample, cache_swa_k, cache_swa_v, state_conv, state_gla, state_pool,
              norm_mix, norm_ffn, w_in_even, w_out_even, q_norm, k_norm, attn_sinks, conv_w,
              w_in_odd, w_out_odd, w_alpha_up, b_alpha, gla_out_norm, w_pool, pool_scale,
              w_gate, w_up, w_down):
    prm = {'norm_mix': norm_mix, 'norm_ffn': norm_ffn, 'w_in_even': w_in_even, 'w_out_even': w_out_even,
           'q_norm': q_norm, 'k_norm': k_norm, 'attn_sinks': attn_sinks, 'conv_w': conv_w,
           'w_in_odd': w_in_odd, 'w_out_odd': w_out_odd, 'w_alpha_up': w_alpha_up, 'b_alpha': b_alpha,
           'gla_out_norm': gla_out_norm, 'w_pool': w_pool, 'pool_scale': pool_scale,
           'w_gate': w_gate, 'w_up': w_up, 'w_down': w_down}
    cache_len = cache_swa_k.shape[2]
    B, T = x_prompt.shape[:2]
    dt = x_prompt.dtype
    zero_conv = jnp.zeros((N_EVEN, B, CONV_W - 1, CONV_DIM), dt)
    zero_gla = jnp.zeros((N_ODD, B, GLA_HEADS, GLA_DK, GLA_DV), dt)
    zero_pool = jnp.zeros((N_ODD, B, POOL_BUF, POOL_DIM), dt)
    y_prompt, pk, pv, pc, pg, pp = run_trunk(x_prompt, jnp.arange(T), None, None, zero_conv, zero_gla, zero_pool,
                                             0, cache_len, prm, True)
    y_sample, sk, sv, sc, sg, sp = run_trunk(x_sample, PAST_LEN + jnp.arange(x_sample.shape[1]),
                                             cache_swa_k, cache_swa_v, state_conv, state_gla, state_pool,
                                             min(POOL_BUF, PAST_LEN), cache_len, prm, False)
    return (y_prompt, y_sample, pk, pv, pc, pg, pp, sk, sv, sc, sg, sp)
```

```python
import functools

import jax
import jax.numpy as jnp
from jax import lax
from jax.experimental import pallas as pl
from jax.experimental.pallas import tpu as pltpu

F32 = jnp.float32
BF16 = jnp.bfloat16

D_MODEL = 2048
BATCH = 4
SEQ = 2048
DEPTH = 4
DEC_BATCH = 32
PAST_LEN = 16384
N_EVEN = 2
N_ODD = 2
EPS = 1e-6
NEG_INF = -1e30
A_HEADS = 16
A_KV_HEADS = 4
HEAD_DIM = 64
A_WIDTH = A_HEADS * HEAD_DIM
A_KV_WIDTH = A_KV_HEADS * HEAD_DIM
WINDOW = 128
ROPE_THETA = 10000.0
CONV_DIM = D_MODEL // 2
CONV_W = 3
GLA_HEADS = 4
GLA_WIDTH = D_MODEL // 2
GLA_DV = GLA_WIDTH // GLA_HEADS
GLA_DK = GLA_DV // 2
GLA_K_WIDTH = GLA_HEADS * GLA_DK
GLA_RANK = 16
GLA_TAU = 16.0
GLA_CHUNK = 64
POOL_DIM = D_MODEL // 2
POOL_WINDOWS = (2, 4, 8, 16)
POOL_GDIM = POOL_DIM // 4
POOL_BUF = 15
D_FF = 5632
EVEN_IN = A_WIDTH + 2 * A_KV_WIDTH + 3 * CONV_DIM
ODD_MAIN = 2 * GLA_K_WIDTH + 2 * GLA_WIDTH

M_PROMPT = BATCH * SEQ
LANES = 128
BLK = 128
NBLK = SEQ // BLK
TM = 1024
N_TM = M_PROMPT // TM
VMEM_LIMIT = 56 * 1024 * 1024


def _cparams(*sem):
    return pltpu.CompilerParams(dimension_semantics=sem, vmem_limit_bytes=VMEM_LIMIT)


def _dot(a, b):
    return jnp.dot(a, b, preferred_element_type=F32)


def _dot_nt(a, b):
    return lax.dot_general(a, b, (((1,), (1,)), ((), ())), preferred_element_type=F32)


def _dot_tn(a, b):
    return lax.dot_general(a, b, (((0,), (0,)), ((), ())), preferred_element_type=F32)


def _silu(x):
    return x * (1.0 / (1.0 + jnp.exp(-x)))


def _rmsnorm_kernel(xp_ref, xs_ref, g_ref, op_ref, os_ref):
    def run(x_ref, o_ref):
        x = x_ref[...]
        ms = jnp.mean(x * x, axis=-1, keepdims=True)
        o_ref[...] = ((x * lax.rsqrt(ms + EPS)) * g_ref[...]).astype(o_ref.dtype)

    i = pl.program_id(0)

    @pl.when(i < N_TM)
    def _():
        run(xp_ref, op_ref)

    @pl.when(i == N_TM)
    def _():
        run(xs_ref, os_ref)


def _rmsnorm(x_p, x_s, gain):
    row = lambda i: (jnp.minimum(i, N_TM - 1), 0)
    fixed = lambda i: (0, 0)
    return pl.pallas_call(
        _rmsnorm_kernel,
        grid=(N_TM + 1,),
        in_specs=[pl.BlockSpec((TM, D_MODEL), row),
                  pl.BlockSpec((DEC_BATCH, D_MODEL), fixed),
                  pl.BlockSpec((1, D_MODEL), fixed)],
        out_specs=[pl.BlockSpec((TM, D_MODEL), row),
                   pl.BlockSpec((DEC_BATCH, D_MODEL), fixed)],
        out_shape=[jax.ShapeDtypeStruct((M_PROMPT, D_MODEL), BF16),
                   jax.ShapeDtypeStruct((DEC_BATCH, D_MODEL), BF16)],
        compiler_params=_cparams("arbitrary"),
        name="rmsnorm",
    )(x_p, x_s, gain.reshape(1, D_MODEL))


def _dense_kernel(*refs, n_w, mode):
    a_p, a_s = refs[0], refs[1]
    w = refs[2:2 + n_w]
    pos = 2 + n_w
    r_p = r_s = None
    if mode == "res":
        r_p, r_s = refs[pos], refs[pos + 1]
        pos += 2
    o_p, o_s = refs[pos], refs[pos + 1]
    wbf = refs[pos + 2:pos + 2 + n_w]
    i = pl.program_id(1)

    @pl.when(i == 0)
    def _():
        for k in range(n_w):
            wbf[k][...] = w[k][...].astype(BF16)

    def run(a_ref, r_ref, o_ref):
        a = a_ref[...]
        if mode == "swiglu":
            g = _dot(a, wbf[0][...])
            u = _dot(a, wbf[1][...])
            y = _silu(g) * u
        else:
            y = _dot(a, wbf[0][...])
            if mode == "res":
                y = r_ref[...] + y
        o_ref[...] = y.astype(o_ref.dtype)

    @pl.when(i < N_TM)
    def _():
        run(a_p, r_p, o_p)

    @pl.when(i == N_TM)
    def _():
        run(a_s, r_s, o_s)


def _dense(a_p, a_s, weights, *, n_cols, tn, mode="plain", res=None, out_dtype=F32, name):
    k_dim = a_p.shape[1]
    row = lambda j, i: (jnp.minimum(i, N_TM - 1), 0)
    fixed = lambda j, i: (0, 0)
    tile = lambda j, i: (jnp.minimum(i, N_TM - 1), j)
    panel = lambda j, i: (0, j)
    in_specs = [pl.BlockSpec((TM, k_dim), row), pl.BlockSpec((DEC_BATCH, k_dim), fixed)]
    args = [a_p, a_s]
    for arr, layer, col0 in weights:
        off = col0 // tn
        if layer is None:
            in_specs.append(pl.BlockSpec((k_dim, tn), lambda j, i, off=off: (0, j + off)))
        else:
            in_specs.append(pl.BlockSpec((None, k_dim, tn),
                                         lambda j, i, off=off, layer=layer: (layer, 0, j + off)))
        args.append(arr)
    if mode == "res":
        in_specs += [pl.BlockSpec((TM, tn), tile), pl.BlockSpec((DEC_BATCH, tn), panel)]
        args += list(res)
    n_w = len(weights)
    return pl.pallas_call(
        functools.partial(_dense_kernel, n_w=n_w, mode=mode),
        grid=(n_cols // tn, N_TM + 1),
        in_specs=in_specs,
        out_specs=[pl.BlockSpec((TM, tn), tile), pl.BlockSpec((DEC_BATCH, tn), panel)],
        out_shape=[jax.ShapeDtypeStruct((M_PROMPT, n_cols), out_dtype),
                   jax.ShapeDtypeStruct((DEC_BATCH, n_cols), out_dtype)],
        scratch_shapes=[pltpu.VMEM((k_dim, tn), BF16) for _ in range(n_w)],
        compiler_params=_cparams("arbitrary", "arbitrary"),
        name=name,
    )(*args)


def _norm_rope_chunk(xc, gain, cos, sin):
    lane = lax.broadcasted_iota(jnp.int32, xc.shape, 1)
    lo = lane < HEAD_DIM
    sq = xc * xc
    s_lo = jnp.sum(jnp.where(lo, sq, 0.0), axis=-1, keepdims=True)
    s_hi = jnp.sum(jnp.where(lo, 0.0, sq), axis=-1, keepdims=True)
    ms = jnp.where(lo, s_lo, s_hi) * (1.0 / HEAD_DIM)
    y = (xc * lax.rsqrt(ms + EPS)) * gain
    first = (lane % HEAD_DIM) < (HEAD_DIM // 2)
    swapped = jnp.where(first, pltpu.roll(y, LANES - HEAD_DIM // 2, axis=1),
                        pltpu.roll(y, HEAD_DIM // 2, axis=1))
    return y * cos + swapped * sin


def _spread_heads(xc, own_lo):
    lane = lax.broadcasted_iota(jnp.int32, xc.shape, 1)
    keep = (lane < HEAD_DIM) if own_lo else (lane >= HEAD_DIM)
    nat = jnp.where(keep, xc, 0.0)
    rol = pltpu.roll(nat, HEAD_DIM, axis=1)
    parts = (nat, rol) if own_lo else (rol, nat)
    return jnp.concatenate(parts, axis=0).astype(BF16)


def _attn_core(qs, kfull, vfull, mask, sink_ref):
    rows = qs[0].shape[0]
    nkeys = kfull.shape[0]
    outs = [None] * 8
    for kh in range(A_KV_HEADS):
        c0 = (kh // 2) * LANES
        kk = _spread_heads(kfull[:, c0:c0 + LANES], kh % 2 == 0)
        vv = _spread_heads(vfull[:, c0:c0 + LANES], kh % 2 == 0)
        lhs = jnp.concatenate([qs[2 * kh], qs[2 * kh + 1]], axis=0)
        s = _dot_nt(lhs, kk)
        prow = []
        for cc in range(2):
            pcol = []
            for half in range(2):
                sb = s[cc * rows:(cc + 1) * rows, half * nkeys:(half + 1) * nkeys]
                sb = jnp.where(mask, sb, NEG_INF)
                sink = sink_ref[kh * 4 + 2 * cc + half]
                m = jnp.maximum(jnp.max(sb, axis=-1, keepdims=True), sink)
                e = jnp.exp(sb - m)
                den = jnp.sum(e, axis=-1, keepdims=True) + jnp.exp(sink - m)
                pcol.append((e / den).astype(BF16))
            prow.append(jnp.concatenate(pcol, axis=1))
        p = jnp.concatenate(prow, axis=0)
        o = _dot(p, vv)
        outs[2 * kh] = o[0:rows]
        outs[2 * kh + 1] = o[rows:2 * rows]
    return outs


def _even_prompt_kernel(sink_ref, z_ref, cos_ref, sin_ref, qg_ref, kg_ref, cw_ref,
                        o_ref, nk_ref, nv_ref, nc_ref, kf_ref, vf_ref, ub_ref):
    i = pl.program_id(1)

    @pl.when(i == 0)
    def _():
        kf_ref[...] = jnp.zeros_like(kf_ref)
        vf_ref[...] = jnp.zeros_like(vf_ref)
        ub_ref[0:8, :] = jnp.zeros((8, CONV_DIM), F32)

    kf_ref[0:BLK, :] = kf_ref[BLK:2 * BLK, :]
    vf_ref[0:BLK, :] = vf_ref[BLK:2 * BLK, :]
    cos = cos_ref[...]
    sin = sin_ref[...]
    k = jnp.concatenate(
        [_norm_rope_chunk(z_ref[:, A_WIDTH + c * LANES:A_WIDTH + (c + 1) * LANES], kg_ref[...], cos, sin)
         for c in range(A_KV_WIDTH // LANES)], axis=1)
    v = z_ref[:, A_WIDTH + A_KV_WIDTH:A_WIDTH + 2 * A_KV_WIDTH]
    kf_ref[BLK:2 * BLK, :] = k
    vf_ref[BLK:2 * BLK, :] = v
    qs = [(_norm_rope_chunk(z_ref[:, c * LANES:(c + 1) * LANES], qg_ref[...], cos, sin)
           * (HEAD_DIM ** -0.5)).astype(BF16) for c in range(A_WIDTH // LANES)]
    r = lax.broadcasted_iota(jnp.int32, (BLK, 2 * BLK), 0)
    kk = lax.broadcasted_iota(jnp.int32, (BLK, 2 * BLK), 1)
    d = kk - r
    mask = (d >= 0) & (d <= WINDOW) & ((kk >= BLK) | (i > 0))
    outs = _attn_core(qs, kf_ref[...], vf_ref[...], mask, sink_ref)
    for c in range(A_WIDTH // LANES):
        o_ref[:, c * LANES:(c + 1) * LANES] = outs[c].astype(o_ref.dtype)

    o0 = A_WIDTH + 2 * A_KV_WIDTH
    u = z_ref[:, o0 + CONV_DIM:o0 + 2 * CONV_DIM] * z_ref[:, o0 + 2 * CONV_DIM:o0 + 3 * CONV_DIM]
    ub_ref[8:8 + BLK, :] = u
    y = ub_ref[6:6 + BLK, :] * cw_ref[0:1, :]
    y = y + ub_ref[7:7 + BLK, :] * cw_ref[1:2, :]
    y = y + ub_ref[8:8 + BLK, :] * cw_ref[2:3, :]
    o_ref[:, A_WIDTH:A_WIDTH + CONV_DIM] = (z_ref[:, o0:o0 + CONV_DIM] * y).astype(o_ref.dtype)
    ub_ref[0:8, :] = ub_ref[BLK:BLK + 8, :]

    @pl.when(i == pl.num_programs(1) - 1)
    def _():
        nk_ref[...] = k
        nv_ref[...] = v
        nc_ref[...] = ub_ref[BLK + 6:BLK + 8, :]


def _even_prompt(z_p, cos, sin, qg, kg, cw, sinks):
    blk = lambda b, i: (b * NBLK + i, 0)
    fixed = lambda b, i: (0, 0)
    per_b = lambda b, i: (b, 0, 0)
    return pl.pallas_call(
        _even_prompt_kernel,
        grid=(BATCH, NBLK),
        in_specs=[pl.BlockSpec(memory_space=pltpu.SMEM),
                  pl.BlockSpec((BLK, EVEN_IN), blk),
                  pl.BlockSpec((BLK, LANES), lambda b, i: (i, 0)),
                  pl.BlockSpec((BLK, LANES), lambda b, i: (i, 0)),
                  pl.BlockSpec((1, LANES), fixed),
                  pl.BlockSpec((1, LANES), fixed),
                  pl.BlockSpec((CONV_W, CONV_DIM), fixed)],
        out_specs=[pl.BlockSpec((BLK, D_MODEL), blk),
                   pl.BlockSpec((None, BLK, A_KV_WIDTH), per_b),
                   pl.BlockSpec((None, BLK, A_KV_WIDTH), per_b),
                   pl.BlockSpec((None, CONV_W - 1, CONV_DIM), per_b)],
        out_shape=[jax.ShapeDtypeStruct((M_PROMPT, D_MODEL), BF16),
                   jax.ShapeDtypeStruct((BATCH, BLK, A_KV_WIDTH), F32),
                   jax.ShapeDtypeStruct((BATCH, BLK, A_KV_WIDTH), F32),
                   jax.ShapeDtypeStruct((BATCH, CONV_W - 1, CONV_DIM), F32)],
        scratch_shapes=[pltpu.VMEM((2 * BLK, A_KV_WIDTH), F32),
                        pltpu.VMEM((2 * BLK, A_KV_WIDTH), F32),
                        pltpu.VMEM((BLK + 8, CONV_DIM), F32)],
        compiler_params=_cparams("arbitrary", "arbitrary"),
        name="even_prompt",
    )(sinks, z_p, cos, sin, qg, kg, cw)


def _even_sample_prep_kernel(z_ref, cos_ref, sin_ref, qg_ref, kg_ref, c0_ref, c1_ref, cw_ref,
                             q_ref, k_ref, u_ref, o_ref):
    cos = cos_ref[...]
    sin = sin_ref[...]
    for c in range(A_WIDTH // LANES):
        q = _norm_rope_chunk(z_ref[:, c * LANES:(c + 1) * LANES], qg_ref[...], cos, sin)
        q_ref[:, c * LANES:(c + 1) * LANES] = (q * (HEAD_DIM ** -0.5)).astype(q_ref.dtype)
    for c in range(A_KV_WIDTH // LANES):
        k_ref[:, c * LANES:(c + 1) * LANES] = _norm_rope_chunk(
            z_ref[:, A_WIDTH + c * LANES:A_WIDTH + (c + 1) * LANES], kg_ref[...], cos, sin)
    o0 = A_WIDTH + 2 * A_KV_WIDTH
    u = z_ref[:, o0 + CONV_DIM:o0 + 2 * CONV_DIM] * z_ref[:, o0 + 2 * CONV_DIM:o0 + 3 * CONV_DIM]
    u_ref[...] = u
    y = c0_ref[...] * cw_ref[0:1, :]
    y = y + c1_ref[...] * cw_ref[1:2, :]
    y = y + u * cw_ref[2:3, :]
    o_ref[...] = (z_ref[:, o0:o0 + CONV_DIM] * y).astype(o_ref.dtype)


def _even_sample_prep(z_s, cos, sin, qg, kg, c0, c1, cw):
    return pl.pallas_call(
        _even_sample_prep_kernel,
        out_shape=[jax.ShapeDtypeStruct((DEC_BATCH, A_WIDTH), BF16),
                   jax.ShapeDtypeStruct((DEC_BATCH, A_KV_WIDTH), F32),
                   jax.ShapeDtypeStruct((DEC_BATCH, CONV_DIM), F32),
                   jax.ShapeDtypeStruct((DEC_BATCH, CONV_DIM), BF16)],
        compiler_params=pltpu.CompilerParams(vmem_limit_bytes=VMEM_LIMIT),
        name="even_sample_prep",
    )(z_s, cos, sin, qg, kg, c0, c1, cw)


SROWS = 8


def _even_sample_attn_kernel(sink_ref, q_ref, kn_ref, vn_ref, kc_ref, vc_ref,
                             o_ref, nk_ref, nv_ref):
    lc = kc_ref.shape[0]
    row = lax.broadcasted_iota(jnp.int32, (lc, A_KV_WIDTH), 0)
    kfull = jnp.concatenate([kc_ref[...], jnp.where(row == 0, kn_ref[...], 0.0)], axis=0)
    vfull = jnp.concatenate([vc_ref[...], jnp.where(row == 0, vn_ref[...], 0.0)], axis=0)
    qs = [jnp.broadcast_to(q_ref[:, c * LANES:(c + 1) * LANES], (SROWS, LANES))
          for c in range(A_WIDTH // LANES)]
    kk = lax.broadcasted_iota(jnp.int32, (SROWS, 2 * lc), 1)
    mask = (kk <= lc) & (lc - kk <= WINDOW)
    outs = _attn_core(qs, kfull, vfull, mask, sink_ref)
    for c in range(A_WIDTH // LANES):
        o_ref[:, c * LANES:(c + 1) * LANES] = outs[c][0:1].astype(o_ref.dtype)
    nk_ref[0:lc - 1, :] = kc_ref[1:lc, :]
    nk_ref[lc - 1:lc, :] = kn_ref[...]
    nv_ref[0:lc - 1, :] = vc_ref[1:lc, :]
    nv_ref[lc - 1:lc, :] = vn_ref[...]


def _even_sample_attn(sinks, q_s, k_s, v_s, k_cache, v_cache, li):
    lc = k_cache.shape[2]
    vec = lambda w: pl.BlockSpec((None, 1, w), lambda b: (b, 0, 0))
    cache_in = pl.BlockSpec((None, None, lc, A_KV_WIDTH), lambda b: (li, b, 0, 0))
    cache = pl.BlockSpec((None, lc, A_KV_WIDTH), lambda b: (b, 0, 0))
    return pl.pallas_call(
        _even_sample_attn_kernel,
        grid=(DEC_BATCH,),
        in_specs=[pl.BlockSpec(memory_space=pltpu.SMEM), vec(A_WIDTH), vec(A_KV_WIDTH), vec(A_KV_WIDTH),
                  cache_in, cache_in],
        out_specs=[vec(A_WIDTH), cache, cache],
        out_shape=[jax.ShapeDtypeStruct((DEC_BATCH, 1, A_WIDTH), BF16),
                   jax.ShapeDtypeStruct((DEC_BATCH, lc, A_KV_WIDTH), F32),
                   jax.ShapeDtypeStruct((DEC_BATCH, lc, A_KV_WIDTH), F32)],
        compiler_params=_cparams("arbitrary"),
        name="even_sample_attn",
    )(sinks, q_s.reshape(DEC_BATCH, 1, A_WIDTH), k_s.reshape(DEC_BATCH, 1, A_KV_WIDTH),
      v_s.reshape(DEC_BATCH, 1, A_KV_WIDTH), k_cache, v_cache)


def _log_decay(zr, wa_ref, ba_ref):
    pre = _dot(zr.astype(BF16), wa_ref[...]) + ba_ref[...]
    return (jnp.minimum(pre, 0.0) - jnp.log1p(jnp.exp(-jnp.abs(pre)))) * (1.0 / GLA_TAU)


def _split_bf16(x):
    hi = x.astype(BF16)
    lo = (x - hi.astype(F32)).astype(BF16)
    return hi, lo


def _row_to_col(row):
    n = row.shape[1]
    r = lax.broadcasted_iota(jnp.int32, (n, n), 0)
    c = lax.broadcasted_iota(jnp.int32, (n, n), 1)
    return jnp.sum(jnp.where(r == c, jnp.broadcast_to(row, (n, n)), 0.0), axis=-1, keepdims=True)


def _head_rmsnorm_gate(o, gain, zg):
    ms = jnp.mean(o * o, axis=-1, keepdims=True)
    return ((o * lax.rsqrt(ms + EPS)) * gain) * _silu(zg)


def _odd_prompt_kernel(zm_ref, zp_ref, zr_ref, wa_ref, ba_ref, gain_ref, wp_ref, ps_ref,
                       o_ref, s_ref, np_ref, pb_ref):
    i = pl.program_id(1)

    @pl.when(i == 0)
    def _():
        s_ref[...] = jnp.zeros_like(s_ref)
        pb_ref[0:BLK, :] = jnp.zeros((BLK, POOL_DIM), F32)

    gk = _log_decay(zr_ref[...], wa_ref, ba_ref)
    cr = lax.broadcasted_iota(jnp.int32, (GLA_CHUNK, GLA_CHUNK), 0)
    cc = lax.broadcasted_iota(jnp.int32, (GLA_CHUNK, GLA_CHUNK), 1)
    causal = cr >= cc
    tri = jnp.where(causal, 1.0, 0.0).astype(BF16)
    o_k, o_v, o_g = GLA_K_WIDTH, 2 * GLA_K_WIDTH, 2 * GLA_K_WIDTH + GLA_WIDTH
    for c in range(BLK // GLA_CHUNK):
        rows = slice(c * GLA_CHUNK, (c + 1) * GLA_CHUNK)
        g_hi, g_lo = _split_bf16(gk[rows])
        b = _dot(tri, g_hi) + _dot(tri, g_lo)
        b_last = b[GLA_CHUNK - 1:GLA_CHUNK, :]
        zq = zm_ref[rows, 0:GLA_K_WIDTH]
        zk = zm_ref[rows, o_k:o_k + GLA_K_WIDTH]
        qd = ((zq * (GLA_DK ** -0.5)) * jnp.exp(b)).astype(BF16)
        kd = (zk * jnp.exp(-b)).astype(BF16)
        k2 = (zk * jnp.exp(b_last - b)).astype(BF16)
        for h in range(GLA_HEADS):
            ks = slice(h * GLA_DK, (h + 1) * GLA_DK)
            vs = slice(h * GLA_DV, (h + 1) * GLA_DV)
            v_h = zm_ref[rows, o_v + h * GLA_DV:o_v + (h + 1) * GLA_DV].astype(BF16)
            att = jnp.where(causal, _dot_nt(qd[:, ks], kd[:, ks]), 0.0).astype(BF16)
            s_h = s_ref[h]
            o = _dot(att, v_h) + _dot(qd[:, ks], s_h.astype(BF16))
            delta = _dot_tn(k2[:, ks], v_h)
            s_ref[h] = jnp.exp(_row_to_col(b_last[:, ks])) * s_h + delta
            zg = zm_ref[rows, o_g + h * GLA_DV:o_g + (h + 1) * GLA_DV]
            o_ref[rows, vs] = _head_rmsnorm_gate(o, gain_ref[...], zg).astype(o_ref.dtype)

    x = zp_ref[...]
    pb_ref[BLK:2 * BLK, :] = x
    e_hi, e_lo = _split_bf16(pb_ref[...])
    t = lax.broadcasted_iota(jnp.int32, (BLK, 2 * BLK), 0)
    j = lax.broadcasted_iota(jnp.int32, (BLK, 2 * BLK), 1)
    back = t + BLK - j
    t_glob = i * BLK + lax.broadcasted_iota(jnp.int32, (BLK, 1), 0)
    for g, w in enumerate(POOL_WINDOWS):
        gs = slice(g * POOL_GDIM, (g + 1) * POOL_GDIM)
        band = jnp.where((back >= 0) & (back < w), 1.0, 0.0).astype(BF16)
        s = _dot(band, e_hi[:, gs]) + _dot(band, e_lo[:, gs])
        cnt = jnp.minimum(w, t_glob + 1).astype(F32)
        d = s / cnt - x[:, gs]
        y = _dot(d.astype(BF16), wp_ref[g]) * ps_ref[:, gs]
        o_ref[:, GLA_WIDTH + g * POOL_GDIM:GLA_WIDTH + (g + 1) * POOL_GDIM] = y.astype(o_ref.dtype)
    pb_ref[0:BLK, :] = x

    @pl.when(i == pl.num_programs(1) - 1)
    def _():
        np_ref[...] = pb_ref[2 * BLK - POOL_BUF:2 * BLK, :]


def _odd_prompt(zm_p, zp_p, zr_p, wa, ba, gain, wp, ps):
    blk = lambda b, i: (b * NBLK + i, 0)
    fixed2 = lambda b, i: (0, 0)
    return pl.pallas_call(
        _odd_prompt_kernel,
        grid=(BATCH, NBLK),
        in_specs=[pl.BlockSpec((BLK, ODD_MAIN), blk),
                  pl.BlockSpec((BLK, POOL_DIM), blk),
                  pl.BlockSpec((BLK, LANES), blk),
                  pl.BlockSpec((LANES, GLA_K_WIDTH), fixed2),
                  pl.BlockSpec((1, GLA_K_WIDTH), fixed2),
                  pl.BlockSpec((1, GLA_DV), fixed2),
                  pl.BlockSpec((4, POOL_GDIM, POOL_GDIM), lambda b, i: (0, 0, 0)),
                  pl.BlockSpec((1, POOL_DIM), fixed2)],
        out_specs=[pl.BlockSpec((BLK, D_MODEL), blk),
                   pl.BlockSpec((None, GLA_HEADS, GLA_DK, GLA_DV), lambda b, i: (b, 0, 0, 0)),
                   pl.BlockSpec((None, POOL_BUF, POOL_DIM), lambda b, i: (b, 0, 0))],
        out_shape=[jax.ShapeDtypeStruct((M_PROMPT, D_MODEL), BF16),
                   jax.ShapeDtypeStruct((BATCH, GLA_HEADS, GLA_DK, GLA_DV), F32),
                   jax.ShapeDtypeStruct((BATCH, POOL_BUF, POOL_DIM), F32)],
        scratch_shapes=[pltpu.VMEM((2 * BLK, POOL_DIM), F32)],
        compiler_params=_cparams("arbitrary", "arbitrary"),
        name="odd_prompt",
    )(zm_p, zp_p, zr_p, wa, ba, gain, wp, ps)


def _odd_sample_prep_kernel(zm_ref, zr_ref, wa_ref, ba_ref, dec_ref, q_ref):
    dec_ref[...] = jnp.exp(_log_decay(zr_ref[...], wa_ref, ba_ref))
    q_ref[...] = zm_ref[:, 0:GLA_K_WIDTH] * (GLA_DK ** -0.5)


def _odd_sample_prep(zm_s, zr_s, wa, ba):
    return pl.pallas_call(
        _odd_sample_prep_kernel,
        out_shape=[jax.ShapeDtypeStruct((DEC_BATCH, GLA_K_WIDTH), F32),
                   jax.ShapeDtypeStruct((DEC_BATCH, GLA_K_WIDTH), F32)],
        compiler_params=pltpu.CompilerParams(vmem_limit_bytes=VMEM_LIMIT),
        name="odd_sample_prep",
    )(zm_s, zr_s, wa, ba)


def _odd_sample_state_kernel(dec_ref, q_ref, k_ref, v_ref, s_ref, ns_ref, o_ref):
    for h in range(GLA_HEADS):
        ks = slice(h * GLA_DK, (h + 1) * GLA_DK)
        vs = slice(h * GLA_DV, (h + 1) * GLA_DV)
        s_new = _row_to_col(dec_ref[:, ks]) * s_ref[h] + _row_to_col(k_ref[:, ks]) * v_ref[:, vs]
        ns_ref[h] = s_new
        o_ref[:, vs] = jnp.sum(_row_to_col(q_ref[:, ks]) * s_new, axis=0, keepdims=True)


def _odd_sample_state(dec, q, k, v, state, li):
    vec = lambda w: pl.BlockSpec((None, 1, w), lambda b: (b, 0, 0))
    st_in = pl.BlockSpec((None, None, GLA_HEADS, GLA_DK, GLA_DV), lambda b: (li, b, 0, 0, 0))
    st = pl.BlockSpec((None, GLA_HEADS, GLA_DK, GLA_DV), lambda b: (b, 0, 0, 0))
    r3 = lambda a: a.reshape(DEC_BATCH, 1, a.shape[-1])
    return pl.pallas_call(
        _odd_sample_state_kernel,
        grid=(DEC_BATCH,),
        in_specs=[vec(GLA_K_WIDTH), vec(GLA_K_WIDTH), vec(GLA_K_WIDTH), vec(GLA_WIDTH), st_in],
        out_specs=[st, vec(GLA_WIDTH)],
        out_shape=[jax.ShapeDtypeStruct(state.shape[1:], F32),
                   jax.ShapeDtypeStruct((DEC_BATCH, 1, GLA_WIDTH), F32)],
        compiler_params=_cparams("arbitrary"),
        name="odd_sample_state",
    )(r3(dec), r3(q), r3(k), r3(v), state)


def _odd_sample_post_kernel(o_in_ref, zm_ref, zp_ref, hist_ref, gain_ref, wp_ref, ps_ref, o_ref):
    o_g = 2 * GLA_K_WIDTH + GLA_WIDTH
    for h in range(GLA_HEADS):
        vs = slice(h * GLA_DV, (h + 1) * GLA_DV)
        zg = zm_ref[:, o_g + h * GLA_DV:o_g + (h + 1) * GLA_DV]
        o_ref[:, vs] = _head_rmsnorm_gate(o_in_ref[:, vs], gain_ref[...], zg).astype(o_ref.dtype)
    n_prev = hist_ref.shape[0]
    for g, w in enumerate(POOL_WINDOWS):
        gs = slice(g * POOL_GDIM, (g + 1) * POOL_GDIM)
        x = zp_ref[:, gs]
        s = hist_ref[n_prev - (w - 1)][:, gs]
        for jj in range(n_prev - (w - 1) + 1, n_prev):
            s = s + hist_ref[jj][:, gs]
        s = s + x
        d = s / float(min(w, n_prev + 1)) - x
        y = _dot(d.astype(BF16), wp_ref[g]) * ps_ref[:, gs]
        o_ref[:, GLA_WIDTH + g * POOL_GDIM:GLA_WIDTH + (g + 1) * POOL_GDIM] = y.astype(o_ref.dtype)


def _odd_sample_post(o_raw, zm_s, zp_s, hist_t, gain, wp, ps):
    return pl.pallas_call(
        _odd_sample_post_kernel,
        out_shape=jax.ShapeDtypeStruct((DEC_BATCH, D_MODEL), BF16),
        compiler_params=pltpu.CompilerParams(vmem_limit_bytes=VMEM_LIMIT),
        name="odd_sample_post",
    )(o_raw, zm_s, zp_s, hist_t, gain, wp, ps)


def _rope_tables(pos):
    half = HEAD_DIM // 2
    inv = jnp.power(ROPE_THETA, -jnp.arange(half, dtype=F32) / half)
    ang = pos.astype(F32)[:, None] * inv[None, :]
    c, s = jnp.cos(ang), jnp.sin(ang)
    return jnp.tile(c, (1, 4)), jnp.tile(jnp.concatenate([-s, s], axis=1), (1, 2))


def kernel(x_prompt, x_sample, cache_swa_k, cache_swa_v, state_conv, state_gla, state_pool, norm_mix, norm_ffn, w_in_even, w_out_even, q_norm, k_norm, attn_sinks, conv_w, w_in_odd, w_out_odd, w_alpha_up, b_alpha, gla_out_norm, w_pool, pool_scale, w_gate, w_up, w_down):
    lc = cache_swa_k.shape[2]
    x_p = x_prompt.reshape(M_PROMPT, D_MODEL)
    x_s = x_sample.reshape(DEC_BATCH, D_MODEL)
    cos_p, sin_p = _rope_tables(jnp.arange(SEQ))
    cos_s, sin_s = _rope_tables(PAST_LEN + jnp.arange(1))
    pk, pv, pc, pg, pp, sk, sv, sc, sg, sp = ([] for _ in range(10))

    for layer in range(DEPTH):
        li = layer // 2
        h_p, h_s = _rmsnorm(x_p, x_s, norm_mix[layer])
        if layer % 2 == 0:
            z_p, z_s = _dense(h_p, h_s, [(w_in_even, li, 0)], n_cols=EVEN_IN, tn=512, name="in_even")
            qg = jnp.tile(q_norm[li], 2).reshape(1, LANES)
            kg = jnp.tile(k_norm[li], 2).reshape(1, LANES)
            m_p, nk, nv, nc = _even_prompt(z_p, cos_p, sin_p, qg, kg, conv_w[li], attn_sinks[li])
            pk.append(nk.reshape(BATCH, lc, A_KV_HEADS, HEAD_DIM))
            pv.append(nv.reshape(BATCH, lc, A_KV_HEADS, HEAD_DIM))
            pc.append(nc)
            q_s, k_s, u_s, conv_s = _even_sample_prep(
                z_s, cos_s, sin_s, qg, kg, state_conv[li, :, 0], state_conv[li, :, 1], conv_w[li])
            v_s = z_s[:, A_WIDTH + A_KV_WIDTH:A_WIDTH + 2 * A_KV_WIDTH]
            attn_s, nk_s, nv_s = _even_sample_attn(
                attn_sinks[li], q_s, k_s, v_s,
                cache_swa_k.reshape(N_EVEN, DEC_BATCH, lc, A_KV_WIDTH),
                cache_swa_v.reshape(N_EVEN, DEC_BATCH, lc, A_KV_WIDTH), li)
            sk.append(nk_s.reshape(DEC_BATCH, lc, A_KV_HEADS, HEAD_DIM))
            sv.append(nv_s.reshape(DEC_BATCH, lc, A_KV_HEADS, HEAD_DIM))
            sc.append(jnp.stack([state_conv[li, :, 1], u_s], axis=1))
            m_s = jnp.concatenate([attn_s.reshape(DEC_BATCH, A_WIDTH), conv_s], axis=1)
            w_out = (w_out_even, li, 0)
        else:
            w = w_in_odd[li]
            o_r = ODD_MAIN
            w_r = jnp.pad(w[:, o_r:o_r + GLA_RANK], ((0, 0), (0, LANES - GLA_RANK)))
            w_pl = w[:, o_r + GLA_RANK:]
            zm_p, zm_s = _dense(h_p, h_s, [(w_in_odd, li, 0)], n_cols=ODD_MAIN, tn=512, name="in_odd_main")
            zp_p, zp_s = _dense(h_p, h_s, [(w_pl, None, 0)], n_cols=POOL_DIM, tn=512, name="in_odd_pool")
            zr_p, zr_s = _dense(h_p, h_s, [(w_r, None, 0)], n_cols=LANES, tn=LANES, name="in_odd_rank")
            wa = jnp.pad(w_alpha_up[li], ((0, LANES - GLA_RANK), (0, 0))).astype(BF16)
            ba = b_alpha[li].reshape(1, GLA_K_WIDTH)
            gain = gla_out_norm[li].reshape(1, GLA_DV)
            wp = w_pool[li].astype(BF16)
            ps = pool_scale[li].reshape(1, POOL_DIM)
            m_p, ng, npool = _odd_prompt(zm_p, zp_p, zr_p, wa, ba, gain, wp, ps)
            pg.append(ng)
            pp.append(npool)
            dec_s, q_s = _odd_sample_prep(zm_s, zr_s, wa, ba)
            ng_s, o_raw = _odd_sample_state(
                dec_s, q_s, zm_s[:, GLA_K_WIDTH:2 * GLA_K_WIDTH],
                zm_s[:, 2 * GLA_K_WIDTH:2 * GLA_K_WIDTH + GLA_WIDTH], state_gla, li)
            sg.append(ng_s)
            m_s = _odd_sample_post(o_raw.reshape(DEC_BATCH, GLA_WIDTH), zm_s, zp_s,
                                   jnp.swapaxes(state_pool[li], 0, 1), gain, wp, ps)
            sp.append(jnp.concatenate([state_pool[li, :, 1:], zp_s[:, None, :]], axis=1))
            w_out = (w_out_odd, li, 0)
        x_p, x_s = _dense(m_p, m_s, [w_out], n_cols=D_MODEL, tn=512, mode="res", res=(x_p, x_s),
                          name="out_proj")
        h_p, h_s = _rmsnorm(x_p, x_s, norm_ffn[layer])
        a_p, a_s = _dense(h_p, h_s, [(w_gate, layer, 0), (w_up, layer, 0)], n_cols=D_FF, tn=512,
                          mode="swiglu", out_dtype=BF16, name="ffn_up")
        x_p, x_s = _dense(a_p, a_s, [(w_down, layer, 0)], n_cols=D_MODEL, tn=256, mode="res",
                          res=(x_p, x_s), name="ffn_down")

    st = lambda parts: jnp.stack(parts)
    return (x_p.reshape(BATCH, SEQ, D_MODEL), x_s.reshape(DEC_BATCH, 1, D_MODEL),
            st(pk), st(pv), st(pc), st(pg), st(pp), st(sk), st(sv), st(sc), st(sg), st(sp))
```

```python
import functools

import jax
import jax.numpy as jnp
from jax import lax
from jax.experimental import pallas as pl
from jax.experimental.pallas import tpu as pltpu

F32 = jnp.float32
BF16 = jnp.bfloat16

D_MODEL = 2048
BATCH = 4
SEQ = 2048
DEPTH = 4
DEC_BATCH = 32
PAST_LEN = 16384
N_EVEN = 2
N_ODD = 2
EPS = 1e-6
NEG_INF = -1e30
A_HEADS = 16
A_KV_HEADS = 4
HEAD_DIM = 64
A_WIDTH = A_HEADS * HEAD_DIM
A_KV_WIDTH = A_KV_HEADS * HEAD_DIM
WINDOW = 128
ROPE_THETA = 10000.0
CONV_DIM = D_MODEL // 2
CONV_W = 3
GLA_HEADS = 4
GLA_WIDTH = D_MODEL // 2
GLA_DV = GLA_WIDTH // GLA_HEADS
GLA_DK = GLA_DV // 2
GLA_K_WIDTH = GLA_HEADS * GLA_DK
GLA_RANK = 16
GLA_TAU = 16.0
GLA_CHUNK = 64
POOL_DIM = D_MODEL // 2
POOL_WINDOWS = (2, 4, 8, 16)
POOL_GDIM = POOL_DIM // 4
POOL_BUF = 15
D_FF = 5632
EVEN_IN = A_WIDTH + 2 * A_KV_WIDTH + 3 * CONV_DIM
ODD_MAIN = 2 * GLA_K_WIDTH + 2 * GLA_WIDTH

M_PROMPT = BATCH * SEQ
LANES = 128
BLK = 128
NBLK = SEQ // BLK
POOL_RANK_W = POOL_DIM + LANES
VMEM_LIMIT = 58 * 1024 * 1024


def _cparams(*sem):
    return pltpu.CompilerParams(dimension_semantics=sem, vmem_limit_bytes=VMEM_LIMIT)


def _dot(a, b):
    return jnp.dot(a, b, preferred_element_type=F32)


def _dot_nt(a, b):
    return lax.dot_general(a, b, (((1,), (1,)), ((), ())), preferred_element_type=F32)


def _dot_tn(a, b):
    return lax.dot_general(a, b, (((0,), (0,)), ((), ())), preferred_element_type=F32)


def _silu(x):
    return x * (1.0 / (1.0 + jnp.exp(-x)))


def _rmsnorm_rows(x, gain):
    ms = jnp.mean(x * x, axis=-1, keepdims=True)
    return (x * lax.rsqrt(ms + EPS)) * gain


def _rmsnorm_kernel(xp_ref, xs_ref, g_ref, op_ref, os_ref, *, n_i):
    op_ref[...] = _rmsnorm_rows(xp_ref[...], g_ref[...]).astype(op_ref.dtype)

    @pl.when(pl.program_id(0) == n_i - 1)
    def _():
        os_ref[...] = _rmsnorm_rows(xs_ref[...], g_ref[...]).astype(os_ref.dtype)


def _rmsnorm(x_p, x_s, gain, *, tm=1024):
    n_i = M_PROMPT // tm
    row = lambda i: (i, 0)
    fixed = lambda i: (0, 0)
    return pl.pallas_call(
        functools.partial(_rmsnorm_kernel, n_i=n_i),
        grid=(n_i,),
        in_specs=[pl.BlockSpec((tm, D_MODEL), row),
                  pl.BlockSpec((DEC_BATCH, D_MODEL), fixed),
                  pl.BlockSpec((1, D_MODEL), fixed)],
        out_specs=[pl.BlockSpec((tm, D_MODEL), row),
                   pl.BlockSpec((DEC_BATCH, D_MODEL), fixed)],
        out_shape=[jax.ShapeDtypeStruct((M_PROMPT, D_MODEL), BF16),
                   jax.ShapeDtypeStruct((DEC_BATCH, D_MODEL), BF16)],
        compiler_params=_cparams("arbitrary"),
        name="rmsnorm",
    )(x_p, x_s, gain.reshape(1, D_MODEL))


def _dense_kernel(*refs, n_w, mode, n_i):
    a_p, a_s = refs[0], refs[1]
    w = refs[2:2 + n_w]
    o_p, o_s = refs[2 + n_w], refs[3 + n_w]
    wbf = refs[4 + n_w:4 + 2 * n_w]
    i = pl.program_id(1)

    @pl.when(i == 0)
    def _():
        for k in range(n_w):
            wbf[k][...] = w[k][...].astype(BF16)

    def run(a_ref, o_ref):
        a = a_ref[...]
        if mode == "swiglu":
            y = _silu(_dot(a, wbf[0][...])) * _dot(a, wbf[1][...])
        else:
            y = _dot(a, wbf[0][...])
        o_ref[...] = y.astype(o_ref.dtype)

    run(a_p, o_p)

    @pl.when(i == n_i - 1)
    def _():
        run(a_s, o_s)


def _dense(a_p, a_s, weights, *, n_cols, tm, tn, mode="plain", out_dtype=F32, name):
    k_dim = a_p.shape[1]
    n_i = M_PROMPT // tm
    row = lambda j, i: (i, 0)
    fixed = lambda j, i: (0, 0)
    tile = lambda j, i: (i, j)
    panel = lambda j, i: (0, j)
    in_specs = [pl.BlockSpec((tm, k_dim), row), pl.BlockSpec((DEC_BATCH, k_dim), fixed)]
    args = [a_p, a_s]
    for arr, layer in weights:
        if layer is None:
            in_specs.append(pl.BlockSpec((k_dim, tn), panel))
        else:
            in_specs.append(pl.BlockSpec((None, k_dim, tn), lambda j, i, layer=layer: (layer, 0, j)))
        args.append(arr)
    n_w = len(weights)
    return pl.pallas_call(
        functools.partial(_dense_kernel, n_w=n_w, mode=mode, n_i=n_i),
        grid=(n_cols // tn, n_i),
        in_specs=in_specs,
        out_specs=[pl.BlockSpec((tm, tn), tile), pl.BlockSpec((DEC_BATCH, tn), panel)],
        out_shape=[jax.ShapeDtypeStruct((M_PROMPT, n_cols), out_dtype),
                   jax.ShapeDtypeStruct((DEC_BATCH, n_cols), out_dtype)],
        scratch_shapes=[pltpu.VMEM((k_dim, tn), BF16) for _ in range(n_w)],
        compiler_params=_cparams("arbitrary", "arbitrary"),
        name=name,
    )(*args)


def _proj_res_kernel(*refs, n_i, with_norm):
    a_p, a_s, w_ref, r_p, r_s = refs[:5]
    if with_norm:
        g_ref, x_p, x_s, h_p, h_s = refs[5:]
    else:
        x_p, x_s = refs[5:]
        g_ref = h_p = h_s = None

    def run(a_ref, r_ref, x_ref, h_ref):
        x = r_ref[...] + _dot(a_ref[...], w_ref[...])
        x_ref[...] = x
        if with_norm:
            h_ref[...] = _rmsnorm_rows(x, g_ref[...]).astype(h_ref.dtype)

    run(a_p, r_p, x_p, h_p)

    @pl.when(pl.program_id(0) == n_i - 1)
    def _():
        run(a_s, r_s, x_s, h_s)


def _proj_res(a_p, a_s, w_bf, res, gain, *, tm, name):
    k_dim = a_p.shape[1]
    n_i = M_PROMPT // tm
    with_norm = gain is not None
    row = lambda i: (i, 0)
    fixed = lambda i: (0, 0)
    rows_p = pl.BlockSpec((tm, D_MODEL), row)
    rows_s = pl.BlockSpec((DEC_BATCH, D_MODEL), fixed)
    in_specs = [pl.BlockSpec((tm, k_dim), row), pl.BlockSpec((DEC_BATCH, k_dim), fixed),
                pl.BlockSpec((k_dim, D_MODEL), fixed, pipeline_mode=pl.Buffered(1)), rows_p, rows_s]
    args = [a_p, a_s, w_bf, res[0], res[1]]
    out_specs = [rows_p, rows_s]
    out_shape = [jax.ShapeDtypeStruct((M_PROMPT, D_MODEL), F32), jax.ShapeDtypeStruct((DEC_BATCH, D_MODEL), F32)]
    if with_norm:
        in_specs.append(pl.BlockSpec((1, D_MODEL), fixed))
        args.append(gain.reshape(1, D_MODEL))
        out_specs += [rows_p, rows_s]
        out_shape += [jax.ShapeDtypeStruct((M_PROMPT, D_MODEL), BF16),
                      jax.ShapeDtypeStruct((DEC_BATCH, D_MODEL), BF16)]
    return pl.pallas_call(
        functools.partial(_proj_res_kernel, n_i=n_i, with_norm=with_norm),
        grid=(n_i,),
        in_specs=in_specs,
        out_specs=out_specs,
        out_shape=out_shape,
        compiler_params=_cparams("arbitrary"),
        name=name,
    )(*args)


def _norm_rope_chunk(xc, gain, cos, sin):
    lane = lax.broadcasted_iota(jnp.int32, xc.shape, 1)
    lo = lane < HEAD_DIM
    sq = xc * xc
    s_lo = jnp.sum(jnp.where(lo, sq, 0.0), axis=-1, keepdims=True)
    s_hi = jnp.sum(jnp.where(lo, 0.0, sq), axis=-1, keepdims=True)
    ms = jnp.where(lo, s_lo, s_hi) * (1.0 / HEAD_DIM)
    y = (xc * lax.rsqrt(ms + EPS)) * gain
    first = (lane % HEAD_DIM) < (HEAD_DIM // 2)
    swapped = jnp.where(first, pltpu.roll(y, LANES - HEAD_DIM // 2, axis=1),
                        pltpu.roll(y, HEAD_DIM // 2, axis=1))
    return y * cos + swapped * sin


def _spread_heads(xc, own_lo):
    lane = lax.broadcasted_iota(jnp.int32, xc.shape, 1)
    keep = (lane < HEAD_DIM) if own_lo else (lane >= HEAD_DIM)
    nat = jnp.where(keep, xc, 0.0)
    rol = pltpu.roll(nat, HEAD_DIM, axis=1)
    parts = (nat, rol) if own_lo else (rol, nat)
    return jnp.concatenate(parts, axis=0).astype(BF16)


def _attn_core(qs, kfull, vfull, mask, sink_ref):
    rows = qs[0].shape[0]
    nkeys = kfull.shape[0]
    outs = [None] * 8
    for kh in range(A_KV_HEADS):
        c0 = (kh // 2) * LANES
        kk = _spread_heads(kfull[:, c0:c0 + LANES], kh % 2 == 0)
        vv = _spread_heads(vfull[:, c0:c0 + LANES], kh % 2 == 0)
        lhs = jnp.concatenate([qs[2 * kh], qs[2 * kh + 1]], axis=0)
        s = _dot_nt(lhs, kk)
        prow = []
        for cc in range(2):
            pcol = []
            for half in range(2):
                sb = s[cc * rows:(cc + 1) * rows, half * nkeys:(half + 1) * nkeys]
                sb = jnp.where(mask, sb, NEG_INF)
                sink = sink_ref[kh * 4 + 2 * cc + half]
                m = jnp.maximum(jnp.max(sb, axis=-1, keepdims=True), sink)
                e = jnp.exp(sb - m)
                den = jnp.sum(e, axis=-1, keepdims=True) + jnp.exp(sink - m)
                pcol.append((e / den).astype(BF16))
            prow.append(jnp.concatenate(pcol, axis=1))
        p = jnp.concatenate(prow, axis=0)
        o = _dot(p, vv)
        outs[2 * kh] = o[0:rows]
        outs[2 * kh + 1] = o[rows:2 * rows]
    return outs


def _even_prompt_kernel(sink_ref, z_ref, cos_ref, sin_ref, qg_ref, kg_ref, cw_ref,
                        o_ref, nk_ref, nv_ref, nc_ref, kf_ref, vf_ref, ub_ref):
    i = pl.program_id(1)

    @pl.when(i == 0)
    def _():
        kf_ref[...] = jnp.zeros_like(kf_ref)
        vf_ref[...] = jnp.zeros_like(vf_ref)
        ub_ref[0:8, :] = jnp.zeros((8, CONV_DIM), F32)

    kf_ref[0:BLK, :] = kf_ref[BLK:2 * BLK, :]
    vf_ref[0:BLK, :] = vf_ref[BLK:2 * BLK, :]
    cos = cos_ref[...]
    sin = sin_ref[...]
    k = jnp.concatenate(
        [_norm_rope_chunk(z_ref[:, A_WIDTH + c * LANES:A_WIDTH + (c + 1) * LANES], kg_ref[...], cos, sin)
         for c in range(A_KV_WIDTH // LANES)], axis=1)
    v = z_ref[:, A_WIDTH + A_KV_WIDTH:A_WIDTH + 2 * A_KV_WIDTH]
    kf_ref[BLK:2 * BLK, :] = k
    vf_ref[BLK:2 * BLK, :] = v
    qs = [(_norm_rope_chunk(z_ref[:, c * LANES:(c + 1) * LANES], qg_ref[...], cos, sin)
           * (HEAD_DIM ** -0.5)).astype(BF16) for c in range(A_WIDTH // LANES)]
    r = lax.broadcasted_iota(jnp.int32, (BLK, 2 * BLK), 0)
    kk = lax.broadcasted_iota(jnp.int32, (BLK, 2 * BLK), 1)
    d = kk - r
    mask = (d >= 0) & (d <= WINDOW) & ((kk >= BLK) | (i > 0))
    outs = _attn_core(qs, kf_ref[...], vf_ref[...], mask, sink_ref)
    for c in range(A_WIDTH // LANES):
        o_ref[:, c * LANES:(c + 1) * LANES] = outs[c].astype(o_ref.dtype)

    o0 = A_WIDTH + 2 * A_KV_WIDTH
    u = z_ref[:, o0 + CONV_DIM:o0 + 2 * CONV_DIM] * z_ref[:, o0 + 2 * CONV_DIM:o0 + 3 * CONV_DIM]
    ub_ref[8:8 + BLK, :] = u
    y = ub_ref[6:6 + BLK, :] * cw_ref[0:1, :]
    y = y + ub_ref[7:7 + BLK, :] * cw_ref[1:2, :]
    y = y + ub_ref[8:8 + BLK, :] * cw_ref[2:3, :]
    o_ref[:, A_WIDTH:A_WIDTH + CONV_DIM] = (z_ref[:, o0:o0 + CONV_DIM] * y).astype(o_ref.dtype)
    ub_ref[0:8, :] = ub_ref[BLK:BLK + 8, :]

    @pl.when(i == pl.num_programs(1) - 1)
    def _():
        nk_ref[...] = k
        nv_ref[...] = v
        nc_ref[...] = ub_ref[BLK + 6:BLK + 8, :]


def _even_prompt(z_p, cos, sin, qg, kg, cw, sinks):
    blk = lambda b, i: (b * NBLK + i, 0)
    fixed = lambda b, i: (0, 0)
    per_b = lambda b, i: (b, 0, 0)
    return pl.pallas_call(
        _even_prompt_kernel,
        grid=(BATCH, NBLK),
        in_specs=[pl.BlockSpec(memory_space=pltpu.SMEM),
                  pl.BlockSpec((BLK, EVEN_IN), blk),
                  pl.BlockSpec((BLK, LANES), lambda b, i: (i, 0)),
                  pl.BlockSpec((BLK, LANES), lambda b, i: (i, 0)),
                  pl.BlockSpec((1, LANES), fixed),
                  pl.BlockSpec((1, LANES), fixed),
                  pl.BlockSpec((CONV_W, CONV_DIM), fixed)],
        out_specs=[pl.BlockSpec((BLK, D_MODEL), blk),
                   pl.BlockSpec((None, BLK, A_KV_WIDTH), per_b),
                   pl.BlockSpec((None, BLK, A_KV_WIDTH), per_b),
                   pl.BlockSpec((None, CONV_W - 1, CONV_DIM), per_b)],
        out_shape=[jax.ShapeDtypeStruct((M_PROMPT, D_MODEL), BF16),
                   jax.ShapeDtypeStruct((BATCH, BLK, A_KV_WIDTH), F32),
                   jax.ShapeDtypeStruct((BATCH, BLK, A_KV_WIDTH), F32),
                   jax.ShapeDtypeStruct((BATCH, CONV_W - 1, CONV_DIM), F32)],
        scratch_shapes=[pltpu.VMEM((2 * BLK, A_KV_WIDTH), F32),
                        pltpu.VMEM((2 * BLK, A_KV_WIDTH), F32),
                        pltpu.VMEM((BLK + 8, CONV_DIM), F32)],
        compiler_params=_cparams("arbitrary", "arbitrary"),
        name="even_prompt",
    )(sinks, z_p, cos, sin, qg, kg, cw)


def _even_sample_prep_kernel(z_ref, cos_ref, sin_ref, qg_ref, kg_ref, c0_ref, c1_ref, cw_ref,
                             q_ref, k_ref, u_ref, o_ref):
    cos = cos_ref[...]
    sin = sin_ref[...]
    for c in range(A_WIDTH // LANES):
        q = _norm_rope_chunk(z_ref[:, c * LANES:(c + 1) * LANES], qg_ref[...], cos, sin)
        q_ref[:, c * LANES:(c + 1) * LANES] = (q * (HEAD_DIM ** -0.5)).astype(q_ref.dtype)
    for c in range(A_KV_WIDTH // LANES):
        k_ref[:, c * LANES:(c + 1) * LANES] = _norm_rope_chunk(
            z_ref[:, A_WIDTH + c * LANES:A_WIDTH + (c + 1) * LANES], kg_ref[...], cos, sin)
    o0 = A_WIDTH + 2 * A_KV_WIDTH
    u = z_ref[:, o0 + CONV_DIM:o0 + 2 * CONV_DIM] * z_ref[:, o0 + 2 * CONV_DIM:o0 + 3 * CONV_DIM]
    u_ref[...] = u
    y = c0_ref[...] * cw_ref[0:1, :]
    y = y + c1_ref[...] * cw_ref[1:2, :]
    y = y + u * cw_ref[2:3, :]
    o_ref[...] = (z_ref[:, o0:o0 + CONV_DIM] * y).astype(o_ref.dtype)


def _even_sample_prep(z_s, cos, sin, qg, kg, c0, c1, cw):
    return pl.pallas_call(
        _even_sample_prep_kernel,
        out_shape=[jax.ShapeDtypeStruct((DEC_BATCH, A_WIDTH), BF16),
                   jax.ShapeDtypeStruct((DEC_BATCH, A_KV_WIDTH), F32),
                   jax.ShapeDtypeStruct((DEC_BATCH, CONV_DIM), F32),
                   jax.ShapeDtypeStruct((DEC_BATCH, CONV_DIM), BF16)],
        compiler_params=pltpu.CompilerParams(vmem_limit_bytes=VMEM_LIMIT),
        name="even_sample_prep",
    )(z_s, cos, sin, qg, kg, c0, c1, cw)


SROWS = 8


def _even_sample_attn_kernel(sink_ref, q_ref, kn_ref, vn_ref, kc_ref, vc_ref,
                             o_ref, nk_ref, nv_ref):
    lc = kc_ref.shape[0]
    row = lax.broadcasted_iota(jnp.int32, (lc, A_KV_WIDTH), 0)
    kfull = jnp.concatenate([kc_ref[...], jnp.where(row == 0, kn_ref[...], 0.0)], axis=0)
    vfull = jnp.concatenate([vc_ref[...], jnp.where(row == 0, vn_ref[...], 0.0)], axis=0)
    qs = [jnp.broadcast_to(q_ref[:, c * LANES:(c + 1) * LANES], (SROWS, LANES))
          for c in range(A_WIDTH // LANES)]
    kk = lax.broadcasted_iota(jnp.int32, (SROWS, 2 * lc), 1)
    mask = (kk <= lc) & (lc - kk <= WINDOW)
    outs = _attn_core(qs, kfull, vfull, mask, sink_ref)
    for c in range(A_WIDTH // LANES):
        o_ref[:, c * LANES:(c + 1) * LANES] = outs[c][0:1].astype(o_ref.dtype)
    nk_ref[0:lc - 1, :] = kc_ref[1:lc, :]
    nk_ref[lc - 1:lc, :] = kn_ref[...]
    nv_ref[0:lc - 1, :] = vc_ref[1:lc, :]
    nv_ref[lc - 1:lc, :] = vn_ref[...]


def _even_sample_attn(sinks, q_s, k_s, v_s, k_cache, v_cache, li):
    lc = k_cache.shape[2]
    vec = lambda w: pl.BlockSpec((None, 1, w), lambda b: (b, 0, 0))
    cache_in = pl.BlockSpec((None, None, lc, A_KV_WIDTH), lambda b: (li, b, 0, 0))
    cache = pl.BlockSpec((None, lc, A_KV_WIDTH), lambda b: (b, 0, 0))
    return pl.pallas_call(
        _even_sample_attn_kernel,
        grid=(DEC_BATCH,),
        in_specs=[pl.BlockSpec(memory_space=pltpu.SMEM), vec(A_WIDTH), vec(A_KV_WIDTH), vec(A_KV_WIDTH),
                  cache_in, cache_in],
        out_specs=[vec(A_WIDTH), cache, cache],
        out_shape=[jax.ShapeDtypeStruct((DEC_BATCH, 1, A_WIDTH), BF16),
                   jax.ShapeDtypeStruct((DEC_BATCH, lc, A_KV_WIDTH), F32),
                   jax.ShapeDtypeStruct((DEC_BATCH, lc, A_KV_WIDTH), F32)],
        compiler_params=_cparams("arbitrary"),
        name="even_sample_attn",
    )(sinks, q_s.reshape(DEC_BATCH, 1, A_WIDTH), k_s.reshape(DEC_BATCH, 1, A_KV_WIDTH),
      v_s.reshape(DEC_BATCH, 1, A_KV_WIDTH), k_cache, v_cache)


def _log_decay(zr, wa_ref, ba_ref):
    pre = _dot(zr.astype(BF16), wa_ref[...]) + ba_ref[...]
    return (jnp.minimum(pre, 0.0) - jnp.log1p(jnp.exp(-jnp.abs(pre)))) * (1.0 / GLA_TAU)


def _split_bf16(x):
    hi = x.astype(BF16)
    lo = (x - hi.astype(F32)).astype(BF16)
    return hi, lo


def _row_to_col(row):
    n = row.shape[1]
    r = lax.broadcasted_iota(jnp.int32, (n, n), 0)
    c = lax.broadcasted_iota(jnp.int32, (n, n), 1)
    return jnp.sum(jnp.where(r == c, jnp.broadcast_to(row, (n, n)), 0.0), axis=-1, keepdims=True)


def _head_rmsnorm_gate(o, gain, zg):
    ms = jnp.mean(o * o, axis=-1, keepdims=True)
    return ((o * lax.rsqrt(ms + EPS)) * gain) * _silu(zg)


def _odd_prompt_kernel(zm_ref, zpr_ref, wa_ref, ba_ref, gain_ref, wp_ref, ps_ref,
                       o_ref, s_ref, np_ref, pb_ref):
    i = pl.program_id(1)

    @pl.when(i == 0)
    def _():
        s_ref[...] = jnp.zeros_like(s_ref)
        pb_ref[0:BLK, :] = jnp.zeros((BLK, POOL_DIM), F32)

    gk = _log_decay(zpr_ref[:, POOL_DIM:POOL_RANK_W], wa_ref, ba_ref)
    cr = lax.broadcasted_iota(jnp.int32, (GLA_CHUNK, GLA_CHUNK), 0)
    cc = lax.broadcasted_iota(jnp.int32, (GLA_CHUNK, GLA_CHUNK), 1)
    causal = cr >= cc
    tri = jnp.where(causal, 1.0, 0.0).astype(BF16)
    o_k, o_v, o_g = GLA_K_WIDTH, 2 * GLA_K_WIDTH, 2 * GLA_K_WIDTH + GLA_WIDTH
    for c in range(BLK // GLA_CHUNK):
        rows = slice(c * GLA_CHUNK, (c + 1) * GLA_CHUNK)
        g_hi, g_lo = _split_bf16(gk[rows])
        b = _dot(tri, g_hi) + _dot(tri, g_lo)
        b_last = b[GLA_CHUNK - 1:GLA_CHUNK, :]
        zq = zm_ref[rows, 0:GLA_K_WIDTH]
        zk = zm_ref[rows, o_k:o_k + GLA_K_WIDTH]
        qd = ((zq * (GLA_DK ** -0.5)) * jnp.exp(b)).astype(BF16)
        kd = (zk * jnp.exp(-b)).astype(BF16)
        k2 = (zk * jnp.exp(b_last - b)).astype(BF16)
        for h in range(GLA_HEADS):
            ks = slice(h * GLA_DK, (h + 1) * GLA_DK)
            vs = slice(h * GLA_DV, (h + 1) * GLA_DV)
            v_h = zm_ref[rows, o_v + h * GLA_DV:o_v + (h + 1) * GLA_DV].astype(BF16)
            att = jnp.where(causal, _dot_nt(qd[:, ks], kd[:, ks]), 0.0).astype(BF16)
            s_h = s_ref[h]
            o = _dot(att, v_h) + _dot(qd[:, ks], s_h.astype(BF16))
            delta = _dot_tn(k2[:, ks], v_h)
            s_ref[h] = jnp.exp(_row_to_col(b_last[:, ks])) * s_h + delta
            zg = zm_ref[rows, o_g + h * GLA_DV:o_g + (h + 1) * GLA_DV]
            o_ref[rows, vs] = _head_rmsnorm_gate(o, gain_ref[...], zg).astype(o_ref.dtype)

    x = zpr_ref[:, 0:POOL_DIM]
    pb_ref[BLK:2 * BLK, :] = x
    e_hi, e_lo = _split_bf16(pb_ref[...])
    t = lax.broadcasted_iota(jnp.int32, (BLK, 2 * BLK), 0)
    j = lax.broadcasted_iota(jnp.int32, (BLK, 2 * BLK), 1)
    back = t + BLK - j
    t_glob = i * BLK + lax.broadcasted_iota(jnp.int32, (BLK, 1), 0)
    for g, w in enumerate(POOL_WINDOWS):
        gs = slice(g * POOL_GDIM, (g + 1) * POOL_GDIM)
        band = jnp.where((back >= 0) & (back < w), 1.0, 0.0).astype(BF16)
        s = _dot(band, e_hi[:, gs]) + _dot(band, e_lo[:, gs])
        cnt = jnp.minimum(w, t_glob + 1).astype(F32)
        d = s / cnt - x[:, gs]
        y = _dot(d.astype(BF16), wp_ref[g]) * ps_ref[:, gs]
        o_ref[:, GLA_WIDTH + g * POOL_GDIM:GLA_WIDTH + (g + 1) * POOL_GDIM] = y.astype(o_ref.dtype)
    pb_ref[0:BLK, :] = x

    @pl.when(i == pl.num_programs(1) - 1)
    def _():
        np_ref[...] = pb_ref[2 * BLK - POOL_BUF:2 * BLK, :]


def _odd_prompt(zm_p, zpr_p, wa, ba, gain, wp, ps):
    blk = lambda b, i: (b * NBLK + i, 0)
    fixed2 = lambda b, i: (0, 0)
    return pl.pallas_call(
        _odd_prompt_kernel,
        grid=(BATCH, NBLK),
        in_specs=[pl.BlockSpec((BLK, ODD_MAIN), blk),
                  pl.BlockSpec((BLK, POOL_RANK_W), blk),
                  pl.BlockSpec((LANES, GLA_K_WIDTH), fixed2),
                  pl.BlockSpec((1, GLA_K_WIDTH), fixed2),
                  pl.BlockSpec((1, GLA_DV), fixed2),
                  pl.BlockSpec((4, POOL_GDIM, POOL_GDIM), lambda b, i: (0, 0, 0)),
                  pl.BlockSpec((1, POOL_DIM), fixed2)],
        out_specs=[pl.BlockSpec((BLK, D_MODEL), blk),
                   pl.BlockSpec((None, GLA_HEADS, GLA_DK, GLA_DV), lambda b, i: (b, 0, 0, 0)),
                   pl.BlockSpec((None, POOL_BUF, POOL_DIM), lambda b, i: (b, 0, 0))],
        out_shape=[jax.ShapeDtypeStruct((M_PROMPT, D_MODEL), BF16),
                   jax.ShapeDtypeStruct((BATCH, GLA_HEADS, GLA_DK, GLA_DV), F32),
                   jax.ShapeDtypeStruct((BATCH, POOL_BUF, POOL_DIM), F32)],
        scratch_shapes=[pltpu.VMEM((2 * BLK, POOL_DIM), F32)],
        compiler_params=_cparams("arbitrary", "arbitrary"),
        name="odd_prompt",
    )(zm_p, zpr_p, wa, ba, gain, wp, ps)


def _odd_sample_prep_kernel(zm_ref, zr_ref, wa_ref, ba_ref, dec_ref, q_ref):
    dec_ref[...] = jnp.exp(_log_decay(zr_ref[...], wa_ref, ba_ref))
    q_ref[...] = zm_ref[:, 0:GLA_K_WIDTH] * (GLA_DK ** -0.5)


def _odd_sample_prep(zm_s, zr_s, wa, ba):
    return pl.pallas_call(
        _odd_sample_prep_kernel,
        out_shape=[jax.ShapeDtypeStruct((DEC_BATCH, GLA_K_WIDTH), F32),
                   jax.ShapeDtypeStruct((DEC_BATCH, GLA_K_WIDTH), F32)],
        compiler_params=pltpu.CompilerParams(vmem_limit_bytes=VMEM_LIMIT),
        name="odd_sample_prep",
    )(zm_s, zr_s, wa, ba)


def _odd_sample_state_kernel(dec_ref, q_ref, k_ref, v_ref, s_ref, ns_ref, o_ref):
    for h in range(GLA_HEADS):
        ks = slice(h * GLA_DK, (h + 1) * GLA_DK)
        vs = slice(h * GLA_DV, (h + 1) * GLA_DV)
        s_new = _row_to_col(dec_ref[:, ks]) * s_ref[h] + _row_to_col(k_ref[:, ks]) * v_ref[:, vs]
        ns_ref[h] = s_new
        o_ref[:, vs] = jnp.sum(_row_to_col(q_ref[:, ks]) * s_new, axis=0, keepdims=True)


def _odd_sample_state(dec, q, k, v, state, li):
    vec = lambda w: pl.BlockSpec((None, 1, w), lambda b: (b, 0, 0))
    st_in = pl.BlockSpec((None, None, GLA_HEADS, GLA_DK, GLA_DV), lambda b: (li, b, 0, 0, 0))
    st = pl.BlockSpec((None, GLA_HEADS, GLA_DK, GLA_DV), lambda b: (b, 0, 0, 0))
    r3 = lambda a: a.reshape(DEC_BATCH, 1, a.shape[-1])
    return pl.pallas_call(
        _odd_sample_state_kernel,
        grid=(DEC_BATCH,),
        in_specs=[vec(GLA_K_WIDTH), vec(GLA_K_WIDTH), vec(GLA_K_WIDTH), vec(GLA_WIDTH), st_in],
        out_specs=[st, vec(GLA_WIDTH)],
        out_shape=[jax.ShapeDtypeStruct(state.shape[1:], F32),
                   jax.ShapeDtypeStruct((DEC_BATCH, 1, GLA_WIDTH), F32)],
        compiler_params=_cparams("arbitrary"),
        name="odd_sample_state",
    )(r3(dec), r3(q), r3(k), r3(v), state)


def _odd_sample_post_kernel(o_in_ref, zm_ref, zp_ref, hist_ref, gain_ref, wp_ref, ps_ref, o_ref):
    o_g = 2 * GLA_K_WIDTH + GLA_WIDTH
    for h in range(GLA_HEADS):
        vs = slice(h * GLA_DV, (h + 1) * GLA_DV)
        zg = zm_ref[:, o_g + h * GLA_DV:o_g + (h + 1) * GLA_DV]
        o_ref[:, vs] = _head_rmsnorm_gate(o_in_ref[:, vs], gain_ref[...], zg).astype(o_ref.dtype)
    n_prev = hist_ref.shape[0]
    for g, w in enumerate(POOL_WINDOWS):
        gs = slice(g * POOL_GDIM, (g + 1) * POOL_GDIM)
        x = zp_ref[:, gs]
        s = hist_ref[n_prev - (w - 1)][:, gs]
        for jj in range(n_prev - (w - 1) + 1, n_prev):
            s = s + hist_ref[jj][:, gs]
        s = s + x
        d = s / float(min(w, n_prev + 1)) - x
        y = _dot(d.astype(BF16), wp_ref[g]) * ps_ref[:, gs]
        o_ref[:, GLA_WIDTH + g * POOL_GDIM:GLA_WIDTH + (g + 1) * POOL_GDIM] = y.astype(o_ref.dtype)


def _odd_sample_post(o_raw, zm_s, zp_s, hist_t, gain, wp, ps):
    return pl.pallas_call(
        _odd_sample_post_kernel,
        out_shape=jax.ShapeDtypeStruct((DEC_BATCH, D_MODEL), BF16),
        compiler_params=pltpu.CompilerParams(vmem_limit_bytes=VMEM_LIMIT),
        name="odd_sample_post",
    )(o_raw, zm_s, zp_s, hist_t, gain, wp, ps)


def _rope_tables(pos):
    half = HEAD_DIM // 2
    inv = jnp.power(ROPE_THETA, -jnp.arange(half, dtype=F32) / half)
    ang = pos.astype(F32)[:, None] * inv[None, :]
    c, s = jnp.cos(ang), jnp.sin(ang)
    return jnp.tile(c, (1, 4)), jnp.tile(jnp.concatenate([-s, s], axis=1), (1, 2))


def kernel(x_prompt, x_sample, cache_swa_k, cache_swa_v, state_conv, state_gla, state_pool, norm_mix, norm_ffn, w_in_even, w_out_even, q_norm, k_norm, attn_sinks, conv_w, w_in_odd, w_out_odd, w_alpha_up, b_alpha, gla_out_norm, w_pool, pool_scale, w_gate, w_up, w_down):
    lc = cache_swa_k.shape[2]
    x_p = x_prompt.reshape(M_PROMPT, D_MODEL)
    x_s = x_sample.reshape(DEC_BATCH, D_MODEL)
    cos_p, sin_p = _rope_tables(jnp.arange(SEQ))
    cos_s, sin_s = _rope_tables(PAST_LEN + jnp.arange(1))
    pk, pv, pc, pg, pp, sk, sv, sc, sg, sp = ([] for _ in range(10))

    h_p, h_s = _rmsnorm(x_p, x_s, norm_mix[0])
    for layer in range(DEPTH):
        li = layer // 2
        if layer % 2 == 0:
            z_p, z_s = _dense(h_p, h_s, [(w_in_even, li)], n_cols=EVEN_IN, tm=1024, tn=1536, name="in_even")
            qg = jnp.tile(q_norm[li], 2).reshape(1, LANES)
            kg = jnp.tile(k_norm[li], 2).reshape(1, LANES)
            m_p, nk, nv, nc = _even_prompt(z_p, cos_p, sin_p, qg, kg, conv_w[li], attn_sinks[li])
            pk.append(nk.reshape(BATCH, lc, A_KV_HEADS, HEAD_DIM))
            pv.append(nv.reshape(BATCH, lc, A_KV_HEADS, HEAD_DIM))
            pc.append(nc)
            q_s, k_s, u_s, conv_s = _even_sample_prep(
                z_s, cos_s, sin_s, qg, kg, state_conv[li, :, 0], state_conv[li, :, 1], conv_w[li])
            v_s = z_s[:, A_WIDTH + A_KV_WIDTH:A_WIDTH + 2 * A_KV_WIDTH]
            attn_s, nk_s, nv_s = _even_sample_attn(
                attn_sinks[li], q_s, k_s, v_s,
                cache_swa_k.reshape(N_EVEN, DEC_BATCH, lc, A_KV_WIDTH),
                cache_swa_v.reshape(N_EVEN, DEC_BATCH, lc, A_KV_WIDTH), li)
            sk.append(nk_s.reshape(DEC_BATCH, lc, A_KV_HEADS, HEAD_DIM))
            sv.append(nv_s.reshape(DEC_BATCH, lc, A_KV_HEADS, HEAD_DIM))
            sc.append(jnp.stack([state_conv[li, :, 1], u_s], axis=1))
            m_s = jnp.concatenate([attn_s.reshape(DEC_BATCH, A_WIDTH), conv_s], axis=1)
            w_out = w_out_even[li]
        else:
            w = w_in_odd[li]
            o_r = ODD_MAIN
            w_pr = jnp.concatenate([w[:, o_r + GLA_RANK:], w[:, o_r:o_r + GLA_RANK],
                                    jnp.zeros((D_MODEL, LANES - GLA_RANK), F32)], axis=1)
            zm_p, zm_s = _dense(h_p, h_s, [(w_in_odd, li)], n_cols=ODD_MAIN, tm=1024, tn=1536,
                                name="in_odd_main")
            zpr_p, zpr_s = _dense(h_p, h_s, [(w_pr, None)], n_cols=POOL_RANK_W, tm=1024, tn=POOL_RANK_W,
                                  name="in_odd_pool_rank")
            zp_s, zr_s = zpr_s[:, :POOL_DIM], zpr_s[:, POOL_DIM:]
            wa = jnp.pad(w_alpha_up[li], ((0, LANES - GLA_RANK), (0, 0))).astype(BF16)
            ba = b_alpha[li].reshape(1, GLA_K_WIDTH)
            gain = gla_out_norm[li].reshape(1, GLA_DV)
            wp = w_pool[li].astype(BF16)
            ps = pool_scale[li].reshape(1, POOL_DIM)
            m_p, ng, npool = _odd_prompt(zm_p, zpr_p, wa, ba, gain, wp, ps)
            pg.append(ng)
            pp.append(npool)
            dec_s, q_s = _odd_sample_prep(zm_s, zr_s, wa, ba)
            ng_s, o_raw = _odd_sample_state(
                dec_s, q_s, zm_s[:, GLA_K_WIDTH:2 * GLA_K_WIDTH],
                zm_s[:, 2 * GLA_K_WIDTH:2 * GLA_K_WIDTH + GLA_WIDTH], state_gla, li)
            sg.append(ng_s)
            m_s = _odd_sample_post(o_raw.reshape(DEC_BATCH, GLA_WIDTH), zm_s, zp_s,
                                   jnp.swapaxes(state_pool[li], 0, 1), gain, wp, ps)
            sp.append(jnp.concatenate([state_pool[li, :, 1:], zp_s[:, None, :]], axis=1))
            w_out = w_out_odd[li]
        x_p, x_s, h_p, h_s = _proj_res(m_p, m_s, w_out.astype(BF16), (x_p, x_s), norm_ffn[layer],
                                       tm=512, name="out_proj")
        a_p, a_s = _dense(h_p, h_s, [(w_gate, layer), (w_up, layer)], n_cols=D_FF, tm=1024, tn=512,
                          mode="swiglu", out_dtype=BF16, name="ffn_up")
        if layer + 1 < DEPTH:
            x_p, x_s, h_p, h_s = _proj_res(a_p, a_s, w_down[layer].astype(BF16), (x_p, x_s),
                                           norm_mix[layer + 1], tm=256, name="ffn_down")
        else:
            x_p, x_s = _proj_res(a_p, a_s, w_down[layer].astype(BF16), (x_p, x_s), None,
                                 tm=256, name="ffn_down_last")

    st = lambda parts: jnp.stack(parts)
    return (x_p.reshape(BATCH, SEQ, D_MODEL), x_s.reshape(DEC_BATCH, 1, D_MODEL),
            st(pk), st(pv), st(pc), st(pg), st(pp), st(sk), st(sv), st(sc), st(sg), st(sp))
```

```python
import functools

import jax
import jax.numpy as jnp
from jax import lax
from jax.experimental import pallas as pl
from jax.experimental.pallas import tpu as pltpu

F32 = jnp.float32
BF16 = jnp.bfloat16

D_MODEL = 2048
BATCH = 4
SEQ = 2048
DEPTH = 4
DEC_BATCH = 32
PAST_LEN = 16384
N_EVEN = 2
N_ODD = 2
EPS = 1e-6
NEG_INF = -1e30
A_HEADS = 16
A_KV_HEADS = 4
HEAD_DIM = 64
A_WIDTH = A_HEADS * HEAD_DIM
A_KV_WIDTH = A_KV_HEADS * HEAD_DIM
WINDOW = 128
ROPE_THETA = 10000.0
CONV_DIM = D_MODEL // 2
CONV_W = 3
GLA_HEADS = 4
GLA_WIDTH = D_MODEL // 2
GLA_DV = GLA_WIDTH // GLA_HEADS
GLA_DK = GLA_DV // 2
GLA_K_WIDTH = GLA_HEADS * GLA_DK
GLA_RANK = 16
GLA_TAU = 16.0
GLA_CHUNK = 64
POOL_DIM = D_MODEL // 2
POOL_WINDOWS = (2, 4, 8, 16)
POOL_GDIM = POOL_DIM // 4
POOL_BUF = 15
D_FF = 5632
EVEN_IN = A_WIDTH + 2 * A_KV_WIDTH + 3 * CONV_DIM
ODD_MAIN = 2 * GLA_K_WIDTH + 2 * GLA_WIDTH

M_PROMPT = BATCH * SEQ
LANES = 128
BLK = 128
NBLK = SEQ // BLK
ODD_IN_PAD = ODD_MAIN + POOL_DIM + LANES
VMEM_LIMIT = 58 * 1024 * 1024


def _cparams(*sem):
    return pltpu.CompilerParams(dimension_semantics=sem, vmem_limit_bytes=VMEM_LIMIT)


def _dot(a, b):
    return jnp.dot(a, b, preferred_element_type=F32)


def _dot_nt(a, b):
    return lax.dot_general(a, b, (((1,), (1,)), ((), ())), preferred_element_type=F32)


def _dot_tn(a, b):
    return lax.dot_general(a, b, (((0,), (0,)), ((), ())), preferred_element_type=F32)


def _silu(x):
    return x * (1.0 / (1.0 + jnp.exp(-x)))


def _rmsnorm_rows(x, gain):
    ms = jnp.mean(x * x, axis=-1, keepdims=True)
    return (x * lax.rsqrt(ms + EPS)) * gain


def _rmsnorm_kernel(xp_ref, xs_ref, g_ref, op_ref, os_ref, *, n_i):
    op_ref[...] = _rmsnorm_rows(xp_ref[...], g_ref[...]).astype(op_ref.dtype)

    @pl.when(pl.program_id(0) == n_i - 1)
    def _():
        os_ref[...] = _rmsnorm_rows(xs_ref[...], g_ref[...]).astype(os_ref.dtype)


def _rmsnorm(x_p, x_s, gain, *, tm=1024):
    n_i = M_PROMPT // tm
    row = lambda i: (i, 0)
    fixed = lambda i: (0, 0)
    return pl.pallas_call(
        functools.partial(_rmsnorm_kernel, n_i=n_i),
        grid=(n_i,),
        in_specs=[pl.BlockSpec((tm, D_MODEL), row),
                  pl.BlockSpec((DEC_BATCH, D_MODEL), fixed),
                  pl.BlockSpec((1, D_MODEL), fixed)],
        out_specs=[pl.BlockSpec((tm, D_MODEL), row),
                   pl.BlockSpec((DEC_BATCH, D_MODEL), fixed)],
        out_shape=[jax.ShapeDtypeStruct((M_PROMPT, D_MODEL), BF16),
                   jax.ShapeDtypeStruct((DEC_BATCH, D_MODEL), BF16)],
        compiler_params=_cparams("arbitrary"),
        name="rmsnorm",
    )(x_p, x_s, gain.reshape(1, D_MODEL))


def _dense_kernel(*refs, n_w, mode, n_i):
    a_p, a_s = refs[0], refs[1]
    w = refs[2:2 + n_w]
    o_p, o_s = refs[2 + n_w], refs[3 + n_w]
    wbf = refs[4 + n_w:4 + 2 * n_w]
    i = pl.program_id(1)

    @pl.when(i == 0)
    def _():
        for k in range(n_w):
            wbf[k][...] = w[k][...].astype(BF16)

    def run(a_ref, o_ref):
        a = a_ref[...]
        if mode == "swiglu":
            y = _silu(_dot(a, wbf[0][...])) * _dot(a, wbf[1][...])
        else:
            y = _dot(a, wbf[0][...])
        o_ref[...] = y.astype(o_ref.dtype)

    run(a_p, o_p)

    @pl.when(i == n_i - 1)
    def _():
        run(a_s, o_s)


def _dense(a_p, a_s, weights, *, n_cols, tm, tn, mode="plain", out_dtype=F32, name):
    k_dim = a_p.shape[1]
    n_i = M_PROMPT // tm
    row = lambda j, i: (i, 0)
    fixed = lambda j, i: (0, 0)
    tile = lambda j, i: (i, j)
    panel = lambda j, i: (0, j)
    in_specs = [pl.BlockSpec((tm, k_dim), row), pl.BlockSpec((DEC_BATCH, k_dim), fixed)]
    args = [a_p, a_s]
    for arr, layer in weights:
        if layer is None:
            in_specs.append(pl.BlockSpec((k_dim, tn), panel))
        else:
            in_specs.append(pl.BlockSpec((None, k_dim, tn), lambda j, i, layer=layer: (layer, 0, j)))
        args.append(arr)
    n_w = len(weights)
    return pl.pallas_call(
        functools.partial(_dense_kernel, n_w=n_w, mode=mode, n_i=n_i),
        grid=(n_cols // tn, n_i),
        in_specs=in_specs,
        out_specs=[pl.BlockSpec((tm, tn), tile), pl.BlockSpec((DEC_BATCH, tn), panel)],
        out_shape=[jax.ShapeDtypeStruct((M_PROMPT, n_cols), out_dtype),
                   jax.ShapeDtypeStruct((DEC_BATCH, n_cols), out_dtype)],
        scratch_shapes=[pltpu.VMEM((k_dim, tn), BF16) for _ in range(n_w)],
        compiler_params=_cparams("arbitrary", "arbitrary"),
        name=name,
    )(*args)


def _proj_res_kernel(*refs, n_i, with_norm):
    a_p, a_s, w_ref, r_p, r_s = refs[:5]
    if with_norm:
        g_ref, x_p, x_s, h_p, h_s = refs[5:]
    else:
        x_p, x_s = refs[5:]
        g_ref = h_p = h_s = None

    def run(a_ref, r_ref, x_ref, h_ref):
        x = r_ref[...] + _dot(a_ref[...], w_ref[...])
        x_ref[...] = x
        if with_norm:
            h_ref[...] = _rmsnorm_rows(x, g_ref[...]).astype(h_ref.dtype)

    run(a_p, r_p, x_p, h_p)

    @pl.when(pl.program_id(0) == n_i - 1)
    def _():
        run(a_s, r_s, x_s, h_s)


def _proj_res(a_p, a_s, weight, res, gain, *, tm, name):
    w_bf, layer = weight
    k_dim = a_p.shape[1]
    n_i = M_PROMPT // tm
    with_norm = gain is not None
    row = lambda i: (i, 0)
    fixed = lambda i: (0, 0)
    rows_p = pl.BlockSpec((tm, D_MODEL), row)
    rows_s = pl.BlockSpec((DEC_BATCH, D_MODEL), fixed)
    in_specs = [pl.BlockSpec((tm, k_dim), row), pl.BlockSpec((DEC_BATCH, k_dim), fixed),
                pl.BlockSpec((None, k_dim, D_MODEL), lambda i: (layer, 0, 0), pipeline_mode=pl.Buffered(1)),
                rows_p, rows_s]
    args = [a_p, a_s, w_bf, res[0], res[1]]
    out_specs = [rows_p, rows_s]
    out_shape = [jax.ShapeDtypeStruct((M_PROMPT, D_MODEL), F32), jax.ShapeDtypeStruct((DEC_BATCH, D_MODEL), F32)]
    if with_norm:
        in_specs.append(pl.BlockSpec((1, D_MODEL), fixed))
        args.append(gain.reshape(1, D_MODEL))
        out_specs += [rows_p, rows_s]
        out_shape += [jax.ShapeDtypeStruct((M_PROMPT, D_MODEL), BF16),
                      jax.ShapeDtypeStruct((DEC_BATCH, D_MODEL), BF16)]
    return pl.pallas_call(
        functools.partial(_proj_res_kernel, n_i=n_i, with_norm=with_norm),
        grid=(n_i,),
        in_specs=in_specs,
        out_specs=out_specs,
        out_shape=out_shape,
        compiler_params=_cparams("arbitrary"),
        name=name,
    )(*args)


def _split_cat(x):
    hi = x.astype(BF16)
    lo = (x - hi.astype(F32)).astype(BF16)
    return jnp.concatenate([hi, lo], axis=1)


def _rope_mats():
    j = jnp.arange(2 * LANES)[:, None] % LANES
    l = jnp.arange(LANES)[None, :]
    head_sum = (j // HEAD_DIM == l // HEAD_DIM).astype(BF16)
    src = jnp.where(l % HEAD_DIM < HEAD_DIM // 2, l + HEAD_DIM // 2, l - HEAD_DIM // 2)
    half_swap = (j == src).astype(BF16)
    return head_sum, half_swap


def _norm_rope_chunk(xc, gain, cos, sin, hs_ref, sw_ref):
    ms = _dot(_split_cat(xc * xc), hs_ref[...]) * (1.0 / HEAD_DIM)
    y = (xc * lax.rsqrt(ms + EPS)) * gain
    swapped = _dot(_split_cat(y), sw_ref[...])
    return y * cos + swapped * sin


def _spread_heads(xc, own_lo):
    lane = lax.broadcasted_iota(jnp.int32, xc.shape, 1)
    keep = (lane < HEAD_DIM) if own_lo else (lane >= HEAD_DIM)
    nat = jnp.where(keep, xc, 0.0)
    rol = pltpu.roll(nat, HEAD_DIM, axis=1)
    parts = (nat, rol) if own_lo else (rol, nat)
    return jnp.concatenate(parts, axis=0).astype(BF16)


def _attn_core(qs, kfull, vfull, mask_t, sink_ref):
    rows = qs[0].shape[0]
    nkeys = kfull.shape[0]
    outs = [None] * 8
    for kh in range(A_KV_HEADS):
        c0 = (kh // 2) * LANES
        kk = _spread_heads(kfull[:, c0:c0 + LANES], kh % 2 == 0)
        vv = _spread_heads(vfull[:, c0:c0 + LANES], kh % 2 == 0)
        lhs = jnp.concatenate([qs[2 * kh], qs[2 * kh + 1]], axis=0)
        s = _dot_nt(kk, lhs)
        prow = []
        for half in range(2):
            pcol = []
            for cc in range(2):
                sb = s[half * nkeys:(half + 1) * nkeys, cc * rows:(cc + 1) * rows]
                sb = jnp.where(mask_t, sb, NEG_INF)
                sink = sink_ref[kh * 4 + 2 * cc + half]
                m = jnp.maximum(jnp.max(sb, axis=0, keepdims=True), sink)
                e = jnp.exp(sb - m)
                den = jnp.sum(e, axis=0, keepdims=True) + jnp.exp(sink - m)
                pcol.append((e / den).astype(BF16))
            prow.append(jnp.concatenate(pcol, axis=1))
        p = jnp.concatenate(prow, axis=0)
        o = _dot_tn(p, vv)
        outs[2 * kh] = o[0:rows]
        outs[2 * kh + 1] = o[rows:2 * rows]
    return outs


def _even_prompt_kernel(sink_ref, z_ref, cos_ref, sin_ref, qg_ref, kg_ref, cw_ref, hs_ref, sw_ref,
                        o_ref, nk_ref, nv_ref, nc_ref, kf_ref, vf_ref, ub_ref):
    i = pl.program_id(1)

    @pl.when(i == 0)
    def _():
        kf_ref[...] = jnp.zeros_like(kf_ref)
        vf_ref[...] = jnp.zeros_like(vf_ref)
        ub_ref[0:8, :] = jnp.zeros((8, CONV_DIM), F32)

    kf_ref[0:BLK, :] = kf_ref[BLK:2 * BLK, :]
    vf_ref[0:BLK, :] = vf_ref[BLK:2 * BLK, :]
    cos = cos_ref[...]
    sin = sin_ref[...]
    k = jnp.concatenate(
        [_norm_rope_chunk(z_ref[:, A_WIDTH + c * LANES:A_WIDTH + (c + 1) * LANES], kg_ref[...], cos, sin,
                          hs_ref, sw_ref)
         for c in range(A_KV_WIDTH // LANES)], axis=1)
    v = z_ref[:, A_WIDTH + A_KV_WIDTH:A_WIDTH + 2 * A_KV_WIDTH]
    kf_ref[BLK:2 * BLK, :] = k
    vf_ref[BLK:2 * BLK, :] = v
    qs = [(_norm_rope_chunk(z_ref[:, c * LANES:(c + 1) * LANES], qg_ref[...], cos, sin, hs_ref, sw_ref)
           * (HEAD_DIM ** -0.5)).astype(BF16) for c in range(A_WIDTH // LANES)]
    kk = lax.broadcasted_iota(jnp.int32, (2 * BLK, BLK), 0)
    r = lax.broadcasted_iota(jnp.int32, (2 * BLK, BLK), 1)
    d = kk - r
    mask_t = (d >= 0) & (d <= WINDOW) & ((kk >= BLK) | (i > 0))
    outs = _attn_core(qs, kf_ref[...], vf_ref[...], mask_t, sink_ref)
    for c in range(A_WIDTH // LANES):
        o_ref[:, c * LANES:(c + 1) * LANES] = outs[c].astype(o_ref.dtype)

    o0 = A_WIDTH + 2 * A_KV_WIDTH
    u = z_ref[:, o0 + CONV_DIM:o0 + 2 * CONV_DIM] * z_ref[:, o0 + 2 * CONV_DIM:o0 + 3 * CONV_DIM]
    ub_ref[8:8 + BLK, :] = u
    y = ub_ref[6:6 + BLK, :] * cw_ref[0:1, :]
    y = y + ub_ref[7:7 + BLK, :] * cw_ref[1:2, :]
    y = y + ub_ref[8:8 + BLK, :] * cw_ref[2:3, :]
    o_ref[:, A_WIDTH:A_WIDTH + CONV_DIM] = (z_ref[:, o0:o0 + CONV_DIM] * y).astype(o_ref.dtype)
    ub_ref[0:8, :] = ub_ref[BLK:BLK + 8, :]

    @pl.when(i == pl.num_programs(1) - 1)
    def _():
        nk_ref[...] = k
        nv_ref[...] = v
        nc_ref[...] = ub_ref[BLK + 6:BLK + 8, :]


def _even_prompt(z_p, cos, sin, qg, kg, cw, sinks, mats):
    blk = lambda b, i: (b * NBLK + i, 0)
    fixed = lambda b, i: (0, 0)
    per_b = lambda b, i: (b, 0, 0)
    return pl.pallas_call(
        _even_prompt_kernel,
        grid=(BATCH, NBLK),
        in_specs=[pl.BlockSpec(memory_space=pltpu.SMEM),
                  pl.BlockSpec((BLK, EVEN_IN), blk),
                  pl.BlockSpec((BLK, LANES), lambda b, i: (i, 0)),
                  pl.BlockSpec((BLK, LANES), lambda b, i: (i, 0)),
                  pl.BlockSpec((1, LANES), fixed),
                  pl.BlockSpec((1, LANES), fixed),
                  pl.BlockSpec((CONV_W, CONV_DIM), fixed),
                  pl.BlockSpec((2 * LANES, LANES), fixed),
                  pl.BlockSpec((2 * LANES, LANES), fixed)],
        out_specs=[pl.BlockSpec((BLK, D_MODEL), blk),
                   pl.BlockSpec((None, BLK, A_KV_WIDTH), per_b),
                   pl.BlockSpec((None, BLK, A_KV_WIDTH), per_b),
                   pl.BlockSpec((None, CONV_W - 1, CONV_DIM), per_b)],
        out_shape=[jax.ShapeDtypeStruct((M_PROMPT, D_MODEL), BF16),
                   jax.ShapeDtypeStruct((BATCH, BLK, A_KV_WIDTH), F32),
                   jax.ShapeDtypeStruct((BATCH, BLK, A_KV_WIDTH), F32),
                   jax.ShapeDtypeStruct((BATCH, CONV_W - 1, CONV_DIM), F32)],
        scratch_shapes=[pltpu.VMEM((2 * BLK, A_KV_WIDTH), F32),
                        pltpu.VMEM((2 * BLK, A_KV_WIDTH), F32),
                        pltpu.VMEM((BLK + 8, CONV_DIM), F32)],
        compiler_params=_cparams("arbitrary", "arbitrary"),
        name="even_prompt",
    )(sinks, z_p, cos, sin, qg, kg, cw, *mats)


def _even_sample_prep_kernel(z_ref, cos_ref, sin_ref, qg_ref, kg_ref, c0_ref, c1_ref, cw_ref, hs_ref, sw_ref,
                             q_ref, k_ref, u_ref, o_ref):
    cos = cos_ref[...]
    sin = sin_ref[...]
    for c in range(A_WIDTH // LANES):
        q = _norm_rope_chunk(z_ref[:, c * LANES:(c + 1) * LANES], qg_ref[...], cos, sin, hs_ref, sw_ref)
        q_ref[:, c * LANES:(c + 1) * LANES] = (q * (HEAD_DIM ** -0.5)).astype(q_ref.dtype)
    for c in range(A_KV_WIDTH // LANES):
        k_ref[:, c * LANES:(c + 1) * LANES] = _norm_rope_chunk(
            z_ref[:, A_WIDTH + c * LANES:A_WIDTH + (c + 1) * LANES], kg_ref[...], cos, sin, hs_ref, sw_ref)
    o0 = A_WIDTH + 2 * A_KV_WIDTH
    u = z_ref[:, o0 + CONV_DIM:o0 + 2 * CONV_DIM] * z_ref[:, o0 + 2 * CONV_DIM:o0 + 3 * CONV_DIM]
    u_ref[...] = u
    y = c0_ref[...] * cw_ref[0:1, :]
    y = y + c1_ref[...] * cw_ref[1:2, :]
    y = y + u * cw_ref[2:3, :]
    o_ref[...] = (z_ref[:, o0:o0 + CONV_DIM] * y).astype(o_ref.dtype)


def _even_sample_prep(z_s, cos, sin, qg, kg, c0, c1, cw, mats):
    return pl.pallas_call(
        _even_sample_prep_kernel,
        out_shape=[jax.ShapeDtypeStruct((DEC_BATCH, A_WIDTH), BF16),
                   jax.ShapeDtypeStruct((DEC_BATCH, A_KV_WIDTH), F32),
                   jax.ShapeDtypeStruct((DEC_BATCH, CONV_DIM), F32),
                   jax.ShapeDtypeStruct((DEC_BATCH, CONV_DIM), BF16)],
        compiler_params=pltpu.CompilerParams(vmem_limit_bytes=VMEM_LIMIT),
        name="even_sample_prep",
    )(z_s, cos, sin, qg, kg, c0, c1, cw, *mats)


SROWS = LANES


def _even_sample_attn_kernel(*refs):
    sink_ref, q_ref, kn_ref, vn_ref, kc_ref, vc_ref = refs[:6]
    o_ref, nk_ref, nv_ref = refs[-3:]
    lc = kc_ref.shape[0]
    row = lax.broadcasted_iota(jnp.int32, (lc, A_KV_WIDTH), 0)
    kfull = jnp.concatenate([kc_ref[...], jnp.where(row == 0, kn_ref[...], 0.0)], axis=0)
    vfull = jnp.concatenate([vc_ref[...], jnp.where(row == 0, vn_ref[...], 0.0)], axis=0)
    qs = [jnp.broadcast_to(q_ref[:, c * LANES:(c + 1) * LANES], (SROWS, LANES))
          for c in range(A_WIDTH // LANES)]
    kk = lax.broadcasted_iota(jnp.int32, (2 * lc, SROWS), 0)
    mask_t = (kk <= lc) & (lc - kk <= WINDOW)
    outs = _attn_core(qs, kfull, vfull, mask_t, sink_ref)
    for c in range(A_WIDTH // LANES):
        o_ref[:, c * LANES:(c + 1) * LANES] = outs[c][0:1].astype(o_ref.dtype)
    nk_ref[0:lc - 1, :] = kc_ref[1:lc, :]
    nk_ref[lc - 1:lc, :] = kn_ref[...]
    nv_ref[0:lc - 1, :] = vc_ref[1:lc, :]
    nv_ref[lc - 1:lc, :] = vn_ref[...]


def _even_sample_attn(sinks, q_s, k_s, v_s, k_cache, v_cache, li, prev):
    lc = k_cache.shape[2]
    vec = lambda w: pl.BlockSpec((None, 1, w), lambda b: (b, 0, 0))
    cache = pl.BlockSpec((None, None, lc, A_KV_WIDTH), lambda b: (li, b, 0, 0))
    in_specs = [pl.BlockSpec(memory_space=pltpu.SMEM), vec(A_WIDTH), vec(A_KV_WIDTH), vec(A_KV_WIDTH),
                cache, cache]
    args = [sinks, q_s.reshape(DEC_BATCH, 1, A_WIDTH), k_s.reshape(DEC_BATCH, 1, A_KV_WIDTH),
            v_s.reshape(DEC_BATCH, 1, A_KV_WIDTH), k_cache, v_cache]
    aliases = {}
    if prev is not None:
        aliases = {len(args): 1, len(args) + 1: 2}
        in_specs += [pl.BlockSpec(memory_space=pl.ANY)] * 2
        args += list(prev)
    return pl.pallas_call(
        _even_sample_attn_kernel,
        grid=(DEC_BATCH,),
        in_specs=in_specs,
        out_specs=[vec(A_WIDTH), cache, cache],
        out_shape=[jax.ShapeDtypeStruct((DEC_BATCH, 1, A_WIDTH), BF16),
                   jax.ShapeDtypeStruct(k_cache.shape, F32),
                   jax.ShapeDtypeStruct(v_cache.shape, F32)],
        input_output_aliases=aliases,
        compiler_params=_cparams("arbitrary"),
        name="even_sample_attn",
    )(*args)


def _log_decay(zr, wa_ref, ba_ref):
    pre = _dot(zr.astype(BF16), wa_ref[...]) + ba_ref[...]
    return (jnp.minimum(pre, 0.0) - jnp.log1p(jnp.exp(-jnp.abs(pre)))) * (1.0 / GLA_TAU)


def _split_bf16(x):
    hi = x.astype(BF16)
    lo = (x - hi.astype(F32)).astype(BF16)
    return hi, lo


def _row_to_col(row):
    n = row.shape[1]
    r = lax.broadcasted_iota(jnp.int32, (n, n), 0)
    c = lax.broadcasted_iota(jnp.int32, (n, n), 1)
    return jnp.sum(jnp.where(r == c, jnp.broadcast_to(row, (n, n)), 0.0), axis=-1, keepdims=True)


def _head_rmsnorm_gate(o, gain, zg):
    ms = jnp.mean(o * o, axis=-1, keepdims=True)
    return ((o * lax.rsqrt(ms + EPS)) * gain) * _silu(zg)


def _odd_prompt_kernel(zm_ref, wa_ref, ba_ref, gain_ref, wp_ref, ps_ref,
                       o_ref, s_ref, np_ref, pb_ref):
    i = pl.program_id(1)

    @pl.when(i == 0)
    def _():
        s_ref[...] = jnp.zeros_like(s_ref)
        pb_ref[0:BLK, :] = jnp.zeros((BLK, POOL_DIM), F32)

    gk = _log_decay(zm_ref[:, ODD_MAIN + POOL_DIM:ODD_IN_PAD], wa_ref, ba_ref)
    cr = lax.broadcasted_iota(jnp.int32, (GLA_CHUNK, GLA_CHUNK), 0)
    cc = lax.broadcasted_iota(jnp.int32, (GLA_CHUNK, GLA_CHUNK), 1)
    causal = cr >= cc
    tri = jnp.where(causal, 1.0, 0.0).astype(BF16)
    o_k, o_v, o_g = GLA_K_WIDTH, 2 * GLA_K_WIDTH, 2 * GLA_K_WIDTH + GLA_WIDTH
    for c in range(BLK // GLA_CHUNK):
        rows = slice(c * GLA_CHUNK, (c + 1) * GLA_CHUNK)
        g_hi, g_lo = _split_bf16(gk[rows])
        b = _dot(tri, g_hi) + _dot(tri, g_lo)
        b_last = b[GLA_CHUNK - 1:GLA_CHUNK, :]
        zq = zm_ref[rows, 0:GLA_K_WIDTH]
        zk = zm_ref[rows, o_k:o_k + GLA_K_WIDTH]
        qd = ((zq * (GLA_DK ** -0.5)) * jnp.exp(b)).astype(BF16)
        kd = (zk * jnp.exp(-b)).astype(BF16)
        k2 = (zk * jnp.exp(b_last - b)).astype(BF16)
        for h in range(GLA_HEADS):
            ks = slice(h * GLA_DK, (h + 1) * GLA_DK)
            vs = slice(h * GLA_DV, (h + 1) * GLA_DV)
            v_h = zm_ref[rows, o_v + h * GLA_DV:o_v + (h + 1) * GLA_DV].astype(BF16)
            att = jnp.where(causal, _dot_nt(qd[:, ks], kd[:, ks]), 0.0).astype(BF16)
            s_h = s_ref[h]
            o = _dot(att, v_h) + _dot(qd[:, ks], s_h.astype(BF16))
            delta = _dot_tn(k2[:, ks], v_h)
            s_ref[h] = jnp.exp(_row_to_col(b_last[:, ks])) * s_h + delta
            zg = zm_ref[rows, o_g + h * GLA_DV:o_g + (h + 1) * GLA_DV]
            o_ref[rows, vs] = _head_rmsnorm_gate(o, gain_ref[...], zg).astype(o_ref.dtype)

    x = zm_ref[:, ODD_MAIN:ODD_MAIN + POOL_DIM]
    pb_ref[BLK:2 * BLK, :] = x
    e_hi, e_lo = _split_bf16(pb_ref[...])
    t = lax.broadcasted_iota(jnp.int32, (BLK, 2 * BLK), 0)
    j = lax.broadcasted_iota(jnp.int32, (BLK, 2 * BLK), 1)
    back = t + BLK - j
    t_glob = i * BLK + lax.broadcasted_iota(jnp.int32, (BLK, 1), 0)
    for g, w in enumerate(POOL_WINDOWS):
        gs = slice(g * POOL_GDIM, (g + 1) * POOL_GDIM)
        band = jnp.where((back >= 0) & (back < w), 1.0, 0.0).astype(BF16)
        s = _dot(band, e_hi[:, gs]) + _dot(band, e_lo[:, gs])
        cnt = jnp.minimum(w, t_glob + 1).astype(F32)
        d = s / cnt - x[:, gs]
        y = _dot(d.astype(BF16), wp_ref[g]) * ps_ref[:, gs]
        o_ref[:, GLA_WIDTH + g * POOL_GDIM:GLA_WIDTH + (g + 1) * POOL_GDIM] = y.astype(o_ref.dtype)
    pb_ref[0:BLK, :] = x

    @pl.when(i == pl.num_programs(1) - 1)
    def _():
        np_ref[...] = pb_ref[2 * BLK - POOL_BUF:2 * BLK, :]


def _odd_prompt(zm_p, wa, ba, gain, wp, ps):
    blk = lambda b, i: (b * NBLK + i, 0)
    fixed2 = lambda b, i: (0, 0)
    return pl.pallas_call(
        _odd_prompt_kernel,
        grid=(BATCH, NBLK),
        in_specs=[pl.BlockSpec((BLK, ODD_IN_PAD), blk),
                  pl.BlockSpec((LANES, GLA_K_WIDTH), fixed2),
                  pl.BlockSpec((1, GLA_K_WIDTH), fixed2),
                  pl.BlockSpec((1, GLA_DV), fixed2),
                  pl.BlockSpec((4, POOL_GDIM, POOL_GDIM), lambda b, i: (0, 0, 0)),
                  pl.BlockSpec((1, POOL_DIM), fixed2)],
        out_specs=[pl.BlockSpec((BLK, D_MODEL), blk),
                   pl.BlockSpec((None, GLA_HEADS, GLA_DK, GLA_DV), lambda b, i: (b, 0, 0, 0)),
                   pl.BlockSpec((None, POOL_BUF, POOL_DIM), lambda b, i: (b, 0, 0))],
        out_shape=[jax.ShapeDtypeStruct((M_PROMPT, D_MODEL), BF16),
                   jax.ShapeDtypeStruct((BATCH, GLA_HEADS, GLA_DK, GLA_DV), F32),
                   jax.ShapeDtypeStruct((BATCH, POOL_BUF, POOL_DIM), F32)],
        scratch_shapes=[pltpu.VMEM((2 * BLK, POOL_DIM), F32)],
        compiler_params=_cparams("arbitrary", "arbitrary"),
        name="odd_prompt",
    )(zm_p, wa, ba, gain, wp, ps)


def _odd_sample_prep_kernel(zm_ref, zr_ref, wa_ref, ba_ref, dec_ref, q_ref):
    dec_ref[...] = jnp.exp(_log_decay(zr_ref[...], wa_ref, ba_ref))
    q_ref[...] = zm_ref[:, 0:GLA_K_WIDTH] * (GLA_DK ** -0.5)


def _odd_sample_prep(zm_s, zr_s, wa, ba):
    return pl.pallas_call(
        _odd_sample_prep_kernel,
        out_shape=[jax.ShapeDtypeStruct((DEC_BATCH, GLA_K_WIDTH), F32),
                   jax.ShapeDtypeStruct((DEC_BATCH, GLA_K_WIDTH), F32)],
        compiler_params=pltpu.CompilerParams(vmem_limit_bytes=VMEM_LIMIT),
        name="odd_sample_prep",
    )(zm_s, zr_s, wa, ba)


def _odd_sample_state_kernel(*refs):
    dec_ref, q_ref, k_ref, v_ref, s_ref = refs[:5]
    ns_ref, o_ref = refs[-2:]
    for h in range(GLA_HEADS):
        ks = slice(h * GLA_DK, (h + 1) * GLA_DK)
        vs = slice(h * GLA_DV, (h + 1) * GLA_DV)
        s_new = _row_to_col(dec_ref[:, ks]) * s_ref[h] + _row_to_col(k_ref[:, ks]) * v_ref[:, vs]
        ns_ref[h] = s_new
        o_ref[:, vs] = jnp.sum(_row_to_col(q_ref[:, ks]) * s_new, axis=0, keepdims=True)


def _odd_sample_state(dec, q, k, v, state, li, prev):
    vec = lambda w: pl.BlockSpec((None, 1, w), lambda b: (b, 0, 0))
    st = pl.BlockSpec((None, None, GLA_HEADS, GLA_DK, GLA_DV), lambda b: (li, b, 0, 0, 0))
    r3 = lambda a: a.reshape(DEC_BATCH, 1, a.shape[-1])
    in_specs = [vec(GLA_K_WIDTH), vec(GLA_K_WIDTH), vec(GLA_K_WIDTH), vec(GLA_WIDTH), st]
    args = [r3(dec), r3(q), r3(k), r3(v), state]
    aliases = {}
    if prev is not None:
        aliases = {len(args): 0}
        in_specs.append(pl.BlockSpec(memory_space=pl.ANY))
        args.append(prev)
    return pl.pallas_call(
        _odd_sample_state_kernel,
        grid=(DEC_BATCH,),
        in_specs=in_specs,
        out_specs=[st, vec(GLA_WIDTH)],
        out_shape=[jax.ShapeDtypeStruct(state.shape, F32),
                   jax.ShapeDtypeStruct((DEC_BATCH, 1, GLA_WIDTH), F32)],
        input_output_aliases=aliases,
        compiler_params=_cparams("arbitrary"),
        name="odd_sample_state",
    )(*args)


def _odd_sample_post_kernel(o_in_ref, zm_ref, zp_ref, hist_ref, gain_ref, wp_ref, ps_ref, o_ref):
    o_g = 2 * GLA_K_WIDTH + GLA_WIDTH
    for h in range(GLA_HEADS):
        vs = slice(h * GLA_DV, (h + 1) * GLA_DV)
        zg = zm_ref[:, o_g + h * GLA_DV:o_g + (h + 1) * GLA_DV]
        o_ref[:, vs] = _head_rmsnorm_gate(o_in_ref[:, vs], gain_ref[...], zg).astype(o_ref.dtype)
    n_prev = hist_ref.shape[0]
    for g, w in enumerate(POOL_WINDOWS):
        gs = slice(g * POOL_GDIM, (g + 1) * POOL_GDIM)
        x = zp_ref[:, gs]
        s = hist_ref[n_prev - (w - 1)][:, gs]
        for jj in range(n_prev - (w - 1) + 1, n_prev):
            s = s + hist_ref[jj][:, gs]
        s = s + x
        d = s / float(min(w, n_prev + 1)) - x
        y = _dot(d.astype(BF16), wp_ref[g]) * ps_ref[:, gs]
        o_ref[:, GLA_WIDTH + g * POOL_GDIM:GLA_WIDTH + (g + 1) * POOL_GDIM] = y.astype(o_ref.dtype)


def _odd_sample_post(o_raw, zm_s, zp_s, hist_t, gain, wp, ps):
    return pl.pallas_call(
        _odd_sample_post_kernel,
        out_shape=jax.ShapeDtypeStruct((DEC_BATCH, D_MODEL), BF16),
        compiler_params=pltpu.CompilerParams(vmem_limit_bytes=VMEM_LIMIT),
        name="odd_sample_post",
    )(o_raw, zm_s, zp_s, hist_t, gain, wp, ps)


def _rope_tables(pos):
    half = HEAD_DIM // 2
    inv = jnp.power(ROPE_THETA, -jnp.arange(half, dtype=F32) / half)
    ang = pos.astype(F32)[:, None] * inv[None, :]
    c, s = jnp.cos(ang), jnp.sin(ang)
    return jnp.tile(c, (1, 4)), jnp.tile(jnp.concatenate([-s, s], axis=1), (1, 2))


def kernel(x_prompt, x_sample, cache_swa_k, cache_swa_v, state_conv, state_gla, state_pool, norm_mix, norm_ffn, w_in_even, w_out_even, q_norm, k_norm, attn_sinks, conv_w, w_in_odd, w_out_odd, w_alpha_up, b_alpha, gla_out_norm, w_pool, pool_scale, w_gate, w_up, w_down):
    lc = cache_swa_k.shape[2]
    x_p = x_prompt.reshape(M_PROMPT, D_MODEL)
    x_s = x_sample.reshape(DEC_BATCH, D_MODEL)
    cos_p, sin_p = _rope_tables(jnp.arange(SEQ))
    cos_s, sin_s = _rope_tables(PAST_LEN + jnp.arange(1))
    mats = _rope_mats()
    kc_all = cache_swa_k.reshape(N_EVEN, DEC_BATCH, lc, A_KV_WIDTH)
    vc_all = cache_swa_v.reshape(N_EVEN, DEC_BATCH, lc, A_KV_WIDTH)
    w_out_bf = (w_out_even.astype(BF16), w_out_odd.astype(BF16))
    w_down_bf = w_down.astype(BF16)
    o_r = ODD_MAIN
    w_odd = jnp.concatenate([w_in_odd[:, :, :o_r], w_in_odd[:, :, o_r + GLA_RANK:],
                             w_in_odd[:, :, o_r:o_r + GLA_RANK],
                             jnp.zeros((N_ODD, D_MODEL, LANES - GLA_RANK), F32)], axis=2)
    pk, pv, pc, pg, pp, sc, sp = ([] for _ in range(7))
    sk_all = sv_all = sg_all = None

    h_p, h_s = _rmsnorm(x_p, x_s, norm_mix[0])
    for layer in range(DEPTH):
        li = layer // 2
        if layer % 2 == 0:
            z_p, z_s = _dense(h_p, h_s, [(w_in_even, li)], n_cols=EVEN_IN, tm=1024, tn=1536, name="in_even")
            qg = jnp.tile(q_norm[li], 2).reshape(1, LANES)
            kg = jnp.tile(k_norm[li], 2).reshape(1, LANES)
            m_p, nk, nv, nc = _even_prompt(z_p, cos_p, sin_p, qg, kg, conv_w[li], attn_sinks[li], mats)
            pk.append(nk.reshape(BATCH, lc, A_KV_HEADS, HEAD_DIM))
            pv.append(nv.reshape(BATCH, lc, A_KV_HEADS, HEAD_DIM))
            pc.append(nc)
            q_s, k_s, u_s, conv_s = _even_sample_prep(
                z_s, cos_s, sin_s, qg, kg, state_conv[li, :, 0], state_conv[li, :, 1], conv_w[li], mats)
            v_s = z_s[:, A_WIDTH + A_KV_WIDTH:A_WIDTH + 2 * A_KV_WIDTH]
            attn_s, sk_all, sv_all = _even_sample_attn(
                attn_sinks[li], q_s, k_s, v_s, kc_all, vc_all, li,
                None if sk_all is None else (sk_all, sv_all))
            sc.append(jnp.stack([state_conv[li, :, 1], u_s], axis=1))
            m_s = jnp.concatenate([attn_s.reshape(DEC_BATCH, A_WIDTH), conv_s], axis=1)
        else:
            zm_p, zm_s = _dense(h_p, h_s, [(w_odd, li)], n_cols=ODD_IN_PAD, tm=1024, tn=ODD_IN_PAD // 3,
                                name="in_odd")
            zp_s, zr_s = zm_s[:, o_r:o_r + POOL_DIM], zm_s[:, o_r + POOL_DIM:]
            wa = jnp.pad(w_alpha_up[li], ((0, LANES - GLA_RANK), (0, 0))).astype(BF16)
            ba = b_alpha[li].reshape(1, GLA_K_WIDTH)
            gain = gla_out_norm[li].reshape(1, GLA_DV)
            wp = w_pool[li].astype(BF16)
            ps = pool_scale[li].reshape(1, POOL_DIM)
            m_p, ng, npool = _odd_prompt(zm_p, wa, ba, gain, wp, ps)
            pg.append(ng)
            pp.append(npool)
            dec_s, q_s = _odd_sample_prep(zm_s, zr_s, wa, ba)
            sg_all, o_raw = _odd_sample_state(
                dec_s, q_s, zm_s[:, GLA_K_WIDTH:2 * GLA_K_WIDTH],
                zm_s[:, 2 * GLA_K_WIDTH:2 * GLA_K_WIDTH + GLA_WIDTH], state_gla, li, sg_all)
            m_s = _odd_sample_post(o_raw.reshape(DEC_BATCH, GLA_WIDTH), zm_s, zp_s,
                                   jnp.swapaxes(state_pool[li], 0, 1), gain, wp, ps)
            sp.append(jnp.concatenate([state_pool[li, :, 1:], zp_s[:, None, :]], axis=1))
        x_p, x_s, h_p, h_s = _proj_res(m_p, m_s, (w_out_bf[layer % 2], li), (x_p, x_s), norm_ffn[layer],
                                       tm=512, name="out_proj")
        a_p, a_s = _dense(h_p, h_s, [(w_gate, layer), (w_up, layer)], n_cols=D_FF, tm=1024, tn=512,
                          mode="swiglu", out_dtype=BF16, name="ffn_up")
        if layer + 1 < DEPTH:
            x_p, x_s, h_p, h_s = _proj_res(a_p, a_s, (w_down_bf, layer), (x_p, x_s),
                                           norm_mix[layer + 1], tm=256, name="ffn_down")
        else:
            x_p, x_s = _proj_res(a_p, a_s, (w_down_bf, layer), (x_p, x_s), None,
                                 tm=256, name="ffn_down_last")

    st = lambda parts: jnp.stack(parts)
    cache5 = lambda a: a.reshape(N_EVEN, DEC_BATCH, lc, A_KV_HEADS, HEAD_DIM)
    return (x_p.reshape(BATCH, SEQ, D_MODEL), x_s.reshape(DEC_BATCH, 1, D_MODEL),
            st(pk), st(pv), st(pc), st(pg), st(pp), cache5(sk_all), cache5(sv_all), st(sc), sg_all, st(sp))
```

```python
import functools

import jax
import jax.numpy as jnp
from jax import lax
from jax.experimental import pallas as pl
from jax.experimental.pallas import tpu as pltpu

F32 = jnp.float32
BF16 = jnp.bfloat16

D_MODEL = 2048
BATCH = 4
SEQ = 2048
DEPTH = 4
DEC_BATCH = 32
PAST_LEN = 16384
N_EVEN = 2
N_ODD = 2
EPS = 1e-6
NEG_INF = -1e30
A_HEADS = 16
A_KV_HEADS = 4
HEAD_DIM = 64
A_WIDTH = A_HEADS * HEAD_DIM
A_KV_WIDTH = A_KV_HEADS * HEAD_DIM
WINDOW = 128
ROPE_THETA = 10000.0
CONV_DIM = D_MODEL // 2
CONV_W = 3
GLA_HEADS = 4
GLA_WIDTH = D_MODEL // 2
GLA_DV = GLA_WIDTH // GLA_HEADS
GLA_DK = GLA_DV // 2
GLA_K_WIDTH = GLA_HEADS * GLA_DK
GLA_RANK = 16
GLA_TAU = 16.0
GLA_CHUNK = 64
POOL_DIM = D_MODEL // 2
POOL_WINDOWS = (2, 4, 8, 16)
POOL_GDIM = POOL_DIM // 4
POOL_BUF = 15
D_FF = 5632
EVEN_IN = A_WIDTH + 2 * A_KV_WIDTH + 3 * CONV_DIM
ODD_MAIN = 2 * GLA_K_WIDTH + 2 * GLA_WIDTH

M_PROMPT = BATCH * SEQ
LANES = 128
BLK = 128
NBLK = SEQ // BLK
POOL_RANK_W = POOL_DIM + LANES
VMEM_LIMIT = 58 * 1024 * 1024


def _cparams(*sem):
    return pltpu.CompilerParams(dimension_semantics=sem, vmem_limit_bytes=VMEM_LIMIT)


def _dot(a, b):
    return jnp.dot(a, b, preferred_element_type=F32)


def _dot_nt(a, b):
    return lax.dot_general(a, b, (((1,), (1,)), ((), ())), preferred_element_type=F32)


def _dot_tn(a, b):
    return lax.dot_general(a, b, (((0,), (0,)), ((), ())), preferred_element_type=F32)


def _silu(x):
    return x * (1.0 / (1.0 + jnp.exp(-x)))


def _rmsnorm_rows(x, gain):
    ms = jnp.mean(x * x, axis=-1, keepdims=True)
    return (x * lax.rsqrt(ms + EPS)) * gain


def _rmsnorm_kernel(xp_ref, xs_ref, g_ref, op_ref, os_ref, *, n_i):
    op_ref[...] = _rmsnorm_rows(xp_ref[...], g_ref[...]).astype(op_ref.dtype)

    @pl.when(pl.program_id(0) == n_i - 1)
    def _():
        os_ref[...] = _rmsnorm_rows(xs_ref[...], g_ref[...]).astype(os_ref.dtype)


def _rmsnorm(x_p, x_s, gain, *, tm=1024):
    n_i = M_PROMPT // tm
    row = lambda i: (i, 0)
    fixed = lambda i: (0, 0)
    return pl.pallas_call(
        functools.partial(_rmsnorm_kernel, n_i=n_i),
        grid=(n_i,),
        in_specs=[pl.BlockSpec((tm, D_MODEL), row),
                  pl.BlockSpec((DEC_BATCH, D_MODEL), fixed),
                  pl.BlockSpec((1, D_MODEL), fixed)],
        out_specs=[pl.BlockSpec((tm, D_MODEL), row),
                   pl.BlockSpec((DEC_BATCH, D_MODEL), fixed)],
        out_shape=[jax.ShapeDtypeStruct((M_PROMPT, D_MODEL), BF16),
                   jax.ShapeDtypeStruct((DEC_BATCH, D_MODEL), BF16)],
        compiler_params=_cparams("arbitrary"),
        name="rmsnorm",
    )(x_p, x_s, gain.reshape(1, D_MODEL))


def _dense_kernel(*refs, n_w, mode, n_i):
    a_p, a_s = refs[0], refs[1]
    w = refs[2:2 + n_w]
    o_p, o_s = refs[2 + n_w], refs[3 + n_w]
    wbf = refs[4 + n_w:4 + 2 * n_w]
    i = pl.program_id(1)

    @pl.when(i == 0)
    def _():
        for k in range(n_w):
            wbf[k][...] = w[k][...].astype(BF16)

    def run(a_ref, o_ref):
        a = a_ref[...]
        if mode == "swiglu":
            y = _silu(_dot(a, wbf[0][...])) * _dot(a, wbf[1][...])
        else:
            y = _dot(a, wbf[0][...])
        o_ref[...] = y.astype(o_ref.dtype)

    run(a_p, o_p)

    @pl.when(i == n_i - 1)
    def _():
        run(a_s, o_s)


def _dense(a_p, a_s, weights, *, n_cols, tm, tn, mode="plain", out_dtype=F32, name):
    k_dim = a_p.shape[1]
    n_i = M_PROMPT // tm
    row = lambda j, i: (i, 0)
    fixed = lambda j, i: (0, 0)
    tile = lambda j, i: (i, j)
    panel = lambda j, i: (0, j)
    in_specs = [pl.BlockSpec((tm, k_dim), row), pl.BlockSpec((DEC_BATCH, k_dim), fixed)]
    args = [a_p, a_s]
    for arr, layer in weights:
        if layer is None:
            in_specs.append(pl.BlockSpec((k_dim, tn), panel))
        else:
            in_specs.append(pl.BlockSpec((None, k_dim, tn), lambda j, i, layer=layer: (layer, 0, j)))
        args.append(arr)
    n_w = len(weights)
    return pl.pallas_call(
        functools.partial(_dense_kernel, n_w=n_w, mode=mode, n_i=n_i),
        grid=(n_cols // tn, n_i),
        in_specs=in_specs,
        out_specs=[pl.BlockSpec((tm, tn), tile), pl.BlockSpec((DEC_BATCH, tn), panel)],
        out_shape=[jax.ShapeDtypeStruct((M_PROMPT, n_cols), out_dtype),
                   jax.ShapeDtypeStruct((DEC_BATCH, n_cols), out_dtype)],
        scratch_shapes=[pltpu.VMEM((k_dim, tn), BF16) for _ in range(n_w)],
        compiler_params=_cparams("arbitrary", "arbitrary"),
        name=name,
    )(*args)


def _proj_res_kernel(*refs, n_i, with_norm):
    a_p, a_s, w_ref, r_p, r_s = refs[:5]
    if with_norm:
        g_ref, x_p, x_s, h_p, h_s = refs[5:]
    else:
        x_p, x_s = refs[5:]
        g_ref = h_p = h_s = None

    def run(a_ref, r_ref, x_ref, h_ref):
        x = r_ref[...] + _dot(a_ref[...], w_ref[...])
        x_ref[...] = x
        if with_norm:
            h_ref[...] = _rmsnorm_rows(x, g_ref[...]).astype(h_ref.dtype)

    run(a_p, r_p, x_p, h_p)

    @pl.when(pl.program_id(0) == n_i - 1)
    def _():
        run(a_s, r_s, x_s, h_s)


def _proj_res(a_p, a_s, weight, res, gain, *, tm, name):
    w_bf, layer = weight
    k_dim = a_p.shape[1]
    n_i = M_PROMPT // tm
    with_norm = gain is not None
    row = lambda i: (i, 0)
    fixed = lambda i: (0, 0)
    rows_p = pl.BlockSpec((tm, D_MODEL), row)
    rows_s = pl.BlockSpec((DEC_BATCH, D_MODEL), fixed)
    in_specs = [pl.BlockSpec((tm, k_dim), row), pl.BlockSpec((DEC_BATCH, k_dim), fixed),
                pl.BlockSpec((None, k_dim, D_MODEL), lambda i: (layer, 0, 0), pipeline_mode=pl.Buffered(1)),
                rows_p, rows_s]
    args = [a_p, a_s, w_bf, res[0], res[1]]
    out_specs = [rows_p, rows_s]
    out_shape = [jax.ShapeDtypeStruct((M_PROMPT, D_MODEL), F32), jax.ShapeDtypeStruct((DEC_BATCH, D_MODEL), F32)]
    if with_norm:
        in_specs.append(pl.BlockSpec((1, D_MODEL), fixed))
        args.append(gain.reshape(1, D_MODEL))
        out_specs += [rows_p, rows_s]
        out_shape += [jax.ShapeDtypeStruct((M_PROMPT, D_MODEL), BF16),
                      jax.ShapeDtypeStruct((DEC_BATCH, D_MODEL), BF16)]
    return pl.pallas_call(
        functools.partial(_proj_res_kernel, n_i=n_i, with_norm=with_norm),
        grid=(n_i,),
        in_specs=in_specs,
        out_specs=out_specs,
        out_shape=out_shape,
        compiler_params=_cparams("arbitrary"),
        name=name,
    )(*args)


def _split_cat(x):
    hi = x.astype(BF16)
    lo = (x - hi.astype(F32)).astype(BF16)
    return jnp.concatenate([hi, lo], axis=1)


def _rope_mats():
    j = jnp.arange(2 * LANES)[:, None] % LANES
    l = jnp.arange(LANES)[None, :]
    head_sum = (j // HEAD_DIM == l // HEAD_DIM).astype(BF16)
    src = jnp.where(l % HEAD_DIM < HEAD_DIM // 2, l + HEAD_DIM // 2, l - HEAD_DIM // 2)
    half_swap = (j == src).astype(BF16)
    return head_sum, half_swap


def _norm_rope_chunk(xc, gain, cos, sin, hs_ref, sw_ref):
    ms = _dot(_split_cat(xc * xc), hs_ref[...]) * (1.0 / HEAD_DIM)
    y = (xc * lax.rsqrt(ms + EPS)) * gain
    swapped = _dot(_split_cat(y), sw_ref[...])
    return y * cos + swapped * sin


def _spread_heads(xc, own_lo):
    lane = lax.broadcasted_iota(jnp.int32, xc.shape, 1)
    keep = (lane < HEAD_DIM) if own_lo else (lane >= HEAD_DIM)
    nat = jnp.where(keep, xc, 0.0)
    rol = pltpu.roll(nat, HEAD_DIM, axis=1)
    parts = (nat, rol) if own_lo else (rol, nat)
    return jnp.concatenate(parts, axis=0).astype(BF16)


def _attn_core(qs, kfull, vfull, mask_t, sink_ref):
    rows = qs[0].shape[0]
    nkeys = kfull.shape[0]
    outs = [None] * 8
    for kh in range(A_KV_HEADS):
        c0 = (kh // 2) * LANES
        kk = _spread_heads(kfull[:, c0:c0 + LANES], kh % 2 == 0)
        vv = _spread_heads(vfull[:, c0:c0 + LANES], kh % 2 == 0)
        lhs = jnp.concatenate([qs[2 * kh], qs[2 * kh + 1]], axis=0)
        s = _dot_nt(kk, lhs)
        prow = []
        for half in range(2):
            pcol = []
            for cc in range(2):
                sb = s[half * nkeys:(half + 1) * nkeys, cc * rows:(cc + 1) * rows]
                sb = jnp.where(mask_t, sb, NEG_INF)
                sink = sink_ref[kh * 4 + 2 * cc + half]
                m = jnp.maximum(jnp.max(sb, axis=0, keepdims=True), sink)
                e = jnp.exp(sb - m)
                den = jnp.sum(e, axis=0, keepdims=True) + jnp.exp(sink - m)
                pcol.append((e / den).astype(BF16))
            prow.append(jnp.concatenate(pcol, axis=1))
        p = jnp.concatenate(prow, axis=0)
        o = _dot_tn(p, vv)
        outs[2 * kh] = o[0:rows]
        outs[2 * kh + 1] = o[rows:2 * rows]
    return outs


def _even_prompt_kernel(sink_ref, z_ref, cos_ref, sin_ref, qg_ref, kg_ref, cw_ref, hs_ref, sw_ref,
                        o_ref, nk_ref, nv_ref, nc_ref, kf_ref, vf_ref, ub_ref):
    i = pl.program_id(1)

    @pl.when(i == 0)
    def _():
        kf_ref[...] = jnp.zeros_like(kf_ref)
        vf_ref[...] = jnp.zeros_like(vf_ref)
        ub_ref[0:8, :] = jnp.zeros((8, CONV_DIM), F32)

    kf_ref[0:BLK, :] = kf_ref[BLK:2 * BLK, :]
    vf_ref[0:BLK, :] = vf_ref[BLK:2 * BLK, :]
    cos = cos_ref[...]
    sin = sin_ref[...]
    k = jnp.concatenate(
        [_norm_rope_chunk(z_ref[:, A_WIDTH + c * LANES:A_WIDTH + (c + 1) * LANES], kg_ref[...], cos, sin,
                          hs_ref, sw_ref)
         for c in range(A_KV_WIDTH // LANES)], axis=1)
    v = z_ref[:, A_WIDTH + A_KV_WIDTH:A_WIDTH + 2 * A_KV_WIDTH]
    kf_ref[BLK:2 * BLK, :] = k
    vf_ref[BLK:2 * BLK, :] = v
    qs = [(_norm_rope_chunk(z_ref[:, c * LANES:(c + 1) * LANES], qg_ref[...], cos, sin, hs_ref, sw_ref)
           * (HEAD_DIM ** -0.5)).astype(BF16) for c in range(A_WIDTH // LANES)]
    kk = lax.broadcasted_iota(jnp.int32, (2 * BLK, BLK), 0)
    r = lax.broadcasted_iota(jnp.int32, (2 * BLK, BLK), 1)
    d = kk - r
    mask_t = (d >= 0) & (d <= WINDOW) & ((kk >= BLK) | (i > 0))
    outs = _attn_core(qs, kf_ref[...], vf_ref[...], mask_t, sink_ref)
    for c in range(A_WIDTH // LANES):
        o_ref[:, c * LANES:(c + 1) * LANES] = outs[c].astype(o_ref.dtype)

    o0 = A_WIDTH + 2 * A_KV_WIDTH
    u = z_ref[:, o0 + CONV_DIM:o0 + 2 * CONV_DIM] * z_ref[:, o0 + 2 * CONV_DIM:o0 + 3 * CONV_DIM]
    ub_ref[8:8 + BLK, :] = u
    y = ub_ref[6:6 + BLK, :] * cw_ref[0:1, :]
    y = y + ub_ref[7:7 + BLK, :] * cw_ref[1:2, :]
    y = y + ub_ref[8:8 + BLK, :] * cw_ref[2:3, :]
    o_ref[:, A_WIDTH:A_WIDTH + CONV_DIM] = (z_ref[:, o0:o0 + CONV_DIM] * y).astype(o_ref.dtype)
    ub_ref[0:8, :] = ub_ref[BLK:BLK + 8, :]

    @pl.when(i == pl.num_programs(1) - 1)
    def _():
        nk_ref[...] = k
        nv_ref[...] = v
        nc_ref[...] = ub_ref[BLK + 6:BLK + 8, :]


def _even_prompt(z_p, cos, sin, qg, kg, cw, sinks, mats):
    blk = lambda b, i: (b * NBLK + i, 0)
    fixed = lambda b, i: (0, 0)
    per_b = lambda b, i: (b, 0, 0)
    return pl.pallas_call(
        _even_prompt_kernel,
        grid=(BATCH, NBLK),
        in_specs=[pl.BlockSpec(memory_space=pltpu.SMEM),
                  pl.BlockSpec((BLK, EVEN_IN), blk),
                  pl.BlockSpec((BLK, LANES), lambda b, i: (i, 0)),
                  pl.BlockSpec((BLK, LANES), lambda b, i: (i, 0)),
                  pl.BlockSpec((1, LANES), fixed),
                  pl.BlockSpec((1, LANES), fixed),
                  pl.BlockSpec((CONV_W, CONV_DIM), fixed),
                  pl.BlockSpec((2 * LANES, LANES), fixed),
                  pl.BlockSpec((2 * LANES, LANES), fixed)],
        out_specs=[pl.BlockSpec((BLK, D_MODEL), blk),
                   pl.BlockSpec((None, BLK, A_KV_WIDTH), per_b),
                   pl.BlockSpec((None, BLK, A_KV_WIDTH), per_b),
                   pl.BlockSpec((None, CONV_W - 1, CONV_DIM), per_b)],
        out_shape=[jax.ShapeDtypeStruct((M_PROMPT, D_MODEL), BF16),
                   jax.ShapeDtypeStruct((BATCH, BLK, A_KV_WIDTH), F32),
                   jax.ShapeDtypeStruct((BATCH, BLK, A_KV_WIDTH), F32),
                   jax.ShapeDtypeStruct((BATCH, CONV_W - 1, CONV_DIM), F32)],
        scratch_shapes=[pltpu.VMEM((2 * BLK, A_KV_WIDTH), F32),
                        pltpu.VMEM((2 * BLK, A_KV_WIDTH), F32),
                        pltpu.VMEM((BLK + 8, CONV_DIM), F32)],
        compiler_params=_cparams("arbitrary", "arbitrary"),
        name="even_prompt",
    )(sinks, z_p, cos, sin, qg, kg, cw, *mats)


def _even_sample_prep_kernel(z_ref, cos_ref, sin_ref, qg_ref, kg_ref, c0_ref, c1_ref, cw_ref, hs_ref, sw_ref,
                             q_ref, k_ref, u_ref, o_ref):
    cos = cos_ref[...]
    sin = sin_ref[...]
    for c in range(A_WIDTH // LANES):
        q = _norm_rope_chunk(z_ref[:, c * LANES:(c + 1) * LANES], qg_ref[...], cos, sin, hs_ref, sw_ref)
        q_ref[:, c * LANES:(c + 1) * LANES] = (q * (HEAD_DIM ** -0.5)).astype(q_ref.dtype)
    for c in range(A_KV_WIDTH // LANES):
        k_ref[:, c * LANES:(c + 1) * LANES] = _norm_rope_chunk(
            z_ref[:, A_WIDTH + c * LANES:A_WIDTH + (c + 1) * LANES], kg_ref[...], cos, sin, hs_ref, sw_ref)
    o0 = A_WIDTH + 2 * A_KV_WIDTH
    u = z_ref[:, o0 + CONV_DIM:o0 + 2 * CONV_DIM] * z_ref[:, o0 + 2 * CONV_DIM:o0 + 3 * CONV_DIM]
    u_ref[...] = u
    y = c0_ref[...] * cw_ref[0:1, :]
    y = y + c1_ref[...] * cw_ref[1:2, :]
    y = y + u * cw_ref[2:3, :]
    o_ref[...] = (z_ref[:, o0:o0 + CONV_DIM] * y).astype(o_ref.dtype)


def _even_sample_prep(z_s, cos, sin, qg, kg, c0, c1, cw, mats):
    return pl.pallas_call(
        _even_sample_prep_kernel,
        out_shape=[jax.ShapeDtypeStruct((DEC_BATCH, A_WIDTH), BF16),
                   jax.ShapeDtypeStruct((DEC_BATCH, A_KV_WIDTH), F32),
                   jax.ShapeDtypeStruct((DEC_BATCH, CONV_DIM), F32),
                   jax.ShapeDtypeStruct((DEC_BATCH, CONV_DIM), BF16)],
        compiler_params=pltpu.CompilerParams(vmem_limit_bytes=VMEM_LIMIT),
        name="even_sample_prep",
    )(z_s, cos, sin, qg, kg, c0, c1, cw, *mats)


SROWS = LANES


def _even_sample_attn_kernel(*refs):
    sink_ref, q_ref, kn_ref, vn_ref, kc_ref, vc_ref = refs[:6]
    o_ref, nk_ref, nv_ref = refs[-3:]
    lc = kc_ref.shape[0]
    row = lax.broadcasted_iota(jnp.int32, (lc, A_KV_WIDTH), 0)
    kfull = jnp.concatenate([kc_ref[...], jnp.where(row == 0, kn_ref[...], 0.0)], axis=0)
    vfull = jnp.concatenate([vc_ref[...], jnp.where(row == 0, vn_ref[...], 0.0)], axis=0)
    qs = [jnp.broadcast_to(q_ref[:, c * LANES:(c + 1) * LANES], (SROWS, LANES))
          for c in range(A_WIDTH // LANES)]
    kk = lax.broadcasted_iota(jnp.int32, (2 * lc, SROWS), 0)
    mask_t = (kk <= lc) & (lc - kk <= WINDOW)
    outs = _attn_core(qs, kfull, vfull, mask_t, sink_ref)
    for c in range(A_WIDTH // LANES):
        o_ref[:, c * LANES:(c + 1) * LANES] = outs[c][0:1].astype(o_ref.dtype)
    nk_ref[0:lc - 1, :] = kc_ref[1:lc, :]
    nk_ref[lc - 1:lc, :] = kn_ref[...]
    nv_ref[0:lc - 1, :] = vc_ref[1:lc, :]
    nv_ref[lc - 1:lc, :] = vn_ref[...]


def _even_sample_attn(sinks, q_s, k_s, v_s, k_cache, v_cache, li, prev):
    lc = k_cache.shape[2]
    vec = lambda w: pl.BlockSpec((None, 1, w), lambda b: (b, 0, 0))
    cache = pl.BlockSpec((None, None, lc, A_KV_WIDTH), lambda b: (li, b, 0, 0))
    in_specs = [pl.BlockSpec(memory_space=pltpu.SMEM), vec(A_WIDTH), vec(A_KV_WIDTH), vec(A_KV_WIDTH),
                cache, cache]
    args = [sinks, q_s.reshape(DEC_BATCH, 1, A_WIDTH), k_s.reshape(DEC_BATCH, 1, A_KV_WIDTH),
            v_s.reshape(DEC_BATCH, 1, A_KV_WIDTH), k_cache, v_cache]
    aliases = {}
    if prev is not None:
        aliases = {len(args): 1, len(args) + 1: 2}
        in_specs += [pl.BlockSpec(memory_space=pl.ANY)] * 2
        args += list(prev)
    return pl.pallas_call(
        _even_sample_attn_kernel,
        grid=(DEC_BATCH,),
        in_specs=in_specs,
        out_specs=[vec(A_WIDTH), cache, cache],
        out_shape=[jax.ShapeDtypeStruct((DEC_BATCH, 1, A_WIDTH), BF16),
                   jax.ShapeDtypeStruct(k_cache.shape, F32),
                   jax.ShapeDtypeStruct(v_cache.shape, F32)],
        input_output_aliases=aliases,
        compiler_params=_cparams("arbitrary"),
        name="even_sample_attn",
    )(*args)


def _log_decay(zr, wa_ref, ba_ref):
    pre = _dot(zr.astype(BF16), wa_ref[...]) + ba_ref[...]
    return (jnp.minimum(pre, 0.0) - jnp.log1p(jnp.exp(-jnp.abs(pre)))) * (1.0 / GLA_TAU)


def _split_bf16(x):
    hi = x.astype(BF16)
    lo = (x - hi.astype(F32)).astype(BF16)
    return hi, lo


def _row_to_col(row):
    n = row.shape[1]
    r = lax.broadcasted_iota(jnp.int32, (n, n), 0)
    c = lax.broadcasted_iota(jnp.int32, (n, n), 1)
    return jnp.sum(jnp.where(r == c, jnp.broadcast_to(row, (n, n)), 0.0), axis=-1, keepdims=True)


def _head_rmsnorm_gate(o, gain, zg):
    ms = jnp.mean(o * o, axis=-1, keepdims=True)
    return ((o * lax.rsqrt(ms + EPS)) * gain) * _silu(zg)


def _odd_prompt_kernel(zm_ref, zpr_ref, wa_ref, ba_ref, gain_ref, wp_ref, ps_ref,
                       o_ref, s_ref, np_ref, pbh_ref, pbl_ref):
    i = pl.program_id(1)

    @pl.when(i == 0)
    def _():
        s_ref[...] = jnp.zeros_like(s_ref)
        pbh_ref[0:BLK, :] = jnp.zeros((BLK, POOL_DIM), BF16)
        pbl_ref[0:BLK, :] = jnp.zeros((BLK, POOL_DIM), BF16)

    n_ch = BLK // GLA_CHUNK
    gk = _log_decay(zpr_ref[:, POOL_DIM:POOL_RANK_W], wa_ref, ba_ref)
    r = lax.broadcasted_iota(jnp.int32, (BLK, BLK), 0)
    c = lax.broadcasted_iota(jnp.int32, (BLK, BLK), 1)
    causal = (r >= c) & (r // GLA_CHUNK == c // GLA_CHUNK)
    tri = jnp.where(causal, 1.0, 0.0).astype(BF16)
    g_hi, g_lo = _split_bf16(gk)
    b = _dot(jnp.concatenate([tri, tri], axis=1), jnp.concatenate([g_hi, g_lo], axis=0))
    b_last = [b[(ch + 1) * GLA_CHUNK - 1:(ch + 1) * GLA_CHUNK, :] for ch in range(n_ch)]
    b_end = jnp.concatenate([jnp.broadcast_to(bl, (GLA_CHUNK, GLA_K_WIDTH)) for bl in b_last], axis=0)
    o_k, o_v, o_g = GLA_K_WIDTH, 2 * GLA_K_WIDTH, 2 * GLA_K_WIDTH + GLA_WIDTH
    zq = zm_ref[:, 0:GLA_K_WIDTH]
    zk = zm_ref[:, o_k:o_k + GLA_K_WIDTH]
    qd = ((zq * (GLA_DK ** -0.5)) * jnp.exp(b)).astype(BF16)
    kd = (zk * jnp.exp(-b)).astype(BF16)
    k2 = (zk * jnp.exp(b_end - b)).astype(BF16)
    sub = lax.broadcasted_iota(jnp.int32, (8, LANES), 0)
    sel = jnp.where(sub < 2, 1.0, 0.0).astype(BF16)
    hk = [slice(h * GLA_DK, (h + 1) * GLA_DK) for h in range(GLA_HEADS)]
    chunk = [slice(ch * GLA_CHUNK, (ch + 1) * GLA_CHUNK) for ch in range(n_ch)]
    v_hs = [zm_ref[:, o_v + h * GLA_DV:o_v + (h + 1) * GLA_DV].astype(BF16) for h in range(GLA_HEADS)]
    atts = [_dot_nt(qd[:, hk[h]], kd[:, hk[h]]) for h in range(GLA_HEADS)]
    log_decs = []
    for h in range(GLA_HEADS):
        for ch in range(n_ch):
            bl = b_last[ch][:, hk[h]]
            bl_hi = bl.astype(BF16).astype(F32)
            rows8 = jnp.where(sub == 0, bl_hi, jnp.where(sub == 1, bl - bl_hi, 0.0)).astype(BF16)
            log_decs.append(_dot_tn(rows8, sel))
    deltas = [[_dot_tn(k2[chunk[ch], hk[h]], v_hs[h][chunk[ch]]) for ch in range(n_ch)]
              for h in range(GLA_HEADS)]
    o_intras = [_dot(jnp.where(causal, atts[h], 0.0).astype(BF16), v_hs[h]) for h in range(GLA_HEADS)]
    o_inters = []
    for h in range(GLA_HEADS):
        s_h = s_ref[h]
        parts = []
        for ch in range(n_ch):
            parts.append(_dot(qd[chunk[ch], hk[h]], s_h.astype(BF16)))
            dec = jnp.exp(log_decs[h * n_ch + ch])
            s_h = jnp.concatenate([dec, dec], axis=1) * s_h + deltas[h][ch]
        s_ref[h] = s_h
        o_inters.append(jnp.concatenate(parts, axis=0))
    for h in range(GLA_HEADS):
        vs = slice(h * GLA_DV, (h + 1) * GLA_DV)
        zg = zm_ref[:, o_g + h * GLA_DV:o_g + (h + 1) * GLA_DV]
        o_ref[:, vs] = _head_rmsnorm_gate(o_intras[h] + o_inters[h], gain_ref[...], zg).astype(o_ref.dtype)

    x = zpr_ref[:, 0:POOL_DIM]
    x_hi, x_lo = _split_bf16(x)
    pbh_ref[BLK:2 * BLK, :] = x_hi
    pbl_ref[BLK:2 * BLK, :] = x_lo
    t = lax.broadcasted_iota(jnp.int32, (BLK, 2 * BLK), 0)
    j = lax.broadcasted_iota(jnp.int32, (BLK, 2 * BLK), 1)
    back = t + BLK - j
    t_glob = i * BLK + lax.broadcasted_iota(jnp.int32, (BLK, 1), 0)
    groups = [slice(g * POOL_GDIM, (g + 1) * POOL_GDIM) for g in range(len(POOL_WINDOWS))]
    sums = []
    for g, w in enumerate(POOL_WINDOWS):
        band = jnp.where((back >= 0) & (back < w), 1.0, 0.0).astype(BF16)
        sums.append(_dot(jnp.concatenate([band, band], axis=1),
                         jnp.concatenate([pbh_ref[:, groups[g]], pbl_ref[:, groups[g]]], axis=0)))
    ys = []
    for g, w in enumerate(POOL_WINDOWS):
        inv_cnt = 1.0 / jnp.minimum(w, t_glob + 1).astype(F32)
        d = sums[g] * inv_cnt - x[:, groups[g]]
        ys.append(_dot(d.astype(BF16), wp_ref[g]))
    for g in range(len(POOL_WINDOWS)):
        o_ref[:, GLA_WIDTH + g * POOL_GDIM:GLA_WIDTH + (g + 1) * POOL_GDIM] = (
            ys[g] * ps_ref[:, groups[g]]).astype(o_ref.dtype)
    pbh_ref[0:BLK, :] = x_hi
    pbl_ref[0:BLK, :] = x_lo

    @pl.when(i == pl.num_programs(1) - 1)
    def _():
        np_ref[...] = x[BLK - POOL_BUF:BLK, :]


def _odd_prompt(zm_p, zpr_p, wa, ba, gain, wp, ps):
    blk = lambda b, i: (b * NBLK + i, 0)
    fixed2 = lambda b, i: (0, 0)
    return pl.pallas_call(
        _odd_prompt_kernel,
        grid=(BATCH, NBLK),
        in_specs=[pl.BlockSpec((BLK, ODD_MAIN), blk),
                  pl.BlockSpec((BLK, POOL_RANK_W), blk),
                  pl.BlockSpec((LANES, GLA_K_WIDTH), fixed2),
                  pl.BlockSpec((1, GLA_K_WIDTH), fixed2),
                  pl.BlockSpec((1, GLA_DV), fixed2),
                  pl.BlockSpec((4, POOL_GDIM, POOL_GDIM), lambda b, i: (0, 0, 0)),
                  pl.BlockSpec((1, POOL_DIM), fixed2)],
        out_specs=[pl.BlockSpec((BLK, D_MODEL), blk),
                   pl.BlockSpec((None, GLA_HEADS, GLA_DK, GLA_DV), lambda b, i: (b, 0, 0, 0)),
                   pl.BlockSpec((None, POOL_BUF, POOL_DIM), lambda b, i: (b, 0, 0))],
        out_shape=[jax.ShapeDtypeStruct((M_PROMPT, D_MODEL), BF16),
                   jax.ShapeDtypeStruct((BATCH, GLA_HEADS, GLA_DK, GLA_DV), F32),
                   jax.ShapeDtypeStruct((BATCH, POOL_BUF, POOL_DIM), F32)],
        scratch_shapes=[pltpu.VMEM((2 * BLK, POOL_DIM), BF16), pltpu.VMEM((2 * BLK, POOL_DIM), BF16)],
        compiler_params=_cparams("arbitrary", "arbitrary"),
        name="odd_prompt",
    )(zm_p, zpr_p, wa, ba, gain, wp, ps)


def _odd_sample_prep_kernel(zm_ref, zr_ref, wa_ref, ba_ref, dec_ref, q_ref):
    dec_ref[...] = jnp.exp(_log_decay(zr_ref[...], wa_ref, ba_ref))
    q_ref[...] = zm_ref[:, 0:GLA_K_WIDTH] * (GLA_DK ** -0.5)


def _odd_sample_prep(zm_s, zr_s, wa, ba):
    return pl.pallas_call(
        _odd_sample_prep_kernel,
        out_shape=[jax.ShapeDtypeStruct((DEC_BATCH, GLA_K_WIDTH), F32),
                   jax.ShapeDtypeStruct((DEC_BATCH, GLA_K_WIDTH), F32)],
        compiler_params=pltpu.CompilerParams(vmem_limit_bytes=VMEM_LIMIT),
        name="odd_sample_prep",
    )(zm_s, zr_s, wa, ba)


STATE_BB = 4


def _odd_sample_state_kernel(*refs):
    dec_ref, q_ref, k_ref, v_ref, s_ref = refs[:5]
    ns_ref, o_ref = refs[-2:]
    for bb in range(STATE_BB):
        for h in range(GLA_HEADS):
            ks = slice(h * GLA_DK, (h + 1) * GLA_DK)
            vs = slice(h * GLA_DV, (h + 1) * GLA_DV)
            s_new = (_row_to_col(dec_ref[bb][:, ks]) * s_ref[bb, h]
                     + _row_to_col(k_ref[bb][:, ks]) * v_ref[bb][:, vs])
            ns_ref[bb, h] = s_new
            o_ref[bb, :, vs] = jnp.sum(_row_to_col(q_ref[bb][:, ks]) * s_new, axis=0, keepdims=True)


def _odd_sample_state(dec, q, k, v, state, li, prev):
    vec = lambda w: pl.BlockSpec((STATE_BB, 1, w), lambda b: (b, 0, 0))
    st = pl.BlockSpec((None, STATE_BB, GLA_HEADS, GLA_DK, GLA_DV), lambda b: (li, b, 0, 0, 0))
    r3 = lambda a: a.reshape(DEC_BATCH, 1, a.shape[-1])
    in_specs = [vec(GLA_K_WIDTH), vec(GLA_K_WIDTH), vec(GLA_K_WIDTH), vec(GLA_WIDTH), st]
    args = [r3(dec), r3(q), r3(k), r3(v), state]
    aliases = {}
    if prev is not None:
        aliases = {len(args): 0}
        in_specs.append(pl.BlockSpec(memory_space=pl.ANY))
        args.append(prev)
    return pl.pallas_call(
        _odd_sample_state_kernel,
        grid=(DEC_BATCH // STATE_BB,),
        in_specs=in_specs,
        out_specs=[st, vec(GLA_WIDTH)],
        out_shape=[jax.ShapeDtypeStruct(state.shape, F32),
                   jax.ShapeDtypeStruct((DEC_BATCH, 1, GLA_WIDTH), F32)],
        input_output_aliases=aliases,
        compiler_params=_cparams("arbitrary"),
        name="odd_sample_state",
    )(*args)


def _odd_sample_post_kernel(o_in_ref, zm_ref, zp_ref, hist_ref, gain_ref, wp_ref, ps_ref, o_ref):
    o_g = 2 * GLA_K_WIDTH + GLA_WIDTH
    for h in range(GLA_HEADS):
        vs = slice(h * GLA_DV, (h + 1) * GLA_DV)
        zg = zm_ref[:, o_g + h * GLA_DV:o_g + (h + 1) * GLA_DV]
        o_ref[:, vs] = _head_rmsnorm_gate(o_in_ref[:, vs], gain_ref[...], zg).astype(o_ref.dtype)
    n_prev = hist_ref.shape[0]
    for g, w in enumerate(POOL_WINDOWS):
        gs = slice(g * POOL_GDIM, (g + 1) * POOL_GDIM)
        x = zp_ref[:, gs]
        s = hist_ref[n_prev - (w - 1)][:, gs]
        for jj in range(n_prev - (w - 1) + 1, n_prev):
            s = s + hist_ref[jj][:, gs]
        s = s + x
        d = s / float(min(w, n_prev + 1)) - x
        y = _dot(d.astype(BF16), wp_ref[g]) * ps_ref[:, gs]
        o_ref[:, GLA_WIDTH + g * POOL_GDIM:GLA_WIDTH + (g + 1) * POOL_GDIM] = y.astype(o_ref.dtype)


def _odd_sample_post(o_raw, zm_s, zp_s, hist_t, gain, wp, ps):
    return pl.pallas_call(
        _odd_sample_post_kernel,
        out_shape=jax.ShapeDtypeStruct((DEC_BATCH, D_MODEL), BF16),
        compiler_params=pltpu.CompilerParams(vmem_limit_bytes=VMEM_LIMIT),
        name="odd_sample_post",
    )(o_raw, zm_s, zp_s, hist_t, gain, wp, ps)


def _rope_tables(pos):
    half = HEAD_DIM // 2
    inv = jnp.power(ROPE_THETA, -jnp.arange(half, dtype=F32) / half)
    ang = pos.astype(F32)[:, None] * inv[None, :]
    c, s = jnp.cos(ang), jnp.sin(ang)
    return jnp.tile(c, (1, 4)), jnp.tile(jnp.concatenate([-s, s], axis=1), (1, 2))


def kernel(x_prompt, x_sample, cache_swa_k, cache_swa_v, state_conv, state_gla, state_pool, norm_mix, norm_ffn, w_in_even, w_out_even, q_norm, k_norm, attn_sinks, conv_w, w_in_odd, w_out_odd, w_alpha_up, b_alpha, gla_out_norm, w_pool, pool_scale, w_gate, w_up, w_down):
    lc = cache_swa_k.shape[2]
    x_p = x_prompt.reshape(M_PROMPT, D_MODEL)
    x_s = x_sample.reshape(DEC_BATCH, D_MODEL)
    cos_p, sin_p = _rope_tables(jnp.arange(SEQ))
    cos_s, sin_s = _rope_tables(PAST_LEN + jnp.arange(1))
    mats = _rope_mats()
    kc_all = cache_swa_k.reshape(N_EVEN, DEC_BATCH, lc, A_KV_WIDTH)
    vc_all = cache_swa_v.reshape(N_EVEN, DEC_BATCH, lc, A_KV_WIDTH)
    w_out_bf = (w_out_even.astype(BF16), w_out_odd.astype(BF16))
    w_down_bf = w_down.astype(BF16)
    o_r = ODD_MAIN
    w_odd_main = w_in_odd[:, :, :o_r]
    w_odd_pr = jnp.concatenate([w_in_odd[:, :, o_r + GLA_RANK:], w_in_odd[:, :, o_r:o_r + GLA_RANK],
                                jnp.zeros((N_ODD, D_MODEL, LANES - GLA_RANK), F32)], axis=2)
    pk, pv, pc, pg, pp, sc, sp = ([] for _ in range(7))
    sk_all = sv_all = sg_all = None

    h_p, h_s = _rmsnorm(x_p, x_s, norm_mix[0])
    for layer in range(DEPTH):
        li = layer // 2
        if layer % 2 == 0:
            z_p, z_s = _dense(h_p, h_s, [(w_in_even, li)], n_cols=EVEN_IN, tm=1024, tn=1536, name="in_even")
            qg = jnp.tile(q_norm[li], 2).reshape(1, LANES)
            kg = jnp.tile(k_norm[li], 2).reshape(1, LANES)
            m_p, nk, nv, nc = _even_prompt(z_p, cos_p, sin_p, qg, kg, conv_w[li], attn_sinks[li], mats)
            pk.append(nk.reshape(BATCH, lc, A_KV_HEADS, HEAD_DIM))
            pv.append(nv.reshape(BATCH, lc, A_KV_HEADS, HEAD_DIM))
            pc.append(nc)
            q_s, k_s, u_s, conv_s = _even_sample_prep(
                z_s, cos_s, sin_s, qg, kg, state_conv[li, :, 0], state_conv[li, :, 1], conv_w[li], mats)
            v_s = z_s[:, A_WIDTH + A_KV_WIDTH:A_WIDTH + 2 * A_KV_WIDTH]
            attn_s, sk_all, sv_all = _even_sample_attn(
                attn_sinks[li], q_s, k_s, v_s, kc_all, vc_all, li,
                None if sk_all is None else (sk_all, sv_all))
            sc.append(jnp.stack([state_conv[li, :, 1], u_s], axis=1))
            m_s = jnp.concatenate([attn_s.reshape(DEC_BATCH, A_WIDTH), conv_s], axis=1)
        else:
            zm_p, zm_s = _dense(h_p, h_s, [(w_odd_main, li)], n_cols=ODD_MAIN, tm=1024, tn=ODD_MAIN // 2,
                                name="in_odd_main")
            zpr_p, zpr_s = _dense(h_p, h_s, [(w_odd_pr, li)], n_cols=POOL_RANK_W, tm=1024, tn=POOL_RANK_W,
                                  name="in_odd_pool_rank")
            zp_s, zr_s = zpr_s[:, :POOL_DIM], zpr_s[:, POOL_DIM:]
            wa = jnp.pad(w_alpha_up[li], ((0, LANES - GLA_RANK), (0, 0))).astype(BF16)
            ba = b_alpha[li].reshape(1, GLA_K_WIDTH)
            gain = gla_out_norm[li].reshape(1, GLA_DV)
            wp = w_pool[li].astype(BF16)
            ps = pool_scale[li].reshape(1, POOL_DIM)
            m_p, ng, npool = _odd_prompt(zm_p, zpr_p, wa, ba, gain, wp, ps)
            pg.append(ng)
            pp.append(npool)
            dec_s, q_s = _odd_sample_prep(zm_s, zr_s, wa, ba)
            sg_all, o_raw = _odd_sample_state(
                dec_s, q_s, zm_s[:, GLA_K_WIDTH:2 * GLA_K_WIDTH],
                zm_s[:, 2 * GLA_K_WIDTH:2 * GLA_K_WIDTH + GLA_WIDTH], state_gla, li, sg_all)
            m_s = _odd_sample_post(o_raw.reshape(DEC_BATCH, GLA_WIDTH), zm_s, zp_s,
                                   jnp.swapaxes(state_pool[li], 0, 1), gain, wp, ps)
            sp.append(jnp.concatenate([state_pool[li, :, 1:], zp_s[:, None, :]], axis=1))
        x_p, x_s, h_p, h_s = _proj_res(m_p, m_s, (w_out_bf[layer % 2], li), (x_p, x_s), norm_ffn[layer],
                                       tm=512, name="out_proj")
        a_p, a_s = _dense(h_p, h_s, [(w_gate, layer), (w_up, layer)], n_cols=D_FF, tm=1024, tn=512,
                          mode="swiglu", out_dtype=BF16, name="ffn_up")
        if layer + 1 < DEPTH:
            x_p, x_s, h_p, h_s = _proj_res(a_p, a_s, (w_down_bf, layer), (x_p, x_s),
                                           norm_mix[layer + 1], tm=256, name="ffn_down")
        else:
            x_p, x_s = _proj_res(a_p, a_s, (w_down_bf, layer), (x_p, x_s), None,
                                 tm=256, name="ffn_down_last")

    st = lambda parts: jnp.stack(parts)
    cache5 = lambda a: a.reshape(N_EVEN, DEC_BATCH, lc, A_KV_HEADS, HEAD_DIM)
    return (x_p.reshape(BATCH, SEQ, D_MODEL), x_s.reshape(DEC_BATCH, 1, D_MODEL),
            st(pk), st(pv), st(pc), st(pg), st(pp), cache5(sk_all), cache5(sv_all), st(sc), sg_all, st(sp))
```

```python
import functools

import jax
import jax.numpy as jnp
from jax import lax
from jax.experimental import pallas as pl
from jax.experimental.pallas import tpu as pltpu

F32 = jnp.float32
BF16 = jnp.bfloat16

D_MODEL = 2048
BATCH = 4
SEQ = 2048
DEPTH = 4
DEC_BATCH = 32
PAST_LEN = 16384
N_EVEN = 2
N_ODD = 2
EPS = 1e-6
NEG_INF = -1e30
A_HEADS = 16
A_KV_HEADS = 4
HEAD_DIM = 64
A_WIDTH = A_HEADS * HEAD_DIM
A_KV_WIDTH = A_KV_HEADS * HEAD_DIM
WINDOW = 128
ROPE_THETA = 10000.0
CONV_DIM = D_MODEL // 2
CONV_W = 3
GLA_HEADS = 4
GLA_WIDTH = D_MODEL // 2
GLA_DV = GLA_WIDTH // GLA_HEADS
GLA_DK = GLA_DV // 2
GLA_K_WIDTH = GLA_HEADS * GLA_DK
GLA_RANK = 16
GLA_TAU = 16.0
GLA_CHUNK = 64
POOL_DIM = D_MODEL // 2
POOL_WINDOWS = (2, 4, 8, 16)
POOL_GDIM = POOL_DIM // 4
POOL_BUF = 15
D_FF = 5632
EVEN_IN = A_WIDTH + 2 * A_KV_WIDTH + 3 * CONV_DIM
ODD_MAIN = 2 * GLA_K_WIDTH + 2 * GLA_WIDTH

M_PROMPT = BATCH * SEQ
LANES = 128
BLK = 128
NBLK = SEQ // BLK
POOL_RANK_W = POOL_DIM + LANES
VMEM_LIMIT = 58 * 1024 * 1024


def _cparams(*sem):
    return pltpu.CompilerParams(dimension_semantics=sem, vmem_limit_bytes=VMEM_LIMIT)


def _dot(a, b):
    return jnp.dot(a, b, preferred_element_type=F32)


def _dot_nt(a, b):
    return lax.dot_general(a, b, (((1,), (1,)), ((), ())), preferred_element_type=F32)


def _dot_tn(a, b):
    return lax.dot_general(a, b, (((0,), (0,)), ((), ())), preferred_element_type=F32)


def _silu(x):
    return x * (1.0 / (1.0 + jnp.exp(-x)))


def _rmsnorm_rows(x, gain):
    ms = jnp.mean(x * x, axis=-1, keepdims=True)
    return (x * lax.rsqrt(ms + EPS)) * gain


def _rmsnorm_kernel(xp_ref, xs_ref, g_ref, op_ref, os_ref, *, n_i):
    op_ref[...] = _rmsnorm_rows(xp_ref[...], g_ref[...]).astype(op_ref.dtype)

    @pl.when(pl.program_id(0) == n_i - 1)
    def _():
        os_ref[...] = _rmsnorm_rows(xs_ref[...], g_ref[...]).astype(os_ref.dtype)


def _rmsnorm(x_p, x_s, gain, *, tm=1024):
    n_i = M_PROMPT // tm
    row = lambda i: (i, 0)
    fixed = lambda i: (0, 0)
    return pl.pallas_call(
        functools.partial(_rmsnorm_kernel, n_i=n_i),
        grid=(n_i,),
        in_specs=[pl.BlockSpec((tm, D_MODEL), row),
                  pl.BlockSpec((DEC_BATCH, D_MODEL), fixed),
                  pl.BlockSpec((1, D_MODEL), fixed)],
        out_specs=[pl.BlockSpec((tm, D_MODEL), row),
                   pl.BlockSpec((DEC_BATCH, D_MODEL), fixed)],
        out_shape=[jax.ShapeDtypeStruct((M_PROMPT, D_MODEL), BF16),
                   jax.ShapeDtypeStruct((DEC_BATCH, D_MODEL), BF16)],
        compiler_params=_cparams("arbitrary"),
        name="rmsnorm",
    )(x_p, x_s, gain.reshape(1, D_MODEL))


XPOSE_COLS = 128


def _dense_kernel(*refs, n_w, mode, n_i, transposed, side_cast):
    a_p, a_s = refs[0], refs[1]
    w = refs[2:2 + n_w]
    pos = 2 + n_w
    side_in = side_out = None
    if side_cast:
        side_in = refs[pos]
        pos += 1
    o_p, o_s = refs[pos], refs[pos + 1]
    pos += 2
    if side_cast:
        side_out = refs[pos]
        pos += 1
    wbf = refs[pos:pos + n_w]
    i = pl.program_id(1)

    @pl.when(i == 0)
    def _():
        for k in range(n_w):
            if transposed:
                tn = wbf[k].shape[1]
                for c in range(0, tn, XPOSE_COLS):
                    wbf[k][:, c:c + XPOSE_COLS] = w[k][c:c + XPOSE_COLS, :].T.astype(BF16)
            else:
                wbf[k][...] = w[k][...].astype(BF16)
        if side_cast:
            side_out[...] = side_in[...].astype(BF16)

    def run(a_ref, o_ref):
        a = a_ref[...]
        if mode == "swiglu":
            y = _silu(_dot(a, wbf[0][...])) * _dot(a, wbf[1][...])
        else:
            y = _dot(a, wbf[0][...])
        o_ref[...] = y.astype(o_ref.dtype)

    run(a_p, o_p)

    @pl.when(i == n_i - 1)
    def _():
        run(a_s, o_s)


def _dense(a_p, a_s, weights, *, n_cols, tm, tn, mode="plain", out_dtype=F32, transposed=False,
           side_cast=None, name):
    k_dim = a_p.shape[1]
    n_i = M_PROMPT // tm
    n_j = n_cols // tn
    row = lambda j, i: (i, 0)
    fixed = lambda j, i: (0, 0)
    tile = lambda j, i: (i, j)
    panel = lambda j, i: (0, j)
    in_specs = [pl.BlockSpec((tm, k_dim), row), pl.BlockSpec((DEC_BATCH, k_dim), fixed)]
    args = [a_p, a_s]
    for arr, layer in weights:
        if transposed:
            in_specs.append(pl.BlockSpec((None, tn, k_dim), lambda j, i, layer=layer: (layer, j, 0)))
        else:
            in_specs.append(pl.BlockSpec((None, k_dim, tn), lambda j, i, layer=layer: (layer, 0, j)))
        args.append(arr)
    out_specs = [pl.BlockSpec((tm, tn), tile), pl.BlockSpec((DEC_BATCH, tn), panel)]
    out_shape = [jax.ShapeDtypeStruct((M_PROMPT, n_cols), out_dtype),
                 jax.ShapeDtypeStruct((DEC_BATCH, n_cols), out_dtype)]
    if side_cast is not None:
        s_arr, s_layer = side_cast
        slab = s_arr.shape[1] // n_j
        in_specs.append(pl.BlockSpec((None, slab, s_arr.shape[2]), lambda j, i: (s_layer, j, 0)))
        args.append(s_arr)
        out_specs.append(pl.BlockSpec((slab, s_arr.shape[2]), lambda j, i: (j, 0)))
        out_shape.append(jax.ShapeDtypeStruct(s_arr.shape[1:], BF16))
    n_w = len(weights)
    return pl.pallas_call(
        functools.partial(_dense_kernel, n_w=n_w, mode=mode, n_i=n_i, transposed=transposed,
                          side_cast=side_cast is not None),
        grid=(n_j, n_i),
        in_specs=in_specs,
        out_specs=out_specs,
        out_shape=out_shape,
        scratch_shapes=[pltpu.VMEM((k_dim, tn), BF16) for _ in range(n_w)],
        compiler_params=_cparams("arbitrary", "arbitrary"),
        name=name,
    )(*args)


def _proj_res_kernel(*refs, n_i, with_norm):
    a_p, a_s, w_ref, r_p, r_s = refs[:5]
    if with_norm:
        g_ref, x_p, x_s, h_p, h_s = refs[5:]
    else:
        x_p, x_s = refs[5:]
        g_ref = h_p = h_s = None

    def run(a_ref, r_ref, x_ref, h_ref):
        x = r_ref[...] + _dot(a_ref[...], w_ref[...])
        x_ref[...] = x
        if with_norm:
            h_ref[...] = _rmsnorm_rows(x, g_ref[...]).astype(h_ref.dtype)

    run(a_p, r_p, x_p, h_p)

    @pl.when(pl.program_id(0) == n_i - 1)
    def _():
        run(a_s, r_s, x_s, h_s)


def _proj_res(a_p, a_s, weight, res, gain, *, tm, name):
    w_bf, layer = weight
    k_dim = a_p.shape[1]
    if layer is None:
        w_spec = pl.BlockSpec((k_dim, D_MODEL), lambda i: (0, 0), pipeline_mode=pl.Buffered(1))
    else:
        w_spec = pl.BlockSpec((None, k_dim, D_MODEL), lambda i: (layer, 0, 0), pipeline_mode=pl.Buffered(1))
    n_i = M_PROMPT // tm
    with_norm = gain is not None
    row = lambda i: (i, 0)
    fixed = lambda i: (0, 0)
    rows_p = pl.BlockSpec((tm, D_MODEL), row)
    rows_s = pl.BlockSpec((DEC_BATCH, D_MODEL), fixed)
    in_specs = [pl.BlockSpec((tm, k_dim), row), pl.BlockSpec((DEC_BATCH, k_dim), fixed),
                w_spec, rows_p, rows_s]
    args = [a_p, a_s, w_bf, res[0], res[1]]
    out_specs = [rows_p, rows_s]
    out_shape = [jax.ShapeDtypeStruct((M_PROMPT, D_MODEL), F32), jax.ShapeDtypeStruct((DEC_BATCH, D_MODEL), F32)]
    if with_norm:
        in_specs.append(pl.BlockSpec((1, D_MODEL), fixed))
        args.append(gain.reshape(1, D_MODEL))
        out_specs += [rows_p, rows_s]
        out_shape += [jax.ShapeDtypeStruct((M_PROMPT, D_MODEL), BF16),
                      jax.ShapeDtypeStruct((DEC_BATCH, D_MODEL), BF16)]
    return pl.pallas_call(
        functools.partial(_proj_res_kernel, n_i=n_i, with_norm=with_norm),
        grid=(n_i,),
        in_specs=in_specs,
        out_specs=out_specs,
        out_shape=out_shape,
        compiler_params=_cparams("arbitrary"),
        name=name,
    )(*args)


def _split_cat(x):
    hi = x.astype(BF16)
    lo = (x - hi.astype(F32)).astype(BF16)
    return jnp.concatenate([hi, lo], axis=1)


def _rope_mats():
    j = jnp.arange(2 * LANES)[:, None] % LANES
    l = jnp.arange(LANES)[None, :]
    head_sum = (j // HEAD_DIM == l // HEAD_DIM).astype(BF16)
    src = jnp.where(l % HEAD_DIM < HEAD_DIM // 2, l + HEAD_DIM // 2, l - HEAD_DIM // 2)
    half_swap = (j == src).astype(BF16)
    return head_sum, half_swap


def _norm_rope_chunk(xc, gain, cos, sin, hs_ref, sw_ref):
    ms = _dot(_split_cat(xc * xc), hs_ref[...]) * (1.0 / HEAD_DIM)
    y = (xc * lax.rsqrt(ms + EPS)) * gain
    swapped = _dot(_split_cat(y), sw_ref[...])
    return y * cos + swapped * sin


def _spread_heads(xc, own_lo):
    lane = lax.broadcasted_iota(jnp.int32, xc.shape, 1)
    keep = (lane < HEAD_DIM) if own_lo else (lane >= HEAD_DIM)
    nat = jnp.where(keep, xc, 0.0)
    rol = pltpu.roll(nat, HEAD_DIM, axis=1)
    parts = (nat, rol) if own_lo else (rol, nat)
    return jnp.concatenate(parts, axis=0).astype(BF16)


def _attn_core(qs, kfull, vfull, mask_t, sink_ref):
    rows = qs[0].shape[0]
    nkeys = kfull.shape[0]
    outs = [None] * 8
    for kh in range(A_KV_HEADS):
        c0 = (kh // 2) * LANES
        kk = _spread_heads(kfull[:, c0:c0 + LANES], kh % 2 == 0)
        vv = _spread_heads(vfull[:, c0:c0 + LANES], kh % 2 == 0)
        lhs = jnp.concatenate([qs[2 * kh], qs[2 * kh + 1]], axis=0)
        s = _dot_nt(kk, lhs)
        prow = []
        for half in range(2):
            pcol = []
            for cc in range(2):
                sb = s[half * nkeys:(half + 1) * nkeys, cc * rows:(cc + 1) * rows]
                sb = jnp.where(mask_t, sb, NEG_INF)
                sink = sink_ref[kh * 4 + 2 * cc + half]
                m = jnp.maximum(jnp.max(sb, axis=0, keepdims=True), sink)
                e = jnp.exp(sb - m)
                den = jnp.sum(e, axis=0, keepdims=True) + jnp.exp(sink - m)
                pcol.append((e / den).astype(BF16))
            prow.append(jnp.concatenate(pcol, axis=1))
        p = jnp.concatenate(prow, axis=0)
        o = _dot_tn(p, vv)
        outs[2 * kh] = o[0:rows]
        outs[2 * kh + 1] = o[rows:2 * rows]
    return outs


def _even_prompt_kernel(sink_ref, z_ref, cos_ref, sin_ref, qg_ref, kg_ref, cw_ref, hs_ref, sw_ref,
                        o_ref, nk_ref, nv_ref, nc_ref, kf_ref, vf_ref, ub_ref):
    i = pl.program_id(1)

    @pl.when(i == 0)
    def _():
        kf_ref[...] = jnp.zeros_like(kf_ref)
        vf_ref[...] = jnp.zeros_like(vf_ref)
        ub_ref[0:8, :] = jnp.zeros((8, CONV_DIM), F32)

    kf_ref[0:BLK, :] = kf_ref[BLK:2 * BLK, :]
    vf_ref[0:BLK, :] = vf_ref[BLK:2 * BLK, :]
    cos = cos_ref[...]
    sin = sin_ref[...]
    k = jnp.concatenate(
        [_norm_rope_chunk(z_ref[:, A_WIDTH + c * LANES:A_WIDTH + (c + 1) * LANES], kg_ref[...], cos, sin,
                          hs_ref, sw_ref)
         for c in range(A_KV_WIDTH // LANES)], axis=1)
    v = z_ref[:, A_WIDTH + A_KV_WIDTH:A_WIDTH + 2 * A_KV_WIDTH]
    kf_ref[BLK:2 * BLK, :] = k
    vf_ref[BLK:2 * BLK, :] = v
    qs = [(_norm_rope_chunk(z_ref[:, c * LANES:(c + 1) * LANES], qg_ref[...], cos, sin, hs_ref, sw_ref)
           * (HEAD_DIM ** -0.5)).astype(BF16) for c in range(A_WIDTH // LANES)]
    kk = lax.broadcasted_iota(jnp.int32, (2 * BLK, BLK), 0)
    r = lax.broadcasted_iota(jnp.int32, (2 * BLK, BLK), 1)
    d = kk - r
    mask_t = (d >= 0) & (d <= WINDOW) & ((kk >= BLK) | (i > 0))
    outs = _attn_core(qs, kf_ref[...], vf_ref[...], mask_t, sink_ref)
    for c in range(A_WIDTH // LANES):
        o_ref[:, c * LANES:(c + 1) * LANES] = outs[c].astype(o_ref.dtype)

    o0 = A_WIDTH + 2 * A_KV_WIDTH
    u = z_ref[:, o0 + CONV_DIM:o0 + 2 * CONV_DIM] * z_ref[:, o0 + 2 * CONV_DIM:o0 + 3 * CONV_DIM]
    ub_ref[8:8 + BLK, :] = u
    y = ub_ref[6:6 + BLK, :] * cw_ref[0:1, :]
    y = y + ub_ref[7:7 + BLK, :] * cw_ref[1:2, :]
    y = y + ub_ref[8:8 + BLK, :] * cw_ref[2:3, :]
    o_ref[:, A_WIDTH:A_WIDTH + CONV_DIM] = (z_ref[:, o0:o0 + CONV_DIM] * y).astype(o_ref.dtype)
    ub_ref[0:8, :] = ub_ref[BLK:BLK + 8, :]

    @pl.when(i == pl.num_programs(1) - 1)
    def _():
        nk_ref[...] = k
        nv_ref[...] = v
        nc_ref[...] = ub_ref[BLK + 6:BLK + 8, :]


def _even_prompt(z_p, cos, sin, qg, kg, cw, sinks, mats):
    blk = lambda b, i: (b * NBLK + i, 0)
    fixed = lambda b, i: (0, 0)
    per_b = lambda b, i: (b, 0, 0)
    return pl.pallas_call(
        _even_prompt_kernel,
        grid=(BATCH, NBLK),
        in_specs=[pl.BlockSpec(memory_space=pltpu.SMEM),
                  pl.BlockSpec((BLK, EVEN_IN), blk),
                  pl.BlockSpec((BLK, LANES), lambda b, i: (i, 0)),
                  pl.BlockSpec((BLK, LANES), lambda b, i: (i, 0)),
                  pl.BlockSpec((1, LANES), fixed),
                  pl.BlockSpec((1, LANES), fixed),
                  pl.BlockSpec((CONV_W, CONV_DIM), fixed),
                  pl.BlockSpec((2 * LANES, LANES), fixed),
                  pl.BlockSpec((2 * LANES, LANES), fixed)],
        out_specs=[pl.BlockSpec((BLK, D_MODEL), blk),
                   pl.BlockSpec((None, BLK, A_KV_WIDTH), per_b),
                   pl.BlockSpec((None, BLK, A_KV_WIDTH), per_b),
                   pl.BlockSpec((None, CONV_W - 1, CONV_DIM), per_b)],
        out_shape=[jax.ShapeDtypeStruct((M_PROMPT, D_MODEL), BF16),
                   jax.ShapeDtypeStruct((BATCH, BLK, A_KV_WIDTH), F32),
                   jax.ShapeDtypeStruct((BATCH, BLK, A_KV_WIDTH), F32),
                   jax.ShapeDtypeStruct((BATCH, CONV_W - 1, CONV_DIM), F32)],
        scratch_shapes=[pltpu.VMEM((2 * BLK, A_KV_WIDTH), F32),
                        pltpu.VMEM((2 * BLK, A_KV_WIDTH), F32),
                        pltpu.VMEM((BLK + 8, CONV_DIM), F32)],
        compiler_params=_cparams("arbitrary", "arbitrary"),
        name="even_prompt",
    )(sinks, z_p, cos, sin, qg, kg, cw, *mats)


def _even_sample_prep_kernel(z_ref, cos_ref, sin_ref, qg_ref, kg_ref, c0_ref, c1_ref, cw_ref, hs_ref, sw_ref,
                             q_ref, k_ref, u_ref, o_ref):
    cos = cos_ref[...]
    sin = sin_ref[...]
    for c in range(A_WIDTH // LANES):
        q = _norm_rope_chunk(z_ref[:, c * LANES:(c + 1) * LANES], qg_ref[...], cos, sin, hs_ref, sw_ref)
        q_ref[:, c * LANES:(c + 1) * LANES] = (q * (HEAD_DIM ** -0.5)).astype(q_ref.dtype)
    for c in range(A_KV_WIDTH // LANES):
        k_ref[:, c * LANES:(c + 1) * LANES] = _norm_rope_chunk(
            z_ref[:, A_WIDTH + c * LANES:A_WIDTH + (c + 1) * LANES], kg_ref[...], cos, sin, hs_ref, sw_ref)
    o0 = A_WIDTH + 2 * A_KV_WIDTH
    u = z_ref[:, o0 + CONV_DIM:o0 + 2 * CONV_DIM] * z_ref[:, o0 + 2 * CONV_DIM:o0 + 3 * CONV_DIM]
    u_ref[...] = u
    y = c0_ref[...] * cw_ref[0:1, :]
    y = y + c1_ref[...] * cw_ref[1:2, :]
    y = y + u * cw_ref[2:3, :]
    o_ref[...] = (z_ref[:, o0:o0 + CONV_DIM] * y).astype(o_ref.dtype)


def _even_sample_prep(z_s, cos, sin, qg, kg, c0, c1, cw, mats):
    return pl.pallas_call(
        _even_sample_prep_kernel,
        out_shape=[jax.ShapeDtypeStruct((DEC_BATCH, A_WIDTH), BF16),
                   jax.ShapeDtypeStruct((DEC_BATCH, A_KV_WIDTH), F32),
                   jax.ShapeDtypeStruct((DEC_BATCH, CONV_DIM), F32),
                   jax.ShapeDtypeStruct((DEC_BATCH, CONV_DIM), BF16)],
        compiler_params=pltpu.CompilerParams(vmem_limit_bytes=VMEM_LIMIT),
        name="even_sample_prep",
    )(z_s, cos, sin, qg, kg, c0, c1, cw, *mats)


SROWS = LANES


def _even_sample_attn_kernel(*refs):
    sink_ref, q_ref, kn_ref, vn_ref, kc_ref, vc_ref = refs[:6]
    o_ref, nk_ref, nv_ref = refs[-3:]
    lc = kc_ref.shape[0]
    row = lax.broadcasted_iota(jnp.int32, (lc, A_KV_WIDTH), 0)
    kfull = jnp.concatenate([kc_ref[...], jnp.where(row == 0, kn_ref[...], 0.0)], axis=0)
    vfull = jnp.concatenate([vc_ref[...], jnp.where(row == 0, vn_ref[...], 0.0)], axis=0)
    qs = [jnp.broadcast_to(q_ref[:, c * LANES:(c + 1) * LANES], (SROWS, LANES))
          for c in range(A_WIDTH // LANES)]
    kk = lax.broadcasted_iota(jnp.int32, (2 * lc, SROWS), 0)
    mask_t = (kk <= lc) & (lc - kk <= WINDOW)
    outs = _attn_core(qs, kfull, vfull, mask_t, sink_ref)
    for c in range(A_WIDTH // LANES):
        o_ref[:, c * LANES:(c + 1) * LANES] = outs[c][0:1].astype(o_ref.dtype)
    nk_ref[0:lc - 1, :] = kc_ref[1:lc, :]
    nk_ref[lc - 1:lc, :] = kn_ref[...]
    nv_ref[0:lc - 1, :] = vc_ref[1:lc, :]
    nv_ref[lc - 1:lc, :] = vn_ref[...]


def _even_sample_attn(sinks, q_s, k_s, v_s, k_cache, v_cache, li, prev):
    lc = k_cache.shape[2]
    vec = lambda w: pl.BlockSpec((None, 1, w), lambda b: (b, 0, 0))
    cache = pl.BlockSpec((None, None, lc, A_KV_WIDTH), lambda b: (li, b, 0, 0))
    in_specs = [pl.BlockSpec(memory_space=pltpu.SMEM), vec(A_WIDTH), vec(A_KV_WIDTH), vec(A_KV_WIDTH),
                cache, cache]
    args = [sinks, q_s.reshape(DEC_BATCH, 1, A_WIDTH), k_s.reshape(DEC_BATCH, 1, A_KV_WIDTH),
            v_s.reshape(DEC_BATCH, 1, A_KV_WIDTH), k_cache, v_cache]
    aliases = {}
    if prev is not None:
        aliases = {len(args): 1, len(args) + 1: 2}
        in_specs += [pl.BlockSpec(memory_space=pl.ANY)] * 2
        args += list(prev)
    return pl.pallas_call(
        _even_sample_attn_kernel,
        grid=(DEC_BATCH,),
        in_specs=in_specs,
        out_specs=[vec(A_WIDTH), cache, cache],
        out_shape=[jax.ShapeDtypeStruct((DEC_BATCH, 1, A_WIDTH), BF16),
                   jax.ShapeDtypeStruct(k_cache.shape, F32),
                   jax.ShapeDtypeStruct(v_cache.shape, F32)],
        input_output_aliases=aliases,
        compiler_params=_cparams("arbitrary"),
        name="even_sample_attn",
    )(*args)


def _log_decay(zr, wa_ref, ba_ref):
    pre = _dot(zr.astype(BF16), wa_ref[...]) + ba_ref[...]
    return (jnp.minimum(pre, 0.0) - jnp.log1p(jnp.exp(-jnp.abs(pre)))) * (1.0 / GLA_TAU)


def _split_bf16(x):
    hi = x.astype(BF16)
    lo = (x - hi.astype(F32)).astype(BF16)
    return hi, lo


def _row_to_col(row):
    n = row.shape[1]
    r = lax.broadcasted_iota(jnp.int32, (n, n), 0)
    c = lax.broadcasted_iota(jnp.int32, (n, n), 1)
    return jnp.sum(jnp.where(r == c, jnp.broadcast_to(row, (n, n)), 0.0), axis=-1, keepdims=True)


def _head_rmsnorm_gate(o, gain, zg):
    ms = jnp.mean(o * o, axis=-1, keepdims=True)
    return ((o * lax.rsqrt(ms + EPS)) * gain) * _silu(zg)


def _odd_prompt_kernel(zm_ref, zpr_ref, wa_ref, ba_ref, gain_ref, wp_ref, ps_ref,
                       o_ref, s_ref, np_ref, pbh_ref, pbl_ref):
    i = pl.program_id(1)

    @pl.when(i == 0)
    def _():
        s_ref[...] = jnp.zeros_like(s_ref)
        pbh_ref[0:BLK, :] = jnp.zeros((BLK, POOL_DIM), BF16)
        pbl_ref[0:BLK, :] = jnp.zeros((BLK, POOL_DIM), BF16)

    n_ch = BLK // GLA_CHUNK
    gk = _log_decay(zpr_ref[:, POOL_DIM:POOL_RANK_W], wa_ref, ba_ref)
    r = lax.broadcasted_iota(jnp.int32, (BLK, BLK), 0)
    c = lax.broadcasted_iota(jnp.int32, (BLK, BLK), 1)
    causal = (r >= c) & (r // GLA_CHUNK == c // GLA_CHUNK)
    tri = jnp.where(causal, 1.0, 0.0).astype(BF16)
    g_hi, g_lo = _split_bf16(gk)
    b = _dot(jnp.concatenate([tri, tri], axis=1), jnp.concatenate([g_hi, g_lo], axis=0))
    b_last = [b[(ch + 1) * GLA_CHUNK - 1:(ch + 1) * GLA_CHUNK, :] for ch in range(n_ch)]
    b_end = jnp.concatenate([jnp.broadcast_to(bl, (GLA_CHUNK, GLA_K_WIDTH)) for bl in b_last], axis=0)
    o_k, o_v, o_g = GLA_K_WIDTH, 2 * GLA_K_WIDTH, 2 * GLA_K_WIDTH + GLA_WIDTH
    zq = zm_ref[:, 0:GLA_K_WIDTH]
    zk = zm_ref[:, o_k:o_k + GLA_K_WIDTH]
    qd = ((zq * (GLA_DK ** -0.5)) * jnp.exp(b)).astype(BF16)
    kd = (zk * jnp.exp(-b)).astype(BF16)
    k2 = (zk * jnp.exp(b_end - b)).astype(BF16)
    sub = lax.broadcasted_iota(jnp.int32, (8, LANES), 0)
    sel = jnp.where(sub < 2, 1.0, 0.0).astype(BF16)
    hk = [slice(h * GLA_DK, (h + 1) * GLA_DK) for h in range(GLA_HEADS)]
    chunk = [slice(ch * GLA_CHUNK, (ch + 1) * GLA_CHUNK) for ch in range(n_ch)]
    v_hs = [zm_ref[:, o_v + h * GLA_DV:o_v + (h + 1) * GLA_DV].astype(BF16) for h in range(GLA_HEADS)]
    atts = [_dot_nt(qd[:, hk[h]], kd[:, hk[h]]) for h in range(GLA_HEADS)]
    log_decs = []
    for h in range(GLA_HEADS):
        for ch in range(n_ch):
            bl = b_last[ch][:, hk[h]]
            bl_hi = bl.astype(BF16).astype(F32)
            rows8 = jnp.where(sub == 0, bl_hi, jnp.where(sub == 1, bl - bl_hi, 0.0)).astype(BF16)
            log_decs.append(_dot_tn(rows8, sel))
    deltas = [[_dot_tn(k2[chunk[ch], hk[h]], v_hs[h][chunk[ch]]) for ch in range(n_ch)]
              for h in range(GLA_HEADS)]
    o_intras = [_dot(jnp.where(causal, atts[h], 0.0).astype(BF16), v_hs[h]) for h in range(GLA_HEADS)]
    o_inters = []
    for h in range(GLA_HEADS):
        s_h = s_ref[h]
        parts = []
        for ch in range(n_ch):
            parts.append(_dot(qd[chunk[ch], hk[h]], s_h.astype(BF16)))
            dec = jnp.exp(log_decs[h * n_ch + ch])
            s_h = jnp.concatenate([dec, dec], axis=1) * s_h + deltas[h][ch]
        s_ref[h] = s_h
        o_inters.append(jnp.concatenate(parts, axis=0))
    for h in range(GLA_HEADS):
        vs = slice(h * GLA_DV, (h + 1) * GLA_DV)
        zg = zm_ref[:, o_g + h * GLA_DV:o_g + (h + 1) * GLA_DV]
        o_ref[:, vs] = _head_rmsnorm_gate(o_intras[h] + o_inters[h], gain_ref[...], zg).astype(o_ref.dtype)

    x = zpr_ref[:, 0:POOL_DIM]
    x_hi, x_lo = _split_bf16(x)
    pbh_ref[BLK:2 * BLK, :] = x_hi
    pbl_ref[BLK:2 * BLK, :] = x_lo
    t = lax.broadcasted_iota(jnp.int32, (BLK, 2 * BLK), 0)
    j = lax.broadcasted_iota(jnp.int32, (BLK, 2 * BLK), 1)
    back = t + BLK - j
    t_glob = i * BLK + lax.broadcasted_iota(jnp.int32, (BLK, 1), 0)
    groups = [slice(g * POOL_GDIM, (g + 1) * POOL_GDIM) for g in range(len(POOL_WINDOWS))]
    sums = []
    for g, w in enumerate(POOL_WINDOWS):
        band = jnp.where((back >= 0) & (back < w), 1.0, 0.0).astype(BF16)
        sums.append(_dot(jnp.concatenate([band, band], axis=1),
                         jnp.concatenate([pbh_ref[:, groups[g]], pbl_ref[:, groups[g]]], axis=0)))
    ys = []
    for g, w in enumerate(POOL_WINDOWS):
        inv_cnt = 1.0 / jnp.minimum(w, t_glob + 1).astype(F32)
        d = sums[g] * inv_cnt - x[:, groups[g]]
        ys.append(_dot(d.astype(BF16), wp_ref[g]))
    for g in range(len(POOL_WINDOWS)):
        o_ref[:, GLA_WIDTH + g * POOL_GDIM:GLA_WIDTH + (g + 1) * POOL_GDIM] = (
            ys[g] * ps_ref[:, groups[g]]).astype(o_ref.dtype)
    pbh_ref[0:BLK, :] = x_hi
    pbl_ref[0:BLK, :] = x_lo

    @pl.when(i == pl.num_programs(1) - 1)
    def _():
        np_ref[...] = x[BLK - POOL_BUF:BLK, :]


def _odd_prompt(zm_p, zpr_p, wa, ba, gain, wp, ps):
    blk = lambda b, i: (b * NBLK + i, 0)
    fixed2 = lambda b, i: (0, 0)
    return pl.pallas_call(
        _odd_prompt_kernel,
        grid=(BATCH, NBLK),
        in_specs=[pl.BlockSpec((BLK, ODD_MAIN), blk),
                  pl.BlockSpec((BLK, POOL_RANK_W), blk),
                  pl.BlockSpec((LANES, GLA_K_WIDTH), fixed2),
                  pl.BlockSpec((1, GLA_K_WIDTH), fixed2),
                  pl.BlockSpec((1, GLA_DV), fixed2),
                  pl.BlockSpec((4, POOL_GDIM, POOL_GDIM), lambda b, i: (0, 0, 0)),
                  pl.BlockSpec((1, POOL_DIM), fixed2)],
        out_specs=[pl.BlockSpec((BLK, D_MODEL), blk),
                   pl.BlockSpec((None, GLA_HEADS, GLA_DK, GLA_DV), lambda b, i: (b, 0, 0, 0)),
                   pl.BlockSpec((None, POOL_BUF, POOL_DIM), lambda b, i: (b, 0, 0))],
        out_shape=[jax.ShapeDtypeStruct((M_PROMPT, D_MODEL), BF16),
                   jax.ShapeDtypeStruct((BATCH, GLA_HEADS, GLA_DK, GLA_DV), F32),
                   jax.ShapeDtypeStruct((BATCH, POOL_BUF, POOL_DIM), F32)],
        scratch_shapes=[pltpu.VMEM((2 * BLK, POOL_DIM), BF16), pltpu.VMEM((2 * BLK, POOL_DIM), BF16)],
        compiler_params=_cparams("arbitrary", "arbitrary"),
        name="odd_prompt",
    )(zm_p, zpr_p, wa, ba, gain, wp, ps)


def _odd_sample_prep_kernel(zm_ref, zr_ref, wa_ref, ba_ref, dec_ref, q_ref):
    dec_ref[...] = jnp.exp(_log_decay(zr_ref[...], wa_ref, ba_ref))
    q_ref[...] = zm_ref[:, 0:GLA_K_WIDTH] * (GLA_DK ** -0.5)


def _odd_sample_prep(zm_s, zr_s, wa, ba):
    return pl.pallas_call(
        _odd_sample_prep_kernel,
        out_shape=[jax.ShapeDtypeStruct((DEC_BATCH, GLA_K_WIDTH), F32),
                   jax.ShapeDtypeStruct((DEC_BATCH, GLA_K_WIDTH), F32)],
        compiler_params=pltpu.CompilerParams(vmem_limit_bytes=VMEM_LIMIT),
        name="odd_sample_prep",
    )(zm_s, zr_s, wa, ba)


STATE_BB = 4


def _odd_sample_state_kernel(*refs):
    dec_ref, q_ref, k_ref, v_ref, s_ref = refs[:5]
    ns_ref, o_ref = refs[-2:]
    for bb in range(STATE_BB):
        for h in range(GLA_HEADS):
            ks = slice(h * GLA_DK, (h + 1) * GLA_DK)
            vs = slice(h * GLA_DV, (h + 1) * GLA_DV)
            s_new = (_row_to_col(dec_ref[bb][:, ks]) * s_ref[bb, h]
                     + _row_to_col(k_ref[bb][:, ks]) * v_ref[bb][:, vs])
            ns_ref[bb, h] = s_new
            o_ref[bb, :, vs] = jnp.sum(_row_to_col(q_ref[bb][:, ks]) * s_new, axis=0, keepdims=True)


def _odd_sample_state(dec, q, k, v, state, li, prev):
    vec = lambda w: pl.BlockSpec((STATE_BB, 1, w), lambda b: (b, 0, 0))
    st = pl.BlockSpec((None, STATE_BB, GLA_HEADS, GLA_DK, GLA_DV), lambda b: (li, b, 0, 0, 0))
    r3 = lambda a: a.reshape(DEC_BATCH, 1, a.shape[-1])
    in_specs = [vec(GLA_K_WIDTH), vec(GLA_K_WIDTH), vec(GLA_K_WIDTH), vec(GLA_WIDTH), st]
    args = [r3(dec), r3(q), r3(k), r3(v), state]
    aliases = {}
    if prev is not None:
        aliases = {len(args): 0}
        in_specs.append(pl.BlockSpec(memory_space=pl.ANY))
        args.append(prev)
    return pl.pallas_call(
        _odd_sample_state_kernel,
        grid=(DEC_BATCH // STATE_BB,),
        in_specs=in_specs,
        out_specs=[st, vec(GLA_WIDTH)],
        out_shape=[jax.ShapeDtypeStruct(state.shape, F32),
                   jax.ShapeDtypeStruct((DEC_BATCH, 1, GLA_WIDTH), F32)],
        input_output_aliases=aliases,
        compiler_params=_cparams("arbitrary"),
        name="odd_sample_state",
    )(*args)


def _odd_sample_post_kernel(o_in_ref, zm_ref, zp_ref, hist_ref, gain_ref, wp_ref, ps_ref, o_ref):
    o_g = 2 * GLA_K_WIDTH + GLA_WIDTH
    for h in range(GLA_HEADS):
        vs = slice(h * GLA_DV, (h + 1) * GLA_DV)
        zg = zm_ref[:, o_g + h * GLA_DV:o_g + (h + 1) * GLA_DV]
        o_ref[:, vs] = _head_rmsnorm_gate(o_in_ref[:, vs], gain_ref[...], zg).astype(o_ref.dtype)
    n_prev = hist_ref.shape[0]
    for g, w in enumerate(POOL_WINDOWS):
        gs = slice(g * POOL_GDIM, (g + 1) * POOL_GDIM)
        x = zp_ref[:, gs]
        s = hist_ref[n_prev - (w - 1)][:, gs]
        for jj in range(n_prev - (w - 1) + 1, n_prev):
            s = s + hist_ref[jj][:, gs]
        s = s + x
        d = s / float(min(w, n_prev + 1)) - x
        y = _dot(d.astype(BF16), wp_ref[g]) * ps_ref[:, gs]
        o_ref[:, GLA_WIDTH + g * POOL_GDIM:GLA_WIDTH + (g + 1) * POOL_GDIM] = y.astype(o_ref.dtype)


def _odd_sample_post(o_raw, zm_s, zp_s, hist_t, gain, wp, ps):
    return pl.pallas_call(
        _odd_sample_post_kernel,
        out_shape=jax.ShapeDtypeStruct((DEC_BATCH, D_MODEL), BF16),
        compiler_params=pltpu.CompilerParams(vmem_limit_bytes=VMEM_LIMIT),
        name="odd_sample_post",
    )(o_raw, zm_s, zp_s, hist_t, gain, wp, ps)


def _rope_tables(pos):
    half = HEAD_DIM // 2
    inv = jnp.power(ROPE_THETA, -jnp.arange(half, dtype=F32) / half)
    ang = pos.astype(F32)[:, None] * inv[None, :]
    c, s = jnp.cos(ang), jnp.sin(ang)
    return jnp.tile(c, (1, 4)), jnp.tile(jnp.concatenate([-s, s], axis=1), (1, 2))


def kernel(x_prompt, x_sample, cache_swa_k, cache_swa_v, state_conv, state_gla, state_pool, norm_mix, norm_ffn, w_in_even, w_out_even, q_norm, k_norm, attn_sinks, conv_w, w_in_odd, w_out_odd, w_alpha_up, b_alpha, gla_out_norm, w_pool, pool_scale, w_gate, w_up, w_down):
    lc = cache_swa_k.shape[2]
    x_p = x_prompt.reshape(M_PROMPT, D_MODEL)
    x_s = x_sample.reshape(DEC_BATCH, D_MODEL)
    cos_p, sin_p = _rope_tables(jnp.arange(SEQ))
    cos_s, sin_s = _rope_tables(PAST_LEN + jnp.arange(1))
    mats = _rope_mats()
    kc_all = cache_swa_k.reshape(N_EVEN, DEC_BATCH, lc, A_KV_WIDTH)
    vc_all = cache_swa_v.reshape(N_EVEN, DEC_BATCH, lc, A_KV_WIDTH)
    w_out_bf = (w_out_even.astype(BF16), w_out_odd.astype(BF16))
    o_r = ODD_MAIN
    w_odd_t = jnp.swapaxes(w_in_odd, 1, 2)
    w_odd_pr_t = jnp.concatenate([w_odd_t[:, o_r + GLA_RANK:], w_odd_t[:, o_r:o_r + GLA_RANK],
                                  jnp.zeros((N_ODD, LANES - GLA_RANK, D_MODEL), F32)], axis=1)
    pk, pv, pc, pg, pp, sc, sp = ([] for _ in range(7))
    sk_all = sv_all = sg_all = None

    h_p, h_s = _rmsnorm(x_p, x_s, norm_mix[0])
    for layer in range(DEPTH):
        li = layer // 2
        if layer % 2 == 0:
            z_p, z_s = _dense(h_p, h_s, [(w_in_even, li)], n_cols=EVEN_IN, tm=1024, tn=1536, name="in_even")
            qg = jnp.tile(q_norm[li], 2).reshape(1, LANES)
            kg = jnp.tile(k_norm[li], 2).reshape(1, LANES)
            m_p, nk, nv, nc = _even_prompt(z_p, cos_p, sin_p, qg, kg, conv_w[li], attn_sinks[li], mats)
            pk.append(nk.reshape(BATCH, lc, A_KV_HEADS, HEAD_DIM))
            pv.append(nv.reshape(BATCH, lc, A_KV_HEADS, HEAD_DIM))
            pc.append(nc)
            q_s, k_s, u_s, conv_s = _even_sample_prep(
                z_s, cos_s, sin_s, qg, kg, state_conv[li, :, 0], state_conv[li, :, 1], conv_w[li], mats)
            v_s = z_s[:, A_WIDTH + A_KV_WIDTH:A_WIDTH + 2 * A_KV_WIDTH]
            attn_s, sk_all, sv_all = _even_sample_attn(
                attn_sinks[li], q_s, k_s, v_s, kc_all, vc_all, li,
                None if sk_all is None else (sk_all, sv_all))
            sc.append(jnp.stack([state_conv[li, :, 1], u_s], axis=1))
            m_s = jnp.concatenate([attn_s.reshape(DEC_BATCH, A_WIDTH), conv_s], axis=1)
        else:
            zm_p, zm_s = _dense(h_p, h_s, [(w_odd_t, li)], n_cols=ODD_MAIN, tm=512, tn=ODD_MAIN // 2,
                                transposed=True, name="in_odd_main")
            zpr_p, zpr_s = _dense(h_p, h_s, [(w_odd_pr_t, li)], n_cols=POOL_RANK_W, tm=1024, tn=POOL_RANK_W,
                                  transposed=True, name="in_odd_pool_rank")
            zp_s, zr_s = zpr_s[:, :POOL_DIM], zpr_s[:, POOL_DIM:]
            wa = jnp.pad(w_alpha_up[li], ((0, LANES - GLA_RANK), (0, 0))).astype(BF16)
            ba = b_alpha[li].reshape(1, GLA_K_WIDTH)
            gain = gla_out_norm[li].reshape(1, GLA_DV)
            wp = w_pool[li].astype(BF16)
            ps = pool_scale[li].reshape(1, POOL_DIM)
            m_p, ng, npool = _odd_prompt(zm_p, zpr_p, wa, ba, gain, wp, ps)
            pg.append(ng)
            pp.append(npool)
            dec_s, q_s = _odd_sample_prep(zm_s, zr_s, wa, ba)
            sg_all, o_raw = _odd_sample_state(
                dec_s, q_s, zm_s[:, GLA_K_WIDTH:2 * GLA_K_WIDTH],
                zm_s[:, 2 * GLA_K_WIDTH:2 * GLA_K_WIDTH + GLA_WIDTH], state_gla, li, sg_all)
            m_s = _odd_sample_post(o_raw.reshape(DEC_BATCH, GLA_WIDTH), zm_s, zp_s,
                                   jnp.swapaxes(state_pool[li], 0, 1), gain, wp, ps)
            sp.append(jnp.concatenate([state_pool[li, :, 1:], zp_s[:, None, :]], axis=1))
        x_p, x_s, h_p, h_s = _proj_res(m_p, m_s, (w_out_bf[layer % 2], li), (x_p, x_s), norm_ffn[layer],
                                       tm=512, name="out_proj")
        a_p, a_s, w_down_bf = _dense(h_p, h_s, [(w_gate, layer), (w_up, layer)], n_cols=D_FF, tm=1024, tn=512,
                                     mode="swiglu", out_dtype=BF16, side_cast=(w_down, layer), name="ffn_up")
        if layer + 1 < DEPTH:
            x_p, x_s, h_p, h_s = _proj_res(a_p, a_s, (w_down_bf, None), (x_p, x_s),
                                           norm_mix[layer + 1], tm=256, name="ffn_down")
        else:
            x_p, x_s = _proj_res(a_p, a_s, (w_down_bf, None), (x_p, x_s), None,
                                 tm=256, name="ffn_down_last")

    st = lambda parts: jnp.stack(parts)
    cache5 = lambda a: a.reshape(N_EVEN, DEC_BATCH, lc, A_KV_HEADS, HEAD_DIM)
    return (x_p.reshape(BATCH, SEQ, D_MODEL), x_s.reshape(DEC_BATCH, 1, D_MODEL),
            st(pk), st(pv), st(pc), st(pg), st(pp), cache5(sk_all), cache5(sv_all), st(sc), sg_all, st(sp))
```

```python
import functools

import jax
import jax.numpy as jnp
from jax import lax
from jax.experimental import pallas as pl
from jax.experimental.pallas import tpu as pltpu

F32 = jnp.float32
BF16 = jnp.bfloat16

D_MODEL = 2048
BATCH = 4
SEQ = 2048
DEPTH = 4
DEC_BATCH = 32
PAST_LEN = 16384
N_EVEN = 2
N_ODD = 2
EPS = 1e-6
NEG_INF = -1e30
A_HEADS = 16
A_KV_HEADS = 4
HEAD_DIM = 64
A_WIDTH = A_HEADS * HEAD_DIM
A_KV_WIDTH = A_KV_HEADS * HEAD_DIM
WINDOW = 128
ROPE_THETA = 10000.0
CONV_DIM = D_MODEL // 2
CONV_W = 3
GLA_HEADS = 4
GLA_WIDTH = D_MODEL // 2
GLA_DV = GLA_WIDTH // GLA_HEADS
GLA_DK = GLA_DV // 2
GLA_K_WIDTH = GLA_HEADS * GLA_DK
GLA_RANK = 16
GLA_TAU = 16.0
GLA_CHUNK = 64
POOL_DIM = D_MODEL // 2
POOL_WINDOWS = (2, 4, 8, 16)
POOL_GDIM = POOL_DIM // 4
POOL_BUF = 15
D_FF = 5632
EVEN_IN = A_WIDTH + 2 * A_KV_WIDTH + 3 * CONV_DIM
ODD_MAIN = 2 * GLA_K_WIDTH + 2 * GLA_WIDTH

M_PROMPT = BATCH * SEQ
LANES = 128
BLK = 128
NBLK = SEQ // BLK
POOL_RANK_W = POOL_DIM + LANES
VMEM_LIMIT = 58 * 1024 * 1024


def _cparams(*sem):
    return pltpu.CompilerParams(dimension_semantics=sem, vmem_limit_bytes=VMEM_LIMIT)


def _dot(a, b):
    return jnp.dot(a, b, preferred_element_type=F32)


def _dot_nt(a, b):
    return lax.dot_general(a, b, (((1,), (1,)), ((), ())), preferred_element_type=F32)


def _dot_tn(a, b):
    return lax.dot_general(a, b, (((0,), (0,)), ((), ())), preferred_element_type=F32)


def _silu(x):
    return x * (1.0 / (1.0 + jnp.exp(-x)))


def _rmsnorm_rows(x, gain):
    ms = jnp.mean(x * x, axis=-1, keepdims=True)
    return (x * lax.rsqrt(ms + EPS)) * gain


def _rmsnorm_kernel(xp_ref, xs_ref, g_ref, op_ref, os_ref, *, n_i):
    op_ref[...] = _rmsnorm_rows(xp_ref[...], g_ref[...]).astype(op_ref.dtype)

    @pl.when(pl.program_id(0) == n_i - 1)
    def _():
        os_ref[...] = _rmsnorm_rows(xs_ref[...], g_ref[...]).astype(os_ref.dtype)


def _rmsnorm(x_p, x_s, gain, *, tm=1024):
    n_i = M_PROMPT // tm
    row = lambda i: (i, 0)
    fixed = lambda i: (0, 0)
    return pl.pallas_call(
        functools.partial(_rmsnorm_kernel, n_i=n_i),
        grid=(n_i,),
        in_specs=[pl.BlockSpec((tm, D_MODEL), row),
                  pl.BlockSpec((DEC_BATCH, D_MODEL), fixed),
                  pl.BlockSpec((1, D_MODEL), fixed)],
        out_specs=[pl.BlockSpec((tm, D_MODEL), row),
                   pl.BlockSpec((DEC_BATCH, D_MODEL), fixed)],
        out_shape=[jax.ShapeDtypeStruct((M_PROMPT, D_MODEL), BF16),
                   jax.ShapeDtypeStruct((DEC_BATCH, D_MODEL), BF16)],
        compiler_params=_cparams("arbitrary"),
        name="rmsnorm",
    )(x_p, x_s, gain.reshape(1, D_MODEL))


XPOSE_COLS = 128


def _dense_kernel(*refs, n_w, mode, n_i, transposed, side_cast, sub):
    a_p, a_s = refs[0], refs[1]
    w = refs[2:2 + n_w]
    pos = 2 + n_w
    side_in = side_out = None
    if side_cast:
        side_in = refs[pos]
        pos += 1
    o_p, o_s = refs[pos], refs[pos + 1]
    pos += 2
    if side_cast:
        side_out = refs[pos]
        pos += 1
    wbf = refs[pos:pos + n_w]
    i = pl.program_id(1)

    @pl.when(i == 0)
    def _():
        for k in range(n_w):
            if transposed:
                tn = wbf[k].shape[1]
                for c in range(0, tn, XPOSE_COLS):
                    wbf[k][:, c:c + XPOSE_COLS] = w[k][c:c + XPOSE_COLS, :].T.astype(BF16)
            else:
                wbf[k][...] = w[k][...].astype(BF16)
        if side_cast:
            side_out[...] = side_in[...].astype(BF16)

    def run(a_ref, o_ref, n_sub):
        rows = a_ref.shape[0] // n_sub
        for r in range(n_sub):
            a = a_ref[r * rows:(r + 1) * rows, :]
            if mode == "swiglu":
                y = _silu(_dot(a, wbf[0][...])) * _dot(a, wbf[1][...])
            else:
                y = _dot(a, wbf[0][...])
            o_ref[r * rows:(r + 1) * rows, :] = y.astype(o_ref.dtype)

    run(a_p, o_p, sub)

    @pl.when(i == n_i - 1)
    def _():
        run(a_s, o_s, 1)


def _dense(a_p, a_s, weights, *, n_cols, tm, tn, mode="plain", out_dtype=F32, transposed=False,
           side_cast=None, sub=1, name):
    k_dim = a_p.shape[1]
    n_i = M_PROMPT // tm
    n_j = n_cols // tn
    row = lambda j, i: (i, 0)
    fixed = lambda j, i: (0, 0)
    tile = lambda j, i: (i, j)
    panel = lambda j, i: (0, j)
    in_specs = [pl.BlockSpec((tm, k_dim), row), pl.BlockSpec((DEC_BATCH, k_dim), fixed)]
    args = [a_p, a_s]
    for arr, layer in weights:
        if transposed:
            in_specs.append(pl.BlockSpec((None, tn, k_dim), lambda j, i, layer=layer: (layer, j, 0)))
        else:
            in_specs.append(pl.BlockSpec((None, k_dim, tn), lambda j, i, layer=layer: (layer, 0, j)))
        args.append(arr)
    out_specs = [pl.BlockSpec((tm, tn), tile), pl.BlockSpec((DEC_BATCH, tn), panel)]
    out_shape = [jax.ShapeDtypeStruct((M_PROMPT, n_cols), out_dtype),
                 jax.ShapeDtypeStruct((DEC_BATCH, n_cols), out_dtype)]
    if side_cast is not None:
        s_arr, s_layer = side_cast
        slab = s_arr.shape[1] // n_j
        in_specs.append(pl.BlockSpec((None, slab, s_arr.shape[2]), lambda j, i: (s_layer, j, 0)))
        args.append(s_arr)
        out_specs.append(pl.BlockSpec((slab, s_arr.shape[2]), lambda j, i: (j, 0)))
        out_shape.append(jax.ShapeDtypeStruct(s_arr.shape[1:], BF16))
    n_w = len(weights)
    return pl.pallas_call(
        functools.partial(_dense_kernel, n_w=n_w, mode=mode, n_i=n_i, transposed=transposed,
                          side_cast=side_cast is not None, sub=sub),
        grid=(n_j, n_i),
        in_specs=in_specs,
        out_specs=out_specs,
        out_shape=out_shape,
        scratch_shapes=[pltpu.VMEM((k_dim, tn), BF16) for _ in range(n_w)],
        compiler_params=_cparams("arbitrary", "arbitrary"),
        name=name,
    )(*args)


def _proj_res_kernel(*refs, n_i, with_norm):
    a_p, a_s, w_ref, r_p, r_s = refs[:5]
    if with_norm:
        g_ref, x_p, x_s, h_p, h_s = refs[5:]
    else:
        x_p, x_s = refs[5:]
        g_ref = h_p = h_s = None

    def run(a_ref, r_ref, x_ref, h_ref):
        x = r_ref[...] + _dot(a_ref[...], w_ref[...])
        x_ref[...] = x
        if with_norm:
            h_ref[...] = _rmsnorm_rows(x, g_ref[...]).astype(h_ref.dtype)

    run(a_p, r_p, x_p, h_p)

    @pl.when(pl.program_id(0) == n_i - 1)
    def _():
        run(a_s, r_s, x_s, h_s)


def _proj_res(a_p, a_s, weight, res, gain, *, tm, name):
    w_bf, layer = weight
    k_dim = a_p.shape[1]
    if layer is None:
        w_spec = pl.BlockSpec((k_dim, D_MODEL), lambda i: (0, 0), pipeline_mode=pl.Buffered(1))
    else:
        w_spec = pl.BlockSpec((None, k_dim, D_MODEL), lambda i: (layer, 0, 0), pipeline_mode=pl.Buffered(1))
    n_i = M_PROMPT // tm
    with_norm = gain is not None
    row = lambda i: (i, 0)
    fixed = lambda i: (0, 0)
    rows_p = pl.BlockSpec((tm, D_MODEL), row)
    rows_s = pl.BlockSpec((DEC_BATCH, D_MODEL), fixed)
    in_specs = [pl.BlockSpec((tm, k_dim), row), pl.BlockSpec((DEC_BATCH, k_dim), fixed),
                w_spec, rows_p, rows_s]
    args = [a_p, a_s, w_bf, res[0], res[1]]
    out_specs = [rows_p, rows_s]
    out_shape = [jax.ShapeDtypeStruct((M_PROMPT, D_MODEL), F32), jax.ShapeDtypeStruct((DEC_BATCH, D_MODEL), F32)]
    if with_norm:
        in_specs.append(pl.BlockSpec((1, D_MODEL), fixed))
        args.append(gain.reshape(1, D_MODEL))
        out_specs += [rows_p, rows_s]
        out_shape += [jax.ShapeDtypeStruct((M_PROMPT, D_MODEL), BF16),
                      jax.ShapeDtypeStruct((DEC_BATCH, D_MODEL), BF16)]
    return pl.pallas_call(
        functools.partial(_proj_res_kernel, n_i=n_i, with_norm=with_norm),
        grid=(n_i,),
        in_specs=in_specs,
        out_specs=out_specs,
        out_shape=out_shape,
        compiler_params=_cparams("arbitrary"),
        name=name,
    )(*args)


def _split_cat(x):
    hi = x.astype(BF16)
    lo = (x - hi.astype(F32)).astype(BF16)
    return jnp.concatenate([hi, lo], axis=1)


def _rope_mats():
    j = jnp.arange(2 * LANES)[:, None] % LANES
    l = jnp.arange(LANES)[None, :]
    head_sum = (j // HEAD_DIM == l // HEAD_DIM).astype(BF16)
    src = jnp.where(l % HEAD_DIM < HEAD_DIM // 2, l + HEAD_DIM // 2, l - HEAD_DIM // 2)
    half_swap = (j == src).astype(BF16)
    return head_sum, half_swap


def _norm_rope_chunk(xc, gain, cos, sin, hs_ref, sw_ref):
    ms = _dot(_split_cat(xc * xc), hs_ref[...]) * (1.0 / HEAD_DIM)
    y = (xc * lax.rsqrt(ms + EPS)) * gain
    swapped = _dot(_split_cat(y), sw_ref[...])
    return y * cos + swapped * sin


def _spread_heads(xc, own_lo):
    lane = lax.broadcasted_iota(jnp.int32, xc.shape, 1)
    keep = (lane < HEAD_DIM) if own_lo else (lane >= HEAD_DIM)
    nat = jnp.where(keep, xc, 0.0)
    rol = pltpu.roll(nat, HEAD_DIM, axis=1)
    parts = (nat, rol) if own_lo else (rol, nat)
    return jnp.concatenate(parts, axis=0).astype(BF16)


def _attn_core(qs, kfull, vfull, mask_t, sink_ref):
    rows = qs[0].shape[0]
    nkeys = kfull.shape[0]
    outs = [None] * 8
    for kh in range(A_KV_HEADS):
        c0 = (kh // 2) * LANES
        kk = _spread_heads(kfull[:, c0:c0 + LANES], kh % 2 == 0)
        vv = _spread_heads(vfull[:, c0:c0 + LANES], kh % 2 == 0)
        lhs = jnp.concatenate([qs[2 * kh], qs[2 * kh + 1]], axis=0)
        s = _dot_nt(kk, lhs)
        prow = []
        for half in range(2):
            pcol = []
            for cc in range(2):
                sb = s[half * nkeys:(half + 1) * nkeys, cc * rows:(cc + 1) * rows]
                sb = jnp.where(mask_t, sb, NEG_INF)
                sink = sink_ref[kh * 4 + 2 * cc + half]
                m = jnp.maximum(jnp.max(sb, axis=0, keepdims=True), sink)
                e = jnp.exp(sb - m)
                den = jnp.sum(e, axis=0, keepdims=True) + jnp.exp(sink - m)
                pcol.append((e / den).astype(BF16))
            prow.append(jnp.concatenate(pcol, axis=1))
        p = jnp.concatenate(prow, axis=0)
        o = _dot_tn(p, vv)
        outs[2 * kh] = o[0:rows]
        outs[2 * kh + 1] = o[rows:2 * rows]
    return outs


def _even_prompt_kernel(sink_ref, z_ref, cos_ref, sin_ref, qg_ref, kg_ref, cw_ref, hs_ref, sw_ref,
                        o_ref, nk_ref, nv_ref, nc_ref, kf_ref, vf_ref, ub_ref):
    i = pl.program_id(1)

    @pl.when(i == 0)
    def _():
        kf_ref[...] = jnp.zeros_like(kf_ref)
        vf_ref[...] = jnp.zeros_like(vf_ref)
        ub_ref[0:8, :] = jnp.zeros((8, CONV_DIM), F32)

    kf_ref[0:BLK, :] = kf_ref[BLK:2 * BLK, :]
    vf_ref[0:BLK, :] = vf_ref[BLK:2 * BLK, :]
    cos = cos_ref[...]
    sin = sin_ref[...]
    k = jnp.concatenate(
        [_norm_rope_chunk(z_ref[:, A_WIDTH + c * LANES:A_WIDTH + (c + 1) * LANES], kg_ref[...], cos, sin,
                          hs_ref, sw_ref)
         for c in range(A_KV_WIDTH // LANES)], axis=1)
    v = z_ref[:, A_WIDTH + A_KV_WIDTH:A_WIDTH + 2 * A_KV_WIDTH]
    kf_ref[BLK:2 * BLK, :] = k
    vf_ref[BLK:2 * BLK, :] = v
    qs = [(_norm_rope_chunk(z_ref[:, c * LANES:(c + 1) * LANES], qg_ref[...], cos, sin, hs_ref, sw_ref)
           * (HEAD_DIM ** -0.5)).astype(BF16) for c in range(A_WIDTH // LANES)]
    kk = lax.broadcasted_iota(jnp.int32, (2 * BLK, BLK), 0)
    r = lax.broadcasted_iota(jnp.int32, (2 * BLK, BLK), 1)
    d = kk - r
    mask_t = (d >= 0) & (d <= WINDOW) & ((kk >= BLK) | (i > 0))
    outs = _attn_core(qs, kf_ref[...], vf_ref[...], mask_t, sink_ref)
    for c in range(A_WIDTH // LANES):
        o_ref[:, c * LANES:(c + 1) * LANES] = outs[c].astype(o_ref.dtype)

    o0 = A_WIDTH + 2 * A_KV_WIDTH
    u = z_ref[:, o0 + CONV_DIM:o0 + 2 * CONV_DIM] * z_ref[:, o0 + 2 * CONV_DIM:o0 + 3 * CONV_DIM]
    ub_ref[8:8 + BLK, :] = u
    y = ub_ref[6:6 + BLK, :] * cw_ref[0:1, :]
    y = y + ub_ref[7:7 + BLK, :] * cw_ref[1:2, :]
    y = y + ub_ref[8:8 + BLK, :] * cw_ref[2:3, :]
    o_ref[:, A_WIDTH:A_WIDTH + CONV_DIM] = (z_ref[:, o0:o0 + CONV_DIM] * y).astype(o_ref.dtype)
    ub_ref[0:8, :] = ub_ref[BLK:BLK + 8, :]

    @pl.when(i == pl.num_programs(1) - 1)
    def _():
        nk_ref[...] = k
        nv_ref[...] = v
        nc_ref[...] = ub_ref[BLK + 6:BLK + 8, :]


def _even_prompt(z_p, cos, sin, qg, kg, cw, sinks, mats):
    blk = lambda b, i: (b * NBLK + i, 0)
    fixed = lambda b, i: (0, 0)
    per_b = lambda b, i: (b, 0, 0)
    return pl.pallas_call(
        _even_prompt_kernel,
        grid=(BATCH, NBLK),
        in_specs=[pl.BlockSpec(memory_space=pltpu.SMEM),
                  pl.BlockSpec((BLK, EVEN_IN), blk),
                  pl.BlockSpec((BLK, LANES), lambda b, i: (i, 0)),
                  pl.BlockSpec((BLK, LANES), lambda b, i: (i, 0)),
                  pl.BlockSpec((1, LANES), fixed),
                  pl.BlockSpec((1, LANES), fixed),
                  pl.BlockSpec((CONV_W, CONV_DIM), fixed),
                  pl.BlockSpec((2 * LANES, LANES), fixed),
                  pl.BlockSpec((2 * LANES, LANES), fixed)],
        out_specs=[pl.BlockSpec((BLK, D_MODEL), blk),
                   pl.BlockSpec((None, BLK, A_KV_WIDTH), per_b),
                   pl.BlockSpec((None, BLK, A_KV_WIDTH), per_b),
                   pl.BlockSpec((None, CONV_W - 1, CONV_DIM), per_b)],
        out_shape=[jax.ShapeDtypeStruct((M_PROMPT, D_MODEL), BF16),
                   jax.ShapeDtypeStruct((BATCH, BLK, A_KV_WIDTH), F32),
                   jax.ShapeDtypeStruct((BATCH, BLK, A_KV_WIDTH), F32),
                   jax.ShapeDtypeStruct((BATCH, CONV_W - 1, CONV_DIM), F32)],
        scratch_shapes=[pltpu.VMEM((2 * BLK, A_KV_WIDTH), F32),
                        pltpu.VMEM((2 * BLK, A_KV_WIDTH), F32),
                        pltpu.VMEM((BLK + 8, CONV_DIM), F32)],
        compiler_params=_cparams("arbitrary", "arbitrary"),
        name="even_prompt",
    )(sinks, z_p, cos, sin, qg, kg, cw, *mats)


def _even_sample_prep_kernel(z_ref, cos_ref, sin_ref, qg_ref, kg_ref, c0_ref, c1_ref, cw_ref, hs_ref, sw_ref,
                             q_ref, k_ref, u_ref, o_ref):
    cos = cos_ref[...]
    sin = sin_ref[...]
    for c in range(A_WIDTH // LANES):
        q = _norm_rope_chunk(z_ref[:, c * LANES:(c + 1) * LANES], qg_ref[...], cos, sin, hs_ref, sw_ref)
        q_ref[:, c * LANES:(c + 1) * LANES] = (q * (HEAD_DIM ** -0.5)).astype(q_ref.dtype)
    for c in range(A_KV_WIDTH // LANES):
        k_ref[:, c * LANES:(c + 1) * LANES] = _norm_rope_chunk(
            z_ref[:, A_WIDTH + c * LANES:A_WIDTH + (c + 1) * LANES], kg_ref[...], cos, sin, hs_ref, sw_ref)
    o0 = A_WIDTH + 2 * A_KV_WIDTH
    u = z_ref[:, o0 + CONV_DIM:o0 + 2 * CONV_DIM] * z_ref[:, o0 + 2 * CONV_DIM:o0 + 3 * CONV_DIM]
    u_ref[...] = u
    y = c0_ref[...] * cw_ref[0:1, :]
    y = y + c1_ref[...] * cw_ref[1:2, :]
    y = y + u * cw_ref[2:3, :]
    o_ref[...] = (z_ref[:, o0:o0 + CONV_DIM] * y).astype(o_ref.dtype)


def _even_sample_prep(z_s, cos, sin, qg, kg, c0, c1, cw, mats):
    return pl.pallas_call(
        _even_sample_prep_kernel,
        out_shape=[jax.ShapeDtypeStruct((DEC_BATCH, A_WIDTH), BF16),
                   jax.ShapeDtypeStruct((DEC_BATCH, A_KV_WIDTH), F32),
                   jax.ShapeDtypeStruct((DEC_BATCH, CONV_DIM), F32),
                   jax.ShapeDtypeStruct((DEC_BATCH, CONV_DIM), BF16)],
        compiler_params=pltpu.CompilerParams(vmem_limit_bytes=VMEM_LIMIT),
        name="even_sample_prep",
    )(z_s, cos, sin, qg, kg, c0, c1, cw, *mats)


SROWS = LANES


def _even_sample_attn_kernel(*refs):
    sink_ref, q_ref, kn_ref, vn_ref, kc_ref, vc_ref = refs[:6]
    o_ref, nk_ref, nv_ref = refs[-3:]
    lc = kc_ref.shape[0]
    row = lax.broadcasted_iota(jnp.int32, (lc, A_KV_WIDTH), 0)
    kfull = jnp.concatenate([kc_ref[...], jnp.where(row == 0, kn_ref[...], 0.0)], axis=0)
    vfull = jnp.concatenate([vc_ref[...], jnp.where(row == 0, vn_ref[...], 0.0)], axis=0)
    qs = [jnp.broadcast_to(q_ref[:, c * LANES:(c + 1) * LANES], (SROWS, LANES))
          for c in range(A_WIDTH // LANES)]
    kk = lax.broadcasted_iota(jnp.int32, (2 * lc, SROWS), 0)
    mask_t = (kk <= lc) & (lc - kk <= WINDOW)
    outs = _attn_core(qs, kfull, vfull, mask_t, sink_ref)
    for c in range(A_WIDTH // LANES):
        o_ref[:, c * LANES:(c + 1) * LANES] = outs[c][0:1].astype(o_ref.dtype)
    nk_ref[0:lc - 1, :] = kc_ref[1:lc, :]
    nk_ref[lc - 1:lc, :] = kn_ref[...]
    nv_ref[0:lc - 1, :] = vc_ref[1:lc, :]
    nv_ref[lc - 1:lc, :] = vn_ref[...]


def _even_sample_attn(sinks, q_s, k_s, v_s, k_cache, v_cache, li, prev):
    lc = k_cache.shape[2]
    vec = lambda w: pl.BlockSpec((None, 1, w), lambda b: (b, 0, 0))
    cache = pl.BlockSpec((None, None, lc, A_KV_WIDTH), lambda b: (li, b, 0, 0))
    in_specs = [pl.BlockSpec(memory_space=pltpu.SMEM), vec(A_WIDTH), vec(A_KV_WIDTH), vec(A_KV_WIDTH),
                cache, cache]
    args = [sinks, q_s.reshape(DEC_BATCH, 1, A_WIDTH), k_s.reshape(DEC_BATCH, 1, A_KV_WIDTH),
            v_s.reshape(DEC_BATCH, 1, A_KV_WIDTH), k_cache, v_cache]
    aliases = {}
    if prev is not None:
        aliases = {len(args): 1, len(args) + 1: 2}
        in_specs += [pl.BlockSpec(memory_space=pl.ANY)] * 2
        args += list(prev)
    return pl.pallas_call(
        _even_sample_attn_kernel,
        grid=(DEC_BATCH,),
        in_specs=in_specs,
        out_specs=[vec(A_WIDTH), cache, cache],
        out_shape=[jax.ShapeDtypeStruct((DEC_BATCH, 1, A_WIDTH), BF16),
                   jax.ShapeDtypeStruct(k_cache.shape, F32),
                   jax.ShapeDtypeStruct(v_cache.shape, F32)],
        input_output_aliases=aliases,
        compiler_params=_cparams("arbitrary"),
        name="even_sample_attn",
    )(*args)


def _log_decay(zr, wa_ref, ba_ref):
    pre = _dot(zr.astype(BF16), wa_ref[...]) + ba_ref[...]
    return (jnp.minimum(pre, 0.0) - jnp.log1p(jnp.exp(-jnp.abs(pre)))) * (1.0 / GLA_TAU)


def _split_bf16(x):
    hi = x.astype(BF16)
    lo = (x - hi.astype(F32)).astype(BF16)
    return hi, lo


def _row_to_col(row):
    n = row.shape[1]
    r = lax.broadcasted_iota(jnp.int32, (n, n), 0)
    c = lax.broadcasted_iota(jnp.int32, (n, n), 1)
    return jnp.sum(jnp.where(r == c, jnp.broadcast_to(row, (n, n)), 0.0), axis=-1, keepdims=True)


def _head_rmsnorm_gate(o, gain, zg):
    ms = jnp.mean(o * o, axis=-1, keepdims=True)
    return ((o * lax.rsqrt(ms + EPS)) * gain) * _silu(zg)


def _odd_prompt_kernel(zm_ref, zpr_ref, wa_ref, ba_ref, gain_ref, wp_ref, ps_ref,
                       o_ref, s_ref, np_ref, pbh_ref, pbl_ref):
    i = pl.program_id(1)

    @pl.when(i == 0)
    def _():
        s_ref[...] = jnp.zeros_like(s_ref)
        pbh_ref[0:BLK, :] = jnp.zeros((BLK, POOL_DIM), BF16)
        pbl_ref[0:BLK, :] = jnp.zeros((BLK, POOL_DIM), BF16)

    n_ch = BLK // GLA_CHUNK
    gk = _log_decay(zpr_ref[:, POOL_DIM:POOL_RANK_W], wa_ref, ba_ref)
    r = lax.broadcasted_iota(jnp.int32, (BLK, BLK), 0)
    c = lax.broadcasted_iota(jnp.int32, (BLK, BLK), 1)
    causal = (r >= c) & (r // GLA_CHUNK == c // GLA_CHUNK)
    tri = jnp.where(causal, 1.0, 0.0).astype(BF16)
    g_hi, g_lo = _split_bf16(gk)
    b = _dot(jnp.concatenate([tri, tri], axis=1), jnp.concatenate([g_hi, g_lo], axis=0))
    b_last = [b[(ch + 1) * GLA_CHUNK - 1:(ch + 1) * GLA_CHUNK, :] for ch in range(n_ch)]
    b_end = jnp.concatenate([jnp.broadcast_to(bl, (GLA_CHUNK, GLA_K_WIDTH)) for bl in b_last], axis=0)
    o_k, o_v, o_g = GLA_K_WIDTH, 2 * GLA_K_WIDTH, 2 * GLA_K_WIDTH + GLA_WIDTH
    zq = zm_ref[:, 0:GLA_K_WIDTH]
    zk = zm_ref[:, o_k:o_k + GLA_K_WIDTH]
    qd = ((zq * (GLA_DK ** -0.5)) * jnp.exp(b)).astype(BF16)
    kd = (zk * jnp.exp(-b)).astype(BF16)
    k2 = (zk * jnp.exp(b_end - b)).astype(BF16)
    sub = lax.broadcasted_iota(jnp.int32, (8, LANES), 0)
    sel = jnp.where(sub < 2, 1.0, 0.0).astype(BF16)
    hk = [slice(h * GLA_DK, (h + 1) * GLA_DK) for h in range(GLA_HEADS)]
    chunk = [slice(ch * GLA_CHUNK, (ch + 1) * GLA_CHUNK) for ch in range(n_ch)]
    v_hs = [zm_ref[:, o_v + h * GLA_DV:o_v + (h + 1) * GLA_DV].astype(BF16) for h in range(GLA_HEADS)]
    atts = [_dot_nt(qd[:, hk[h]], kd[:, hk[h]]) for h in range(GLA_HEADS)]
    log_decs = []
    for h in range(GLA_HEADS):
        for ch in range(n_ch):
            bl = b_last[ch][:, hk[h]]
            bl_hi = bl.astype(BF16).astype(F32)
            rows8 = jnp.where(sub == 0, bl_hi, jnp.where(sub == 1, bl - bl_hi, 0.0)).astype(BF16)
            log_decs.append(_dot_tn(rows8, sel))
    deltas = [[_dot_tn(k2[chunk[ch], hk[h]], v_hs[h][chunk[ch]]) for ch in range(n_ch)]
              for h in range(GLA_HEADS)]
    o_intras = [_dot(jnp.where(causal, atts[h], 0.0).astype(BF16), v_hs[h]) for h in range(GLA_HEADS)]
    o_inters = []
    for h in range(GLA_HEADS):
        s_h = s_ref[h]
        parts = []
        for ch in range(n_ch):
            parts.append(_dot(qd[chunk[ch], hk[h]], s_h.astype(BF16)))
            dec = jnp.exp(log_decs[h * n_ch + ch])
            s_h = jnp.concatenate([dec, dec], axis=1) * s_h + deltas[h][ch]
        s_ref[h] = s_h
        o_inters.append(jnp.concatenate(parts, axis=0))
    for h in range(GLA_HEADS):
        vs = slice(h * GLA_DV, (h + 1) * GLA_DV)
        zg = zm_ref[:, o_g + h * GLA_DV:o_g + (h + 1) * GLA_DV]
        o_ref[:, vs] = _head_rmsnorm_gate(o_intras[h] + o_inters[h], gain_ref[...], zg).astype(o_ref.dtype)

    x = zpr_ref[:, 0:POOL_DIM]
    x_hi, x_lo = _split_bf16(x)
    pbh_ref[BLK:2 * BLK, :] = x_hi
    pbl_ref[BLK:2 * BLK, :] = x_lo
    t = lax.broadcasted_iota(jnp.int32, (BLK, 2 * BLK), 0)
    j = lax.broadcasted_iota(jnp.int32, (BLK, 2 * BLK), 1)
    back = t + BLK - j
    t_glob = i * BLK + lax.broadcasted_iota(jnp.int32, (BLK, 1), 0)
    groups = [slice(g * POOL_GDIM, (g + 1) * POOL_GDIM) for g in range(len(POOL_WINDOWS))]
    sums = []
    for g, w in enumerate(POOL_WINDOWS):
        band = jnp.where((back >= 0) & (back < w), 1.0, 0.0).astype(BF16)
        sums.append(_dot(jnp.concatenate([band, band], axis=1),
                         jnp.concatenate([pbh_ref[:, groups[g]], pbl_ref[:, groups[g]]], axis=0)))
    ys = []
    for g, w in enumerate(POOL_WINDOWS):
        inv_cnt = 1.0 / jnp.minimum(w, t_glob + 1).astype(F32)
        d = sums[g] * inv_cnt - x[:, groups[g]]
        ys.append(_dot(d.astype(BF16), wp_ref[g]))
    for g in range(len(POOL_WINDOWS)):
        o_ref[:, GLA_WIDTH + g * POOL_GDIM:GLA_WIDTH + (g + 1) * POOL_GDIM] = (
            ys[g] * ps_ref[:, groups[g]]).astype(o_ref.dtype)
    pbh_ref[0:BLK, :] = x_hi
    pbl_ref[0:BLK, :] = x_lo

    @pl.when(i == pl.num_programs(1) - 1)
    def _():
        np_ref[...] = x[BLK - POOL_BUF:BLK, :]


def _odd_prompt(zm_p, zpr_p, wa, ba, gain, wp, ps):
    blk = lambda b, i: (b * NBLK + i, 0)
    fixed2 = lambda b, i: (0, 0)
    return pl.pallas_call(
        _odd_prompt_kernel,
        grid=(BATCH, NBLK),
        in_specs=[pl.BlockSpec((BLK, ODD_MAIN), blk),
                  pl.BlockSpec((BLK, POOL_RANK_W), blk),
                  pl.BlockSpec((LANES, GLA_K_WIDTH), fixed2),
                  pl.BlockSpec((1, GLA_K_WIDTH), fixed2),
                  pl.BlockSpec((1, GLA_DV), fixed2),
                  pl.BlockSpec((4, POOL_GDIM, POOL_GDIM), lambda b, i: (0, 0, 0)),
                  pl.BlockSpec((1, POOL_DIM), fixed2)],
        out_specs=[pl.BlockSpec((BLK, D_MODEL), blk),
                   pl.BlockSpec((None, GLA_HEADS, GLA_DK, GLA_DV), lambda b, i: (b, 0, 0, 0)),
                   pl.BlockSpec((None, POOL_BUF, POOL_DIM), lambda b, i: (b, 0, 0))],
        out_shape=[jax.ShapeDtypeStruct((M_PROMPT, D_MODEL), BF16),
                   jax.ShapeDtypeStruct((BATCH, GLA_HEADS, GLA_DK, GLA_DV), F32),
                   jax.ShapeDtypeStruct((BATCH, POOL_BUF, POOL_DIM), F32)],
        scratch_shapes=[pltpu.VMEM((2 * BLK, POOL_DIM), BF16), pltpu.VMEM((2 * BLK, POOL_DIM), BF16)],
        compiler_params=_cparams("arbitrary", "arbitrary"),
        name="odd_prompt",
    )(zm_p, zpr_p, wa, ba, gain, wp, ps)


def _odd_sample_prep_kernel(zm_ref, zr_ref, wa_ref, ba_ref, dec_ref, q_ref):
    dec_ref[...] = jnp.exp(_log_decay(zr_ref[...], wa_ref, ba_ref))
    q_ref[...] = zm_ref[:, 0:GLA_K_WIDTH] * (GLA_DK ** -0.5)


def _odd_sample_prep(zm_s, zr_s, wa, ba):
    return pl.pallas_call(
        _odd_sample_prep_kernel,
        out_shape=[jax.ShapeDtypeStruct((DEC_BATCH, GLA_K_WIDTH), F32),
                   jax.ShapeDtypeStruct((DEC_BATCH, GLA_K_WIDTH), F32)],
        compiler_params=pltpu.CompilerParams(vmem_limit_bytes=VMEM_LIMIT),
        name="odd_sample_prep",
    )(zm_s, zr_s, wa, ba)


STATE_BB = 4


def _odd_sample_state_kernel(*refs):
    dec_ref, q_ref, k_ref, v_ref, s_ref = refs[:5]
    ns_ref, o_ref = refs[-2:]
    for bb in range(STATE_BB):
        for h in range(GLA_HEADS):
            ks = slice(h * GLA_DK, (h + 1) * GLA_DK)
            vs = slice(h * GLA_DV, (h + 1) * GLA_DV)
            s_new = (_row_to_col(dec_ref[bb][:, ks]) * s_ref[bb, h]
                     + _row_to_col(k_ref[bb][:, ks]) * v_ref[bb][:, vs])
            ns_ref[bb, h] = s_new
            o_ref[bb, :, vs] = jnp.sum(_row_to_col(q_ref[bb][:, ks]) * s_new, axis=0, keepdims=True)


def _odd_sample_state(dec, q, k, v, state, li, prev):
    vec = lambda w: pl.BlockSpec((STATE_BB, 1, w), lambda b: (b, 0, 0))
    st = pl.BlockSpec((None, STATE_BB, GLA_HEADS, GLA_DK, GLA_DV), lambda b: (li, b, 0, 0, 0))
    r3 = lambda a: a.reshape(DEC_BATCH, 1, a.shape[-1])
    in_specs = [vec(GLA_K_WIDTH), vec(GLA_K_WIDTH), vec(GLA_K_WIDTH), vec(GLA_WIDTH), st]
    args = [r3(dec), r3(q), r3(k), r3(v), state]
    aliases = {}
    if prev is not None:
        aliases = {len(args): 0}
        in_specs.append(pl.BlockSpec(memory_space=pl.ANY))
        args.append(prev)
    return pl.pallas_call(
        _odd_sample_state_kernel,
        grid=(DEC_BATCH // STATE_BB,),
        in_specs=in_specs,
        out_specs=[st, vec(GLA_WIDTH)],
        out_shape=[jax.ShapeDtypeStruct(state.shape, F32),
                   jax.ShapeDtypeStruct((DEC_BATCH, 1, GLA_WIDTH), F32)],
        input_output_aliases=aliases,
        compiler_params=_cparams("arbitrary"),
        name="odd_sample_state",
    )(*args)


def _odd_sample_post_kernel(o_in_ref, zm_ref, zp_ref, hist_ref, gain_ref, wp_ref, ps_ref, o_ref):
    o_g = 2 * GLA_K_WIDTH + GLA_WIDTH
    for h in range(GLA_HEADS):
        vs = slice(h * GLA_DV, (h + 1) * GLA_DV)
        zg = zm_ref[:, o_g + h * GLA_DV:o_g + (h + 1) * GLA_DV]
        o_ref[:, vs] = _head_rmsnorm_gate(o_in_ref[:, vs], gain_ref[...], zg).astype(o_ref.dtype)
    n_prev = hist_ref.shape[0]
    for g, w in enumerate(POOL_WINDOWS):
        gs = slice(g * POOL_GDIM, (g + 1) * POOL_GDIM)
        x = zp_ref[:, gs]
        s = hist_ref[n_prev - (w - 1)][:, gs]
        for jj in range(n_prev - (w - 1) + 1, n_prev):
            s = s + hist_ref[jj][:, gs]
        s = s + x
        d = s / float(min(w, n_prev + 1)) - x
        y = _dot(d.astype(BF16), wp_ref[g]) * ps_ref[:, gs]
        o_ref[:, GLA_WIDTH + g * POOL_GDIM:GLA_WIDTH + (g + 1) * POOL_GDIM] = y.astype(o_ref.dtype)


def _odd_sample_post(o_raw, zm_s, zp_s, hist_t, gain, wp, ps):
    return pl.pallas_call(
        _odd_sample_post_kernel,
        out_shape=jax.ShapeDtypeStruct((DEC_BATCH, D_MODEL), BF16),
        compiler_params=pltpu.CompilerParams(vmem_limit_bytes=VMEM_LIMIT),
        name="odd_sample_post",
    )(o_raw, zm_s, zp_s, hist_t, gain, wp, ps)


def _rope_tables(pos):
    half = HEAD_DIM // 2
    inv = jnp.power(ROPE_THETA, -jnp.arange(half, dtype=F32) / half)
    ang = pos.astype(F32)[:, None] * inv[None, :]
    c, s = jnp.cos(ang), jnp.sin(ang)
    return jnp.tile(c, (1, 4)), jnp.tile(jnp.concatenate([-s, s], axis=1), (1, 2))


def kernel(x_prompt, x_sample, cache_swa_k, cache_swa_v, state_conv, state_gla, state_pool, norm_mix, norm_ffn, w_in_even, w_out_even, q_norm, k_norm, attn_sinks, conv_w, w_in_odd, w_out_odd, w_alpha_up, b_alpha, gla_out_norm, w_pool, pool_scale, w_gate, w_up, w_down):
    lc = cache_swa_k.shape[2]
    x_p = x_prompt.reshape(M_PROMPT, D_MODEL)
    x_s = x_sample.reshape(DEC_BATCH, D_MODEL)
    cos_p, sin_p = _rope_tables(jnp.arange(SEQ))
    cos_s, sin_s = _rope_tables(PAST_LEN + jnp.arange(1))
    mats = _rope_mats()
    kc_all = cache_swa_k.reshape(N_EVEN, DEC_BATCH, lc, A_KV_WIDTH)
    vc_all = cache_swa_v.reshape(N_EVEN, DEC_BATCH, lc, A_KV_WIDTH)
    w_out_bf = (w_out_even.astype(BF16), w_out_odd.astype(BF16))
    o_r = ODD_MAIN
    w_odd_t = jnp.swapaxes(w_in_odd, 1, 2)
    w_odd_pr_t = jnp.concatenate([w_odd_t[:, o_r + GLA_RANK:], w_odd_t[:, o_r:o_r + GLA_RANK],
                                  jnp.zeros((N_ODD, LANES - GLA_RANK, D_MODEL), F32)], axis=1)
    pk, pv, pc, pg, pp, sc, sp = ([] for _ in range(7))
    sk_all = sv_all = sg_all = None

    h_p, h_s = _rmsnorm(x_p, x_s, norm_mix[0])
    for layer in range(DEPTH):
        li = layer // 2
        if layer % 2 == 0:
            z_p, z_s = _dense(h_p, h_s, [(w_in_even, li)], n_cols=EVEN_IN, tm=1024, tn=1536, name="in_even")
            qg = jnp.tile(q_norm[li], 2).reshape(1, LANES)
            kg = jnp.tile(k_norm[li], 2).reshape(1, LANES)
            m_p, nk, nv, nc = _even_prompt(z_p, cos_p, sin_p, qg, kg, conv_w[li], attn_sinks[li], mats)
            pk.append(nk.reshape(BATCH, lc, A_KV_HEADS, HEAD_DIM))
            pv.append(nv.reshape(BATCH, lc, A_KV_HEADS, HEAD_DIM))
            pc.append(nc)
            q_s, k_s, u_s, conv_s = _even_sample_prep(
                z_s, cos_s, sin_s, qg, kg, state_conv[li, :, 0], state_conv[li, :, 1], conv_w[li], mats)
            v_s = z_s[:, A_WIDTH + A_KV_WIDTH:A_WIDTH + 2 * A_KV_WIDTH]
            attn_s, sk_all, sv_all = _even_sample_attn(
                attn_sinks[li], q_s, k_s, v_s, kc_all, vc_all, li,
                None if sk_all is None else (sk_all, sv_all))
            sc.append(jnp.stack([state_conv[li, :, 1], u_s], axis=1))
            m_s = jnp.concatenate([attn_s.reshape(DEC_BATCH, A_WIDTH), conv_s], axis=1)
        else:
            zm_p, zm_s = _dense(h_p, h_s, [(w_odd_t, li)], n_cols=ODD_MAIN, tm=512, tn=ODD_MAIN // 2,
                                transposed=True, name="in_odd_main")
            zpr_p, zpr_s = _dense(h_p, h_s, [(w_odd_pr_t, li)], n_cols=POOL_RANK_W, tm=1024, tn=POOL_RANK_W,
                                  transposed=True, name="in_odd_pool_rank")
            zp_s, zr_s = zpr_s[:, :POOL_DIM], zpr_s[:, POOL_DIM:]
            wa = jnp.pad(w_alpha_up[li], ((0, LANES - GLA_RANK), (0, 0))).astype(BF16)
            ba = b_alpha[li].reshape(1, GLA_K_WIDTH)
            gain = gla_out_norm[li].reshape(1, GLA_DV)
            wp = w_pool[li].astype(BF16)
            ps = pool_scale[li].reshape(1, POOL_DIM)
            m_p, ng, npool = _odd_prompt(zm_p, zpr_p, wa, ba, gain, wp, ps)
            pg.append(ng)
            pp.append(npool)
            dec_s, q_s = _odd_sample_prep(zm_s, zr_s, wa, ba)
            sg_all, o_raw = _odd_sample_state(
                dec_s, q_s, zm_s[:, GLA_K_WIDTH:2 * GLA_K_WIDTH],
                zm_s[:, 2 * GLA_K_WIDTH:2 * GLA_K_WIDTH + GLA_WIDTH], state_gla, li, sg_all)
            m_s = _odd_sample_post(o_raw.reshape(DEC_BATCH, GLA_WIDTH), zm_s, zp_s,
                                   jnp.swapaxes(state_pool[li], 0, 1), gain, wp, ps)
            sp.append(jnp.concatenate([state_pool[li, :, 1:], zp_s[:, None, :]], axis=1))
        x_p, x_s, h_p, h_s = _proj_res(m_p, m_s, (w_out_bf[layer % 2], li), (x_p, x_s), norm_ffn[layer],
                                       tm=512, name="out_proj")
        up_tm, up_sub = ((1024, 1), (2048, 4), (1024, 2), (2048, 2))[layer]
        a_p, a_s, w_down_bf = _dense(h_p, h_s, [(w_gate, layer), (w_up, layer)], n_cols=D_FF, tm=up_tm, tn=512,
                                     mode="swiglu", out_dtype=BF16, side_cast=(w_down, layer), sub=up_sub,
                                     name="ffn_up")
        if layer + 1 < DEPTH:
            x_p, x_s, h_p, h_s = _proj_res(a_p, a_s, (w_down_bf, None), (x_p, x_s),
                                           norm_mix[layer + 1], tm=(256, 512, 256)[layer], name="ffn_down")
        else:
            x_p, x_s = _proj_res(a_p, a_s, (w_down_bf, None), (x_p, x_s), None,
                                 tm=256, name="ffn_down_last")

    st = lambda parts: jnp.stack(parts)
    cache5 = lambda a: a.reshape(N_EVEN, DEC_BATCH, lc, A_KV_HEADS, HEAD_DIM)
    return (x_p.reshape(BATCH, SEQ, D_MODEL), x_s.reshape(DEC_BATCH, 1, D_MODEL),
            st(pk), st(pv), st(pc), st(pg), st(pp), cache5(sk_all), cache5(sv_all), st(sc), sg_all, st(sp))
```

```python
import functools

import jax
import jax.numpy as jnp
from jax import lax
from jax.experimental import pallas as pl
from jax.experimental.pallas import tpu as pltpu

F32 = jnp.float32
BF16 = jnp.bfloat16

D_MODEL = 2048
BATCH = 4
SEQ = 2048
DEPTH = 4
DEC_BATCH = 32
PAST_LEN = 16384
N_EVEN = 2
N_ODD = 2
EPS = 1e-6
NEG_INF = -1e30
A_HEADS = 16
A_KV_HEADS = 4
HEAD_DIM = 64
A_WIDTH = A_HEADS * HEAD_DIM
A_KV_WIDTH = A_KV_HEADS * HEAD_DIM
WINDOW = 128
ROPE_THETA = 10000.0
CONV_DIM = D_MODEL // 2
CONV_W = 3
GLA_HEADS = 4
GLA_WIDTH = D_MODEL // 2
GLA_DV = GLA_WIDTH // GLA_HEADS
GLA_DK = GLA_DV // 2
GLA_K_WIDTH = GLA_HEADS * GLA_DK
GLA_RANK = 16
GLA_TAU = 16.0
GLA_CHUNK = 64
POOL_DIM = D_MODEL // 2
POOL_WINDOWS = (2, 4, 8, 16)
POOL_GDIM = POOL_DIM // 4
POOL_BUF = 15
D_FF = 5632
EVEN_IN = A_WIDTH + 2 * A_KV_WIDTH + 3 * CONV_DIM
ODD_MAIN = 2 * GLA_K_WIDTH + 2 * GLA_WIDTH

M_PROMPT = BATCH * SEQ
LANES = 128
BLK = 128
NBLK = SEQ // BLK
MIX_SUB = 2
POOL_RANK_W = POOL_DIM + LANES
VMEM_LIMIT = 58 * 1024 * 1024


def _cparams(*sem):
    return pltpu.CompilerParams(dimension_semantics=sem, vmem_limit_bytes=VMEM_LIMIT)


def _dot(a, b):
    return jnp.dot(a, b, preferred_element_type=F32)


def _dot_nt(a, b):
    return lax.dot_general(a, b, (((1,), (1,)), ((), ())), preferred_element_type=F32)


def _dot_tn(a, b):
    return lax.dot_general(a, b, (((0,), (0,)), ((), ())), preferred_element_type=F32)


def _silu(x):
    return x * (1.0 / (1.0 + jnp.exp(-x)))


def _rmsnorm_rows(x, gain):
    ms = jnp.mean(x * x, axis=-1, keepdims=True)
    return (x * lax.rsqrt(ms + EPS)) * gain


def _rmsnorm_kernel(xp_ref, xs_ref, g_ref, op_ref, os_ref, *, n_i):
    op_ref[...] = _rmsnorm_rows(xp_ref[...], g_ref[...]).astype(op_ref.dtype)

    @pl.when(pl.program_id(0) == n_i - 1)
    def _():
        os_ref[...] = _rmsnorm_rows(xs_ref[...], g_ref[...]).astype(os_ref.dtype)


def _rmsnorm(x_p, x_s, gain, *, tm=1024):
    n_i = M_PROMPT // tm
    row = lambda i: (i, 0)
    fixed = lambda i: (0, 0)
    return pl.pallas_call(
        functools.partial(_rmsnorm_kernel, n_i=n_i),
        grid=(n_i,),
        in_specs=[pl.BlockSpec((tm, D_MODEL), row),
                  pl.BlockSpec((DEC_BATCH, D_MODEL), fixed),
                  pl.BlockSpec((1, D_MODEL), fixed)],
        out_specs=[pl.BlockSpec((tm, D_MODEL), row),
                   pl.BlockSpec((DEC_BATCH, D_MODEL), fixed)],
        out_shape=[jax.ShapeDtypeStruct((M_PROMPT, D_MODEL), BF16),
                   jax.ShapeDtypeStruct((DEC_BATCH, D_MODEL), BF16)],
        compiler_params=_cparams("arbitrary"),
        name="rmsnorm",
    )(x_p, x_s, gain.reshape(1, D_MODEL))


XPOSE_COLS = 128


def _dense_kernel(*refs, n_w, mode, n_i, transposed, side_cast, sub):
    a_p, a_s = refs[0], refs[1]
    w = refs[2:2 + n_w]
    pos = 2 + n_w
    side_in = side_out = None
    if side_cast:
        side_in = refs[pos]
        pos += 1
    o_p, o_s = refs[pos], refs[pos + 1]
    pos += 2
    if side_cast:
        side_out = refs[pos]
        pos += 1
    wbf = refs[pos:pos + n_w]
    i = pl.program_id(1)

    @pl.when(i == 0)
    def _():
        for k in range(n_w):
            if transposed:
                tn = wbf[k].shape[1]
                for c in range(0, tn, XPOSE_COLS):
                    wbf[k][:, c:c + XPOSE_COLS] = w[k][c:c + XPOSE_COLS, :].T.astype(BF16)
            else:
                wbf[k][...] = w[k][...].astype(BF16)
        if side_cast:
            side_out[...] = side_in[...].astype(BF16)

    def run(a_ref, o_ref, n_sub):
        rows = a_ref.shape[0] // n_sub
        for r in range(n_sub):
            a = a_ref[r * rows:(r + 1) * rows, :]
            if mode == "swiglu":
                y = _silu(_dot(a, wbf[0][...])) * _dot(a, wbf[1][...])
            else:
                y = _dot(a, wbf[0][...])
            o_ref[r * rows:(r + 1) * rows, :] = y.astype(o_ref.dtype)

    run(a_p, o_p, sub)

    @pl.when(i == n_i - 1)
    def _():
        run(a_s, o_s, 1)


def _dense(a_p, a_s, weights, *, n_cols, tm, tn, mode="plain", out_dtype=F32, transposed=False,
           side_cast=None, sub=1, name):
    k_dim = a_p.shape[1]
    n_i = M_PROMPT // tm
    n_j = n_cols // tn
    row = lambda j, i: (i, 0)
    fixed = lambda j, i: (0, 0)
    tile = lambda j, i: (i, j)
    panel = lambda j, i: (0, j)
    in_specs = [pl.BlockSpec((tm, k_dim), row), pl.BlockSpec((DEC_BATCH, k_dim), fixed)]
    args = [a_p, a_s]
    for arr, layer in weights:
        if transposed:
            in_specs.append(pl.BlockSpec((None, tn, k_dim), lambda j, i, layer=layer: (layer, j, 0)))
        else:
            in_specs.append(pl.BlockSpec((None, k_dim, tn), lambda j, i, layer=layer: (layer, 0, j)))
        args.append(arr)
    out_specs = [pl.BlockSpec((tm, tn), tile), pl.BlockSpec((DEC_BATCH, tn), panel)]
    out_shape = [jax.ShapeDtypeStruct((M_PROMPT, n_cols), out_dtype),
                 jax.ShapeDtypeStruct((DEC_BATCH, n_cols), out_dtype)]
    if side_cast is not None:
        s_arr, s_layer = side_cast
        slab = s_arr.shape[1] // n_j
        in_specs.append(pl.BlockSpec((None, slab, s_arr.shape[2]), lambda j, i: (s_layer, j, 0)))
        args.append(s_arr)
        out_specs.append(pl.BlockSpec((slab, s_arr.shape[2]), lambda j, i: (j, 0)))
        out_shape.append(jax.ShapeDtypeStruct(s_arr.shape[1:], BF16))
    n_w = len(weights)
    return pl.pallas_call(
        functools.partial(_dense_kernel, n_w=n_w, mode=mode, n_i=n_i, transposed=transposed,
                          side_cast=side_cast is not None, sub=sub),
        grid=(n_j, n_i),
        in_specs=in_specs,
        out_specs=out_specs,
        out_shape=out_shape,
        scratch_shapes=[pltpu.VMEM((k_dim, tn), BF16) for _ in range(n_w)],
        compiler_params=_cparams("arbitrary", "arbitrary"),
        name=name,
    )(*args)


def _proj_res_kernel(*refs, n_i, with_norm):
    a_p, a_s, w_ref, r_p, r_s = refs[:5]
    if with_norm:
        g_ref, x_p, x_s, h_p, h_s = refs[5:]
    else:
        x_p, x_s = refs[5:]
        g_ref = h_p = h_s = None

    def run(a_ref, r_ref, x_ref, h_ref):
        x = r_ref[...] + _dot(a_ref[...], w_ref[...])
        x_ref[...] = x
        if with_norm:
            h_ref[...] = _rmsnorm_rows(x, g_ref[...]).astype(h_ref.dtype)

    run(a_p, r_p, x_p, h_p)

    @pl.when(pl.program_id(0) == n_i - 1)
    def _():
        run(a_s, r_s, x_s, h_s)


def _proj_res(a_p, a_s, weight, res, gain, *, tm, name):
    w_bf, layer = weight
    k_dim = a_p.shape[1]
    if layer is None:
        w_spec = pl.BlockSpec((k_dim, D_MODEL), lambda i: (0, 0), pipeline_mode=pl.Buffered(1))
    else:
        w_spec = pl.BlockSpec((None, k_dim, D_MODEL), lambda i: (layer, 0, 0), pipeline_mode=pl.Buffered(1))
    n_i = M_PROMPT // tm
    with_norm = gain is not None
    row = lambda i: (i, 0)
    fixed = lambda i: (0, 0)
    rows_p = pl.BlockSpec((tm, D_MODEL), row)
    rows_s = pl.BlockSpec((DEC_BATCH, D_MODEL), fixed)
    in_specs = [pl.BlockSpec((tm, k_dim), row), pl.BlockSpec((DEC_BATCH, k_dim), fixed),
                w_spec, rows_p, rows_s]
    args = [a_p, a_s, w_bf, res[0], res[1]]
    out_specs = [rows_p, rows_s]
    out_shape = [jax.ShapeDtypeStruct((M_PROMPT, D_MODEL), F32), jax.ShapeDtypeStruct((DEC_BATCH, D_MODEL), F32)]
    if with_norm:
        in_specs.append(pl.BlockSpec((1, D_MODEL), fixed))
        args.append(gain.reshape(1, D_MODEL))
        out_specs += [rows_p, rows_s]
        out_shape += [jax.ShapeDtypeStruct((M_PROMPT, D_MODEL), BF16),
                      jax.ShapeDtypeStruct((DEC_BATCH, D_MODEL), BF16)]
    return pl.pallas_call(
        functools.partial(_proj_res_kernel, n_i=n_i, with_norm=with_norm),
        grid=(n_i,),
        in_specs=in_specs,
        out_specs=out_specs,
        out_shape=out_shape,
        compiler_params=_cparams("arbitrary"),
        name=name,
    )(*args)


def _split_cat(x):
    hi = x.astype(BF16)
    lo = (x - hi.astype(F32)).astype(BF16)
    return jnp.concatenate([hi, lo], axis=1)


def _rope_mats():
    j = jnp.arange(2 * LANES)[:, None] % LANES
    l = jnp.arange(LANES)[None, :]
    head_sum = (j // HEAD_DIM == l // HEAD_DIM).astype(BF16)
    src = jnp.where(l % HEAD_DIM < HEAD_DIM // 2, l + HEAD_DIM // 2, l - HEAD_DIM // 2)
    half_swap = (j == src).astype(BF16)
    return head_sum, half_swap


def _norm_rope_chunk(xc, gain, cos, sin, hs_ref, sw_ref):
    ms = _dot(_split_cat(xc * xc), hs_ref[...]) * (1.0 / HEAD_DIM)
    y = (xc * lax.rsqrt(ms + EPS)) * gain
    swapped = _dot(_split_cat(y), sw_ref[...])
    return y * cos + swapped * sin


def _spread_heads(xc, own_lo):
    lane = lax.broadcasted_iota(jnp.int32, xc.shape, 1)
    keep = (lane < HEAD_DIM) if own_lo else (lane >= HEAD_DIM)
    nat = jnp.where(keep, xc, 0.0)
    rol = pltpu.roll(nat, HEAD_DIM, axis=1)
    parts = (nat, rol) if own_lo else (rol, nat)
    return jnp.concatenate(parts, axis=0).astype(BF16)


def _attn_core(qs, kfull, vfull, mask_t, sink_ref):
    rows = qs[0].shape[0]
    nkeys = kfull.shape[0]
    outs = [None] * 8
    for kh in range(A_KV_HEADS):
        c0 = (kh // 2) * LANES
        kk = _spread_heads(kfull[:, c0:c0 + LANES], kh % 2 == 0)
        vv = _spread_heads(vfull[:, c0:c0 + LANES], kh % 2 == 0)
        lhs = jnp.concatenate([qs[2 * kh], qs[2 * kh + 1]], axis=0)
        s = _dot_nt(kk, lhs)
        prow = []
        for half in range(2):
            pcol = []
            for cc in range(2):
                sb = s[half * nkeys:(half + 1) * nkeys, cc * rows:(cc + 1) * rows]
                sb = jnp.where(mask_t, sb, NEG_INF)
                sink = sink_ref[kh * 4 + 2 * cc + half]
                m = jnp.maximum(jnp.max(sb, axis=0, keepdims=True), sink)
                e = jnp.exp(sb - m)
                den = jnp.sum(e, axis=0, keepdims=True) + jnp.exp(sink - m)
                pcol.append((e / den).astype(BF16))
            prow.append(jnp.concatenate(pcol, axis=1))
        p = jnp.concatenate(prow, axis=0)
        o = _dot_tn(p, vv)
        outs[2 * kh] = o[0:rows]
        outs[2 * kh + 1] = o[rows:2 * rows]
    return outs


def _even_prompt_kernel(sink_ref, z_ref, cos_ref, sin_ref, qg_ref, kg_ref, cw_ref, hs_ref, sw_ref, wo_ref,
                        o_ref, nk_ref, nv_ref, nc_ref, wo_bf_ref, kf_ref, vf_ref, ub_ref):
    i = pl.program_id(1)
    wo_bf_ref[...] = wo_ref[...].astype(BF16)

    @pl.when(i == 0)
    def _():
        kf_ref[...] = jnp.zeros_like(kf_ref)
        vf_ref[...] = jnp.zeros_like(vf_ref)
        ub_ref[0:8, :] = jnp.zeros((8, CONV_DIM), F32)

    kk = lax.broadcasted_iota(jnp.int32, (2 * BLK, BLK), 0)
    r = lax.broadcasted_iota(jnp.int32, (2 * BLK, BLK), 1)
    d = kk - r
    band = (d >= 0) & (d <= WINDOW)
    o0 = A_WIDTH + 2 * A_KV_WIDTH
    k = v = None
    for t in range(MIX_SUB):
        rows = slice(t * BLK, (t + 1) * BLK)
        kf_ref[0:BLK, :] = kf_ref[BLK:2 * BLK, :]
        vf_ref[0:BLK, :] = vf_ref[BLK:2 * BLK, :]
        cos = cos_ref[rows, :]
        sin = sin_ref[rows, :]
        k = jnp.concatenate(
            [_norm_rope_chunk(z_ref[rows, A_WIDTH + c * LANES:A_WIDTH + (c + 1) * LANES], kg_ref[...], cos, sin,
                              hs_ref, sw_ref)
             for c in range(A_KV_WIDTH // LANES)], axis=1)
        v = z_ref[rows, A_WIDTH + A_KV_WIDTH:A_WIDTH + 2 * A_KV_WIDTH]
        kf_ref[BLK:2 * BLK, :] = k
        vf_ref[BLK:2 * BLK, :] = v
        qs = [(_norm_rope_chunk(z_ref[rows, c * LANES:(c + 1) * LANES], qg_ref[...], cos, sin, hs_ref, sw_ref)
               * (HEAD_DIM ** -0.5)).astype(BF16) for c in range(A_WIDTH // LANES)]
        mask_t = band & ((kk >= BLK) | (i > 0)) if t == 0 else band
        outs = _attn_core(qs, kf_ref[...], vf_ref[...], mask_t, sink_ref)
        for c in range(A_WIDTH // LANES):
            o_ref[rows, c * LANES:(c + 1) * LANES] = outs[c].astype(o_ref.dtype)

        u = z_ref[rows, o0 + CONV_DIM:o0 + 2 * CONV_DIM] * z_ref[rows, o0 + 2 * CONV_DIM:o0 + 3 * CONV_DIM]
        ub_ref[8:8 + BLK, :] = u
        y = ub_ref[6:6 + BLK, :] * cw_ref[0:1, :]
        y = y + ub_ref[7:7 + BLK, :] * cw_ref[1:2, :]
        y = y + ub_ref[8:8 + BLK, :] * cw_ref[2:3, :]
        o_ref[rows, A_WIDTH:A_WIDTH + CONV_DIM] = (z_ref[rows, o0:o0 + CONV_DIM] * y).astype(o_ref.dtype)
        ub_ref[0:8, :] = ub_ref[BLK:BLK + 8, :]

    @pl.when(i == pl.num_programs(1) - 1)
    def _():
        nk_ref[...] = k
        nv_ref[...] = v
        nc_ref[...] = ub_ref[BLK + 6:BLK + 8, :]


def _w_slab_specs(w_out, layer, n_step):
    slab = w_out.shape[1] // (BATCH * n_step)
    return (pl.BlockSpec((None, slab, w_out.shape[2]), lambda b, i: (layer, b * n_step + i, 0)),
            pl.BlockSpec((slab, w_out.shape[2]), lambda b, i: (b * n_step + i, 0)),
            jax.ShapeDtypeStruct(w_out.shape[1:], BF16))


def _even_prompt(z_p, cos, sin, qg, kg, cw, sinks, mats, w_out, layer):
    n_step = NBLK // MIX_SUB
    step_rows = MIX_SUB * BLK
    blk = lambda b, i: (b * n_step + i, 0)
    fixed = lambda b, i: (0, 0)
    per_b = lambda b, i: (b, 0, 0)
    wo_in, wo_out, wo_shape = _w_slab_specs(w_out, layer, n_step)
    return pl.pallas_call(
        _even_prompt_kernel,
        grid=(BATCH, n_step),
        in_specs=[pl.BlockSpec(memory_space=pltpu.SMEM),
                  pl.BlockSpec((step_rows, EVEN_IN), blk),
                  pl.BlockSpec((step_rows, LANES), lambda b, i: (i, 0)),
                  pl.BlockSpec((step_rows, LANES), lambda b, i: (i, 0)),
                  pl.BlockSpec((1, LANES), fixed),
                  pl.BlockSpec((1, LANES), fixed),
                  pl.BlockSpec((CONV_W, CONV_DIM), fixed),
                  pl.BlockSpec((2 * LANES, LANES), fixed),
                  pl.BlockSpec((2 * LANES, LANES), fixed),
                  wo_in],
        out_specs=[pl.BlockSpec((step_rows, D_MODEL), blk),
                   pl.BlockSpec((None, BLK, A_KV_WIDTH), per_b),
                   pl.BlockSpec((None, BLK, A_KV_WIDTH), per_b),
                   pl.BlockSpec((None, CONV_W - 1, CONV_DIM), per_b),
                   wo_out],
        out_shape=[jax.ShapeDtypeStruct((M_PROMPT, D_MODEL), BF16),
                   jax.ShapeDtypeStruct((BATCH, BLK, A_KV_WIDTH), F32),
                   jax.ShapeDtypeStruct((BATCH, BLK, A_KV_WIDTH), F32),
                   jax.ShapeDtypeStruct((BATCH, CONV_W - 1, CONV_DIM), F32),
                   wo_shape],
        scratch_shapes=[pltpu.VMEM((2 * BLK, A_KV_WIDTH), F32),
                        pltpu.VMEM((2 * BLK, A_KV_WIDTH), F32),
                        pltpu.VMEM((BLK + 8, CONV_DIM), F32)],
        compiler_params=_cparams("arbitrary", "arbitrary"),
        name="even_prompt",
    )(sinks, z_p, cos, sin, qg, kg, cw, *mats, w_out)


def _even_sample_prep_kernel(z_ref, cos_ref, sin_ref, qg_ref, kg_ref, c0_ref, c1_ref, cw_ref, hs_ref, sw_ref,
                             q_ref, k_ref, u_ref, o_ref):
    cos = cos_ref[...]
    sin = sin_ref[...]
    for c in range(A_WIDTH // LANES):
        q = _norm_rope_chunk(z_ref[:, c * LANES:(c + 1) * LANES], qg_ref[...], cos, sin, hs_ref, sw_ref)
        q_ref[:, c * LANES:(c + 1) * LANES] = (q * (HEAD_DIM ** -0.5)).astype(q_ref.dtype)
    for c in range(A_KV_WIDTH // LANES):
        k_ref[:, c * LANES:(c + 1) * LANES] = _norm_rope_chunk(
            z_ref[:, A_WIDTH + c * LANES:A_WIDTH + (c + 1) * LANES], kg_ref[...], cos, sin, hs_ref, sw_ref)
    o0 = A_WIDTH + 2 * A_KV_WIDTH
    u = z_ref[:, o0 + CONV_DIM:o0 + 2 * CONV_DIM] * z_ref[:, o0 + 2 * CONV_DIM:o0 + 3 * CONV_DIM]
    u_ref[...] = u
    y = c0_ref[...] * cw_ref[0:1, :]
    y = y + c1_ref[...] * cw_ref[1:2, :]
    y = y + u * cw_ref[2:3, :]
    o_ref[...] = (z_ref[:, o0:o0 + CONV_DIM] * y).astype(o_ref.dtype)


def _even_sample_prep(z_s, cos, sin, qg, kg, c0, c1, cw, mats):
    return pl.pallas_call(
        _even_sample_prep_kernel,
        out_shape=[jax.ShapeDtypeStruct((DEC_BATCH, A_WIDTH), BF16),
                   jax.ShapeDtypeStruct((DEC_BATCH, A_KV_WIDTH), F32),
                   jax.ShapeDtypeStruct((DEC_BATCH, CONV_DIM), F32),
                   jax.ShapeDtypeStruct((DEC_BATCH, CONV_DIM), BF16)],
        compiler_params=pltpu.CompilerParams(vmem_limit_bytes=VMEM_LIMIT),
        name="even_sample_prep",
    )(z_s, cos, sin, qg, kg, c0, c1, cw, *mats)


SROWS = LANES


def _even_sample_attn_kernel(*refs):
    sink_ref, q_ref, kn_ref, vn_ref, kc_ref, vc_ref = refs[:6]
    o_ref, nk_ref, nv_ref = refs[-3:]
    lc = kc_ref.shape[0]
    row = lax.broadcasted_iota(jnp.int32, (lc, A_KV_WIDTH), 0)
    kfull = jnp.concatenate([kc_ref[...], jnp.where(row == 0, kn_ref[...], 0.0)], axis=0)
    vfull = jnp.concatenate([vc_ref[...], jnp.where(row == 0, vn_ref[...], 0.0)], axis=0)
    qs = [jnp.broadcast_to(q_ref[:, c * LANES:(c + 1) * LANES], (SROWS, LANES))
          for c in range(A_WIDTH // LANES)]
    kk = lax.broadcasted_iota(jnp.int32, (2 * lc, SROWS), 0)
    mask_t = (kk <= lc) & (lc - kk <= WINDOW)
    outs = _attn_core(qs, kfull, vfull, mask_t, sink_ref)
    for c in range(A_WIDTH // LANES):
        o_ref[:, c * LANES:(c + 1) * LANES] = outs[c][0:1].astype(o_ref.dtype)
    nk_ref[0:lc - 1, :] = kc_ref[1:lc, :]
    nk_ref[lc - 1:lc, :] = kn_ref[...]
    nv_ref[0:lc - 1, :] = vc_ref[1:lc, :]
    nv_ref[lc - 1:lc, :] = vn_ref[...]


def _even_sample_attn(sinks, q_s, k_s, v_s, k_cache, v_cache, li, prev):
    lc = k_cache.shape[2]
    vec = lambda w: pl.BlockSpec((None, 1, w), lambda b: (b, 0, 0))
    cache = pl.BlockSpec((None, None, lc, A_KV_WIDTH), lambda b: (li, b, 0, 0))
    in_specs = [pl.BlockSpec(memory_space=pltpu.SMEM), vec(A_WIDTH), vec(A_KV_WIDTH), vec(A_KV_WIDTH),
                cache, cache]
    args = [sinks, q_s.reshape(DEC_BATCH, 1, A_WIDTH), k_s.reshape(DEC_BATCH, 1, A_KV_WIDTH),
            v_s.reshape(DEC_BATCH, 1, A_KV_WIDTH), k_cache, v_cache]
    aliases = {}
    if prev is not None:
        aliases = {len(args): 1, len(args) + 1: 2}
        in_specs += [pl.BlockSpec(memory_space=pl.ANY)] * 2
        args += list(prev)
    return pl.pallas_call(
        _even_sample_attn_kernel,
        grid=(DEC_BATCH,),
        in_specs=in_specs,
        out_specs=[vec(A_WIDTH), cache, cache],
        out_shape=[jax.ShapeDtypeStruct((DEC_BATCH, 1, A_WIDTH), BF16),
                   jax.ShapeDtypeStruct(k_cache.shape, F32),
                   jax.ShapeDtypeStruct(v_cache.shape, F32)],
        input_output_aliases=aliases,
        compiler_params=_cparams("arbitrary"),
        name="even_sample_attn",
    )(*args)


def _log_decay(zr, wa_ref, ba_ref):
    pre = _dot(zr.astype(BF16), wa_ref[...]) + ba_ref[...]
    return (jnp.minimum(pre, 0.0) - jnp.log1p(jnp.exp(-jnp.abs(pre)))) * (1.0 / GLA_TAU)


def _split_bf16(x):
    hi = x.astype(BF16)
    lo = (x - hi.astype(F32)).astype(BF16)
    return hi, lo


def _row_to_col(row):
    n = row.shape[1]
    r = lax.broadcasted_iota(jnp.int32, (n, n), 0)
    c = lax.broadcasted_iota(jnp.int32, (n, n), 1)
    return jnp.sum(jnp.where(r == c, jnp.broadcast_to(row, (n, n)), 0.0), axis=-1, keepdims=True)


def _head_rmsnorm_gate(o, gain, zg):
    ms = jnp.mean(o * o, axis=-1, keepdims=True)
    return ((o * lax.rsqrt(ms + EPS)) * gain) * _silu(zg)


def _odd_prompt_kernel(zm_ref, zpr_ref, wa_ref, ba_ref, gain_ref, wp_ref, ps_ref, wo_ref,
                       o_ref, s_ref, np_ref, wo_bf_ref, pbh_ref, pbl_ref):
    i = pl.program_id(1)
    wo_bf_ref[...] = wo_ref[...].astype(BF16)

    @pl.when(i == 0)
    def _():
        s_ref[...] = jnp.zeros_like(s_ref)
        pbh_ref[0:BLK, :] = jnp.zeros((BLK, POOL_DIM), BF16)
        pbl_ref[0:BLK, :] = jnp.zeros((BLK, POOL_DIM), BF16)

    x = None
    for t in range(MIX_SUB):
        rows = pl.ds(t * BLK, BLK)
        x = _odd_block(zm_ref.at[rows], zpr_ref.at[rows], wa_ref, ba_ref, gain_ref, wp_ref, ps_ref,
                       o_ref.at[rows], s_ref, pbh_ref, pbl_ref, (i * MIX_SUB + t) * BLK)

    @pl.when(i == pl.num_programs(1) - 1)
    def _():
        np_ref[...] = x[BLK - POOL_BUF:BLK, :]


def _odd_block(zm_ref, zpr_ref, wa_ref, ba_ref, gain_ref, wp_ref, ps_ref, o_ref, s_ref, pbh_ref, pbl_ref, t0):
    n_ch = BLK // GLA_CHUNK
    gk = _log_decay(zpr_ref[:, POOL_DIM:POOL_RANK_W], wa_ref, ba_ref)
    r = lax.broadcasted_iota(jnp.int32, (BLK, BLK), 0)
    c = lax.broadcasted_iota(jnp.int32, (BLK, BLK), 1)
    causal = (r >= c) & (r // GLA_CHUNK == c // GLA_CHUNK)
    tri = jnp.where(causal, 1.0, 0.0).astype(BF16)
    g_hi, g_lo = _split_bf16(gk)
    b = _dot(jnp.concatenate([tri, tri], axis=1), jnp.concatenate([g_hi, g_lo], axis=0))
    b_last = [b[(ch + 1) * GLA_CHUNK - 1:(ch + 1) * GLA_CHUNK, :] for ch in range(n_ch)]
    b_end = jnp.concatenate([jnp.broadcast_to(bl, (GLA_CHUNK, GLA_K_WIDTH)) for bl in b_last], axis=0)
    o_k, o_v, o_g = GLA_K_WIDTH, 2 * GLA_K_WIDTH, 2 * GLA_K_WIDTH + GLA_WIDTH
    zq = zm_ref[:, 0:GLA_K_WIDTH]
    zk = zm_ref[:, o_k:o_k + GLA_K_WIDTH]
    qd = ((zq * (GLA_DK ** -0.5)) * jnp.exp(b)).astype(BF16)
    kd = (zk * jnp.exp(-b)).astype(BF16)
    k2 = (zk * jnp.exp(b_end - b)).astype(BF16)
    sub = lax.broadcasted_iota(jnp.int32, (8, LANES), 0)
    sel = jnp.where(sub < 2, 1.0, 0.0).astype(BF16)
    hk = [slice(h * GLA_DK, (h + 1) * GLA_DK) for h in range(GLA_HEADS)]
    chunk = [slice(ch * GLA_CHUNK, (ch + 1) * GLA_CHUNK) for ch in range(n_ch)]
    v_hs = [zm_ref[:, o_v + h * GLA_DV:o_v + (h + 1) * GLA_DV].astype(BF16) for h in range(GLA_HEADS)]
    atts = [_dot_nt(qd[:, hk[h]], kd[:, hk[h]]) for h in range(GLA_HEADS)]
    log_decs = []
    for h in range(GLA_HEADS):
        for ch in range(n_ch):
            bl = b_last[ch][:, hk[h]]
            bl_hi = bl.astype(BF16).astype(F32)
            rows8 = jnp.where(sub == 0, bl_hi, jnp.where(sub == 1, bl - bl_hi, 0.0)).astype(BF16)
            log_decs.append(_dot_tn(rows8, sel))
    deltas = [[_dot_tn(k2[chunk[ch], hk[h]], v_hs[h][chunk[ch]]) for ch in range(n_ch)]
              for h in range(GLA_HEADS)]
    o_intras = [_dot(jnp.where(causal, atts[h], 0.0).astype(BF16), v_hs[h]) for h in range(GLA_HEADS)]
    o_inters = []
    for h in range(GLA_HEADS):
        s_h = s_ref[h]
        parts = []
        for ch in range(n_ch):
            parts.append(_dot(qd[chunk[ch], hk[h]], s_h.astype(BF16)))
            dec = jnp.exp(log_decs[h * n_ch + ch])
            s_h = jnp.concatenate([dec, dec], axis=1) * s_h + deltas[h][ch]
        s_ref[h] = s_h
        o_inters.append(jnp.concatenate(parts, axis=0))
    for h in range(GLA_HEADS):
        vs = slice(h * GLA_DV, (h + 1) * GLA_DV)
        zg = zm_ref[:, o_g + h * GLA_DV:o_g + (h + 1) * GLA_DV]
        o_ref[:, vs] = _head_rmsnorm_gate(o_intras[h] + o_inters[h], gain_ref[...], zg).astype(o_ref.dtype)

    x = zpr_ref[:, 0:POOL_DIM]
    x_hi, x_lo = _split_bf16(x)
    pbh_ref[BLK:2 * BLK, :] = x_hi
    pbl_ref[BLK:2 * BLK, :] = x_lo
    t = lax.broadcasted_iota(jnp.int32, (BLK, 2 * BLK), 0)
    j = lax.broadcasted_iota(jnp.int32, (BLK, 2 * BLK), 1)
    back = t + BLK - j
    t_glob = t0 + lax.broadcasted_iota(jnp.int32, (BLK, 1), 0)
    groups =[slice(g * POOL_GDIM, (g + 1) * POOL_GDIM) for g in range(len(POOL_WINDOWS))]
    sums = []
    for g, w in enumerate(POOL_WINDOWS):
        band = jnp.where((back >= 0) & (back < w), 1.0, 0.0).astype(BF16)
        sums.append(_dot(jnp.concatenate([band, band], axis=1),
                         jnp.concatenate([pbh_ref[:, groups[g]], pbl_ref[:, groups[g]]], axis=0)))
    ys = []
    for g, w in enumerate(POOL_WINDOWS):
        inv_cnt = 1.0 / jnp.minimum(w, t_glob + 1).astype(F32)
        d = sums[g] * inv_cnt - x[:, groups[g]]
        ys.append(_dot(d.astype(BF16), wp_ref[g]))
    for g in range(len(POOL_WINDOWS)):
        o_ref[:, GLA_WIDTH + g * POOL_GDIM:GLA_WIDTH + (g + 1) * POOL_GDIM] = (
            ys[g] * ps_ref[:, groups[g]]).astype(o_ref.dtype)
    pbh_ref[0:BLK, :] = x_hi
    pbl_ref[0:BLK, :] = x_lo
    return x


def _odd_prompt(zm_p, zpr_p, wa, ba, gain, wp, ps, w_out, layer):
    n_step = NBLK // MIX_SUB
    step_rows = MIX_SUB * BLK
    blk = lambda b, i: (b * n_step + i, 0)
    fixed2 = lambda b, i: (0, 0)
    wo_in, wo_out, wo_shape = _w_slab_specs(w_out, layer, n_step)
    return pl.pallas_call(
        _odd_prompt_kernel,
        grid=(BATCH, n_step),
        in_specs=[pl.BlockSpec((step_rows, ODD_MAIN), blk),
                  pl.BlockSpec((step_rows, POOL_RANK_W), blk),
                  pl.BlockSpec((LANES, GLA_K_WIDTH), fixed2),
                  pl.BlockSpec((1, GLA_K_WIDTH), fixed2),
                  pl.BlockSpec((1, GLA_DV), fixed2),
                  pl.BlockSpec((4, POOL_GDIM, POOL_GDIM), lambda b, i: (0, 0, 0)),
                  pl.BlockSpec((1, POOL_DIM), fixed2),
                  wo_in],
        out_specs=[pl.BlockSpec((step_rows, D_MODEL), blk),
                   pl.BlockSpec((None, GLA_HEADS, GLA_DK, GLA_DV), lambda b, i: (b, 0, 0, 0)),
                   pl.BlockSpec((None, POOL_BUF, POOL_DIM), lambda b, i: (b, 0, 0)),
                   wo_out],
        out_shape=[jax.ShapeDtypeStruct((M_PROMPT, D_MODEL), BF16),
                   jax.ShapeDtypeStruct((BATCH, GLA_HEADS, GLA_DK, GLA_DV), F32),
                   jax.ShapeDtypeStruct((BATCH, POOL_BUF, POOL_DIM), F32),
                   wo_shape],
        scratch_shapes=[pltpu.VMEM((2 * BLK, POOL_DIM), BF16), pltpu.VMEM((2 * BLK, POOL_DIM), BF16)],
        compiler_params=_cparams("arbitrary", "arbitrary"),
        name="odd_prompt",
    )(zm_p, zpr_p, wa, ba, gain, wp, ps, w_out)


def _odd_sample_prep_kernel(zm_ref, zr_ref, wa_ref, ba_ref, dec_ref, q_ref):
    dec_ref[...] = jnp.exp(_log_decay(zr_ref[...], wa_ref, ba_ref))
    q_ref[...] = zm_ref[:, 0:GLA_K_WIDTH] * (GLA_DK ** -0.5)


def _odd_sample_prep(zm_s, zr_s, wa, ba):
    return pl.pallas_call(
        _odd_sample_prep_kernel,
        out_shape=[jax.ShapeDtypeStruct((DEC_BATCH, GLA_K_WIDTH), F32),
                   jax.ShapeDtypeStruct((DEC_BATCH, GLA_K_WIDTH), F32)],
        compiler_params=pltpu.CompilerParams(vmem_limit_bytes=VMEM_LIMIT),
        name="odd_sample_prep",
    )(zm_s, zr_s, wa, ba)


STATE_BB = 4


def _odd_sample_state_kernel(*refs):
    dec_ref, q_ref, k_ref, v_ref, s_ref = refs[:5]
    ns_ref, o_ref = refs[-2:]
    for bb in range(STATE_BB):
        for h in range(GLA_HEADS):
            ks = slice(h * GLA_DK, (h + 1) * GLA_DK)
            vs = slice(h * GLA_DV, (h + 1) * GLA_DV)
            s_new = (_row_to_col(dec_ref[bb][:, ks]) * s_ref[bb, h]
                     + _row_to_col(k_ref[bb][:, ks]) * v_ref[bb][:, vs])
            ns_ref[bb, h] = s_new
            o_ref[bb, :, vs] = jnp.sum(_row_to_col(q_ref[bb][:, ks]) * s_new, axis=0, keepdims=True)


def _odd_sample_state(dec, q, k, v, state, li, prev):
    vec = lambda w: pl.BlockSpec((STATE_BB, 1, w), lambda b: (b, 0, 0))
    st = pl.BlockSpec((None, STATE_BB, GLA_HEADS, GLA_DK, GLA_DV), lambda b: (li, b, 0, 0, 0))
    r3 = lambda a: a.reshape(DEC_BATCH, 1, a.shape[-1])
    in_specs = [vec(GLA_K_WIDTH), vec(GLA_K_WIDTH), vec(GLA_K_WIDTH), vec(GLA_WIDTH), st]
    args = [r3(dec), r3(q), r3(k), r3(v), state]
    aliases = {}
    if prev is not None:
        aliases = {len(args): 0}
        in_specs.append(pl.BlockSpec(memory_space=pl.ANY))
        args.append(prev)
    return pl.pallas_call(
        _odd_sample_state_kernel,
        grid=(DEC_BATCH // STATE_BB,),
        in_specs=in_specs,
        out_specs=[st, vec(GLA_WIDTH)],
        out_shape=[jax.ShapeDtypeStruct(state.shape, F32),
                   jax.ShapeDtypeStruct((DEC_BATCH, 1, GLA_WIDTH), F32)],
        input_output_aliases=aliases,
        compiler_params=_cparams("arbitrary"),
        name="odd_sample_state",
    )(*args)


def _odd_sample_post_kernel(o_in_ref, zm_ref, zp_ref, hist_ref, gain_ref, wp_ref, ps_ref, o_ref):
    o_g = 2 * GLA_K_WIDTH + GLA_WIDTH
    for h in range(GLA_HEADS):
        vs = slice(h * GLA_DV, (h + 1) * GLA_DV)
        zg = zm_ref[:, o_g + h * GLA_DV:o_g + (h + 1) * GLA_DV]
        o_ref[:, vs] = _head_rmsnorm_gate(o_in_ref[:, vs], gain_ref[...], zg).astype(o_ref.dtype)
    n_prev = hist_ref.shape[0]
    for g, w in enumerate(POOL_WINDOWS):
        gs = slice(g * POOL_GDIM, (g + 1) * POOL_GDIM)
        x = zp_ref[:, gs]
        s = hist_ref[n_prev - (w - 1)][:, gs]
        for jj in range(n_prev - (w - 1) + 1, n_prev):
            s = s + hist_ref[jj][:, gs]
        s = s + x
        d = s / float(min(w, n_prev + 1)) - x
        y = _dot(d.astype(BF16), wp_ref[g]) * ps_ref[:, gs]
        o_ref[:, GLA_WIDTH + g * POOL_GDIM:GLA_WIDTH + (g + 1) * POOL_GDIM] = y.astype(o_ref.dtype)


def _odd_sample_post(o_raw, zm_s, zp_s, hist_t, gain, wp, ps):
    return pl.pallas_call(
        _odd_sample_post_kernel,
        out_shape=jax.ShapeDtypeStruct((DEC_BATCH, D_MODEL), BF16),
        compiler_params=pltpu.CompilerParams(vmem_limit_bytes=VMEM_LIMIT),
        name="odd_sample_post",
    )(o_raw, zm_s, zp_s, hist_t, gain, wp, ps)


def _rope_tables(pos):
    half = HEAD_DIM // 2
    inv = jnp.power(ROPE_THETA, -jnp.arange(half, dtype=F32) / half)
    ang = pos.astype(F32)[:, None] * inv[None, :]
    c, s = jnp.cos(ang), jnp.sin(ang)
    return jnp.tile(c, (1, 4)), jnp.tile(jnp.concatenate([-s, s], axis=1), (1, 2))


def kernel(x_prompt, x_sample, cache_swa_k, cache_swa_v, state_conv, state_gla, state_pool, norm_mix, norm_ffn, w_in_even, w_out_even, q_norm, k_norm, attn_sinks, conv_w, w_in_odd, w_out_odd, w_alpha_up, b_alpha, gla_out_norm, w_pool, pool_scale, w_gate, w_up, w_down):
    lc = cache_swa_k.shape[2]
    x_p = x_prompt.reshape(M_PROMPT, D_MODEL)
    x_s = x_sample.reshape(DEC_BATCH, D_MODEL)
    cos_p, sin_p = _rope_tables(jnp.arange(SEQ))
    cos_s, sin_s = _rope_tables(PAST_LEN + jnp.arange(1))
    mats = _rope_mats()
    kc_all = cache_swa_k.reshape(N_EVEN, DEC_BATCH, lc, A_KV_WIDTH)
    vc_all = cache_swa_v.reshape(N_EVEN, DEC_BATCH, lc, A_KV_WIDTH)
    o_r = ODD_MAIN
    w_odd_t = jnp.swapaxes(w_in_odd, 1, 2)
    w_odd_pr_t = jnp.concatenate([w_odd_t[:, o_r + GLA_RANK:], w_odd_t[:, o_r:o_r + GLA_RANK],
                                  jnp.zeros((N_ODD, LANES - GLA_RANK, D_MODEL), F32)], axis=1)
    pk, pv, pc, pg, pp, sc, sp = ([] for _ in range(7))
    sk_all = sv_all = sg_all = None

    h_p, h_s = _rmsnorm(x_p, x_s, norm_mix[0])
    for layer in range(DEPTH):
        li = layer // 2
        if layer % 2 == 0:
            z_p, z_s = _dense(h_p, h_s, [(w_in_even, li)], n_cols=EVEN_IN, tm=1024, tn=1536, name="in_even")
            qg = jnp.tile(q_norm[li], 2).reshape(1, LANES)
            kg = jnp.tile(k_norm[li], 2).reshape(1, LANES)
            m_p, nk, nv, nc, w_out_bf = _even_prompt(z_p, cos_p, sin_p, qg, kg, conv_w[li], attn_sinks[li], mats,
                                                     w_out_even, li)
            pk.append(nk.reshape(BATCH, lc, A_KV_HEADS, HEAD_DIM))
            pv.append(nv.reshape(BATCH, lc, A_KV_HEADS, HEAD_DIM))
            pc.append(nc)
            q_s, k_s, u_s, conv_s = _even_sample_prep(
                z_s, cos_s, sin_s, qg, kg, state_conv[li, :, 0], state_conv[li, :, 1], conv_w[li], mats)
            v_s = z_s[:, A_WIDTH + A_KV_WIDTH:A_WIDTH + 2 * A_KV_WIDTH]
            attn_s, sk_all, sv_all = _even_sample_attn(
                attn_sinks[li], q_s, k_s, v_s, kc_all, vc_all, li,
                None if sk_all is None else (sk_all, sv_all))
            sc.append(jnp.stack([state_conv[li, :, 1], u_s], axis=1))
            m_s = jnp.concatenate([attn_s.reshape(DEC_BATCH, A_WIDTH), conv_s], axis=1)
        else:
            zm_p, zm_s = _dense(h_p, h_s, [(w_odd_t, li)], n_cols=ODD_MAIN, tm=512, tn=ODD_MAIN // 2,
                                transposed=True, name="in_odd_main")
            zpr_p, zpr_s = _dense(h_p, h_s, [(w_odd_pr_t, li)], n_cols=POOL_RANK_W, tm=1024, tn=POOL_RANK_W,
                                  transposed=True, name="in_odd_pool_rank")
            zp_s, zr_s = zpr_s[:, :POOL_DIM], zpr_s[:, POOL_DIM:]
            wa = jnp.pad(w_alpha_up[li], ((0, LANES - GLA_RANK), (0, 0))).astype(BF16)
            ba = b_alpha[li].reshape(1, GLA_K_WIDTH)
            gain = gla_out_norm[li].reshape(1, GLA_DV)
            wp = w_pool[li].astype(BF16)
            ps = pool_scale[li].reshape(1, POOL_DIM)
            m_p, ng, npool, w_out_bf = _odd_prompt(zm_p, zpr_p, wa, ba, gain, wp, ps, w_out_odd, li)
            pg.append(ng)
            pp.append(npool)
            dec_s, q_s = _odd_sample_prep(zm_s, zr_s, wa, ba)
            sg_all, o_raw = _odd_sample_state(
                dec_s, q_s, zm_s[:, GLA_K_WIDTH:2 * GLA_K_WIDTH],
                zm_s[:, 2 * GLA_K_WIDTH:2 * GLA_K_WIDTH + GLA_WIDTH], state_gla, li, sg_all)
            m_s = _odd_sample_post(o_raw.reshape(DEC_BATCH, GLA_WIDTH), zm_s, zp_s,
                                   jnp.swapaxes(state_pool[li], 0, 1), gain, wp, ps)
            sp.append(jnp.concatenate([state_pool[li, :, 1:], zp_s[:, None, :]], axis=1))
        x_p, x_s, h_p, h_s = _proj_res(m_p, m_s, (w_out_bf, None), (x_p, x_s), norm_ffn[layer],
                                       tm=512, name="out_proj")
        a_p, a_s, w_down_bf = _dense(h_p, h_s, [(w_gate, layer), (w_up, layer)], n_cols=D_FF, tm=2048, tn=512,
                                     mode="swiglu", out_dtype=BF16, side_cast=(w_down, layer), sub=2,
                                     name="ffn_up")
        if layer + 1 < DEPTH:
            x_p, x_s, h_p, h_s = _proj_res(a_p, a_s, (w_down_bf, None), (x_p, x_s),
                                           norm_mix[layer + 1], tm=512, name="ffn_down")
        else:
            x_p, x_s = _proj_res(a_p, a_s, (w_down_bf, None), (x_p, x_s), None,
                                 tm=512, name="ffn_down_last")

    st = lambda parts: jnp.stack(parts)
    cache5 = lambda a: a.reshape(N_EVEN, DEC_BATCH, lc, A_KV_HEADS, HEAD_DIM)
    return (x_p.reshape(BATCH, SEQ, D_MODEL), x_s.reshape(DEC_BATCH, 1, D_MODEL),
            st(pk), st(pv), st(pc), st(pg), st(pp), cache5(sk_all), cache5(sv_all), st(sc), sg_all, st(sp))
```

```python
import functools

import jax
import jax.numpy as jnp
from jax import lax
from jax.experimental import pallas as pl
from jax.experimental.pallas import tpu as pltpu

F32 = jnp.float32
BF16 = jnp.bfloat16

D_MODEL = 2048
BATCH = 4
SEQ = 2048
DEPTH = 4
DEC_BATCH = 32
PAST_LEN = 16384
N_EVEN = 2
N_ODD = 2
EPS = 1e-6
NEG_INF = -1e30
A_HEADS = 16
A_KV_HEADS = 4
HEAD_DIM = 64
A_WIDTH = A_HEADS * HEAD_DIM
A_KV_WIDTH = A_KV_HEADS * HEAD_DIM
WINDOW = 128
ROPE_THETA = 10000.0
CONV_DIM = D_MODEL // 2
CONV_W = 3
GLA_HEADS = 4
GLA_WIDTH = D_MODEL // 2
GLA_DV = GLA_WIDTH // GLA_HEADS
GLA_DK = GLA_DV // 2
GLA_K_WIDTH = GLA_HEADS * GLA_DK
GLA_RANK = 16
GLA_TAU = 16.0
GLA_CHUNK = 64
POOL_DIM = D_MODEL // 2
POOL_WINDOWS = (2, 4, 8, 16)
POOL_GDIM = POOL_DIM // 4
POOL_BUF = 15
D_FF = 5632
EVEN_IN = A_WIDTH + 2 * A_KV_WIDTH + 3 * CONV_DIM
ODD_MAIN = 2 * GLA_K_WIDTH + 2 * GLA_WIDTH

M_PROMPT = BATCH * SEQ
LANES = 128
BLK = 128
NBLK = SEQ // BLK
MIX_SUB = 4
POOL_RANK_W = POOL_DIM + LANES
VMEM_LIMIT = 58 * 1024 * 1024


def _cparams(*sem):
    return pltpu.CompilerParams(dimension_semantics=sem, vmem_limit_bytes=VMEM_LIMIT)


def _dot(a, b):
    return jnp.dot(a, b, preferred_element_type=F32)


def _dot_nt(a, b):
    return lax.dot_general(a, b, (((1,), (1,)), ((), ())), preferred_element_type=F32)


def _dot_tn(a, b):
    return lax.dot_general(a, b, (((0,), (0,)), ((), ())), preferred_element_type=F32)


def _silu(x):
    return x * (1.0 / (1.0 + jnp.exp(-x)))


def _rmsnorm_rows(x, gain):
    ms = jnp.mean(x * x, axis=-1, keepdims=True)
    return (x * lax.rsqrt(ms + EPS)) * gain


def _rmsnorm_kernel(xp_ref, xs_ref, g_ref, op_ref, os_ref, *, n_i):
    op_ref[...] = _rmsnorm_rows(xp_ref[...], g_ref[...]).astype(op_ref.dtype)

    @pl.when(pl.program_id(0) == n_i - 1)
    def _():
        os_ref[...] = _rmsnorm_rows(xs_ref[...], g_ref[...]).astype(os_ref.dtype)


def _rmsnorm(x_p, x_s, gain, *, tm=1024):
    n_i = M_PROMPT // tm
    row = lambda i: (i, 0)
    fixed = lambda i: (0, 0)
    return pl.pallas_call(
        functools.partial(_rmsnorm_kernel, n_i=n_i),
        grid=(n_i,),
        in_specs=[pl.BlockSpec((tm, D_MODEL), row),
                  pl.BlockSpec((DEC_BATCH, D_MODEL), fixed),
                  pl.BlockSpec((1, D_MODEL), fixed)],
        out_specs=[pl.BlockSpec((tm, D_MODEL), row),
                   pl.BlockSpec((DEC_BATCH, D_MODEL), fixed)],
        out_shape=[jax.ShapeDtypeStruct((M_PROMPT, D_MODEL), BF16),
                   jax.ShapeDtypeStruct((DEC_BATCH, D_MODEL), BF16)],
        compiler_params=_cparams("arbitrary"),
        name="rmsnorm",
    )(x_p, x_s, gain.reshape(1, D_MODEL))


XPOSE_COLS = 128


def _dense_kernel(*refs, n_w, mode, n_i, transposed, side_cast, sub):
    a_p, a_s = refs[0], refs[1]
    w = refs[2:2 + n_w]
    pos = 2 + n_w
    side_in = side_out = None
    if side_cast:
        side_in = refs[pos]
        pos += 1
    o_p, o_s = refs[pos], refs[pos + 1]
    pos += 2
    if side_cast:
        side_out = refs[pos]
        pos += 1
    wbf = refs[pos:pos + n_w]
    i = pl.program_id(1)

    @pl.when(i == 0)
    def _():
        for k in range(n_w):
            if transposed:
                tn = wbf[k].shape[1]
                for c in range(0, tn, XPOSE_COLS):
                    wbf[k][:, c:c + XPOSE_COLS] = w[k][c:c + XPOSE_COLS, :].T.astype(BF16)
            else:
                wbf[k][...] = w[k][...].astype(BF16)
        if side_cast:
            side_out[...] = side_in[...].astype(BF16)

    def run(a_ref, o_ref, n_sub):
        rows = a_ref.shape[0] // n_sub
        for r in range(n_sub):
            a = a_ref[r * rows:(r + 1) * rows, :]
            if mode == "swiglu":
                y = _silu(_dot(a, wbf[0][...])) * _dot(a, wbf[1][...])
            else:
                y = _dot(a, wbf[0][...])
            o_ref[r * rows:(r + 1) * rows, :] = y.astype(o_ref.dtype)

    run(a_p, o_p, sub)

    @pl.when(i == n_i - 1)
    def _():
        run(a_s, o_s, 1)


def _dense(a_p, a_s, weights, *, n_cols, tm, tn, mode="plain", out_dtype=F32, transposed=False,
           side_cast=None, sub=1, name):
    k_dim = a_p.shape[1]
    n_i = M_PROMPT // tm
    n_j = n_cols // tn
    row = lambda j, i: (i, 0)
    fixed = lambda j, i: (0, 0)
    tile = lambda j, i: (i, j)
    panel = lambda j, i: (0, j)
    in_specs = [pl.BlockSpec((tm, k_dim), row), pl.BlockSpec((DEC_BATCH, k_dim), fixed)]
    args = [a_p, a_s]
    for arr, layer in weights:
        if transposed:
            in_specs.append(pl.BlockSpec((None, tn, k_dim), lambda j, i, layer=layer: (layer, j, 0)))
        else:
            in_specs.append(pl.BlockSpec((None, k_dim, tn), lambda j, i, layer=layer: (layer, 0, j)))
        args.append(arr)
    out_specs = [pl.BlockSpec((tm, tn), tile), pl.BlockSpec((DEC_BATCH, tn), panel)]
    out_shape = [jax.ShapeDtypeStruct((M_PROMPT, n_cols), out_dtype),
                 jax.ShapeDtypeStruct((DEC_BATCH, n_cols), out_dtype)]
    if side_cast is not None:
        s_arr, s_layer = side_cast
        slab = s_arr.shape[1] // n_j
        in_specs.append(pl.BlockSpec((None, slab, s_arr.shape[2]), lambda j, i: (s_layer, j, 0)))
        args.append(s_arr)
        out_specs.append(pl.BlockSpec((slab, s_arr.shape[2]), lambda j, i: (j, 0)))
        out_shape.append(jax.ShapeDtypeStruct(s_arr.shape[1:], BF16))
    n_w = len(weights)
    return pl.pallas_call(
        functools.partial(_dense_kernel, n_w=n_w, mode=mode, n_i=n_i, transposed=transposed,
                          side_cast=side_cast is not None, sub=sub),
        grid=(n_j, n_i),
        in_specs=in_specs,
        out_specs=out_specs,
        out_shape=out_shape,
        scratch_shapes=[pltpu.VMEM((k_dim, tn), BF16) for _ in range(n_w)],
        compiler_params=_cparams("arbitrary", "arbitrary"),
        name=name,
    )(*args)


def _proj_res_kernel(*refs, n_i, with_norm):
    a_p, a_s, w_ref, r_p, r_s = refs[:5]
    if with_norm:
        g_ref, x_p, x_s, h_p, h_s = refs[5:]
    else:
        x_p, x_s = refs[5:]
        g_ref = h_p = h_s = None

    def run(a_ref, r_ref, x_ref, h_ref):
        x = r_ref[...] + _dot(a_ref[...], w_ref[...])
        x_ref[...] = x
        if with_norm:
            h_ref[...] = _rmsnorm_rows(x, g_ref[...]).astype(h_ref.dtype)

    run(a_p, r_p, x_p, h_p)

    @pl.when(pl.program_id(0) == n_i - 1)
    def _():
        run(a_s, r_s, x_s, h_s)


def _proj_res(a_p, a_s, weight, res, gain, *, tm, name):
    w_bf, layer = weight
    k_dim = a_p.shape[1]
    if layer is None:
        w_spec = pl.BlockSpec((k_dim, D_MODEL), lambda i: (0, 0), pipeline_mode=pl.Buffered(1))
    else:
        w_spec = pl.BlockSpec((None, k_dim, D_MODEL), lambda i: (layer, 0, 0), pipeline_mode=pl.Buffered(1))
    n_i = M_PROMPT // tm
    with_norm = gain is not None
    row = lambda i: (i, 0)
    fixed = lambda i: (0, 0)
    rows_p = pl.BlockSpec((tm, D_MODEL), row)
    rows_s = pl.BlockSpec((DEC_BATCH, D_MODEL), fixed)
    in_specs = [pl.BlockSpec((tm, k_dim), row), pl.BlockSpec((DEC_BATCH, k_dim), fixed),
                w_spec, rows_p, rows_s]
    args = [a_p, a_s, w_bf, res[0], res[1]]
    out_specs = [rows_p, rows_s]
    out_shape = [jax.ShapeDtypeStruct((M_PROMPT, D_MODEL), F32), jax.ShapeDtypeStruct((DEC_BATCH, D_MODEL), F32)]
    if with_norm:
        in_specs.append(pl.BlockSpec((1, D_MODEL), fixed))
        args.append(gain.reshape(1, D_MODEL))
        out_specs += [rows_p, rows_s]
        out_shape += [jax.ShapeDtypeStruct((M_PROMPT, D_MODEL), BF16),
                      jax.ShapeDtypeStruct((DEC_BATCH, D_MODEL), BF16)]
    return pl.pallas_call(
        functools.partial(_proj_res_kernel, n_i=n_i, with_norm=with_norm),
        grid=(n_i,),
        in_specs=in_specs,
        out_specs=out_specs,
        out_shape=out_shape,
        compiler_params=_cparams("arbitrary"),
        name=name,
    )(*args)


def _split_cat(x):
    hi = x.astype(BF16)
    lo = (x - hi.astype(F32)).astype(BF16)
    return jnp.concatenate([hi, lo], axis=1)


def _rope_mats():
    j = jnp.arange(2 * LANES)[:, None] % LANES
    l = jnp.arange(LANES)[None, :]
    head_sum = (j // HEAD_DIM == l // HEAD_DIM).astype(BF16)
    src = jnp.where(l % HEAD_DIM < HEAD_DIM // 2, l + HEAD_DIM // 2, l - HEAD_DIM // 2)
    half_swap = (j == src).astype(BF16)
    return head_sum, half_swap


def _norm_rope_chunk(xc, gain, cos, sin, hs_ref, sw_ref):
    ms = _dot(_split_cat(xc * xc), hs_ref[...]) * (1.0 / HEAD_DIM)
    y = (xc * lax.rsqrt(ms + EPS)) * gain
    swapped = _dot(_split_cat(y), sw_ref[...])
    return y * cos + swapped * sin


def _spread_heads(xc, own_lo):
    lane = lax.broadcasted_iota(jnp.int32, xc.shape, 1)
    keep = (lane < HEAD_DIM) if own_lo else (lane >= HEAD_DIM)
    nat = jnp.where(keep, xc, 0.0)
    rol = pltpu.roll(nat, HEAD_DIM, axis=1)
    parts = (nat, rol) if own_lo else (rol, nat)
    return jnp.concatenate(parts, axis=0).astype(BF16)


def _attn_core(qs, kfull, vfull, mask_t, sink_ref):
    rows = qs[0].shape[0]
    nkeys = kfull.shape[0]
    outs = [None] * 8
    for kh in range(A_KV_HEADS):
        c0 = (kh // 2) * LANES
        kk = _spread_heads(kfull[:, c0:c0 + LANES], kh % 2 == 0)
        vv = _spread_heads(vfull[:, c0:c0 + LANES], kh % 2 == 0)
        lhs = jnp.concatenate([qs[2 * kh], qs[2 * kh + 1]], axis=0)
        s = _dot_nt(kk, lhs)
        prow = []
        for half in range(2):
            pcol = []
            for cc in range(2):
                sb = s[half * nkeys:(half + 1) * nkeys, cc * rows:(cc + 1) * rows]
                sb = jnp.where(mask_t, sb, NEG_INF)
                sink = sink_ref[kh * 4 + 2 * cc + half]
                m = jnp.maximum(jnp.max(sb, axis=0, keepdims=True), sink)
                e = jnp.exp(sb - m)
                den = jnp.sum(e, axis=0, keepdims=True) + jnp.exp(sink - m)
                pcol.append((e / den).astype(BF16))
            prow.append(jnp.concatenate(pcol, axis=1))
        p = jnp.concatenate(prow, axis=0)
        o = _dot_tn(p, vv)
        outs[2 * kh] = o[0:rows]
        outs[2 * kh + 1] = o[rows:2 * rows]
    return outs


def _even_prompt_kernel(sink_ref, z_ref, cos_ref, sin_ref, qg_ref, kg_ref, cw_ref, hs_ref, sw_ref, wo_ref,
                        o_ref, nk_ref, nv_ref, nc_ref, wo_bf_ref, kf_ref, vf_ref, ub_ref):
    i = pl.program_id(1)
    wo_bf_ref[...] = wo_ref[...].astype(BF16)

    @pl.when(i == 0)
    def _():
        kf_ref[...] = jnp.zeros_like(kf_ref)
        vf_ref[...] = jnp.zeros_like(vf_ref)
        ub_ref[0:8, :] = jnp.zeros((8, CONV_DIM), F32)

    kk = lax.broadcasted_iota(jnp.int32, (2 * BLK, BLK), 0)
    r = lax.broadcasted_iota(jnp.int32, (2 * BLK, BLK), 1)
    d = kk - r
    band = (d >= 0) & (d <= WINDOW)
    o0 = A_WIDTH + 2 * A_KV_WIDTH
    k = v = None
    for t in range(MIX_SUB):
        rows = slice(t * BLK, (t + 1) * BLK)
        kf_ref[0:BLK, :] = kf_ref[BLK:2 * BLK, :]
        vf_ref[0:BLK, :] = vf_ref[BLK:2 * BLK, :]
        cos = cos_ref[rows, :]
        sin = sin_ref[rows, :]
        k = jnp.concatenate(
            [_norm_rope_chunk(z_ref[rows, A_WIDTH + c * LANES:A_WIDTH + (c + 1) * LANES], kg_ref[...], cos, sin,
                              hs_ref, sw_ref)
             for c in range(A_KV_WIDTH // LANES)], axis=1)
        v = z_ref[rows, A_WIDTH + A_KV_WIDTH:A_WIDTH + 2 * A_KV_WIDTH]
        kf_ref[BLK:2 * BLK, :] = k
        vf_ref[BLK:2 * BLK, :] = v
        qs = [(_norm_rope_chunk(z_ref[rows, c * LANES:(c + 1) * LANES], qg_ref[...], cos, sin, hs_ref, sw_ref)
               * (HEAD_DIM ** -0.5)).astype(BF16) for c in range(A_WIDTH // LANES)]
        mask_t = band & ((kk >= BLK) | (i > 0)) if t == 0 else band
        outs = _attn_core(qs, kf_ref[...], vf_ref[...], mask_t, sink_ref)
        for c in range(A_WIDTH // LANES):
            o_ref[rows, c * LANES:(c + 1) * LANES] = outs[c].astype(o_ref.dtype)

        u = z_ref[rows, o0 + CONV_DIM:o0 + 2 * CONV_DIM] * z_ref[rows, o0 + 2 * CONV_DIM:o0 + 3 * CONV_DIM]
        ub_ref[8:8 + BLK, :] = u
        y = ub_ref[6:6 + BLK, :] * cw_ref[0:1, :]
        y = y + ub_ref[7:7 + BLK, :] * cw_ref[1:2, :]
        y = y + ub_ref[8:8 + BLK, :] * cw_ref[2:3, :]
        o_ref[rows, A_WIDTH:A_WIDTH + CONV_DIM] = (z_ref[rows, o0:o0 + CONV_DIM] * y).astype(o_ref.dtype)
        ub_ref[0:8, :] = ub_ref[BLK:BLK + 8, :]

    @pl.when(i == pl.num_programs(1) - 1)
    def _():
        nk_ref[...] = k
        nv_ref[...] = v
        nc_ref[...] = ub_ref[BLK + 6:BLK + 8, :]


def _w_slab_specs(w_out, layer, n_step):
    slab = w_out.shape[1] // (BATCH * n_step)
    return (pl.BlockSpec((None, slab, w_out.shape[2]), lambda b, i: (layer, b * n_step + i, 0)),
            pl.BlockSpec((slab, w_out.shape[2]), lambda b, i: (b * n_step + i, 0)),
            jax.ShapeDtypeStruct(w_out.shape[1:], BF16))


def _even_prompt(z_p, cos, sin, qg, kg, cw, sinks, mats, w_out, layer):
    n_step = NBLK // MIX_SUB
    step_rows = MIX_SUB * BLK
    blk = lambda b, i: (b * n_step + i, 0)
    fixed = lambda b, i: (0, 0)
    per_b = lambda b, i: (b, 0, 0)
    wo_in, wo_out, wo_shape = _w_slab_specs(w_out, layer, n_step)
    return pl.pallas_call(
        _even_prompt_kernel,
        grid=(BATCH, n_step),
        in_specs=[pl.BlockSpec(memory_space=pltpu.SMEM),
                  pl.BlockSpec((step_rows, EVEN_IN), blk),
                  pl.BlockSpec((step_rows, LANES), lambda b, i: (i, 0)),
                  pl.BlockSpec((step_rows, LANES), lambda b, i: (i, 0)),
                  pl.BlockSpec((1, LANES), fixed),
                  pl.BlockSpec((1, LANES), fixed),
                  pl.BlockSpec((CONV_W, CONV_DIM), fixed),
                  pl.BlockSpec((2 * LANES, LANES), fixed),
                  pl.BlockSpec((2 * LANES, LANES), fixed),
                  wo_in],
        out_specs=[pl.BlockSpec((step_rows, D_MODEL), blk),
                   pl.BlockSpec((None, BLK, A_KV_WIDTH), per_b),
                   pl.BlockSpec((None, BLK, A_KV_WIDTH), per_b),
                   pl.BlockSpec((None, CONV_W - 1, CONV_DIM), per_b),
                   wo_out],
        out_shape=[jax.ShapeDtypeStruct((M_PROMPT, D_MODEL), BF16),
                   jax.ShapeDtypeStruct((BATCH, BLK, A_KV_WIDTH), F32),
                   jax.ShapeDtypeStruct((BATCH, BLK, A_KV_WIDTH), F32),
                   jax.ShapeDtypeStruct((BATCH, CONV_W - 1, CONV_DIM), F32),
                   wo_shape],
        scratch_shapes=[pltpu.VMEM((2 * BLK, A_KV_WIDTH), F32),
                        pltpu.VMEM((2 * BLK, A_KV_WIDTH), F32),
                        pltpu.VMEM((BLK + 8, CONV_DIM), F32)],
        compiler_params=_cparams("arbitrary", "arbitrary"),
        name="even_prompt",
    )(sinks, z_p, cos, sin, qg, kg, cw, *mats, w_out)


def _even_sample_prep_kernel(z_ref, cos_ref, sin_ref, qg_ref, kg_ref, c0_ref, c1_ref, cw_ref, hs_ref, sw_ref,
                             q_ref, k_ref, u_ref, o_ref):
    cos = cos_ref[...]
    sin = sin_ref[...]
    for c in range(A_WIDTH // LANES):
        q = _norm_rope_chunk(z_ref[:, c * LANES:(c + 1) * LANES], qg_ref[...], cos, sin, hs_ref, sw_ref)
        q_ref[:, c * LANES:(c + 1) * LANES] = (q * (HEAD_DIM ** -0.5)).astype(q_ref.dtype)
    for c in range(A_KV_WIDTH // LANES):
        k_ref[:, c * LANES:(c + 1) * LANES] = _norm_rope_chunk(
            z_ref[:, A_WIDTH + c * LANES:A_WIDTH + (c + 1) * LANES], kg_ref[...], cos, sin, hs_ref, sw_ref)
    o0 = A_WIDTH + 2 * A_KV_WIDTH
    u = z_ref[:, o0 + CONV_DIM:o0 + 2 * CONV_DIM] * z_ref[:, o0 + 2 * CONV_DIM:o0 + 3 * CONV_DIM]
    u_ref[...] = u
    y = c0_ref[...] * cw_ref[0:1, :]
    y = y + c1_ref[...] * cw_ref[1:2, :]
    y = y + u * cw_ref[2:3, :]
    o_ref[...] = (z_ref[:, o0:o0 + CONV_DIM] * y).astype(o_ref.dtype)


def _even_sample_prep(z_s, cos, sin, qg, kg, c0, c1, cw, mats):
    return pl.pallas_call(
        _even_sample_prep_kernel,
        out_shape=[jax.ShapeDtypeStruct((DEC_BATCH, A_WIDTH), BF16),
                   jax.ShapeDtypeStruct((DEC_BATCH, A_KV_WIDTH), F32),
                   jax.ShapeDtypeStruct((DEC_BATCH, CONV_DIM), F32),
                   jax.ShapeDtypeStruct((DEC_BATCH, CONV_DIM), BF16)],
        compiler_params=pltpu.CompilerParams(vmem_limit_bytes=VMEM_LIMIT),
        name="even_sample_prep",
    )(z_s, cos, sin, qg, kg, c0, c1, cw, *mats)


SROWS = LANES


def _even_sample_attn_kernel(*refs):
    sink_ref, q_ref, kn_ref, vn_ref, kc_ref, vc_ref = refs[:6]
    o_ref, nk_ref, nv_ref = refs[-3:]
    lc = kc_ref.shape[0]
    row = lax.broadcasted_iota(jnp.int32, (lc, A_KV_WIDTH), 0)
    kfull = jnp.concatenate([kc_ref[...], jnp.where(row == 0, kn_ref[...], 0.0)], axis=0)
    vfull = jnp.concatenate([vc_ref[...], jnp.where(row == 0, vn_ref[...], 0.0)], axis=0)
    qs = [jnp.broadcast_to(q_ref[:, c * LANES:(c + 1) * LANES], (SROWS, LANES))
          for c in range(A_WIDTH // LANES)]
    kk = lax.broadcasted_iota(jnp.int32, (2 * lc, SROWS), 0)
    mask_t = (kk <= lc) & (lc - kk <= WINDOW)
    outs = _attn_core(qs, kfull, vfull, mask_t, sink_ref)
    for c in range(A_WIDTH // LANES):
        o_ref[:, c * LANES:(c + 1) * LANES] = outs[c][0:1].astype(o_ref.dtype)
    nk_ref[0:lc - 1, :] = kc_ref[1:lc, :]
    nk_ref[lc - 1:lc, :] = kn_ref[...]
    nv_ref[0:lc - 1, :] = vc_ref[1:lc, :]
    nv_ref[lc - 1:lc, :] = vn_ref[...]


def _even_sample_attn(sinks, q_s, k_s, v_s, k_cache, v_cache, li, prev):
    lc = k_cache.shape[2]
    vec = lambda w: pl.BlockSpec((None, 1, w), lambda b: (b, 0, 0))
    cache = pl.BlockSpec((None, None, lc, A_KV_WIDTH), lambda b: (li, b, 0, 0))
    in_specs = [pl.BlockSpec(memory_space=pltpu.SMEM), vec(A_WIDTH), vec(A_KV_WIDTH), vec(A_KV_WIDTH),
                cache, cache]
    args = [sinks, q_s.reshape(DEC_BATCH, 1, A_WIDTH), k_s.reshape(DEC_BATCH, 1, A_KV_WIDTH),
            v_s.reshape(DEC_BATCH, 1, A_KV_WIDTH), k_cache, v_cache]
    aliases = {}
    if prev is not None:
        aliases = {len(args): 1, len(args) + 1: 2}
        in_specs += [pl.BlockSpec(memory_space=pl.ANY)] * 2
        args += list(prev)
    return pl.pallas_call(
        _even_sample_attn_kernel,
        grid=(DEC_BATCH,),
        in_specs=in_specs,
        out_specs=[vec(A_WIDTH), cache, cache],
        out_shape=[jax.ShapeDtypeStruct((DEC_BATCH, 1, A_WIDTH), BF16),
                   jax.ShapeDtypeStruct(k_cache.shape, F32),
                   jax.ShapeDtypeStruct(v_cache.shape, F32)],
        input_output_aliases=aliases,
        compiler_params=_cparams("arbitrary"),
        name="even_sample_attn",
    )(*args)


def _log_decay(zr, wa_ref, ba_ref):
    pre = _dot(zr.astype(BF16), wa_ref[...]) + ba_ref[...]
    return (jnp.minimum(pre, 0.0) - jnp.log1p(jnp.exp(-jnp.abs(pre)))) * (1.0 / GLA_TAU)


def _split_bf16(x):
    hi = x.astype(BF16)
    lo = (x - hi.astype(F32)).astype(BF16)
    return hi, lo


def _row_to_col(row):
    n = row.shape[1]
    r = lax.broadcasted_iota(jnp.int32, (n, n), 0)
    c = lax.broadcasted_iota(jnp.int32, (n, n), 1)
    return jnp.sum(jnp.where(r == c, jnp.broadcast_to(row, (n, n)), 0.0), axis=-1, keepdims=True)


def _head_rmsnorm_gate(o, gain, zg):
    ms = jnp.mean(o * o, axis=-1, keepdims=True)
    return ((o * lax.rsqrt(ms + EPS)) * gain) * _silu(zg)


def _odd_prompt_kernel(zm_ref, zpr_ref, wa_ref, ba_ref, gain_ref, wp_ref, ps_ref, wo_ref,
                       o_ref, s_ref, np_ref, wo_bf_ref, pbh_ref, pbl_ref):
    i = pl.program_id(1)
    wo_bf_ref[...] = wo_ref[...].astype(BF16)

    @pl.when(i == 0)
    def _():
        s_ref[...] = jnp.zeros_like(s_ref)
        pbh_ref[0:BLK, :] = jnp.zeros((BLK, POOL_DIM), BF16)
        pbl_ref[0:BLK, :] = jnp.zeros((BLK, POOL_DIM), BF16)

    x = None
    for t in range(MIX_SUB):
        rows = pl.ds(t * BLK, BLK)
        x = _odd_block(zm_ref.at[rows], zpr_ref.at[rows], wa_ref, ba_ref, gain_ref, wp_ref, ps_ref,
                       o_ref.at[rows], s_ref, pbh_ref, pbl_ref, (i * MIX_SUB + t) * BLK)

    @pl.when(i == pl.num_programs(1) - 1)
    def _():
        np_ref[...] = x[BLK - POOL_BUF:BLK, :]


def _odd_block(zm_ref, zpr_ref, wa_ref, ba_ref, gain_ref, wp_ref, ps_ref, o_ref, s_ref, pbh_ref, pbl_ref, t0):
    n_ch = BLK // GLA_CHUNK
    gk = _log_decay(zpr_ref[:, POOL_DIM:POOL_RANK_W], wa_ref, ba_ref)
    r = lax.broadcasted_iota(jnp.int32, (BLK, BLK), 0)
    c = lax.broadcasted_iota(jnp.int32, (BLK, BLK), 1)
    causal = (r >= c) & (r // GLA_CHUNK == c // GLA_CHUNK)
    tri = jnp.where(causal, 1.0, 0.0).astype(BF16)
    g_hi, g_lo = _split_bf16(gk)
    b = _dot(jnp.concatenate([tri, tri], axis=1), jnp.concatenate([g_hi, g_lo], axis=0))
    b_last = [b[(ch + 1) * GLA_CHUNK - 1:(ch + 1) * GLA_CHUNK, :] for ch in range(n_ch)]
    b_end = jnp.concatenate([jnp.broadcast_to(bl, (GLA_CHUNK, GLA_K_WIDTH)) for bl in b_last], axis=0)
    o_k, o_v, o_g = GLA_K_WIDTH, 2 * GLA_K_WIDTH, 2 * GLA_K_WIDTH + GLA_WIDTH
    zq = zm_ref[:, 0:GLA_K_WIDTH]
    zk = zm_ref[:, o_k:o_k + GLA_K_WIDTH]
    qd = ((zq * (GLA_DK ** -0.5)) * jnp.exp(b)).astype(BF16)
    kd = (zk * jnp.exp(-b)).astype(BF16)
    k2 = (zk * jnp.exp(b_end - b)).astype(BF16)
    sub = lax.broadcasted_iota(jnp.int32, (8, LANES), 0)
    sel = jnp.where(sub < 2, 1.0, 0.0).astype(BF16)
    hk = [slice(h * GLA_DK, (h + 1) * GLA_DK) for h in range(GLA_HEADS)]
    chunk = [slice(ch * GLA_CHUNK, (ch + 1) * GLA_CHUNK) for ch in range(n_ch)]
    v_hs = [zm_ref[:, o_v + h * GLA_DV:o_v + (h + 1) * GLA_DV].astype(BF16) for h in range(GLA_HEADS)]
    atts = [_dot_nt(qd[:, hk[h]], kd[:, hk[h]]) for h in range(GLA_HEADS)]
    log_decs = []
    for h in range(GLA_HEADS):
        for ch in range(n_ch):
            bl = b_last[ch][:, hk[h]]
            bl_hi = bl.astype(BF16).astype(F32)
            rows8 = jnp.where(sub == 0, bl_hi, jnp.where(sub == 1, bl - bl_hi, 0.0)).astype(BF16)
            log_decs.append(_dot_tn(rows8, sel))
    deltas = [[_dot_tn(k2[chunk[ch], hk[h]], v_hs[h][chunk[ch]]) for ch in range(n_ch)]
              for h in range(GLA_HEADS)]
    o_intras = [_dot(jnp.where(causal, atts[h], 0.0).astype(BF16), v_hs[h]) for h in range(GLA_HEADS)]
    o_inters = []
    for h in range(GLA_HEADS):
        s_h = s_ref[h]
        parts = []
        for ch in range(n_ch):
            parts.append(_dot(qd[chunk[ch], hk[h]], s_h.astype(BF16)))
            dec = jnp.exp(log_decs[h * n_ch + ch])
            s_h = jnp.concatenate([dec, dec], axis=1) * s_h + deltas[h][ch]
        s_ref[h] = s_h
        o_inters.append(jnp.concatenate(parts, axis=0))
    for h in range(GLA_HEADS):
        vs = slice(h * GLA_DV, (h + 1) * GLA_DV)
        zg = zm_ref[:, o_g + h * GLA_DV:o_g + (h + 1) * GLA_DV]
        o_ref[:, vs] = _head_rmsnorm_gate(o_intras[h] + o_inters[h], gain_ref[...], zg).astype(o_ref.dtype)

    x = zpr_ref[:, 0:POOL_DIM]
    x_hi, x_lo = _split_bf16(x)
    pbh_ref[BLK:2 * BLK, :] = x_hi
    pbl_ref[BLK:2 * BLK, :] = x_lo
    t = lax.broadcasted_iota(jnp.int32, (BLK, 2 * BLK), 0)
    j = lax.broadcasted_iota(jnp.int32, (BLK, 2 * BLK), 1)
    back = t + BLK - j
    t_glob = t0 + lax.broadcasted_iota(jnp.int32, (BLK, 1), 0)
    groups =[slice(g * POOL_GDIM, (g + 1) * POOL_GDIM) for g in range(len(POOL_WINDOWS))]
    sums = []
    for g, w in enumerate(POOL_WINDOWS):
        band = jnp.where((back >= 0) & (back < w), 1.0, 0.0).astype(BF16)
        sums.append(_dot(jnp.concatenate([band, band], axis=1),
                         jnp.concatenate([pbh_ref[:, groups[g]], pbl_ref[:, groups[g]]], axis=0)))
    ys = []
    for g, w in enumerate(POOL_WINDOWS):
        inv_cnt = 1.0 / jnp.minimum(w, t_glob + 1).astype(F32)
        d = sums[g] * inv_cnt - x[:, groups[g]]
        ys.append(_dot(d.astype(BF16), wp_ref[g]))
    for g in range(len(POOL_WINDOWS)):
        o_ref[:, GLA_WIDTH + g * POOL_GDIM:GLA_WIDTH + (g + 1) * POOL_GDIM] = (
            ys[g] * ps_ref[:, groups[g]]).astype(o_ref.dtype)
    pbh_ref[0:BLK, :] = x_hi
    pbl_ref[0:BLK, :] = x_lo
    return x


def _odd_prompt(zm_p, zpr_p, wa, ba, gain, wp, ps, w_out, layer):
    n_step = NBLK // MIX_SUB
    step_rows = MIX_SUB * BLK
    blk = lambda b, i: (b * n_step + i, 0)
    fixed2 = lambda b, i: (0, 0)
    wo_in, wo_out, wo_shape = _w_slab_specs(w_out, layer, n_step)
    return pl.pallas_call(
        _odd_prompt_kernel,
        grid=(BATCH, n_step),
        in_specs=[pl.BlockSpec((step_rows, ODD_MAIN), blk),
                  pl.BlockSpec((step_rows, POOL_RANK_W), blk),
                  pl.BlockSpec((LANES, GLA_K_WIDTH), fixed2),
                  pl.BlockSpec((1, GLA_K_WIDTH), fixed2),
                  pl.BlockSpec((1, GLA_DV), fixed2),
                  pl.BlockSpec((4, POOL_GDIM, POOL_GDIM), lambda b, i: (0, 0, 0)),
                  pl.BlockSpec((1, POOL_DIM), fixed2),
                  wo_in],
        out_specs=[pl.BlockSpec((step_rows, D_MODEL), blk),
                   pl.BlockSpec((None, GLA_HEADS, GLA_DK, GLA_DV), lambda b, i: (b, 0, 0, 0)),
                   pl.BlockSpec((None, POOL_BUF, POOL_DIM), lambda b, i: (b, 0, 0)),
                   wo_out],
        out_shape=[jax.ShapeDtypeStruct((M_PROMPT, D_MODEL), BF16),
                   jax.ShapeDtypeStruct((BATCH, GLA_HEADS, GLA_DK, GLA_DV), F32),
                   jax.ShapeDtypeStruct((BATCH, POOL_BUF, POOL_DIM), F32),
                   wo_shape],
        scratch_shapes=[pltpu.VMEM((2 * BLK, POOL_DIM), BF16), pltpu.VMEM((2 * BLK, POOL_DIM), BF16)],
        compiler_params=_cparams("arbitrary", "arbitrary"),
        name="odd_prompt",
    )(zm_p, zpr_p, wa, ba, gain, wp, ps, w_out)


def _odd_sample_prep_kernel(zm_ref, zr_ref, wa_ref, ba_ref, dec_ref, q_ref):
    dec_ref[...] = jnp.exp(_log_decay(zr_ref[...], wa_ref, ba_ref))
    q_ref[...] = zm_ref[:, 0:GLA_K_WIDTH] * (GLA_DK ** -0.5)


def _odd_sample_prep(zm_s, zr_s, wa, ba):
    return pl.pallas_call(
        _odd_sample_prep_kernel,
        out_shape=[jax.ShapeDtypeStruct((DEC_BATCH, GLA_K_WIDTH), F32),
                   jax.ShapeDtypeStruct((DEC_BATCH, GLA_K_WIDTH), F32)],
        compiler_params=pltpu.CompilerParams(vmem_limit_bytes=VMEM_LIMIT),
        name="odd_sample_prep",
    )(zm_s, zr_s, wa, ba)


STATE_BB = 4


def _odd_sample_state_kernel(*refs):
    dec_ref, q_ref, k_ref, v_ref, s_ref = refs[:5]
    ns_ref, o_ref = refs[-2:]
    for bb in range(STATE_BB):
        for h in range(GLA_HEADS):
            ks = slice(h * GLA_DK, (h + 1) * GLA_DK)
            vs = slice(h * GLA_DV, (h + 1) * GLA_DV)
            s_new = (_row_to_col(dec_ref[bb][:, ks]) * s_ref[bb, h]
                     + _row_to_col(k_ref[bb][:, ks]) * v_ref[bb][:, vs])
            ns_ref[bb, h] = s_new
            o_ref[bb, :, vs] = jnp.sum(_row_to_col(q_ref[bb][:, ks]) * s_new, axis=0, keepdims=True)


def _odd_sample_state(dec, q, k, v, state, li, prev):
    vec = lambda w: pl.BlockSpec((STATE_BB, 1, w), lambda b: (b, 0, 0))
    st = pl.BlockSpec((None, STATE_BB, GLA_HEADS, GLA_DK, GLA_DV), lambda b: (li, b, 0, 0, 0))
    r3 = lambda a: a.reshape(DEC_BATCH, 1, a.shape[-1])
    in_specs = [vec(GLA_K_WIDTH), vec(GLA_K_WIDTH), vec(GLA_K_WIDTH), vec(GLA_WIDTH), st]
    args = [r3(dec), r3(q), r3(k), r3(v), state]
    aliases = {}
    if prev is not None:
        aliases = {len(args): 0}
        in_specs.append(pl.BlockSpec(memory_space=pl.ANY))
        args.append(prev)
    return pl.pallas_call(
        _odd_sample_state_kernel,
        grid=(DEC_BATCH // STATE_BB,),
        in_specs=in_specs,
        out_specs=[st, vec(GLA_WIDTH)],
        out_shape=[jax.ShapeDtypeStruct(state.shape, F32),
                   jax.ShapeDtypeStruct((DEC_BATCH, 1, GLA_WIDTH), F32)],
        input_output_aliases=aliases,
        compiler_params=_cparams("arbitrary"),
        name="odd_sample_state",
    )(*args)


def _odd_sample_post_kernel(o_in_ref, zm_ref, zp_ref, hist_ref, gain_ref, wp_ref, ps_ref, o_ref):
    o_g = 2 * GLA_K_WIDTH + GLA_WIDTH
    for h in range(GLA_HEADS):
        vs = slice(h * GLA_DV, (h + 1) * GLA_DV)
        zg = zm_ref[:, o_g + h * GLA_DV:o_g + (h + 1) * GLA_DV]
        o_ref[:, vs] = _head_rmsnorm_gate(o_in_ref[:, vs], gain_ref[...], zg).astype(o_ref.dtype)
    n_prev = hist_ref.shape[0]
    for g, w in enumerate(POOL_WINDOWS):
        gs = slice(g * POOL_GDIM, (g + 1) * POOL_GDIM)
        x = zp_ref[:, gs]
        s = hist_ref[n_prev - (w - 1)][:, gs]
        for jj in range(n_prev - (w - 1) + 1, n_prev):
            s = s + hist_ref[jj][:, gs]
        s = s + x
        d = s / float(min(w, n_prev + 1)) - x
        y = _dot(d.astype(BF16), wp_ref[g]) * ps_ref[:, gs]
        o_ref[:, GLA_WIDTH + g * POOL_GDIM:GLA_WIDTH + (g + 1) * POOL_GDIM] = y.astype(o_ref.dtype)


def _odd_sample_post(o_raw, zm_s, zp_s, hist_t, gain, wp, ps):
    return pl.pallas_call(
        _odd_sample_post_kernel,
        out_shape=jax.ShapeDtypeStruct((DEC_BATCH, D_MODEL), BF16),
        compiler_params=pltpu.CompilerParams(vmem_limit_bytes=VMEM_LIMIT),
        name="odd_sample_post",
    )(o_raw, zm_s, zp_s, hist_t, gain, wp, ps)


def _rope_tables(pos):
    half = HEAD_DIM // 2
    inv = jnp.power(ROPE_THETA, -jnp.arange(half, dtype=F32) / half)
    ang = pos.astype(F32)[:, None] * inv[None, :]
    c, s = jnp.cos(ang), jnp.sin(ang)
    return jnp.tile(c, (1, 4)), jnp.tile(jnp.concatenate([-s, s], axis=1), (1, 2))


def kernel(x_prompt, x_sample, cache_swa_k, cache_swa_v, state_conv, state_gla, state_pool, norm_mix, norm_ffn, w_in_even, w_out_even, q_norm, k_norm, attn_sinks, conv_w, w_in_odd, w_out_odd, w_alpha_up, b_alpha, gla_out_norm, w_pool, pool_scale, w_gate, w_up, w_down):
    lc = cache_swa_k.shape[2]
    x_p = x_prompt.reshape(M_PROMPT, D_MODEL)
    x_s = x_sample.reshape(DEC_BATCH, D_MODEL)
    cos_p, sin_p = _rope_tables(jnp.arange(SEQ))
    cos_s, sin_s = _rope_tables(PAST_LEN + jnp.arange(1))
    mats = _rope_mats()
    kc_all = cache_swa_k.reshape(N_EVEN, DEC_BATCH, lc, A_KV_WIDTH)
    vc_all = cache_swa_v.reshape(N_EVEN, DEC_BATCH, lc, A_KV_WIDTH)
    o_r = ODD_MAIN
    w_odd_t = jnp.swapaxes(w_in_odd, 1, 2)
    w_odd_pr_t = jnp.concatenate([w_odd_t[:, o_r + GLA_RANK:], w_odd_t[:, o_r:o_r + GLA_RANK],
                                  jnp.zeros((N_ODD, LANES - GLA_RANK, D_MODEL), F32)], axis=1)
    pk, pv, pc, pg, pp, sc, sp = ([] for _ in range(7))
    sk_all = sv_all = sg_all = None

    h_p, h_s = _rmsnorm(x_p, x_s, norm_mix[0])
    for layer in range(DEPTH):
        li = layer // 2
        if layer % 2 == 0:
            z_p, z_s = _dense(h_p, h_s, [(w_in_even, li)], n_cols=EVEN_IN, tm=1024, tn=1536, name="in_even")
            qg = jnp.tile(q_norm[li], 2).reshape(1, LANES)
            kg = jnp.tile(k_norm[li], 2).reshape(1, LANES)
            m_p, nk, nv, nc, w_out_bf = _even_prompt(z_p, cos_p, sin_p, qg, kg, conv_w[li], attn_sinks[li], mats,
                                                     w_out_even, li)
            pk.append(nk.reshape(BATCH, lc, A_KV_HEADS, HEAD_DIM))
            pv.append(nv.reshape(BATCH, lc, A_KV_HEADS, HEAD_DIM))
            pc.append(nc)
            q_s, k_s, u_s, conv_s = _even_sample_prep(
                z_s, cos_s, sin_s, qg, kg, state_conv[li, :, 0], state_conv[li, :, 1], conv_w[li], mats)
            v_s = z_s[:, A_WIDTH + A_KV_WIDTH:A_WIDTH + 2 * A_KV_WIDTH]
            attn_s, sk_all, sv_all = _even_sample_attn(
                attn_sinks[li], q_s, k_s, v_s, kc_all, vc_all, li,
                None if sk_all is None else (sk_all, sv_all))
            sc.append(jnp.stack([state_conv[li, :, 1], u_s], axis=1))
            m_s = jnp.concatenate([attn_s.reshape(DEC_BATCH, A_WIDTH), conv_s], axis=1)
        else:
            zm_p, zm_s = _dense(h_p, h_s, [(w_odd_t, li)], n_cols=ODD_MAIN, tm=1024, tn=ODD_MAIN // 2,
                                transposed=True, name="in_odd_main")
            zpr_p, zpr_s = _dense(h_p, h_s, [(w_odd_pr_t, li)], n_cols=POOL_RANK_W, tm=1024, tn=POOL_RANK_W,
                                  transposed=True, name="in_odd_pool_rank")
            zp_s, zr_s = zpr_s[:, :POOL_DIM], zpr_s[:, POOL_DIM:]
            wa = jnp.pad(w_alpha_up[li], ((0, LANES - GLA_RANK), (0, 0))).astype(BF16)
            ba = b_alpha[li].reshape(1, GLA_K_WIDTH)
            gain = gla_out_norm[li].reshape(1, GLA_DV)
            wp = w_pool[li].astype(BF16)
            ps = pool_scale[li].reshape(1, POOL_DIM)
            m_p, ng, npool, w_out_bf = _odd_prompt(zm_p, zpr_p, wa, ba, gain, wp, ps, w_out_odd, li)
            pg.append(ng)
            pp.append(npool)
            dec_s, q_s = _odd_sample_prep(zm_s, zr_s, wa, ba)
            sg_all, o_raw = _odd_sample_state(
                dec_s, q_s, zm_s[:, GLA_K_WIDTH:2 * GLA_K_WIDTH],
                zm_s[:, 2 * GLA_K_WIDTH:2 * GLA_K_WIDTH + GLA_WIDTH], state_gla, li, sg_all)
            m_s = _odd_sample_post(o_raw.reshape(DEC_BATCH, GLA_WIDTH), zm_s, zp_s,
                                   jnp.swapaxes(state_pool[li], 0, 1), gain, wp, ps)
            sp.append(jnp.concatenate([state_pool[li, :, 1:], zp_s[:, None, :]], axis=1))
        x_p, x_s, h_p, h_s = _proj_res(m_p, m_s, (w_out_bf, None), (x_p, x_s), norm_ffn[layer],
                                       tm=512, name="out_proj")
        a_p, a_s, w_down_bf = _dense(h_p, h_s, [(w_gate, layer), (w_up, layer)], n_cols=D_FF, tm=2048, tn=512,
                                     mode="swiglu", out_dtype=BF16, side_cast=(w_down, layer), sub=2,
                                     name="ffn_up")
        if layer + 1 < DEPTH:
            x_p, x_s, h_p, h_s = _proj_res(a_p, a_s, (w_down_bf, None), (x_p, x_s),
                                           norm_mix[layer + 1], tm=512, name="ffn_down")
        else:
            x_p, x_s = _proj_res(a_p, a_s, (w_down_bf, None), (x_p, x_s), None,
                                 tm=512, name="ffn_down_last")

    st = lambda parts: jnp.stack(parts)
    cache5 = lambda a: a.reshape(N_EVEN, DEC_BATCH, lc, A_KV_HEADS, HEAD_DIM)
    return (x_p.reshape(BATCH, SEQ, D_MODEL), x_s.reshape(DEC_BATCH, 1, D_MODEL),
            st(pk), st(pv), st(pc), st(pg), st(pp), cache5(sk_all), cache5(sv_all), st(sc), sg_all, st(sp))
```

```python
import functools

import jax
import jax.numpy as jnp
from jax import lax
from jax.experimental import pallas as pl
from jax.experimental.pallas import tpu as pltpu

F32 = jnp.float32
BF16 = jnp.bfloat16

D_MODEL = 2048
BATCH = 4
SEQ = 2048
DEPTH = 4
DEC_BATCH = 32
PAST_LEN = 16384
N_EVEN = 2
N_ODD = 2
EPS = 1e-6
NEG_INF = -1e30
A_HEADS = 16
A_KV_HEADS = 4
HEAD_DIM = 64
A_WIDTH = A_HEADS * HEAD_DIM
A_KV_WIDTH = A_KV_HEADS * HEAD_DIM
WINDOW = 128
ROPE_THETA = 10000.0
CONV_DIM = D_MODEL // 2
CONV_W = 3
GLA_HEADS = 4
GLA_WIDTH = D_MODEL // 2
GLA_DV = GLA_WIDTH // GLA_HEADS
GLA_DK = GLA_DV // 2
GLA_K_WIDTH = GLA_HEADS * GLA_DK
GLA_RANK = 16
GLA_TAU = 16.0
GLA_CHUNK = 64
POOL_DIM = D_MODEL // 2
POOL_WINDOWS = (2, 4, 8, 16)
POOL_GDIM = POOL_DIM // 4
POOL_BUF = 15
D_FF = 5632
EVEN_IN = A_WIDTH + 2 * A_KV_WIDTH + 3 * CONV_DIM
ODD_MAIN = 2 * GLA_K_WIDTH + 2 * GLA_WIDTH

M_PROMPT = BATCH * SEQ
LANES = 128
BLK = 128
NBLK = SEQ // BLK
MIX_SUB = 4
POOL_RANK_W = POOL_DIM + LANES
VMEM_LIMIT = 58 * 1024 * 1024


def _cparams(*sem):
    return pltpu.CompilerParams(dimension_semantics=sem, vmem_limit_bytes=VMEM_LIMIT)


def _dot(a, b):
    return jnp.dot(a, b, preferred_element_type=F32)


def _dot_nt(a, b):
    return lax.dot_general(a, b, (((1,), (1,)), ((), ())), preferred_element_type=F32)


def _dot_tn(a, b):
    return lax.dot_general(a, b, (((0,), (0,)), ((), ())), preferred_element_type=F32)


def _silu(x):
    return x * (1.0 / (1.0 + jnp.exp(-x)))


def _rmsnorm_rows(x, gain):
    ms = jnp.mean(x * x, axis=-1, keepdims=True)
    return (x * lax.rsqrt(ms + EPS)) * gain


def _rmsnorm_kernel(xp_ref, xs_ref, g_ref, op_ref, os_ref, *, n_i):
    op_ref[...] = _rmsnorm_rows(xp_ref[...], g_ref[...]).astype(op_ref.dtype)

    @pl.when(pl.program_id(0) == n_i - 1)
    def _():
        os_ref[...] = _rmsnorm_rows(xs_ref[...], g_ref[...]).astype(os_ref.dtype)


def _rmsnorm(x_p, x_s, gain, *, tm=1024):
    n_i = M_PROMPT // tm
    row = lambda i: (i, 0)
    fixed = lambda i: (0, 0)
    return pl.pallas_call(
        functools.partial(_rmsnorm_kernel, n_i=n_i),
        grid=(n_i,),
        in_specs=[pl.BlockSpec((tm, D_MODEL), row),
                  pl.BlockSpec((DEC_BATCH, D_MODEL), fixed),
                  pl.BlockSpec((1, D_MODEL), fixed)],
        out_specs=[pl.BlockSpec((tm, D_MODEL), row),
                   pl.BlockSpec((DEC_BATCH, D_MODEL), fixed)],
        out_shape=[jax.ShapeDtypeStruct((M_PROMPT, D_MODEL), BF16),
                   jax.ShapeDtypeStruct((DEC_BATCH, D_MODEL), BF16)],
        compiler_params=_cparams("arbitrary"),
        name="rmsnorm",
    )(x_p, x_s, gain.reshape(1, D_MODEL))


XPOSE_COLS = 128


def _dense_kernel(*refs, n_w, mode, n_i, transposed, side_cast, sub):
    a_p, a_s = refs[0], refs[1]
    w = refs[2:2 + n_w]
    pos = 2 + n_w
    side_in = side_out = None
    if side_cast:
        side_in = refs[pos]
        pos += 1
    o_p, o_s = refs[pos], refs[pos + 1]
    pos += 2
    if side_cast:
        side_out = refs[pos]
        pos += 1
    wbf = refs[pos:pos + n_w]
    i = pl.program_id(1)

    @pl.when(i == 0)
    def _():
        for k in range(n_w):
            if transposed:
                tn = wbf[k].shape[1]
                for c in range(0, tn, XPOSE_COLS):
                    wbf[k][:, c:c + XPOSE_COLS] = w[k][c:c + XPOSE_COLS, :].T.astype(BF16)
            else:
                wbf[k][...] = w[k][...].astype(BF16)
        if side_cast:
            side_out[...] = side_in[...].astype(BF16)

    def run(a_ref, o_ref, n_sub):
        rows = a_ref.shape[0] // n_sub
        for r in range(n_sub):
            a = a_ref[r * rows:(r + 1) * rows, :]
            if mode == "swiglu":
                y = _silu(_dot(a, wbf[0][...])) * _dot(a, wbf[1][...])
            else:
                y = _dot(a, wbf[0][...])
            o_ref[r * rows:(r + 1) * rows, :] = y.astype(o_ref.dtype)

    run(a_p, o_p, sub)

    @pl.when(i == n_i - 1)
    def _():
        run(a_s, o_s, 1)


def _dense(a_p, a_s, weights, *, n_cols, tm, tn, mode="plain", out_dtype=F32, transposed=False,
           side_cast=None, sub=1, name):
    k_dim = a_p.shape[1]
    n_i = M_PROMPT // tm
    n_j = n_cols // tn
    row = lambda j, i: (i, 0)
    fixed = lambda j, i: (0, 0)
    tile = lambda j, i: (i, j)
    panel = lambda j, i: (0, j)
    in_specs = [pl.BlockSpec((tm, k_dim), row), pl.BlockSpec((DEC_BATCH, k_dim), fixed)]
    args = [a_p, a_s]
    for arr, layer in weights:
        if transposed:
            in_specs.append(pl.BlockSpec((None, tn, k_dim), lambda j, i, layer=layer: (layer, j, 0)))
        else:
            in_specs.append(pl.BlockSpec((None, k_dim, tn), lambda j, i, layer=layer: (layer, 0, j)))
        args.append(arr)
    out_specs = [pl.BlockSpec((tm, tn), tile), pl.BlockSpec((DEC_BATCH, tn), panel)]
    out_shape = [jax.ShapeDtypeStruct((M_PROMPT, n_cols), out_dtype),
                 jax.ShapeDtypeStruct((DEC_BATCH, n_cols), out_dtype)]
    if side_cast is not None:
        s_arr, s_layer = side_cast
        slab = s_arr.shape[1] // n_j
        in_specs.append(pl.BlockSpec((None, slab, s_arr.shape[2]), lambda j, i: (s_layer, j, 0)))
        args.append(s_arr)
        out_specs.append(pl.BlockSpec((slab, s_arr.shape[2]), lambda j, i: (j, 0)))
        out_shape.append(jax.ShapeDtypeStruct(s_arr.shape[1:], BF16))
    n_w = len(weights)
    return pl.pallas_call(
        functools.partial(_dense_kernel, n_w=n_w, mode=mode, n_i=n_i, transposed=transposed,
                          side_cast=side_cast is not None, sub=sub),
        grid=(n_j, n_i),
        in_specs=in_specs,
        out_specs=out_specs,
        out_shape=out_shape,
        scratch_shapes=[pltpu.VMEM((k_dim, tn), BF16) for _ in range(n_w)],
        compiler_params=_cparams("arbitrary", "arbitrary"),
        name=name,
    )(*args)


def _proj_res_kernel(*refs, n_i, with_norm):
    a_p, a_s, w_ref, r_p, r_s = refs[:5]
    if with_norm:
        g_ref, x_p, x_s, h_p, h_s = refs[5:]
    else:
        x_p, x_s = refs[5:]
        g_ref = h_p = h_s = None

    def run(a_ref, r_ref, x_ref, h_ref):
        a = a_ref[...]
        if a.dtype != BF16:
            a = a.astype(BF16)
        x = r_ref[...] + _dot(a, w_ref[...])
        x_ref[...] = x
        if with_norm:
            h_ref[...] = _rmsnorm_rows(x, g_ref[...]).astype(h_ref.dtype)

    run(a_p, r_p, x_p, h_p)

    @pl.when(pl.program_id(0) == n_i - 1)
    def _():
        run(a_s, r_s, x_s, h_s)


def _proj_res(a_p, a_s, weight, res, gain, *, tm, name):
    w_bf, layer = weight
    k_dim = a_p.shape[1]
    if layer is None:
        w_spec = pl.BlockSpec((k_dim, D_MODEL), lambda i: (0, 0), pipeline_mode=pl.Buffered(1))
    else:
        w_spec = pl.BlockSpec((None, k_dim, D_MODEL), lambda i: (layer, 0, 0), pipeline_mode=pl.Buffered(1))
    n_i = M_PROMPT // tm
    with_norm = gain is not None
    row = lambda i: (i, 0)
    fixed = lambda i: (0, 0)
    rows_p = pl.BlockSpec((tm, D_MODEL), row)
    rows_s = pl.BlockSpec((DEC_BATCH, D_MODEL), fixed)
    in_specs = [pl.BlockSpec((tm, k_dim), row), pl.BlockSpec((DEC_BATCH, k_dim), fixed),
                w_spec, rows_p, rows_s]
    args = [a_p, a_s, w_bf, res[0], res[1]]
    out_specs = [rows_p, rows_s]
    out_shape = [jax.ShapeDtypeStruct((M_PROMPT, D_MODEL), F32), jax.ShapeDtypeStruct((DEC_BATCH, D_MODEL), F32)]
    if with_norm:
        in_specs.append(pl.BlockSpec((1, D_MODEL), fixed))
        args.append(gain.reshape(1, D_MODEL))
        out_specs += [rows_p, rows_s]
        out_shape += [jax.ShapeDtypeStruct((M_PROMPT, D_MODEL), BF16),
                      jax.ShapeDtypeStruct((DEC_BATCH, D_MODEL), BF16)]
    return pl.pallas_call(
        functools.partial(_proj_res_kernel, n_i=n_i, with_norm=with_norm),
        grid=(n_i,),
        in_specs=in_specs,
        out_specs=out_specs,
        out_shape=out_shape,
        compiler_params=_cparams("arbitrary"),
        name=name,
    )(*args)


def _split_cat(x):
    hi = x.astype(BF16)
    lo = (x - hi.astype(F32)).astype(BF16)
    return jnp.concatenate([hi, lo], axis=1)


def _rope_mats():
    j = jnp.arange(2 * LANES)[:, None] % LANES
    l = jnp.arange(LANES)[None, :]
    head_sum = (j // HEAD_DIM == l // HEAD_DIM).astype(BF16)
    src = jnp.where(l % HEAD_DIM < HEAD_DIM // 2, l + HEAD_DIM // 2, l - HEAD_DIM // 2)
    half_swap = (j == src).astype(BF16)
    return head_sum, half_swap


def _norm_rope_chunk(xc, gain, cos, sin, hs_ref, sw_ref):
    ms = _dot(_split_cat(xc * xc), hs_ref[...]) * (1.0 / HEAD_DIM)
    y = (xc * lax.rsqrt(ms + EPS)) * gain
    swapped = _dot(_split_cat(y), sw_ref[...])
    return y * cos + swapped * sin


def _spread_heads(xc, own_lo):
    lane = lax.broadcasted_iota(jnp.int32, xc.shape, 1)
    keep = (lane < HEAD_DIM) if own_lo else (lane >= HEAD_DIM)
    nat = jnp.where(keep, xc, 0.0)
    rol = pltpu.roll(nat, HEAD_DIM, axis=1)
    parts = (nat, rol) if own_lo else (rol, nat)
    return jnp.concatenate(parts, axis=0).astype(BF16)


def _attn_core(qs, kfull, vfull, mask_t, sink_ref):
    rows = qs[0].shape[0]
    nkeys = kfull.shape[0]
    outs = [None] * 8
    for kh in range(A_KV_HEADS):
        c0 = (kh // 2) * LANES
        kk = _spread_heads(kfull[:, c0:c0 + LANES], kh % 2 == 0)
        vv = _spread_heads(vfull[:, c0:c0 + LANES], kh % 2 == 0)
        lhs = jnp.concatenate([qs[2 * kh], qs[2 * kh + 1]], axis=0)
        s = _dot_nt(kk, lhs)
        prow = []
        for half in range(2):
            pcol = []
            for cc in range(2):
                sb = s[half * nkeys:(half + 1) * nkeys, cc * rows:(cc + 1) * rows]
                sb = jnp.where(mask_t, sb, NEG_INF)
                sink = sink_ref[kh * 4 + 2 * cc + half]
                m = jnp.maximum(jnp.max(sb, axis=0, keepdims=True), sink)
                e = jnp.exp(sb - m)
                den = jnp.sum(e, axis=0, keepdims=True) + jnp.exp(sink - m)
                pcol.append((e / den).astype(BF16))
            prow.append(jnp.concatenate(pcol, axis=1))
        p = jnp.concatenate(prow, axis=0)
        o = _dot_tn(p, vv)
        outs[2 * kh] = o[0:rows]
        outs[2 * kh + 1] = o[rows:2 * rows]
    return outs


def _even_prompt_kernel(sink_ref, z_ref, cos_ref, sin_ref, qg_ref, kg_ref, cw_ref, hs_ref, sw_ref, wo_ref,
                        o_ref, nk_ref, nv_ref, nc_ref, wo_bf_ref, kf_ref, vf_ref, ub_ref):
    i = pl.program_id(1)
    wo_bf_ref[...] = wo_ref[...].astype(BF16)

    @pl.when(i == 0)
    def _():
        kf_ref[...] = jnp.zeros_like(kf_ref)
        vf_ref[...] = jnp.zeros_like(vf_ref)
        ub_ref[0:8, :] = jnp.zeros((8, CONV_DIM), F32)

    kk = lax.broadcasted_iota(jnp.int32, (2 * BLK, BLK), 0)
    r = lax.broadcasted_iota(jnp.int32, (2 * BLK, BLK), 1)
    d = kk - r
    band = (d >= 0) & (d <= WINDOW)
    o0 = A_WIDTH + 2 * A_KV_WIDTH
    k = v = None
    for t in range(MIX_SUB):
        rows = slice(t * BLK, (t + 1) * BLK)
        kf_ref[0:BLK, :] = kf_ref[BLK:2 * BLK, :]
        vf_ref[0:BLK, :] = vf_ref[BLK:2 * BLK, :]
        cos = cos_ref[rows, :]
        sin = sin_ref[rows, :]
        k = jnp.concatenate(
            [_norm_rope_chunk(z_ref[rows, A_WIDTH + c * LANES:A_WIDTH + (c + 1) * LANES], kg_ref[...], cos, sin,
                              hs_ref, sw_ref)
             for c in range(A_KV_WIDTH // LANES)], axis=1)
        v = z_ref[rows, A_WIDTH + A_KV_WIDTH:A_WIDTH + 2 * A_KV_WIDTH]
        kf_ref[BLK:2 * BLK, :] = k
        vf_ref[BLK:2 * BLK, :] = v
        qs = [(_norm_rope_chunk(z_ref[rows, c * LANES:(c + 1) * LANES], qg_ref[...], cos, sin, hs_ref, sw_ref)
               * (HEAD_DIM ** -0.5)).astype(BF16) for c in range(A_WIDTH // LANES)]
        mask_t = band & ((kk >= BLK) | (i > 0)) if t == 0 else band
        outs = _attn_core(qs, kf_ref[...], vf_ref[...], mask_t, sink_ref)
        for c in range(A_WIDTH // LANES):
            o_ref[rows, c * LANES:(c + 1) * LANES] = outs[c].astype(o_ref.dtype)

        u = z_ref[rows, o0 + CONV_DIM:o0 + 2 * CONV_DIM] * z_ref[rows, o0 + 2 * CONV_DIM:o0 + 3 * CONV_DIM]
        ub_ref[8:8 + BLK, :] = u
        y = ub_ref[6:6 + BLK, :] * cw_ref[0:1, :]
        y = y + ub_ref[7:7 + BLK, :] * cw_ref[1:2, :]
        y = y + ub_ref[8:8 + BLK, :] * cw_ref[2:3, :]
        o_ref[rows, A_WIDTH:A_WIDTH + CONV_DIM] = (z_ref[rows, o0:o0 + CONV_DIM] * y).astype(o_ref.dtype)
        ub_ref[0:8, :] = ub_ref[BLK:BLK + 8, :]

    @pl.when(i == pl.num_programs(1) - 1)
    def _():
        nk_ref[...] = k
        nv_ref[...] = v
        nc_ref[...] = ub_ref[BLK + 6:BLK + 8, :]


def _w_slab_specs(w_out, layer, n_step):
    slab = w_out.shape[1] // (BATCH * n_step)
    return (pl.BlockSpec((None, slab, w_out.shape[2]), lambda b, i: (layer, b * n_step + i, 0)),
            pl.BlockSpec((slab, w_out.shape[2]), lambda b, i: (b * n_step + i, 0)),
            jax.ShapeDtypeStruct(w_out.shape[1:], BF16))


def _even_prompt(z_p, cos, sin, qg, kg, cw, sinks, mats, w_out, layer):
    n_step = NBLK // MIX_SUB
    step_rows = MIX_SUB * BLK
    blk = lambda b, i: (b * n_step + i, 0)
    fixed = lambda b, i: (0, 0)
    per_b = lambda b, i: (b, 0, 0)
    wo_in, wo_out, wo_shape = _w_slab_specs(w_out, layer, n_step)
    return pl.pallas_call(
        _even_prompt_kernel,
        grid=(BATCH, n_step),
        in_specs=[pl.BlockSpec(memory_space=pltpu.SMEM),
                  pl.BlockSpec((step_rows, EVEN_IN), blk),
                  pl.BlockSpec((step_rows, LANES), lambda b, i: (i, 0)),
                  pl.BlockSpec((step_rows, LANES), lambda b, i: (i, 0)),
                  pl.BlockSpec((1, LANES), fixed),
                  pl.BlockSpec((1, LANES), fixed),
                  pl.BlockSpec((CONV_W, CONV_DIM), fixed),
                  pl.BlockSpec((2 * LANES, LANES), fixed),
                  pl.BlockSpec((2 * LANES, LANES), fixed),
                  wo_in],
        out_specs=[pl.BlockSpec((step_rows, D_MODEL), blk),
                   pl.BlockSpec((None, BLK, A_KV_WIDTH), per_b),
                   pl.BlockSpec((None, BLK, A_KV_WIDTH), per_b),
                   pl.BlockSpec((None, CONV_W - 1, CONV_DIM), per_b),
                   wo_out],
        out_shape=[jax.ShapeDtypeStruct((M_PROMPT, D_MODEL), BF16),
                   jax.ShapeDtypeStruct((BATCH, BLK, A_KV_WIDTH), F32),
                   jax.ShapeDtypeStruct((BATCH, BLK, A_KV_WIDTH), F32),
                   jax.ShapeDtypeStruct((BATCH, CONV_W - 1, CONV_DIM), F32),
                   wo_shape],
        scratch_shapes=[pltpu.VMEM((2 * BLK, A_KV_WIDTH), F32),
                        pltpu.VMEM((2 * BLK, A_KV_WIDTH), F32),
                        pltpu.VMEM((BLK + 8, CONV_DIM), F32)],
        compiler_params=_cparams("arbitrary", "arbitrary"),
        name="even_prompt",
    )(sinks, z_p, cos, sin, qg, kg, cw, *mats, w_out)


SROWS = LANES
N_SAMPLE_IN = 12


def _even_sample_kernel(*refs):
    (sink_ref, z_ref, cos_ref, sin_ref, qg_ref, kg_ref, cs_ref, cw_ref, hs_ref, sw_ref,
     kc_ref, vc_ref) = refs[:N_SAMPLE_IN]
    m_ref, ncs_ref, nk_ref, nv_ref, q_all, k_all = refs[-6:]
    b = pl.program_id(0)
    lc = kc_ref.shape[0]

    @pl.when(b == 0)
    def _():
        cos = cos_ref[...]
        sin = sin_ref[...]
        for c in range(A_WIDTH // LANES):
            q = _norm_rope_chunk(z_ref[:, c * LANES:(c + 1) * LANES], qg_ref[...], cos, sin, hs_ref, sw_ref)
            q_all[:, c * LANES:(c + 1) * LANES] = q * (HEAD_DIM ** -0.5)
        for c in range(A_KV_WIDTH // LANES):
            k_all[:, c * LANES:(c + 1) * LANES] = _norm_rope_chunk(
                z_ref[:, A_WIDTH + c * LANES:A_WIDTH + (c + 1) * LANES], kg_ref[...], cos, sin, hs_ref, sw_ref)
        o0 = A_WIDTH + 2 * A_KV_WIDTH
        u = z_ref[:, o0 + CONV_DIM:o0 + 2 * CONV_DIM] * z_ref[:, o0 + 2 * CONV_DIM:o0 + 3 * CONV_DIM]
        y = cs_ref[:, 0:CONV_DIM] * cw_ref[0:1, :]
        y = y + cs_ref[:, CONV_DIM:2 * CONV_DIM] * cw_ref[1:2, :]
        y = y + u * cw_ref[2:3, :]
        m_ref[:, A_WIDTH:A_WIDTH + CONV_DIM] = z_ref[:, o0:o0 + CONV_DIM] * y
        m_ref[:, 0:A_WIDTH] = jnp.zeros((DEC_BATCH, A_WIDTH), F32)
        ncs_ref[:, 0:CONV_DIM] = cs_ref[:, CONV_DIM:2 * CONV_DIM]
        ncs_ref[:, CONV_DIM:2 * CONV_DIM] = u

    grp = pl.ds(pl.multiple_of((b // 8) * 8, 8), 8)
    mine = lax.broadcasted_iota(jnp.int32, (8, 1), 0) == b % 8

    def pick(tile):
        return jnp.sum(jnp.where(mine, tile, 0.0), axis=0, keepdims=True)

    kn = pick(k_all[grp, :])
    vn = pick(z_ref[grp, A_WIDTH + A_KV_WIDTH:A_WIDTH + 2 * A_KV_WIDTH])
    row = lax.broadcasted_iota(jnp.int32, (lc, A_KV_WIDTH), 0)
    kfull = jnp.concatenate([kc_ref[...], jnp.where(row == 0, kn, 0.0)], axis=0)
    vfull = jnp.concatenate([vc_ref[...], jnp.where(row == 0, vn, 0.0)], axis=0)
    qs = [jnp.broadcast_to(pick(q_all[grp, c * LANES:(c + 1) * LANES]), (SROWS, LANES)).astype(BF16)
          for c in range(A_WIDTH // LANES)]
    kk = lax.broadcasted_iota(jnp.int32, (2 * lc, SROWS), 0)
    mask_t = (kk <= lc) & (lc - kk <= WINDOW)
    outs = _attn_core(qs, kfull, vfull, mask_t, sink_ref)
    for c in range(A_WIDTH // LANES):
        cols = slice(c * LANES, (c + 1) * LANES)
        m_ref[grp, cols] = jnp.where(mine, outs[c][0:8], m_ref[grp, cols])
    nk_ref[0:lc - 1, :] = kc_ref[1:lc, :]
    nk_ref[lc - 1:lc, :] = kn
    nv_ref[0:lc - 1, :] = vc_ref[1:lc, :]
    nv_ref[lc - 1:lc, :] = vn


def _even_sample(sinks, z_s, cos, sin, qg, kg, conv_state, cw, mats, k_cache, v_cache, li, prev):
    lc = k_cache.shape[2]
    whole = lambda a: pl.BlockSpec(a.shape, lambda b: (0,) * a.ndim)
    cache = pl.BlockSpec((None, None, lc, A_KV_WIDTH), lambda b: (li, b, 0, 0))
    cs = conv_state.reshape(DEC_BATCH, (CONV_W - 1) * CONV_DIM)
    args = [sinks, z_s, cos, sin, qg, kg, cs, cw, *mats, k_cache, v_cache]
    assert len(args) == N_SAMPLE_IN
    in_specs = [pl.BlockSpec(memory_space=pltpu.SMEM)] + [whole(a) for a in args[1:N_SAMPLE_IN - 2]] + [cache, cache]
    aliases = {}
    if prev is not None:
        aliases = {len(args): 2, len(args) + 1: 3}
        in_specs += [pl.BlockSpec(memory_space=pl.ANY)] * 2
        args += list(prev)
    rows = lambda w: pl.BlockSpec((DEC_BATCH, w), lambda b: (0, 0))
    m_s, ncs, nk, nv = pl.pallas_call(
        _even_sample_kernel,
        grid=(DEC_BATCH,),
        in_specs=in_specs,
        out_specs=[rows(D_MODEL), rows((CONV_W - 1) * CONV_DIM), cache, cache],
        out_shape=[jax.ShapeDtypeStruct((DEC_BATCH, D_MODEL), F32),
                   jax.ShapeDtypeStruct((DEC_BATCH, (CONV_W - 1) * CONV_DIM), F32),
                   jax.ShapeDtypeStruct(k_cache.shape, F32),
                   jax.ShapeDtypeStruct(v_cache.shape, F32)],
        scratch_shapes=[pltpu.VMEM((DEC_BATCH, A_WIDTH), F32), pltpu.VMEM((DEC_BATCH, A_KV_WIDTH), F32)],
        input_output_aliases=aliases,
        compiler_params=_cparams("arbitrary"),
        name="even_sample",
    )(*args)
    return m_s, ncs.reshape(DEC_BATCH, CONV_W - 1, CONV_DIM), nk, nv


def _log_decay(zr, wa_ref, ba_ref):
    pre = _dot(zr.astype(BF16), wa_ref[...]) + ba_ref[...]
    return (jnp.minimum(pre, 0.0) - jnp.log1p(jnp.exp(-jnp.abs(pre)))) * (1.0 / GLA_TAU)


def _split_bf16(x):
    hi = x.astype(BF16)
    lo = (x - hi.astype(F32)).astype(BF16)
    return hi, lo


def _row_to_col(row):
    n = row.shape[1]
    r = lax.broadcasted_iota(jnp.int32, (n, n), 0)
    c = lax.broadcasted_iota(jnp.int32, (n, n), 1)
    return jnp.sum(jnp.where(r == c, jnp.broadcast_to(row, (n, n)), 0.0), axis=-1, keepdims=True)


def _head_rmsnorm_gate(o, gain, zg):
    ms = jnp.mean(o * o, axis=-1, keepdims=True)
    return ((o * lax.rsqrt(ms + EPS)) * gain) * _silu(zg)


def _odd_prompt_kernel(zm_ref, zpr_ref, wa_ref, ba_ref, gain_ref, wp_ref, ps_ref, tri_ref, band_ref, wo_ref,
                       o_ref, s_ref, np_ref, wo_bf_ref, pbh_ref, pbl_ref):
    i = pl.program_id(1)
    wo_bf_ref[...] = wo_ref[...].astype(BF16)

    @pl.when(i == 0)
    def _():
        s_ref[...] = jnp.zeros_like(s_ref)
        pbh_ref[0:BLK, :] = jnp.zeros((BLK, POOL_DIM), BF16)
        pbl_ref[0:BLK, :] = jnp.zeros((BLK, POOL_DIM), BF16)

    x = None
    for t in range(MIX_SUB):
        rows = pl.ds(t * BLK, BLK)
        x = _odd_block(zm_ref.at[rows], zpr_ref.at[rows], wa_ref, ba_ref, gain_ref, wp_ref, ps_ref,
                       tri_ref, band_ref, o_ref.at[rows], s_ref, pbh_ref, pbl_ref, (i * MIX_SUB + t) * BLK)

    @pl.when(i == pl.num_programs(1) - 1)
    def _():
        np_ref[...] = x[BLK - POOL_BUF:BLK, :]


def _odd_block(zm_ref, zpr_ref, wa_ref, ba_ref, gain_ref, wp_ref, ps_ref, tri_ref, band_ref,
               o_ref, s_ref, pbh_ref, pbl_ref, t0):
    n_ch = BLK // GLA_CHUNK
    gk = _log_decay(zpr_ref[:, POOL_DIM:POOL_RANK_W], wa_ref, ba_ref)
    r = lax.broadcasted_iota(jnp.int32, (BLK, BLK), 0)
    c = lax.broadcasted_iota(jnp.int32, (BLK, BLK), 1)
    causal = (r >= c) & (r // GLA_CHUNK == c // GLA_CHUNK)
    g_hi, g_lo = _split_bf16(gk)
    b = _dot(tri_ref[...], jnp.concatenate([g_hi, g_lo], axis=0))
    b_last = [b[(ch + 1) * GLA_CHUNK - 1:(ch + 1) * GLA_CHUNK, :] for ch in range(n_ch)]
    b_end = jnp.concatenate([jnp.broadcast_to(bl, (GLA_CHUNK, GLA_K_WIDTH)) for bl in b_last], axis=0)
    o_k, o_v, o_g = GLA_K_WIDTH, 2 * GLA_K_WIDTH, 2 * GLA_K_WIDTH + GLA_WIDTH
    zq = zm_ref[:, 0:GLA_K_WIDTH]
    zk = zm_ref[:, o_k:o_k + GLA_K_WIDTH]
    qd = ((zq * (GLA_DK ** -0.5)) * jnp.exp(b)).astype(BF16)
    kd = (zk * jnp.exp(-b)).astype(BF16)
    k2 = (zk * jnp.exp(b_end - b)).astype(BF16)
    sub = lax.broadcasted_iota(jnp.int32, (8, LANES), 0)
    sel = jnp.where(sub < 2, 1.0, 0.0).astype(BF16)
    hk = [slice(h * GLA_DK, (h + 1) * GLA_DK) for h in range(GLA_HEADS)]
    chunk = [slice(ch * GLA_CHUNK, (ch + 1) * GLA_CHUNK) for ch in range(n_ch)]
    v_hs = [zm_ref[:, o_v + h * GLA_DV:o_v + (h + 1) * GLA_DV].astype(BF16) for h in range(GLA_HEADS)]
    atts = [_dot_nt(qd[:, hk[h]], kd[:, hk[h]]) for h in range(GLA_HEADS)]
    log_decs = []
    for h in range(GLA_HEADS):
        for ch in range(n_ch):
            bl = b_last[ch][:, hk[h]]
            bl_hi = bl.astype(BF16).astype(F32)
            rows8 = jnp.where(sub == 0, bl_hi, jnp.where(sub == 1, bl - bl_hi, 0.0)).astype(BF16)
            log_decs.append(_dot_tn(rows8, sel))
    deltas = [[_dot_tn(k2[chunk[ch], hk[h]], v_hs[h][chunk[ch]]) for ch in range(n_ch)]
              for h in range(GLA_HEADS)]
    o_intras = [_dot(jnp.where(causal, atts[h], 0.0).astype(BF16), v_hs[h]) for h in range(GLA_HEADS)]
    o_inters = []
    for h in range(GLA_HEADS):
        s_h = s_ref[h]
        parts = []
        for ch in range(n_ch):
            parts.append(_dot(qd[chunk[ch], hk[h]], s_h.astype(BF16)))
            dec = jnp.exp(log_decs[h * n_ch + ch])
            s_h = jnp.concatenate([dec, dec], axis=1) * s_h + deltas[h][ch]
        s_ref[h] = s_h
        o_inters.append(jnp.concatenate(parts, axis=0))
    for h in range(GLA_HEADS):
        vs = slice(h * GLA_DV, (h + 1) * GLA_DV)
        zg = zm_ref[:, o_g + h * GLA_DV:o_g + (h + 1) * GLA_DV]
        o_ref[:, vs] = _head_rmsnorm_gate(o_intras[h] + o_inters[h], gain_ref[...], zg).astype(o_ref.dtype)

    x = zpr_ref[:, 0:POOL_DIM]
    x_hi, x_lo = _split_bf16(x)
    pbh_ref[BLK:2 * BLK, :] = x_hi
    pbl_ref[BLK:2 * BLK, :] = x_lo
    t_glob = t0 + lax.broadcasted_iota(jnp.int32, (BLK, 1), 0)
    groups = [slice(g * POOL_GDIM, (g + 1) * POOL_GDIM) for g in range(len(POOL_WINDOWS))]
    sums = [_dot(band_ref[g], jnp.concatenate([pbh_ref[:, groups[g]], pbl_ref[:, groups[g]]], axis=0))
            for g in range(len(POOL_WINDOWS))]
    ys = []
    for g, w in enumerate(POOL_WINDOWS):
        inv_cnt = 1.0 / jnp.minimum(w, t_glob + 1).astype(F32)
        d = sums[g] * inv_cnt - x[:, groups[g]]
        ys.append(_dot(d.astype(BF16), wp_ref[g]))
    for g in range(len(POOL_WINDOWS)):
        o_ref[:, GLA_WIDTH + g * POOL_GDIM:GLA_WIDTH + (g + 1) * POOL_GDIM] = (
            ys[g] * ps_ref[:, groups[g]]).astype(o_ref.dtype)
    pbh_ref[0:BLK, :] = x_hi
    pbl_ref[0:BLK, :] = x_lo
    return x


def _odd_block_mats():
    r = jnp.arange(BLK)[:, None]
    c = jnp.arange(BLK)[None, :]
    tri = ((r >= c) & (r // GLA_CHUNK == c // GLA_CHUNK)).astype(BF16)
    back = r + BLK - jnp.arange(2 * BLK)[None, :]
    bands = [((back >= 0) & (back < w)).astype(BF16) for w in POOL_WINDOWS]
    return (jnp.concatenate([tri, tri], axis=1),
            jnp.stack([jnp.concatenate([bd, bd], axis=1) for bd in bands]))


def _odd_prompt(zm_p, zpr_p, wa, ba, gain, wp, ps, w_out, layer):
    n_step = NBLK // MIX_SUB
    step_rows = MIX_SUB * BLK
    blk = lambda b, i: (b * n_step + i, 0)
    fixed2 = lambda b, i: (0, 0)
    wo_in, wo_out, wo_shape = _w_slab_specs(w_out, layer, n_step)
    tri2, bands2 = _odd_block_mats()
    return pl.pallas_call(
        _odd_prompt_kernel,
        grid=(BATCH, n_step),
        in_specs=[pl.BlockSpec((step_rows, ODD_MAIN), blk),
                  pl.BlockSpec((step_rows, POOL_RANK_W), blk),
                  pl.BlockSpec((LANES, GLA_K_WIDTH), fixed2),
                  pl.BlockSpec((1, GLA_K_WIDTH), fixed2),
                  pl.BlockSpec((1, GLA_DV), fixed2),
                  pl.BlockSpec((4, POOL_GDIM, POOL_GDIM), lambda b, i: (0, 0, 0)),
                  pl.BlockSpec((1, POOL_DIM), fixed2),
                  pl.BlockSpec(tri2.shape, fixed2),
                  pl.BlockSpec(bands2.shape, lambda b, i: (0, 0, 0)),
                  wo_in],
        out_specs=[pl.BlockSpec((step_rows, D_MODEL), blk),
                   pl.BlockSpec((None, GLA_HEADS, GLA_DK, GLA_DV), lambda b, i: (b, 0, 0, 0)),
                   pl.BlockSpec((None, POOL_BUF, POOL_DIM), lambda b, i: (b, 0, 0)),
                   wo_out],
        out_shape=[jax.ShapeDtypeStruct((M_PROMPT, D_MODEL), BF16),
                   jax.ShapeDtypeStruct((BATCH, GLA_HEADS, GLA_DK, GLA_DV), F32),
                   jax.ShapeDtypeStruct((BATCH, POOL_BUF, POOL_DIM), F32),
                   wo_shape],
        scratch_shapes=[pltpu.VMEM((2 * BLK, POOL_DIM), BF16), pltpu.VMEM((2 * BLK, POOL_DIM), BF16)],
        compiler_params=_cparams("arbitrary", "arbitrary"),
        name="odd_prompt",
    )(zm_p, zpr_p, wa, ba, gain, wp, ps, tri2, bands2, w_out)


def _odd_sample_prep_kernel(zm_ref, zr_ref, wa_ref, ba_ref, dec_ref, q_ref):
    dec_ref[...] = jnp.exp(_log_decay(zr_ref[...], wa_ref, ba_ref))
    q_ref[...] = zm_ref[:, 0:GLA_K_WIDTH] * (GLA_DK ** -0.5)


def _odd_sample_prep(zm_s, zr_s, wa, ba):
    return pl.pallas_call(
        _odd_sample_prep_kernel,
        out_shape=[jax.ShapeDtypeStruct((DEC_BATCH, GLA_K_WIDTH), F32),
                   jax.ShapeDtypeStruct((DEC_BATCH, GLA_K_WIDTH), F32)],
        compiler_params=pltpu.CompilerParams(vmem_limit_bytes=VMEM_LIMIT),
        name="odd_sample_prep",
    )(zm_s, zr_s, wa, ba)


STATE_BB = 4


def _odd_sample_state_kernel(*refs):
    dec_ref, q_ref, k_ref, v_ref, s_ref = refs[:5]
    ns_ref, o_ref = refs[-2:]
    for bb in range(STATE_BB):
        for h in range(GLA_HEADS):
            ks = slice(h * GLA_DK, (h + 1) * GLA_DK)
            vs = slice(h * GLA_DV, (h + 1) * GLA_DV)
            s_new = (_row_to_col(dec_ref[bb][:, ks]) * s_ref[bb, h]
                     + _row_to_col(k_ref[bb][:, ks]) * v_ref[bb][:, vs])
            ns_ref[bb, h] = s_new
            o_ref[bb, :, vs] = jnp.sum(_row_to_col(q_ref[bb][:, ks]) * s_new, axis=0, keepdims=True)


def _odd_sample_state(dec, q, k, v, state, li, prev):
    vec = lambda w: pl.BlockSpec((STATE_BB, 1, w), lambda b: (b, 0, 0))
    st = pl.BlockSpec((None, STATE_BB, GLA_HEADS, GLA_DK, GLA_DV), lambda b: (li, b, 0, 0, 0))
    r3 = lambda a: a.reshape(DEC_BATCH, 1, a.shape[-1])
    in_specs = [vec(GLA_K_WIDTH), vec(GLA_K_WIDTH), vec(GLA_K_WIDTH), vec(GLA_WIDTH), st]
    args = [r3(dec), r3(q), r3(k), r3(v), state]
    aliases = {}
    if prev is not None:
        aliases = {len(args): 0}
        in_specs.append(pl.BlockSpec(memory_space=pl.ANY))
        args.append(prev)
    return pl.pallas_call(
        _odd_sample_state_kernel,
        grid=(DEC_BATCH // STATE_BB,),
        in_specs=in_specs,
        out_specs=[st, vec(GLA_WIDTH)],
        out_shape=[jax.ShapeDtypeStruct(state.shape, F32),
                   jax.ShapeDtypeStruct((DEC_BATCH, 1, GLA_WIDTH), F32)],
        input_output_aliases=aliases,
        compiler_params=_cparams("arbitrary"),
        name="odd_sample_state",
    )(*args)


def _odd_sample_post_kernel(o_in_ref, zm_ref, zp_ref, hist_ref, gain_ref, wp_ref, ps_ref, o_ref):
    o_g = 2 * GLA_K_WIDTH + GLA_WIDTH
    for h in range(GLA_HEADS):
        vs = slice(h * GLA_DV, (h + 1) * GLA_DV)
        zg = zm_ref[:, o_g + h * GLA_DV:o_g + (h + 1) * GLA_DV]
        o_ref[:, vs] = _head_rmsnorm_gate(o_in_ref[:, vs], gain_ref[...], zg).astype(o_ref.dtype)
    n_prev = hist_ref.shape[0]
    for g, w in enumerate(POOL_WINDOWS):
        gs = slice(g * POOL_GDIM, (g + 1) * POOL_GDIM)
        x = zp_ref[:, gs]
        s = hist_ref[n_prev - (w - 1)][:, gs]
        for jj in range(n_prev - (w - 1) + 1, n_prev):
            s = s + hist_ref[jj][:, gs]
        s = s + x
        d = s / float(min(w, n_prev + 1)) - x
        y = _dot(d.astype(BF16), wp_ref[g]) * ps_ref[:, gs]
        o_ref[:, GLA_WIDTH + g * POOL_GDIM:GLA_WIDTH + (g + 1) * POOL_GDIM] = y.astype(o_ref.dtype)


def _odd_sample_post(o_raw, zm_s, zp_s, hist_t, gain, wp, ps):
    return pl.pallas_call(
        _odd_sample_post_kernel,
        out_shape=jax.ShapeDtypeStruct((DEC_BATCH, D_MODEL), BF16),
        compiler_params=pltpu.CompilerParams(vmem_limit_bytes=VMEM_LIMIT),
        name="odd_sample_post",
    )(o_raw, zm_s, zp_s, hist_t, gain, wp, ps)


def _rope_tables(pos):
    half = HEAD_DIM // 2
    inv = jnp.power(ROPE_THETA, -jnp.arange(half, dtype=F32) / half)
    ang = pos.astype(F32)[:, None] * inv[None, :]
    c, s = jnp.cos(ang), jnp.sin(ang)
    return jnp.tile(c, (1, 4)), jnp.tile(jnp.concatenate([-s, s], axis=1), (1, 2))


def kernel(x_prompt, x_sample, cache_swa_k, cache_swa_v, state_conv, state_gla, state_pool, norm_mix, norm_ffn, w_in_even, w_out_even, q_norm, k_norm, attn_sinks, conv_w, w_in_odd, w_out_odd, w_alpha_up, b_alpha, gla_out_norm, w_pool, pool_scale, w_gate, w_up, w_down):
    lc = cache_swa_k.shape[2]
    x_p = x_prompt.reshape(M_PROMPT, D_MODEL)
    x_s = x_sample.reshape(DEC_BATCH, D_MODEL)
    cos_p, sin_p = _rope_tables(jnp.arange(SEQ))
    cos_s, sin_s = _rope_tables(PAST_LEN + jnp.arange(1))
    mats = _rope_mats()
    kc_all = cache_swa_k.reshape(N_EVEN, DEC_BATCH, lc, A_KV_WIDTH)
    vc_all = cache_swa_v.reshape(N_EVEN, DEC_BATCH, lc, A_KV_WIDTH)
    o_r = ODD_MAIN
    w_odd_t = jnp.swapaxes(w_in_odd, 1, 2)
    w_odd_pr_t = jnp.concatenate([w_odd_t[:, o_r + GLA_RANK:], w_odd_t[:, o_r:o_r + GLA_RANK],
                                  jnp.zeros((N_ODD, LANES - GLA_RANK, D_MODEL), F32)], axis=1)
    pk, pv, pc, pg, pp, sc, sp = ([] for _ in range(7))
    sk_all = sv_all = sg_all = None

    h_p, h_s = _rmsnorm(x_p, x_s, norm_mix[0])
    for layer in range(DEPTH):
        li = layer // 2
        if layer % 2 == 0:
            z_p, z_s = _dense(h_p, h_s, [(w_in_even, li)], n_cols=EVEN_IN, tm=1024, tn=1536, name="in_even")
            qg = jnp.tile(q_norm[li], 2).reshape(1, LANES)
            kg = jnp.tile(k_norm[li], 2).reshape(1, LANES)
            m_p, nk, nv, nc, w_out_bf = _even_prompt(z_p, cos_p, sin_p, qg, kg, conv_w[li], attn_sinks[li], mats,
                                                     w_out_even, li)
            pk.append(nk.reshape(BATCH, lc, A_KV_HEADS, HEAD_DIM))
            pv.append(nv.reshape(BATCH, lc, A_KV_HEADS, HEAD_DIM))
            pc.append(nc)
            m_s, nc_s, sk_all, sv_all = _even_sample(
                attn_sinks[li], z_s, cos_s, sin_s, qg, kg, state_conv[li], conv_w[li], mats, kc_all, vc_all, li,
                None if sk_all is None else (sk_all, sv_all))
            sc.append(nc_s)
        else:
            zm_p, zm_s = _dense(h_p, h_s, [(w_odd_t, li)], n_cols=ODD_MAIN, tm=1024, tn=ODD_MAIN // 2,
                                transposed=True, name="in_odd_main")
            zpr_p, zpr_s = _dense(h_p, h_s, [(w_odd_pr_t, li)], n_cols=POOL_RANK_W, tm=1024, tn=POOL_RANK_W,
                                  transposed=True, name="in_odd_pool_rank")
            zp_s, zr_s = zpr_s[:, :POOL_DIM], zpr_s[:, POOL_DIM:]
            wa = jnp.pad(w_alpha_up[li], ((0, LANES - GLA_RANK), (0, 0))).astype(BF16)
            ba = b_alpha[li].reshape(1, GLA_K_WIDTH)
            gain = gla_out_norm[li].reshape(1, GLA_DV)
            wp = w_pool[li].astype(BF16)
            ps = pool_scale[li].reshape(1, POOL_DIM)
            m_p, ng, npool, w_out_bf = _odd_prompt(zm_p, zpr_p, wa, ba, gain, wp, ps, w_out_odd, li)
            pg.append(ng)
            pp.append(npool)
            dec_s, q_s = _odd_sample_prep(zm_s, zr_s, wa, ba)
            sg_all, o_raw = _odd_sample_state(
                dec_s, q_s, zm_s[:, GLA_K_WIDTH:2 * GLA_K_WIDTH],
                zm_s[:, 2 * GLA_K_WIDTH:2 * GLA_K_WIDTH + GLA_WIDTH], state_gla, li, sg_all)
            m_s = _odd_sample_post(o_raw.reshape(DEC_BATCH, GLA_WIDTH), zm_s, zp_s,
                                   jnp.swapaxes(state_pool[li], 0, 1), gain, wp, ps)
            sp.append(jnp.concatenate([state_pool[li, :, 1:], zp_s[:, None, :]], axis=1))
        x_p, x_s, h_p, h_s = _proj_res(m_p, m_s, (w_out_bf, None), (x_p, x_s), norm_ffn[layer],
                                       tm=512, name="out_proj")
        a_p, a_s, w_down_bf = _dense(h_p, h_s, [(w_gate, layer), (w_up, layer)], n_cols=D_FF, tm=2048, tn=512,
                                     mode="swiglu", out_dtype=BF16, side_cast=(w_down, layer), sub=2,
                                     name="ffn_up")
        if layer + 1 < DEPTH:
            x_p, x_s, h_p, h_s = _proj_res(a_p, a_s, (w_down_bf, None), (x_p, x_s),
                                           norm_mix[layer + 1], tm=512, name="ffn_down")
        else:
            x_p, x_s = _proj_res(a_p, a_s, (w_down_bf, None), (x_p, x_s), None,
                                 tm=512, name="ffn_down_last")

    st = lambda parts: jnp.stack(parts)
    cache5 = lambda a: a.reshape(N_EVEN, DEC_BATCH, lc, A_KV_HEADS, HEAD_DIM)
    return (x_p.reshape(BATCH, SEQ, D_MODEL), x_s.reshape(DEC_BATCH, 1, D_MODEL),
            st(pk), st(pv), st(pc), st(pg), st(pp), cache5(sk_all), cache5(sv_all), st(sc), sg_all, st(sp))
```

```python
import functools

import jax
import jax.numpy as jnp
from jax import lax
from jax.experimental import pallas as pl
from jax.experimental.pallas import tpu as pltpu

F32 = jnp.float32
BF16 = jnp.bfloat16

D_MODEL = 2048
BATCH = 4
SEQ = 2048
DEPTH = 4
DEC_BATCH = 32
PAST_LEN = 16384
N_EVEN = 2
N_ODD = 2
EPS = 1e-6
NEG_INF = -1e30
A_HEADS = 16
A_KV_HEADS = 4
HEAD_DIM = 64
A_WIDTH = A_HEADS * HEAD_DIM
A_KV_WIDTH = A_KV_HEADS * HEAD_DIM
WINDOW = 128
ROPE_THETA = 10000.0
CONV_DIM = D_MODEL // 2
CONV_W = 3
GLA_HEADS = 4
GLA_WIDTH = D_MODEL // 2
GLA_DV = GLA_WIDTH // GLA_HEADS
GLA_DK = GLA_DV // 2
GLA_K_WIDTH = GLA_HEADS * GLA_DK
GLA_RANK = 16
GLA_TAU = 16.0
GLA_CHUNK = 64
POOL_DIM = D_MODEL // 2
POOL_WINDOWS = (2, 4, 8, 16)
POOL_GDIM = POOL_DIM // 4
POOL_BUF = 15
D_FF = 5632
EVEN_IN = A_WIDTH + 2 * A_KV_WIDTH + 3 * CONV_DIM
ODD_MAIN = 2 * GLA_K_WIDTH + 2 * GLA_WIDTH

M_PROMPT = BATCH * SEQ
LANES = 128
BLK = 128
NBLK = SEQ // BLK
MIX_SUB = 4
POOL_RANK_W = POOL_DIM + LANES
VMEM_LIMIT = 58 * 1024 * 1024


def _cparams(*sem):
    return pltpu.CompilerParams(dimension_semantics=sem, vmem_limit_bytes=VMEM_LIMIT)


def _dot(a, b):
    return jnp.dot(a, b, preferred_element_type=F32)


def _dot_nt(a, b):
    return lax.dot_general(a, b, (((1,), (1,)), ((), ())), preferred_element_type=F32)


def _dot_tn(a, b):
    return lax.dot_general(a, b, (((0,), (0,)), ((), ())), preferred_element_type=F32)


def _silu(x):
    return x * (1.0 / (1.0 + jnp.exp(-x)))


def _rmsnorm_rows(x, gain):
    ms = jnp.mean(x * x, axis=-1, keepdims=True)
    return (x * lax.rsqrt(ms + EPS)) * gain


def _rmsnorm_kernel(xp_ref, xs_ref, g_ref, op_ref, os_ref, *, n_i):
    op_ref[...] = _rmsnorm_rows(xp_ref[...], g_ref[...]).astype(op_ref.dtype)

    @pl.when(pl.program_id(0) == n_i - 1)
    def _():
        os_ref[...] = _rmsnorm_rows(xs_ref[...], g_ref[...]).astype(os_ref.dtype)


def _rmsnorm(x_p, x_s, gain, *, tm=1024):
    n_i = M_PROMPT // tm
    row = lambda i: (i, 0)
    fixed = lambda i: (0, 0)
    return pl.pallas_call(
        functools.partial(_rmsnorm_kernel, n_i=n_i),
        grid=(n_i,),
        in_specs=[pl.BlockSpec((tm, D_MODEL), row),
                  pl.BlockSpec((DEC_BATCH, D_MODEL), fixed),
                  pl.BlockSpec((1, D_MODEL), fixed)],
        out_specs=[pl.BlockSpec((tm, D_MODEL), row),
                   pl.BlockSpec((DEC_BATCH, D_MODEL), fixed)],
        out_shape=[jax.ShapeDtypeStruct((M_PROMPT, D_MODEL), BF16),
                   jax.ShapeDtypeStruct((DEC_BATCH, D_MODEL), BF16)],
        compiler_params=_cparams("arbitrary"),
        name="rmsnorm",
    )(x_p, x_s, gain.reshape(1, D_MODEL))


XPOSE_COLS = 128


def _dense_kernel(*refs, n_w, mode, n_i, transposed, side_cast, sub):
    a_p, a_s = refs[0], refs[1]
    w = refs[2:2 + n_w]
    pos = 2 + n_w
    side_in = side_out = None
    if side_cast:
        side_in = refs[pos]
        pos += 1
    o_p, o_s = refs[pos], refs[pos + 1]
    pos += 2
    if side_cast:
        side_out = refs[pos]
        pos += 1
    wbf = refs[pos:pos + n_w]
    i = pl.program_id(1)

    @pl.when(i == 0)
    def _():
        for k in range(n_w):
            if transposed:
                tn = wbf[k].shape[1]
                for c in range(0, tn, XPOSE_COLS):
                    wbf[k][:, c:c + XPOSE_COLS] = w[k][c:c + XPOSE_COLS, :].T.astype(BF16)
            else:
                wbf[k][...] = w[k][...].astype(BF16)
        if side_cast:
            side_out[...] = side_in[...].astype(BF16)

    def run(a_ref, o_ref, n_sub):
        rows = a_ref.shape[0] // n_sub
        for r in range(n_sub):
            a = a_ref[r * rows:(r + 1) * rows, :]
            if mode == "swiglu":
                y = _silu(_dot(a, wbf[0][...])) * _dot(a, wbf[1][...])
            else:
                y = _dot(a, wbf[0][...])
            o_ref[r * rows:(r + 1) * rows, :] = y.astype(o_ref.dtype)

    run(a_p, o_p, sub)

    @pl.when(i == n_i - 1)
    def _():
        run(a_s, o_s, 1)


def _dense(a_p, a_s, weights, *, n_cols, tm, tn, mode="plain", out_dtype=F32, transposed=False,
           side_cast=None, sub=1, name):
    k_dim = a_p.shape[1]
    n_i = M_PROMPT // tm
    n_j = n_cols // tn
    row = lambda j, i: (i, 0)
    fixed = lambda j, i: (0, 0)
    tile = lambda j, i: (i, j)
    panel = lambda j, i: (0, j)
    in_specs = [pl.BlockSpec((tm, k_dim), row), pl.BlockSpec((DEC_BATCH, k_dim), fixed)]
    args = [a_p, a_s]
    for arr, layer in weights:
        if transposed:
            in_specs.append(pl.BlockSpec((None, tn, k_dim), lambda j, i, layer=layer: (layer, j, 0)))
        else:
            in_specs.append(pl.BlockSpec((None, k_dim, tn), lambda j, i, layer=layer: (layer, 0, j)))
        args.append(arr)
    out_specs = [pl.BlockSpec((tm, tn), tile), pl.BlockSpec((DEC_BATCH, tn), panel)]
    out_shape = [jax.ShapeDtypeStruct((M_PROMPT, n_cols), out_dtype),
                 jax.ShapeDtypeStruct((DEC_BATCH, n_cols), out_dtype)]
    if side_cast is not None:
        s_arr, s_layer = side_cast
        slab = s_arr.shape[1] // n_j
        in_specs.append(pl.BlockSpec((None, slab, s_arr.shape[2]), lambda j, i: (s_layer, j, 0)))
        args.append(s_arr)
        out_specs.append(pl.BlockSpec((slab, s_arr.shape[2]), lambda j, i: (j, 0)))
        out_shape.append(jax.ShapeDtypeStruct(s_arr.shape[1:], BF16))
    n_w = len(weights)
    return pl.pallas_call(
        functools.partial(_dense_kernel, n_w=n_w, mode=mode, n_i=n_i, transposed=transposed,
                          side_cast=side_cast is not None, sub=sub),
        grid=(n_j, n_i),
        in_specs=in_specs,
        out_specs=out_specs,
        out_shape=out_shape,
        scratch_shapes=[pltpu.VMEM((k_dim, tn), BF16) for _ in range(n_w)],
        compiler_params=_cparams("arbitrary", "arbitrary"),
        name=name,
    )(*args)


def _proj_res_kernel(*refs, n_i, with_norm):
    a_p, a_s, w_ref, r_p, r_s = refs[:5]
    if with_norm:
        g_ref, x_p, x_s, h_p, h_s = refs[5:]
    else:
        x_p, x_s = refs[5:]
        g_ref = h_p = h_s = None

    def run(a_ref, r_ref, x_ref, h_ref):
        a = a_ref[...]
        if a.dtype != BF16:
            a = a.astype(BF16)
        x = r_ref[...] + _dot(a, w_ref[...])
        x_ref[...] = x
        if with_norm:
            h_ref[...] = _rmsnorm_rows(x, g_ref[...]).astype(h_ref.dtype)

    run(a_p, r_p, x_p, h_p)

    @pl.when(pl.program_id(0) == n_i - 1)
    def _():
        run(a_s, r_s, x_s, h_s)


def _proj_res(a_p, a_s, weight, res, gain, *, tm, name):
    w_bf, layer = weight
    k_dim = a_p.shape[1]
    if layer is None:
        w_spec = pl.BlockSpec((k_dim, D_MODEL), lambda i: (0, 0), pipeline_mode=pl.Buffered(1))
    else:
        w_spec = pl.BlockSpec((None, k_dim, D_MODEL), lambda i: (layer, 0, 0), pipeline_mode=pl.Buffered(1))
    n_i = M_PROMPT // tm
    with_norm = gain is not None
    row = lambda i: (i, 0)
    fixed = lambda i: (0, 0)
    rows_p = pl.BlockSpec((tm, D_MODEL), row)
    rows_s = pl.BlockSpec((DEC_BATCH, D_MODEL), fixed)
    in_specs = [pl.BlockSpec((tm, k_dim), row), pl.BlockSpec((DEC_BATCH, k_dim), fixed),
                w_spec, rows_p, rows_s]
    args = [a_p, a_s, w_bf, res[0], res[1]]
    out_specs = [rows_p, rows_s]
    out_shape = [jax.ShapeDtypeStruct((M_PROMPT, D_MODEL), F32), jax.ShapeDtypeStruct((DEC_BATCH, D_MODEL), F32)]
    if with_norm:
        in_specs.append(pl.BlockSpec((1, D_MODEL), fixed))
        args.append(gain.reshape(1, D_MODEL))
        out_specs += [rows_p, rows_s]
        out_shape += [jax.ShapeDtypeStruct((M_PROMPT, D_MODEL), BF16),
                      jax.ShapeDtypeStruct((DEC_BATCH, D_MODEL), BF16)]
    return pl.pallas_call(
        functools.partial(_proj_res_kernel, n_i=n_i, with_norm=with_norm),
        grid=(n_i,),
        in_specs=in_specs,
        out_specs=out_specs,
        out_shape=out_shape,
        compiler_params=_cparams("arbitrary"),
        name=name,
    )(*args)


def _split_cat(x):
    hi = x.astype(BF16)
    lo = (x - hi.astype(F32)).astype(BF16)
    return jnp.concatenate([hi, lo], axis=1)


def _rope_mats():
    j = jnp.arange(2 * LANES)[:, None] % LANES
    l = jnp.arange(LANES)[None, :]
    head_sum = (j // HEAD_DIM == l // HEAD_DIM).astype(BF16)
    src = jnp.where(l % HEAD_DIM < HEAD_DIM // 2, l + HEAD_DIM // 2, l - HEAD_DIM // 2)
    half_swap = (j == src).astype(BF16)
    return head_sum, half_swap


def _norm_rope_chunk(xc, gain, cos, sin, hs_ref, sw_ref):
    ms = _dot(_split_cat(xc * xc), hs_ref[...]) * (1.0 / HEAD_DIM)
    y = (xc * lax.rsqrt(ms + EPS)) * gain
    swapped = _dot(_split_cat(y), sw_ref[...])
    return y * cos + swapped * sin


def _spread_heads(xc, own_lo):
    lane = lax.broadcasted_iota(jnp.int32, xc.shape, 1)
    keep = (lane < HEAD_DIM) if own_lo else (lane >= HEAD_DIM)
    nat = jnp.where(keep, xc, 0.0)
    rol = pltpu.roll(nat, HEAD_DIM, axis=1)
    parts = (nat, rol) if own_lo else (rol, nat)
    return jnp.concatenate(parts, axis=0).astype(BF16)


def _attn_core(qs, kfull, vfull, mask_t, sink_ref):
    rows = qs[0].shape[0]
    nkeys = kfull.shape[0]
    outs = [None] * 8
    for kh in range(A_KV_HEADS):
        c0 = (kh // 2) * LANES
        kk = _spread_heads(kfull[:, c0:c0 + LANES], kh % 2 == 0)
        vv = _spread_heads(vfull[:, c0:c0 + LANES], kh % 2 == 0)
        lhs = jnp.concatenate([qs[2 * kh], qs[2 * kh + 1]], axis=0)
        s = _dot_nt(kk, lhs)
        prow = []
        for half in range(2):
            pcol = []
            for cc in range(2):
                sb = s[half * nkeys:(half + 1) * nkeys, cc * rows:(cc + 1) * rows]
                sb = jnp.where(mask_t, sb, NEG_INF)
                sink = sink_ref[kh * 4 + 2 * cc + half]
                m = jnp.maximum(jnp.max(sb, axis=0, keepdims=True), sink)
                e = jnp.exp(sb - m)
                den = jnp.sum(e, axis=0, keepdims=True) + jnp.exp(sink - m)
                pcol.append((e / den).astype(BF16))
            prow.append(jnp.concatenate(pcol, axis=1))
        p = jnp.concatenate(prow, axis=0)
        o = _dot_tn(p, vv)
        outs[2 * kh] = o[0:rows]
        outs[2 * kh + 1] = o[rows:2 * rows]
    return outs


def _even_prompt_kernel(sink_ref, z_ref, cos_ref, sin_ref, qg_ref, kg_ref, cw_ref, hs_ref, sw_ref, wo_ref,
                        o_ref, nk_ref, nv_ref, nc_ref, wo_bf_ref, kf_ref, vf_ref, ub_ref):
    i = pl.program_id(1)
    wo_bf_ref[...] = wo_ref[...].astype(BF16)

    @pl.when(i == 0)
    def _():
        kf_ref[...] = jnp.zeros_like(kf_ref)
        vf_ref[...] = jnp.zeros_like(vf_ref)
        ub_ref[0:8, :] = jnp.zeros((8, CONV_DIM), F32)

    kk = lax.broadcasted_iota(jnp.int32, (2 * BLK, BLK), 0)
    r = lax.broadcasted_iota(jnp.int32, (2 * BLK, BLK), 1)
    d = kk - r
    band = (d >= 0) & (d <= WINDOW)
    o0 = A_WIDTH + 2 * A_KV_WIDTH
    k = v = None
    for t in range(MIX_SUB):
        rows = slice(t * BLK, (t + 1) * BLK)
        kf_ref[0:BLK, :] = kf_ref[BLK:2 * BLK, :]
        vf_ref[0:BLK, :] = vf_ref[BLK:2 * BLK, :]
        cos = cos_ref[rows, :]
        sin = sin_ref[rows, :]
        k = jnp.concatenate(
            [_norm_rope_chunk(z_ref[rows, A_WIDTH + c * LANES:A_WIDTH + (c + 1) * LANES], kg_ref[...], cos, sin,
                              hs_ref, sw_ref)
             for c in range(A_KV_WIDTH // LANES)], axis=1)
        v = z_ref[rows, A_WIDTH + A_KV_WIDTH:A_WIDTH + 2 * A_KV_WIDTH]
        kf_ref[BLK:2 * BLK, :] = k
        vf_ref[BLK:2 * BLK, :] = v
        qs = [(_norm_rope_chunk(z_ref[rows, c * LANES:(c + 1) * LANES], qg_ref[...], cos, sin, hs_ref, sw_ref)
               * (HEAD_DIM ** -0.5)).astype(BF16) for c in range(A_WIDTH // LANES)]
        mask_t = band & ((kk >= BLK) | (i > 0)) if t == 0 else band
        outs = _attn_core(qs, kf_ref[...], vf_ref[...], mask_t, sink_ref)
        for c in range(A_WIDTH // LANES):
            o_ref[rows, c * LANES:(c + 1) * LANES] = outs[c].astype(o_ref.dtype)

        u = z_ref[rows, o0 + CONV_DIM:o0 + 2 * CONV_DIM] * z_ref[rows, o0 + 2 * CONV_DIM:o0 + 3 * CONV_DIM]
        ub_ref[8:8 + BLK, :] = u
        y = ub_ref[6:6 + BLK, :] * cw_ref[0:1, :]
        y = y + ub_ref[7:7 + BLK, :] * cw_ref[1:2, :]
        y = y + ub_ref[8:8 + BLK, :] * cw_ref[2:3, :]
        o_ref[rows, A_WIDTH:A_WIDTH + CONV_DIM] = (z_ref[rows, o0:o0 + CONV_DIM] * y).astype(o_ref.dtype)
        ub_ref[0:8, :] = ub_ref[BLK:BLK + 8, :]

    @pl.when(i == pl.num_programs(1) - 1)
    def _():
        nk_ref[...] = k
        nv_ref[...] = v
        nc_ref[...] = ub_ref[BLK + 6:BLK + 8, :]


def _w_slab_specs(w_out, layer, n_step):
    slab = w_out.shape[1] // (BATCH * n_step)
    return (pl.BlockSpec((None, slab, w_out.shape[2]), lambda b, i: (layer, b * n_step + i, 0)),
            pl.BlockSpec((slab, w_out.shape[2]), lambda b, i: (b * n_step + i, 0)),
            jax.ShapeDtypeStruct(w_out.shape[1:], BF16))


def _even_prompt(z_p, cos, sin, qg, kg, cw, sinks, mats, w_out, layer):
    n_step = NBLK // MIX_SUB
    step_rows = MIX_SUB * BLK
    blk = lambda b, i: (b * n_step + i, 0)
    fixed = lambda b, i: (0, 0)
    per_b = lambda b, i: (b, 0, 0)
    wo_in, wo_out, wo_shape = _w_slab_specs(w_out, layer, n_step)
    return pl.pallas_call(
        _even_prompt_kernel,
        grid=(BATCH, n_step),
        in_specs=[pl.BlockSpec(memory_space=pltpu.SMEM),
                  pl.BlockSpec((step_rows, EVEN_IN), blk),
                  pl.BlockSpec((step_rows, LANES), lambda b, i: (i, 0)),
                  pl.BlockSpec((step_rows, LANES), lambda b, i: (i, 0)),
                  pl.BlockSpec((1, LANES), fixed),
                  pl.BlockSpec((1, LANES), fixed),
                  pl.BlockSpec((CONV_W, CONV_DIM), fixed),
                  pl.BlockSpec((2 * LANES, LANES), fixed),
                  pl.BlockSpec((2 * LANES, LANES), fixed),
                  wo_in],
        out_specs=[pl.BlockSpec((step_rows, D_MODEL), blk),
                   pl.BlockSpec((None, BLK, A_KV_WIDTH), per_b),
                   pl.BlockSpec((None, BLK, A_KV_WIDTH), per_b),
                   pl.BlockSpec((None, CONV_W - 1, CONV_DIM), per_b),
                   wo_out],
        out_shape=[jax.ShapeDtypeStruct((M_PROMPT, D_MODEL), BF16),
                   jax.ShapeDtypeStruct((BATCH, BLK, A_KV_WIDTH), F32),
                   jax.ShapeDtypeStruct((BATCH, BLK, A_KV_WIDTH), F32),
                   jax.ShapeDtypeStruct((BATCH, CONV_W - 1, CONV_DIM), F32),
                   wo_shape],
        scratch_shapes=[pltpu.VMEM((2 * BLK, A_KV_WIDTH), F32),
                        pltpu.VMEM((2 * BLK, A_KV_WIDTH), F32),
                        pltpu.VMEM((BLK + 8, CONV_DIM), F32)],
        compiler_params=_cparams("arbitrary", "arbitrary"),
        name="even_prompt",
    )(sinks, z_p, cos, sin, qg, kg, cw, *mats, w_out)


SROWS = LANES
N_SAMPLE_IN = 12


def _even_sample_kernel(*refs):
    (sink_ref, z_ref, cos_ref, sin_ref, qg_ref, kg_ref, cs_ref, cw_ref, hs_ref, sw_ref,
     kc_ref, vc_ref) = refs[:N_SAMPLE_IN]
    m_ref, ncs_ref, nk_ref, nv_ref, q_all, k_all = refs[-6:]
    b = pl.program_id(0)
    lc = kc_ref.shape[0]

    @pl.when(b == 0)
    def _():
        cos = cos_ref[...]
        sin = sin_ref[...]
        for c in range(A_WIDTH // LANES):
            q = _norm_rope_chunk(z_ref[:, c * LANES:(c + 1) * LANES], qg_ref[...], cos, sin, hs_ref, sw_ref)
            q_all[:, c * LANES:(c + 1) * LANES] = q * (HEAD_DIM ** -0.5)
        for c in range(A_KV_WIDTH // LANES):
            k_all[:, c * LANES:(c + 1) * LANES] = _norm_rope_chunk(
                z_ref[:, A_WIDTH + c * LANES:A_WIDTH + (c + 1) * LANES], kg_ref[...], cos, sin, hs_ref, sw_ref)
        o0 = A_WIDTH + 2 * A_KV_WIDTH
        u = z_ref[:, o0 + CONV_DIM:o0 + 2 * CONV_DIM] * z_ref[:, o0 + 2 * CONV_DIM:o0 + 3 * CONV_DIM]
        y = cs_ref[:, 0:CONV_DIM] * cw_ref[0:1, :]
        y = y + cs_ref[:, CONV_DIM:2 * CONV_DIM] * cw_ref[1:2, :]
        y = y + u * cw_ref[2:3, :]
        m_ref[:, A_WIDTH:A_WIDTH + CONV_DIM] = z_ref[:, o0:o0 + CONV_DIM] * y
        m_ref[:, 0:A_WIDTH] = jnp.zeros((DEC_BATCH, A_WIDTH), F32)
        ncs_ref[:, 0:CONV_DIM] = cs_ref[:, CONV_DIM:2 * CONV_DIM]
        ncs_ref[:, CONV_DIM:2 * CONV_DIM] = u

    grp = pl.ds(pl.multiple_of((b // 8) * 8, 8), 8)
    mine = lax.broadcasted_iota(jnp.int32, (8, 1), 0) == b % 8

    def pick(tile):
        return jnp.sum(jnp.where(mine, tile, 0.0), axis=0, keepdims=True)

    kn = pick(k_all[grp, :])
    vn = pick(z_ref[grp, A_WIDTH + A_KV_WIDTH:A_WIDTH + 2 * A_KV_WIDTH])
    row = lax.broadcasted_iota(jnp.int32, (lc, A_KV_WIDTH), 0)
    kfull = jnp.concatenate([kc_ref[...], jnp.where(row == 0, kn, 0.0)], axis=0)
    vfull = jnp.concatenate([vc_ref[...], jnp.where(row == 0, vn, 0.0)], axis=0)
    qs = [jnp.broadcast_to(pick(q_all[grp, c * LANES:(c + 1) * LANES]), (SROWS, LANES)).astype(BF16)
          for c in range(A_WIDTH // LANES)]
    kk = lax.broadcasted_iota(jnp.int32, (2 * lc, SROWS), 0)
    mask_t = (kk <= lc) & (lc - kk <= WINDOW)
    outs = _attn_core(qs, kfull, vfull, mask_t, sink_ref)
    for c in range(A_WIDTH // LANES):
        cols = slice(c * LANES, (c + 1) * LANES)
        m_ref[grp, cols] = jnp.where(mine, outs[c][0:8], m_ref[grp, cols])
    nk_ref[0:lc - 1, :] = kc_ref[1:lc, :]
    nk_ref[lc - 1:lc, :] = kn
    nv_ref[0:lc - 1, :] = vc_ref[1:lc, :]
    nv_ref[lc - 1:lc, :] = vn


def _even_sample(sinks, z_s, cos, sin, qg, kg, conv_state, cw, mats, k_cache, v_cache, li, prev):
    lc = k_cache.shape[2]
    whole = lambda a: pl.BlockSpec(a.shape, lambda b: (0,) * a.ndim)
    cache = pl.BlockSpec((None, None, lc, A_KV_WIDTH), lambda b: (li, b, 0, 0))
    cs = conv_state.reshape(DEC_BATCH, (CONV_W - 1) * CONV_DIM)
    args = [sinks, z_s, cos, sin, qg, kg, cs, cw, *mats, k_cache, v_cache]
    assert len(args) == N_SAMPLE_IN
    in_specs = [pl.BlockSpec(memory_space=pltpu.SMEM)] + [whole(a) for a in args[1:N_SAMPLE_IN - 2]] + [cache, cache]
    aliases = {}
    if prev is not None:
        aliases = {len(args): 2, len(args) + 1: 3}
        in_specs += [pl.BlockSpec(memory_space=pl.ANY)] * 2
        args += list(prev)
    rows = lambda w: pl.BlockSpec((DEC_BATCH, w), lambda b: (0, 0))
    m_s, ncs, nk, nv = pl.pallas_call(
        _even_sample_kernel,
        grid=(DEC_BATCH,),
        in_specs=in_specs,
        out_specs=[rows(D_MODEL), rows((CONV_W - 1) * CONV_DIM), cache, cache],
        out_shape=[jax.ShapeDtypeStruct((DEC_BATCH, D_MODEL), F32),
                   jax.ShapeDtypeStruct((DEC_BATCH, (CONV_W - 1) * CONV_DIM), F32),
                   jax.ShapeDtypeStruct(k_cache.shape, F32),
                   jax.ShapeDtypeStruct(v_cache.shape, F32)],
        scratch_shapes=[pltpu.VMEM((DEC_BATCH, A_WIDTH), F32), pltpu.VMEM((DEC_BATCH, A_KV_WIDTH), F32)],
        input_output_aliases=aliases,
        compiler_params=_cparams("arbitrary"),
        name="even_sample",
    )(*args)
    return m_s, ncs.reshape(DEC_BATCH, CONV_W - 1, CONV_DIM), nk, nv


def _log_decay(zr, wa_ref, ba_ref):
    pre = _dot(zr.astype(BF16), wa_ref[...]) + ba_ref[...]
    return (jnp.minimum(pre, 0.0) - jnp.log1p(jnp.exp(-jnp.abs(pre)))) * (1.0 / GLA_TAU)


def _split_bf16(x):
    hi = x.astype(BF16)
    lo = (x - hi.astype(F32)).astype(BF16)
    return hi, lo


def _row_to_col(row):
    n = row.shape[1]
    r = lax.broadcasted_iota(jnp.int32, (n, n), 0)
    c = lax.broadcasted_iota(jnp.int32, (n, n), 1)
    return jnp.sum(jnp.where(r == c, jnp.broadcast_to(row, (n, n)), 0.0), axis=-1, keepdims=True)


def _head_rmsnorm_gate(o, gain, zg):
    ms = jnp.mean(o * o, axis=-1, keepdims=True)
    return ((o * lax.rsqrt(ms + EPS)) * gain) * _silu(zg)


def _odd_prompt_kernel(zm_ref, zpr_ref, wa_ref, ba_ref, gain_ref, wp_ref, ps_ref, tri_ref, band_ref, wo_ref,
                       o_ref, s_ref, np_ref, wo_bf_ref, pbh_ref, pbl_ref):
    i = pl.program_id(1)
    wo_bf_ref[...] = wo_ref[...].astype(BF16)

    @pl.when(i == 0)
    def _():
        s_ref[...] = jnp.zeros_like(s_ref)
        pbh_ref[0:BLK, :] = jnp.zeros((BLK, POOL_DIM), BF16)
        pbl_ref[0:BLK, :] = jnp.zeros((BLK, POOL_DIM), BF16)

    x = None
    for t in range(MIX_SUB):
        rows = pl.ds(t * BLK, BLK)
        x = _odd_block(zm_ref.at[rows], zpr_ref.at[rows], wa_ref, ba_ref, gain_ref, wp_ref, ps_ref,
                       tri_ref, band_ref, o_ref.at[rows], s_ref, pbh_ref, pbl_ref, (i * MIX_SUB + t) * BLK)

    @pl.when(i == pl.num_programs(1) - 1)
    def _():
        np_ref[...] = x[BLK - POOL_BUF:BLK, :]


def _odd_block(zm_ref, zpr_ref, wa_ref, ba_ref, gain_ref, wp_ref, ps_ref, tri_ref, band_ref,
               o_ref, s_ref, pbh_ref, pbl_ref, t0):
    n_ch = BLK // GLA_CHUNK
    gk = _log_decay(zpr_ref[:, POOL_DIM:POOL_RANK_W], wa_ref, ba_ref)
    r = lax.broadcasted_iota(jnp.int32, (BLK, BLK), 0)
    c = lax.broadcasted_iota(jnp.int32, (BLK, BLK), 1)
    causal = (r >= c) & (r // GLA_CHUNK == c // GLA_CHUNK)
    g_hi, g_lo = _split_bf16(gk)
    b = _dot(tri_ref[...], jnp.concatenate([g_hi, g_lo], axis=0))
    b_last = [b[(ch + 1) * GLA_CHUNK - 1:(ch + 1) * GLA_CHUNK, :] for ch in range(n_ch)]
    b_end = jnp.concatenate([jnp.broadcast_to(bl, (GLA_CHUNK, GLA_K_WIDTH)) for bl in b_last], axis=0)
    o_k, o_v, o_g = GLA_K_WIDTH, 2 * GLA_K_WIDTH, 2 * GLA_K_WIDTH + GLA_WIDTH
    zq = zm_ref[:, 0:GLA_K_WIDTH]
    zk = zm_ref[:, o_k:o_k + GLA_K_WIDTH]
    qd = ((zq * (GLA_DK ** -0.5)) * jnp.exp(b)).astype(BF16)
    kd = (zk * jnp.exp(-b)).astype(BF16)
    k2 = (zk * jnp.exp(b_end - b)).astype(BF16)
    sub = lax.broadcasted_iota(jnp.int32, (8, LANES), 0)
    sel = jnp.where(sub < 2, 1.0, 0.0).astype(BF16)
    hk = [slice(h * GLA_DK, (h + 1) * GLA_DK) for h in range(GLA_HEADS)]
    chunk = [slice(ch * GLA_CHUNK, (ch + 1) * GLA_CHUNK) for ch in range(n_ch)]
    v_hs = [zm_ref[:, o_v + h * GLA_DV:o_v + (h + 1) * GLA_DV].astype(BF16) for h in range(GLA_HEADS)]
    atts = [_dot_nt(qd[:, hk[h]], kd[:, hk[h]]) for h in range(GLA_HEADS)]
    log_decs = []
    for h in range(GLA_HEADS):
        for ch in range(n_ch):
            bl = b_last[ch][:, hk[h]]
            bl_hi = bl.astype(BF16).astype(F32)
            rows8 = jnp.where(sub == 0, bl_hi, jnp.where(sub == 1, bl - bl_hi, 0.0)).astype(BF16)
            log_decs.append(_dot_tn(rows8, sel))
    deltas = [[_dot_tn(k2[chunk[ch], hk[h]], v_hs[h][chunk[ch]]) for ch in range(n_ch)]
              for h in range(GLA_HEADS)]
    o_intras = [_dot(jnp.where(causal, atts[h], 0.0).astype(BF16), v_hs[h]) for h in range(GLA_HEADS)]
    o_inters = []
    for h in range(GLA_HEADS):
        s_h = s_ref[h]
        parts = []
        for ch in range(n_ch):
            parts.append(_dot(qd[chunk[ch], hk[h]], s_h.astype(BF16)))
            dec = jnp.exp(log_decs[h * n_ch + ch])
            s_h = jnp.concatenate([dec, dec], axis=1) * s_h + deltas[h][ch]
        s_ref[h] = s_h
        o_inters.append(jnp.concatenate(parts, axis=0))
    for h in range(GLA_HEADS):
        vs = slice(h * GLA_DV, (h + 1) * GLA_DV)
        zg = zm_ref[:, o_g + h * GLA_DV:o_g + (h + 1) * GLA_DV]
        o_ref[:, vs] = _head_rmsnorm_gate(o_intras[h] + o_inters[h], gain_ref[...], zg).astype(o_ref.dtype)

    x = zpr_ref[:, 0:POOL_DIM]
    x_hi, x_lo = _split_bf16(x)
    pbh_ref[BLK:2 * BLK, :] = x_hi
    pbl_ref[BLK:2 * BLK, :] = x_lo
    t_glob = t0 + lax.broadcasted_iota(jnp.int32, (BLK, 1), 0)
    groups = [slice(g * POOL_GDIM, (g + 1) * POOL_GDIM) for g in range(len(POOL_WINDOWS))]
    sums = [_dot(band_ref[g], jnp.concatenate([pbh_ref[:, groups[g]], pbl_ref[:, groups[g]]], axis=0))
            for g in range(len(POOL_WINDOWS))]
    ys = []
    for g, w in enumerate(POOL_WINDOWS):
        inv_cnt = 1.0 / jnp.minimum(w, t_glob + 1).astype(F32)
        d = sums[g] * inv_cnt - x[:, groups[g]]
        ys.append(_dot(d.astype(BF16), wp_ref[g]))
    for g in range(len(POOL_WINDOWS)):
        o_ref[:, GLA_WIDTH + g * POOL_GDIM:GLA_WIDTH + (g + 1) * POOL_GDIM] = (
            ys[g] * ps_ref[:, groups[g]]).astype(o_ref.dtype)
    pbh_ref[0:BLK, :] = x_hi
    pbl_ref[0:BLK, :] = x_lo
    return x


def _odd_block_mats():
    r = jnp.arange(BLK)[:, None]
    c = jnp.arange(BLK)[None, :]
    tri = ((r >= c) & (r // GLA_CHUNK == c // GLA_CHUNK)).astype(BF16)
    back = r + BLK - jnp.arange(2 * BLK)[None, :]
    bands = [((back >= 0) & (back < w)).astype(BF16) for w in POOL_WINDOWS]
    return (jnp.concatenate([tri, tri], axis=1),
            jnp.stack([jnp.concatenate([bd, bd], axis=1) for bd in bands]))


def _odd_prompt(zm_p, zpr_p, wa, ba, gain, wp, ps, w_out, layer):
    n_step = NBLK // MIX_SUB
    step_rows = MIX_SUB * BLK
    blk = lambda b, i: (b * n_step + i, 0)
    fixed2 = lambda b, i: (0, 0)
    wo_in, wo_out, wo_shape = _w_slab_specs(w_out, layer, n_step)
    tri2, bands2 = _odd_block_mats()
    return pl.pallas_call(
        _odd_prompt_kernel,
        grid=(BATCH, n_step),
        in_specs=[pl.BlockSpec((step_rows, ODD_MAIN), blk),
                  pl.BlockSpec((step_rows, POOL_RANK_W), blk),
                  pl.BlockSpec((LANES, GLA_K_WIDTH), fixed2),
                  pl.BlockSpec((1, GLA_K_WIDTH), fixed2),
                  pl.BlockSpec((1, GLA_DV), fixed2),
                  pl.BlockSpec((4, POOL_GDIM, POOL_GDIM), lambda b, i: (0, 0, 0)),
                  pl.BlockSpec((1, POOL_DIM), fixed2),
                  pl.BlockSpec(tri2.shape, fixed2),
                  pl.BlockSpec(bands2.shape, lambda b, i: (0, 0, 0)),
                  wo_in],
        out_specs=[pl.BlockSpec((step_rows, D_MODEL), blk),
                   pl.BlockSpec((None, GLA_HEADS, GLA_DK, GLA_DV), lambda b, i: (b, 0, 0, 0)),
                   pl.BlockSpec((None, POOL_BUF, POOL_DIM), lambda b, i: (b, 0, 0)),
                   wo_out],
        out_shape=[jax.ShapeDtypeStruct((M_PROMPT, D_MODEL), BF16),
                   jax.ShapeDtypeStruct((BATCH, GLA_HEADS, GLA_DK, GLA_DV), F32),
                   jax.ShapeDtypeStruct((BATCH, POOL_BUF, POOL_DIM), F32),
                   wo_shape],
        scratch_shapes=[pltpu.VMEM((2 * BLK, POOL_DIM), BF16), pltpu.VMEM((2 * BLK, POOL_DIM), BF16)],
        compiler_params=_cparams("arbitrary", "arbitrary"),
        name="odd_prompt",
    )(zm_p, zpr_p, wa, ba, gain, wp, ps, tri2, bands2, w_out)


def _odd_sample_prep_kernel(zm_ref, zr_ref, wa_ref, ba_ref, dec_ref, q_ref):
    dec_ref[...] = jnp.exp(_log_decay(zr_ref[...], wa_ref, ba_ref))
    q_ref[...] = zm_ref[:, 0:GLA_K_WIDTH] * (GLA_DK ** -0.5)


def _odd_sample_prep(zm_s, zr_s, wa, ba):
    return pl.pallas_call(
        _odd_sample_prep_kernel,
        out_shape=[jax.ShapeDtypeStruct((DEC_BATCH, GLA_K_WIDTH), F32),
                   jax.ShapeDtypeStruct((DEC_BATCH, GLA_K_WIDTH), F32)],
        compiler_params=pltpu.CompilerParams(vmem_limit_bytes=VMEM_LIMIT),
        name="odd_sample_prep",
    )(zm_s, zr_s, wa, ba)


STATE_BB = 4


def _odd_sample_state_kernel(*refs):
    dec_ref, q_ref, k_ref, v_ref, s_ref = refs[:5]
    ns_ref, o_ref = refs[-2:]
    for bb in range(STATE_BB):
        for h in range(GLA_HEADS):
            ks = slice(h * GLA_DK, (h + 1) * GLA_DK)
            vs = slice(h * GLA_DV, (h + 1) * GLA_DV)
            s_new = (_row_to_col(dec_ref[bb][:, ks]) * s_ref[bb, h]
                     + _row_to_col(k_ref[bb][:, ks]) * v_ref[bb][:, vs])
            ns_ref[bb, h] = s_new
            o_ref[bb, :, vs] = jnp.sum(_row_to_col(q_ref[bb][:, ks]) * s_new, axis=0, keepdims=True)


def _odd_sample_state(dec, q, k, v, state, li, prev):
    vec = lambda w: pl.BlockSpec((STATE_BB, 1, w), lambda b: (b, 0, 0))
    st = pl.BlockSpec((None, STATE_BB, GLA_HEADS, GLA_DK, GLA_DV), lambda b: (li, b, 0, 0, 0))
    r3 = lambda a: a.reshape(DEC_BATCH, 1, a.shape[-1])
    in_specs = [vec(GLA_K_WIDTH), vec(GLA_K_WIDTH), vec(GLA_K_WIDTH), vec(GLA_WIDTH), st]
    args = [r3(dec), r3(q), r3(k), r3(v), state]
    aliases = {}
    if prev is not None:
        aliases = {len(args): 0}
        in_specs.append(pl.BlockSpec(memory_space=pl.ANY))
        args.append(prev)
    return pl.pallas_call(
        _odd_sample_state_kernel,
        grid=(DEC_BATCH // STATE_BB,),
        in_specs=in_specs,
        out_specs=[st, vec(GLA_WIDTH)],
        out_shape=[jax.ShapeDtypeStruct(state.shape, F32),
                   jax.ShapeDtypeStruct((DEC_BATCH, 1, GLA_WIDTH), F32)],
        input_output_aliases=aliases,
        compiler_params=_cparams("arbitrary"),
        name="odd_sample_state",
    )(*args)


def _odd_sample_post_kernel(o_in_ref, zm_ref, zp_ref, hist_ref, gain_ref, wp_ref, ps_ref, o_ref):
    o_g = 2 * GLA_K_WIDTH + GLA_WIDTH
    for h in range(GLA_HEADS):
        vs = slice(h * GLA_DV, (h + 1) * GLA_DV)
        zg = zm_ref[:, o_g + h * GLA_DV:o_g + (h + 1) * GLA_DV]
        o_ref[:, vs] = _head_rmsnorm_gate(o_in_ref[:, vs], gain_ref[...], zg).astype(o_ref.dtype)
    n_prev = hist_ref.shape[0]
    for g, w in enumerate(POOL_WINDOWS):
        gs = slice(g * POOL_GDIM, (g + 1) * POOL_GDIM)
        x = zp_ref[:, gs]
        s = hist_ref[n_prev - (w - 1)][:, gs]
        for jj in range(n_prev - (w - 1) + 1, n_prev):
            s = s + hist_ref[jj][:, gs]
        s = s + x
        d = s / float(min(w, n_prev + 1)) - x
        y = _dot(d.astype(BF16), wp_ref[g]) * ps_ref[:, gs]
        o_ref[:, GLA_WIDTH + g * POOL_GDIM:GLA_WIDTH + (g + 1) * POOL_GDIM] = y.astype(o_ref.dtype)


def _odd_sample_post(o_raw, zm_s, zp_s, hist_t, gain, wp, ps):
    return pl.pallas_call(
        _odd_sample_post_kernel,
        out_shape=jax.ShapeDtypeStruct((DEC_BATCH, D_MODEL), BF16),
        compiler_params=pltpu.CompilerParams(vmem_limit_bytes=VMEM_LIMIT),
        name="odd_sample_post",
    )(o_raw, zm_s, zp_s, hist_t, gain, wp, ps)


def _rope_tables(pos):
    half = HEAD_DIM // 2
    inv = jnp.power(ROPE_THETA, -jnp.arange(half, dtype=F32) / half)
    ang = pos.astype(F32)[:, None] * inv[None, :]
    c, s = jnp.cos(ang), jnp.sin(ang)
    return jnp.tile(c, (1, 4)), jnp.tile(jnp.concatenate([-s, s], axis=1), (1, 2))


def kernel(x_prompt, x_sample, cache_swa_k, cache_swa_v, state_conv, state_gla, state_pool, norm_mix, norm_ffn, w_in_even, w_out_even, q_norm, k_norm, attn_sinks, conv_w, w_in_odd, w_out_odd, w_alpha_up, b_alpha, gla_out_norm, w_pool, pool_scale, w_gate, w_up, w_down):
    lc = cache_swa_k.shape[2]
    x_p = x_prompt.reshape(M_PROMPT, D_MODEL)
    x_s = x_sample.reshape(DEC_BATCH, D_MODEL)
    cos_p, sin_p = _rope_tables(jnp.arange(SEQ))
    cos_s, sin_s = _rope_tables(PAST_LEN + jnp.arange(1))
    mats = _rope_mats()
    kc_all = cache_swa_k.reshape(N_EVEN, DEC_BATCH, lc, A_KV_WIDTH)
    vc_all = cache_swa_v.reshape(N_EVEN, DEC_BATCH, lc, A_KV_WIDTH)
    o_r = ODD_MAIN
    w_odd_t = jnp.swapaxes(w_in_odd, 1, 2)
    w_odd_pr_t = jnp.concatenate([w_odd_t[:, o_r + GLA_RANK:], w_odd_t[:, o_r:o_r + GLA_RANK],
                                  jnp.zeros((N_ODD, LANES - GLA_RANK, D_MODEL), F32)], axis=1)
    pk, pv, pc, pg, pp, sc, sp = ([] for _ in range(7))
    sk_all = sv_all = sg_all = None

    h_p, h_s = _rmsnorm(x_p, x_s, norm_mix[0])
    for layer in range(DEPTH):
        li = layer // 2
        if layer % 2 == 0:
            ie_tm, ie_tn = ((1024, 1536), (2048, 768))[li]
            z_p, z_s = _dense(h_p, h_s, [(w_in_even, li)], n_cols=EVEN_IN, tm=ie_tm, tn=ie_tn, name="in_even")
            qg = jnp.tile(q_norm[li], 2).reshape(1, LANES)
            kg = jnp.tile(k_norm[li], 2).reshape(1, LANES)
            m_p, nk, nv, nc, w_out_bf = _even_prompt(z_p, cos_p, sin_p, qg, kg, conv_w[li], attn_sinks[li], mats,
                                                     w_out_even, li)
            pk.append(nk.reshape(BATCH, lc, A_KV_HEADS, HEAD_DIM))
            pv.append(nv.reshape(BATCH, lc, A_KV_HEADS, HEAD_DIM))
            pc.append(nc)
            m_s, nc_s, sk_all, sv_all = _even_sample(
                attn_sinks[li], z_s, cos_s, sin_s, qg, kg, state_conv[li], conv_w[li], mats, kc_all, vc_all, li,
                None if sk_all is None else (sk_all, sv_all))
            sc.append(nc_s)
        else:
            io_tm, io_tn = ((1024, 1536), (2048, 768))[li]
            zm_p, zm_s = _dense(h_p, h_s, [(w_odd_t, li)], n_cols=ODD_MAIN, tm=io_tm, tn=io_tn,
                                transposed=True, name="in_odd_main")
            zpr_p, zpr_s = _dense(h_p, h_s, [(w_odd_pr_t, li)], n_cols=POOL_RANK_W, tm=1024, tn=POOL_RANK_W,
                                  transposed=True, name="in_odd_pool_rank")
            zp_s, zr_s = zpr_s[:, :POOL_DIM], zpr_s[:, POOL_DIM:]
            wa = jnp.pad(w_alpha_up[li], ((0, LANES - GLA_RANK), (0, 0))).astype(BF16)
            ba = b_alpha[li].reshape(1, GLA_K_WIDTH)
            gain = gla_out_norm[li].reshape(1, GLA_DV)
            wp = w_pool[li].astype(BF16)
            ps = pool_scale[li].reshape(1, POOL_DIM)
            m_p, ng, npool, w_out_bf = _odd_prompt(zm_p, zpr_p, wa, ba, gain, wp, ps, w_out_odd, li)
            pg.append(ng)
            pp.append(npool)
            dec_s, q_s = _odd_sample_prep(zm_s, zr_s, wa, ba)
            sg_all, o_raw = _odd_sample_state(
                dec_s, q_s, zm_s[:, GLA_K_WIDTH:2 * GLA_K_WIDTH],
                zm_s[:, 2 * GLA_K_WIDTH:2 * GLA_K_WIDTH + GLA_WIDTH], state_gla, li, sg_all)
            m_s = _odd_sample_post(o_raw.reshape(DEC_BATCH, GLA_WIDTH), zm_s, zp_s,
                                   jnp.swapaxes(state_pool[li], 0, 1), gain, wp, ps)
            sp.append(jnp.concatenate([state_pool[li, :, 1:], zp_s[:, None, :]], axis=1))
        x_p, x_s, h_p, h_s = _proj_res(m_p, m_s, (w_out_bf, None), (x_p, x_s), norm_ffn[layer],
                                       tm=(512, 256, 512, 256)[layer], name="out_proj")
        a_p, a_s, w_down_bf = _dense(h_p, h_s, [(w_gate, layer), (w_up, layer)], n_cols=D_FF, tm=2048, tn=512,
                                     mode="swiglu", out_dtype=BF16, side_cast=(w_down, layer), sub=2,
                                     name="ffn_up")
        if layer + 1 < DEPTH:
            x_p, x_s, h_p, h_s = _proj_res(a_p, a_s, (w_down_bf, None), (x_p, x_s),
                                           norm_mix[layer + 1], tm=512, name="ffn_down")
        else:
            x_p, x_s = _proj_res(a_p, a_s, (w_down_bf, None), (x_p, x_s), None,
                                 tm=512, name="ffn_down_last")

    st = lambda parts: jnp.stack(parts)
    cache5 = lambda a: a.reshape(N_EVEN, DEC_BATCH, lc, A_KV_HEADS, HEAD_DIM)
    return (x_p.reshape(BATCH, SEQ, D_MODEL), x_s.reshape(DEC_BATCH, 1, D_MODEL),
            st(pk), st(pv), st(pc), st(pg), st(pp), cache5(sk_all), cache5(sv_all), st(sc), sg_all, st(sp))
```

```python
import functools

import jax
import jax.numpy as jnp
from jax import lax
from jax.experimental import pallas as pl
from jax.experimental.pallas import tpu as pltpu

F32 = jnp.float32
BF16 = jnp.bfloat16

D_MODEL = 2048
BATCH = 4
SEQ = 2048
DEPTH = 4
DEC_BATCH = 32
PAST_LEN = 16384
N_EVEN = 2
N_ODD = 2
EPS = 1e-6
NEG_INF = -1e30
A_HEADS = 16
A_KV_HEADS = 4
HEAD_DIM = 64
A_WIDTH = A_HEADS * HEAD_DIM
A_KV_WIDTH = A_KV_HEADS * HEAD_DIM
WINDOW = 128
ROPE_THETA = 10000.0
CONV_DIM = D_MODEL // 2
CONV_W = 3
GLA_HEADS = 4
GLA_WIDTH = D_MODEL // 2
GLA_DV = GLA_WIDTH // GLA_HEADS
GLA_DK = GLA_DV // 2
GLA_K_WIDTH = GLA_HEADS * GLA_DK
GLA_RANK = 16
GLA_TAU = 16.0
GLA_CHUNK = 64
POOL_DIM = D_MODEL // 2
POOL_WINDOWS = (2, 4, 8, 16)
POOL_GDIM = POOL_DIM // 4
POOL_BUF = 15
D_FF = 5632
EVEN_IN = A_WIDTH + 2 * A_KV_WIDTH + 3 * CONV_DIM
ODD_MAIN = 2 * GLA_K_WIDTH + 2 * GLA_WIDTH

M_PROMPT = BATCH * SEQ
LANES = 128
BLK = 128
NBLK = SEQ // BLK
MIX_SUB = 4
POOL_RANK_W = POOL_DIM + LANES
VMEM_LIMIT = 58 * 1024 * 1024

TILES = {
    "in_even": dict(tm=1024, tn=1536),
    "in_odd_main": dict(tm=1024, tn=ODD_MAIN // 2),
    "in_odd_pool_rank": dict(tm=1024, tn=POOL_RANK_W),
    "ffn_up": dict(tm=2048, tn=512, sub=2),
    "out_proj": dict(tm=512),
    "ffn_down": dict(tm=512),
}


def _cparams(*sem):
    return pltpu.CompilerParams(dimension_semantics=sem, vmem_limit_bytes=VMEM_LIMIT)


def _dot(a, b):
    return jnp.dot(a, b, preferred_element_type=F32)


def _dot_nt(a, b):
    return lax.dot_general(a, b, (((1,), (1,)), ((), ())), preferred_element_type=F32)


def _dot_tn(a, b):
    return lax.dot_general(a, b, (((0,), (0,)), ((), ())), preferred_element_type=F32)


def _silu(x):
    return x * (1.0 / (1.0 + jnp.exp(-x)))


def _rmsnorm_rows(x, gain):
    ms = jnp.mean(x * x, axis=-1, keepdims=True)
    return (x * lax.rsqrt(ms + EPS)) * gain


def _rmsnorm_kernel(xp_ref, xs_ref, g_ref, op_ref, os_ref, *, n_i):
    op_ref[...] = _rmsnorm_rows(xp_ref[...], g_ref[...]).astype(op_ref.dtype)

    @pl.when(pl.program_id(0) == n_i - 1)
    def _():
        os_ref[...] = _rmsnorm_rows(xs_ref[...], g_ref[...]).astype(os_ref.dtype)


def _rmsnorm(x_p, x_s, gain, *, tm=1024):
    n_i = M_PROMPT // tm
    row = lambda i: (i, 0)
    fixed = lambda i: (0, 0)
    return pl.pallas_call(
        functools.partial(_rmsnorm_kernel, n_i=n_i),
        grid=(n_i,),
        in_specs=[pl.BlockSpec((tm, D_MODEL), row),
                  pl.BlockSpec((DEC_BATCH, D_MODEL), fixed),
                  pl.BlockSpec((1, D_MODEL), fixed)],
        out_specs=[pl.BlockSpec((tm, D_MODEL), row),
                   pl.BlockSpec((DEC_BATCH, D_MODEL), fixed)],
        out_shape=[jax.ShapeDtypeStruct((M_PROMPT, D_MODEL), BF16),
                   jax.ShapeDtypeStruct((DEC_BATCH, D_MODEL), BF16)],
        compiler_params=_cparams("arbitrary"),
        name="rmsnorm",
    )(x_p, x_s, gain.reshape(1, D_MODEL))


XPOSE_COLS = 128


def _dense_kernel(*refs, n_w, mode, n_i, transposed, side_cast, sub):
    a_p, a_s = refs[0], refs[1]
    w = refs[2:2 + n_w]
    pos = 2 + n_w
    side_in = side_out = None
    if side_cast:
        side_in = refs[pos]
        pos += 1
    o_p, o_s = refs[pos], refs[pos + 1]
    pos += 2
    if side_cast:
        side_out = refs[pos]
        pos += 1
    wbf = refs[pos:pos + n_w]
    i = pl.program_id(1)

    @pl.when(i == 0)
    def _():
        for k in range(n_w):
            if transposed:
                tn = wbf[k].shape[1]
                for c in range(0, tn, XPOSE_COLS):
                    wbf[k][:, c:c + XPOSE_COLS] = w[k][c:c + XPOSE_COLS, :].T.astype(BF16)
            else:
                wbf[k][...] = w[k][...].astype(BF16)
        if side_cast:
            side_out[...] = side_in[...].astype(BF16)

    def run(a_ref, o_ref, n_sub):
        rows = a_ref.shape[0] // n_sub
        for r in range(n_sub):
            a = a_ref[r * rows:(r + 1) * rows, :]
            if mode == "swiglu":
                y = _silu(_dot(a, wbf[0][...])) * _dot(a, wbf[1][...])
            else:
                y = _dot(a, wbf[0][...])
            o_ref[r * rows:(r + 1) * rows, :] = y.astype(o_ref.dtype)

    run(a_p, o_p, sub)

    @pl.when(i == n_i - 1)
    def _():
        run(a_s, o_s, 1)


def _dense(a_p, a_s, weights, *, n_cols, tm, tn, mode="plain", out_dtype=F32, transposed=False,
           side_cast=None, sub=1, name):
    k_dim = a_p.shape[1]
    n_i = M_PROMPT // tm
    n_j = n_cols // tn
    row = lambda j, i: (i, 0)
    fixed = lambda j, i: (0, 0)
    tile = lambda j, i: (i, j)
    panel = lambda j, i: (0, j)
    in_specs = [pl.BlockSpec((tm, k_dim), row), pl.BlockSpec((DEC_BATCH, k_dim), fixed)]
    args = [a_p, a_s]
    for arr, layer in weights:
        if transposed:
            in_specs.append(pl.BlockSpec((None, tn, k_dim), lambda j, i, layer=layer: (layer, j, 0)))
        else:
            in_specs.append(pl.BlockSpec((None, k_dim, tn), lambda j, i, layer=layer: (layer, 0, j)))
        args.append(arr)
    out_specs = [pl.BlockSpec((tm, tn), tile), pl.BlockSpec((DEC_BATCH, tn), panel)]
    out_shape = [jax.ShapeDtypeStruct((M_PROMPT, n_cols), out_dtype),
                 jax.ShapeDtypeStruct((DEC_BATCH, n_cols), out_dtype)]
    if side_cast is not None:
        s_arr, s_layer = side_cast
        slab = s_arr.shape[1] // n_j
        in_specs.append(pl.BlockSpec((None, slab, s_arr.shape[2]), lambda j, i: (s_layer, j, 0)))
        args.append(s_arr)
        out_specs.append(pl.BlockSpec((slab, s_arr.shape[2]), lambda j, i: (j, 0)))
        out_shape.append(jax.ShapeDtypeStruct(s_arr.shape[1:], BF16))
    n_w = len(weights)
    return pl.pallas_call(
        functools.partial(_dense_kernel, n_w=n_w, mode=mode, n_i=n_i, transposed=transposed,
                          side_cast=side_cast is not None, sub=sub),
        grid=(n_j, n_i),
        in_specs=in_specs,
        out_specs=out_specs,
        out_shape=out_shape,
        scratch_shapes=[pltpu.VMEM((k_dim, tn), BF16) for _ in range(n_w)],
        compiler_params=_cparams("arbitrary", "arbitrary"),
        name=name,
    )(*args)


def _proj_res_kernel(*refs, n_i, with_norm):
    a_p, a_s, w_ref, r_p, r_s = refs[:5]
    if with_norm:
        g_ref, x_p, x_s, h_p, h_s = refs[5:]
    else:
        x_p, x_s = refs[5:]
        g_ref = h_p = h_s = None

    def run(a_ref, r_ref, x_ref, h_ref):
        a = a_ref[...]
        if a.dtype != BF16:
            a = a.astype(BF16)
        x = r_ref[...] + _dot(a, w_ref[...])
        x_ref[...] = x
        if with_norm:
            h_ref[...] = _rmsnorm_rows(x, g_ref[...]).astype(h_ref.dtype)

    run(a_p, r_p, x_p, h_p)

    @pl.when(pl.program_id(0) == n_i - 1)
    def _():
        run(a_s, r_s, x_s, h_s)


def _proj_res(a_p, a_s, weight, res, gain, *, tm, name):
    w_bf, layer = weight
    k_dim = a_p.shape[1]
    if layer is None:
        w_spec = pl.BlockSpec((k_dim, D_MODEL), lambda i: (0, 0), pipeline_mode=pl.Buffered(1))
    else:
        w_spec = pl.BlockSpec((None, k_dim, D_MODEL), lambda i: (layer, 0, 0), pipeline_mode=pl.Buffered(1))
    n_i = M_PROMPT // tm
    with_norm = gain is not None
    row = lambda i: (i, 0)
    fixed = lambda i: (0, 0)
    rows_p = pl.BlockSpec((tm, D_MODEL), row)
    rows_s = pl.BlockSpec((DEC_BATCH, D_MODEL), fixed)
    in_specs = [pl.BlockSpec((tm, k_dim), row), pl.BlockSpec((DEC_BATCH, k_dim), fixed),
                w_spec, rows_p, rows_s]
    args = [a_p, a_s, w_bf, res[0], res[1]]
    out_specs = [rows_p, rows_s]
    out_shape = [jax.ShapeDtypeStruct((M_PROMPT, D_MODEL), F32), jax.ShapeDtypeStruct((DEC_BATCH, D_MODEL), F32)]
    if with_norm:
        in_specs.append(pl.BlockSpec((1, D_MODEL), fixed))
        args.append(gain.reshape(1, D_MODEL))
        out_specs += [rows_p, rows_s]
        out_shape += [jax.ShapeDtypeStruct((M_PROMPT, D_MODEL), BF16),
                      jax.ShapeDtypeStruct((DEC_BATCH, D_MODEL), BF16)]
    return pl.pallas_call(
        functools.partial(_proj_res_kernel, n_i=n_i, with_norm=with_norm),
        grid=(n_i,),
        in_specs=in_specs,
        out_specs=out_specs,
        out_shape=out_shape,
        compiler_params=_cparams("arbitrary"),
        name=name,
    )(*args)


def _split_cat(x):
    hi = x.astype(BF16)
    lo = (x - hi.astype(F32)).astype(BF16)
    return jnp.concatenate([hi, lo], axis=1)


def _rope_mats():
    j = jnp.arange(2 * LANES)[:, None] % LANES
    l = jnp.arange(LANES)[None, :]
    head_sum = (j // HEAD_DIM == l // HEAD_DIM).astype(BF16)
    src = jnp.where(l % HEAD_DIM < HEAD_DIM // 2, l + HEAD_DIM // 2, l - HEAD_DIM // 2)
    half_swap = (j == src).astype(BF16)
    return head_sum, half_swap


def _norm_rope_chunk(xc, gain, cos, sin, hs_ref, sw_ref):
    ms = _dot(_split_cat(xc * xc), hs_ref[...]) * (1.0 / HEAD_DIM)
    y = (xc * lax.rsqrt(ms + EPS)) * gain
    swapped = _dot(_split_cat(y), sw_ref[...])
    return y * cos + swapped * sin


def _spread_heads(xc, own_lo):
    lane = lax.broadcasted_iota(jnp.int32, xc.shape, 1)
    keep = (lane < HEAD_DIM) if own_lo else (lane >= HEAD_DIM)
    nat = jnp.where(keep, xc, 0.0)
    rol = pltpu.roll(nat, HEAD_DIM, axis=1)
    parts = (nat, rol) if own_lo else (rol, nat)
    return jnp.concatenate(parts, axis=0).astype(BF16)


def _attn_core(qs, kfull, vfull, mask_t, sink_ref):
    rows = qs[0].shape[0]
    nkeys = kfull.shape[0]
    outs = [None] * 8
    for kh in range(A_KV_HEADS):
        c0 = (kh // 2) * LANES
        kk = _spread_heads(kfull[:, c0:c0 + LANES], kh % 2 == 0)
        vv = _spread_heads(vfull[:, c0:c0 + LANES], kh % 2 == 0)
        lhs = jnp.concatenate([qs[2 * kh], qs[2 * kh + 1]], axis=0)
        s = _dot_nt(kk, lhs)
        prow = []
        for half in range(2):
            pcol = []
            for cc in range(2):
                sb = s[half * nkeys:(half + 1) * nkeys, cc * rows:(cc + 1) * rows]
                sb = jnp.where(mask_t, sb, NEG_INF)
                sink = sink_ref[kh * 4 + 2 * cc + half]
                m = jnp.maximum(jnp.max(sb, axis=0, keepdims=True), sink)
                e = jnp.exp(sb - m)
                den = jnp.sum(e, axis=0, keepdims=True) + jnp.exp(sink - m)
                pcol.append((e / den).astype(BF16))
            prow.append(jnp.concatenate(pcol, axis=1))
        p = jnp.concatenate(prow, axis=0)
        o = _dot_tn(p, vv)
        outs[2 * kh] = o[0:rows]
        outs[2 * kh + 1] = o[rows:2 * rows]
    return outs


def _even_prompt_kernel(sink_ref, z_ref, cos_ref, sin_ref, qg_ref, kg_ref, cw_ref, hs_ref, sw_ref, wo_ref,
                        o_ref, nk_ref, nv_ref, nc_ref, wo_bf_ref, kf_ref, vf_ref, ub_ref):
    i = pl.program_id(1)
    wo_bf_ref[...] = wo_ref[...].astype(BF16)

    @pl.when(i == 0)
    def _():
        kf_ref[...] = jnp.zeros_like(kf_ref)
        vf_ref[...] = jnp.zeros_like(vf_ref)
        ub_ref[0:8, :] = jnp.zeros((8, CONV_DIM), F32)

    kk = lax.broadcasted_iota(jnp.int32, (2 * BLK, BLK), 0)
    r = lax.broadcasted_iota(jnp.int32, (2 * BLK, BLK), 1)
    d = kk - r
    band = (d >= 0) & (d <= WINDOW)
    o0 = A_WIDTH + 2 * A_KV_WIDTH
    k = v = None
    for t in range(MIX_SUB):
        rows = slice(t * BLK, (t + 1) * BLK)
        kf_ref[0:BLK, :] = kf_ref[BLK:2 * BLK, :]
        vf_ref[0:BLK, :] = vf_ref[BLK:2 * BLK, :]
        cos = cos_ref[rows, :]
        sin = sin_ref[rows, :]
        k = jnp.concatenate(
            [_norm_rope_chunk(z_ref[rows, A_WIDTH + c * LANES:A_WIDTH + (c + 1) * LANES], kg_ref[...], cos, sin,
                              hs_ref, sw_ref)
             for c in range(A_KV_WIDTH // LANES)], axis=1)
        v = z_ref[rows, A_WIDTH + A_KV_WIDTH:A_WIDTH + 2 * A_KV_WIDTH]
        kf_ref[BLK:2 * BLK, :] = k
        vf_ref[BLK:2 * BLK, :] = v
        qs = [(_norm_rope_chunk(z_ref[rows, c * LANES:(c + 1) * LANES], qg_ref[...], cos, sin, hs_ref, sw_ref)
               * (HEAD_DIM ** -0.5)).astype(BF16) for c in range(A_WIDTH // LANES)]
        mask_t = band & ((kk >= BLK) | (i > 0)) if t == 0 else band
        outs = _attn_core(qs, kf_ref[...], vf_ref[...], mask_t, sink_ref)
        for c in range(A_WIDTH // LANES):
            o_ref[rows, c * LANES:(c + 1) * LANES] = outs[c].astype(o_ref.dtype)

        u = z_ref[rows, o0 + CONV_DIM:o0 + 2 * CONV_DIM] * z_ref[rows, o0 + 2 * CONV_DIM:o0 + 3 * CONV_DIM]
        ub_ref[8:8 + BLK, :] = u
        y = ub_ref[6:6 + BLK, :] * cw_ref[0:1, :]
        y = y + ub_ref[7:7 + BLK, :] * cw_ref[1:2, :]
        y = y + ub_ref[8:8 + BLK, :] * cw_ref[2:3, :]
        o_ref[rows, A_WIDTH:A_WIDTH + CONV_DIM] = (z_ref[rows, o0:o0 + CONV_DIM] * y).astype(o_ref.dtype)
        ub_ref[0:8, :] = ub_ref[BLK:BLK + 8, :]

    @pl.when(i == pl.num_programs(1) - 1)
    def _():
        nk_ref[...] = k
        nv_ref[...] = v
        nc_ref[...] = ub_ref[BLK + 6:BLK + 8, :]


def _w_slab_specs(w_out, layer, n_step):
    slab = w_out.shape[1] // (BATCH * n_step)
    return (pl.BlockSpec((None, slab, w_out.shape[2]), lambda b, i: (layer, b * n_step + i, 0)),
            pl.BlockSpec((slab, w_out.shape[2]), lambda b, i: (b * n_step + i, 0)),
            jax.ShapeDtypeStruct(w_out.shape[1:], BF16))


def _even_prompt(z_p, cos, sin, qg, kg, cw, sinks, mats, w_out, layer):
    n_step = NBLK // MIX_SUB
    step_rows = MIX_SUB * BLK
    blk = lambda b, i: (b * n_step + i, 0)
    fixed = lambda b, i: (0, 0)
    per_b = lambda b, i: (b, 0, 0)
    wo_in, wo_out, wo_shape = _w_slab_specs(w_out, layer, n_step)
    return pl.pallas_call(
        _even_prompt_kernel,
        grid=(BATCH, n_step),
        in_specs=[pl.BlockSpec(memory_space=pltpu.SMEM),
                  pl.BlockSpec((step_rows, EVEN_IN), blk),
                  pl.BlockSpec((step_rows, LANES), lambda b, i: (i, 0)),
                  pl.BlockSpec((step_rows, LANES), lambda b, i: (i, 0)),
                  pl.BlockSpec((1, LANES), fixed),
                  pl.BlockSpec((1, LANES), fixed),
                  pl.BlockSpec((CONV_W, CONV_DIM), fixed),
                  pl.BlockSpec((2 * LANES, LANES), fixed),
                  pl.BlockSpec((2 * LANES, LANES), fixed),
                  wo_in],
        out_specs=[pl.BlockSpec((step_rows, D_MODEL), blk),
                   pl.BlockSpec((None, BLK, A_KV_WIDTH), per_b),
                   pl.BlockSpec((None, BLK, A_KV_WIDTH), per_b),
                   pl.BlockSpec((None, CONV_W - 1, CONV_DIM), per_b),
                   wo_out],
        out_shape=[jax.ShapeDtypeStruct((M_PROMPT, D_MODEL), BF16),
                   jax.ShapeDtypeStruct((BATCH, BLK, A_KV_WIDTH), F32),
                   jax.ShapeDtypeStruct((BATCH, BLK, A_KV_WIDTH), F32),
                   jax.ShapeDtypeStruct((BATCH, CONV_W - 1, CONV_DIM), F32),
                   wo_shape],
        scratch_shapes=[pltpu.VMEM((2 * BLK, A_KV_WIDTH), F32),
                        pltpu.VMEM((2 * BLK, A_KV_WIDTH), F32),
                        pltpu.VMEM((BLK + 8, CONV_DIM), F32)],
        compiler_params=_cparams("arbitrary", "arbitrary"),
        name="even_prompt",
    )(sinks, z_p, cos, sin, qg, kg, cw, *mats, w_out)


SROWS = LANES
N_SAMPLE_IN = 12


def _even_sample_kernel(*refs):
    (sink_ref, z_ref, cos_ref, sin_ref, qg_ref, kg_ref, cs_ref, cw_ref, hs_ref, sw_ref,
     kc_ref, vc_ref) = refs[:N_SAMPLE_IN]
    m_ref, ncs_ref, nk_ref, nv_ref, q_all, k_all = refs[-6:]
    b = pl.program_id(0)
    lc = kc_ref.shape[0]

    @pl.when(b == 0)
    def _():
        cos = cos_ref[...]
        sin = sin_ref[...]
        for c in range(A_WIDTH // LANES):
            q = _norm_rope_chunk(z_ref[:, c * LANES:(c + 1) * LANES], qg_ref[...], cos, sin, hs_ref, sw_ref)
            q_all[:, c * LANES:(c + 1) * LANES] = q * (HEAD_DIM ** -0.5)
        for c in range(A_KV_WIDTH // LANES):
            k_all[:, c * LANES:(c + 1) * LANES] = _norm_rope_chunk(
                z_ref[:, A_WIDTH + c * LANES:A_WIDTH + (c + 1) * LANES], kg_ref[...], cos, sin, hs_ref, sw_ref)
        o0 = A_WIDTH + 2 * A_KV_WIDTH
        u = z_ref[:, o0 + CONV_DIM:o0 + 2 * CONV_DIM] * z_ref[:, o0 + 2 * CONV_DIM:o0 + 3 * CONV_DIM]
        y = cs_ref[:, 0:CONV_DIM] * cw_ref[0:1, :]
        y = y + cs_ref[:, CONV_DIM:2 * CONV_DIM] * cw_ref[1:2, :]
        y = y + u * cw_ref[2:3, :]
        m_ref[:, A_WIDTH:A_WIDTH + CONV_DIM] = z_ref[:, o0:o0 + CONV_DIM] * y
        m_ref[:, 0:A_WIDTH] = jnp.zeros((DEC_BATCH, A_WIDTH), F32)
        ncs_ref[:, 0:CONV_DIM] = cs_ref[:, CONV_DIM:2 * CONV_DIM]
        ncs_ref[:, CONV_DIM:2 * CONV_DIM] = u

    grp = pl.ds(pl.multiple_of((b // 8) * 8, 8), 8)
    mine = lax.broadcasted_iota(jnp.int32, (8, 1), 0) == b % 8

    def pick(tile):
        return jnp.sum(jnp.where(mine, tile, 0.0), axis=0, keepdims=True)

    kn = pick(k_all[grp, :])
    vn = pick(z_ref[grp, A_WIDTH + A_KV_WIDTH:A_WIDTH + 2 * A_KV_WIDTH])
    row = lax.broadcasted_iota(jnp.int32, (lc, A_KV_WIDTH), 0)
    kfull = jnp.concatenate([kc_ref[...], jnp.where(row == 0, kn, 0.0)], axis=0)
    vfull = jnp.concatenate([vc_ref[...], jnp.where(row == 0, vn, 0.0)], axis=0)
    qs = [jnp.broadcast_to(pick(q_all[grp, c * LANES:(c + 1) * LANES]), (SROWS, LANES)).astype(BF16)
          for c in range(A_WIDTH // LANES)]
    kk = lax.broadcasted_iota(jnp.int32, (2 * lc, SROWS), 0)
    mask_t = (kk <= lc) & (lc - kk <= WINDOW)
    outs = _attn_core(qs, kfull, vfull, mask_t, sink_ref)
    for c in range(A_WIDTH // LANES):
        cols = slice(c * LANES, (c + 1) * LANES)
        m_ref[grp, cols] = jnp.where(mine, outs[c][0:8], m_ref[grp, cols])
    nk_ref[0:lc - 1, :] = kc_ref[1:lc, :]
    nk_ref[lc - 1:lc, :] = kn
    nv_ref[0:lc - 1, :] = vc_ref[1:lc, :]
    nv_ref[lc - 1:lc, :] = vn


def _even_sample(sinks, z_s, cos, sin, qg, kg, conv_state, cw, mats, k_cache, v_cache, li, prev):
    lc = k_cache.shape[2]
    whole = lambda a: pl.BlockSpec(a.shape, lambda b: (0,) * a.ndim)
    cache = pl.BlockSpec((None, None, lc, A_KV_WIDTH), lambda b: (li, b, 0, 0))
    cs = conv_state.reshape(DEC_BATCH, (CONV_W - 1) * CONV_DIM)
    args = [sinks, z_s, cos, sin, qg, kg, cs, cw, *mats, k_cache, v_cache]
    assert len(args) == N_SAMPLE_IN
    in_specs = [pl.BlockSpec(memory_space=pltpu.SMEM)] + [whole(a) for a in args[1:N_SAMPLE_IN - 2]] + [cache, cache]
    aliases = {}
    if prev is not None:
        aliases = {len(args): 2, len(args) + 1: 3}
        in_specs += [pl.BlockSpec(memory_space=pl.ANY)] * 2
        args += list(prev)
    rows = lambda w: pl.BlockSpec((DEC_BATCH, w), lambda b: (0, 0))
    m_s, ncs, nk, nv = pl.pallas_call(
        _even_sample_kernel,
        grid=(DEC_BATCH,),
        in_specs=in_specs,
        out_specs=[rows(D_MODEL), rows((CONV_W - 1) * CONV_DIM), cache, cache],
        out_shape=[jax.ShapeDtypeStruct((DEC_BATCH, D_MODEL), F32),
                   jax.ShapeDtypeStruct((DEC_BATCH, (CONV_W - 1) * CONV_DIM), F32),
                   jax.ShapeDtypeStruct(k_cache.shape, F32),
                   jax.ShapeDtypeStruct(v_cache.shape, F32)],
        scratch_shapes=[pltpu.VMEM((DEC_BATCH, A_WIDTH), F32), pltpu.VMEM((DEC_BATCH, A_KV_WIDTH), F32)],
        input_output_aliases=aliases,
        compiler_params=_cparams("arbitrary"),
        name="even_sample",
    )(*args)
    return m_s, ncs.reshape(DEC_BATCH, CONV_W - 1, CONV_DIM), nk, nv


def _log_decay(zr, wa_ref, ba_ref):
    pre = _dot(zr.astype(BF16), wa_ref[...]) + ba_ref[...]
    return (jnp.minimum(pre, 0.0) - jnp.log1p(jnp.exp(-jnp.abs(pre)))) * (1.0 / GLA_TAU)


def _split_bf16(x):
    hi = x.astype(BF16)
    lo = (x - hi.astype(F32)).astype(BF16)
    return hi, lo


def _row_to_col(row):
    n = row.shape[1]
    r = lax.broadcasted_iota(jnp.int32, (n, n), 0)
    c = lax.broadcasted_iota(jnp.int32, (n, n), 1)
    return jnp.sum(jnp.where(r == c, jnp.broadcast_to(row, (n, n)), 0.0), axis=-1, keepdims=True)


def _head_rmsnorm_gate(o, gain, zg):
    ms = jnp.mean(o * o, axis=-1, keepdims=True)
    return ((o * lax.rsqrt(ms + EPS)) * gain) * _silu(zg)


def _odd_prompt_kernel(zm_ref, zpr_ref, wa_ref, ba_ref, gain_ref, wp_ref, ps_ref, tri_ref, band_ref, wo_ref,
                       o_ref, s_ref, np_ref, wo_bf_ref, pbh_ref, pbl_ref):
    i = pl.program_id(1)
    wo_bf_ref[...] = wo_ref[...].astype(BF16)

    @pl.when(i == 0)
    def _():
        s_ref[...] = jnp.zeros_like(s_ref)
        pbh_ref[0:BLK, :] = jnp.zeros((BLK, POOL_DIM), BF16)
        pbl_ref[0:BLK, :] = jnp.zeros((BLK, POOL_DIM), BF16)

    x = None
    for t in range(MIX_SUB):
        rows = pl.ds(t * BLK, BLK)
        x = _odd_block(zm_ref.at[rows], zpr_ref.at[rows], wa_ref, ba_ref, gain_ref, wp_ref, ps_ref,
                       tri_ref, band_ref, o_ref.at[rows], s_ref, pbh_ref, pbl_ref, (i * MIX_SUB + t) * BLK)

    @pl.when(i == pl.num_programs(1) - 1)
    def _():
        np_ref[...] = x[BLK - POOL_BUF:BLK, :]


def _odd_block(zm_ref, zpr_ref, wa_ref, ba_ref, gain_ref, wp_ref, ps_ref, tri_ref, band_ref,
               o_ref, s_ref, pbh_ref, pbl_ref, t0):
    n_ch = BLK // GLA_CHUNK
    gk = _log_decay(zpr_ref[:, POOL_DIM:POOL_RANK_W], wa_ref, ba_ref)
    r = lax.broadcasted_iota(jnp.int32, (BLK, BLK), 0)
    c = lax.broadcasted_iota(jnp.int32, (BLK, BLK), 1)
    causal = (r >= c) & (r // GLA_CHUNK == c // GLA_CHUNK)
    g_hi, g_lo = _split_bf16(gk)
    b = _dot(tri_ref[...], jnp.concatenate([g_hi, g_lo], axis=0))
    b_last = [b[(ch + 1) * GLA_CHUNK - 1:(ch + 1) * GLA_CHUNK, :] for ch in range(n_ch)]
    b_end = jnp.concatenate([jnp.broadcast_to(bl, (GLA_CHUNK, GLA_K_WIDTH)) for bl in b_last], axis=0)
    o_k, o_v, o_g = GLA_K_WIDTH, 2 * GLA_K_WIDTH, 2 * GLA_K_WIDTH + GLA_WIDTH
    zq = zm_ref[:, 0:GLA_K_WIDTH]
    zk = zm_ref[:, o_k:o_k + GLA_K_WIDTH]
    qd = ((zq * (GLA_DK ** -0.5)) * jnp.exp(b)).astype(BF16)
    kd = (zk * jnp.exp(-b)).astype(BF16)
    k2 = (zk * jnp.exp(b_end - b)).astype(BF16)
    sub = lax.broadcasted_iota(jnp.int32, (8, LANES), 0)
    sel = jnp.where(sub < 2, 1.0, 0.0).astype(BF16)
    hk = [slice(h * GLA_DK, (h + 1) * GLA_DK) for h in range(GLA_HEADS)]
    chunk = [slice(ch * GLA_CHUNK, (ch + 1) * GLA_CHUNK) for ch in range(n_ch)]
    v_hs = [zm_ref[:, o_v + h * GLA_DV:o_v + (h + 1) * GLA_DV].astype(BF16) for h in range(GLA_HEADS)]
    atts = [_dot_nt(qd[:, hk[h]], kd[:, hk[h]]) for h in range(GLA_HEADS)]
    log_decs = []
    for h in range(GLA_HEADS):
        for ch in range(n_ch):
            bl = b_last[ch][:, hk[h]]
            bl_hi = bl.astype(BF16).astype(F32)
            rows8 = jnp.where(sub == 0, bl_hi, jnp.where(sub == 1, bl - bl_hi, 0.0)).astype(BF16)
            log_decs.append(_dot_tn(rows8, sel))
    deltas = [[_dot_tn(k2[chunk[ch], hk[h]], v_hs[h][chunk[ch]]) for ch in range(n_ch)]
              for h in range(GLA_HEADS)]
    o_intras = [_dot(jnp.where(causal, atts[h], 0.0).astype(BF16), v_hs[h]) for h in range(GLA_HEADS)]
    o_inters = []
    for h in range(GLA_HEADS):
        s_h = s_ref[h]
        parts = []
        for ch in range(n_ch):
            parts.append(_dot(qd[chunk[ch], hk[h]], s_h.astype(BF16)))
            dec = jnp.exp(log_decs[h * n_ch + ch])
            s_h = jnp.concatenate([dec, dec], axis=1) * s_h + deltas[h][ch]
        s_ref[h] = s_h
        o_inters.append(jnp.concatenate(parts, axis=0))
    for h in range(GLA_HEADS):
        vs = slice(h * GLA_DV, (h + 1) * GLA_DV)
        zg = zm_ref[:, o_g + h * GLA_DV:o_g + (h + 1) * GLA_DV]
        o_ref[:, vs] = _head_rmsnorm_gate(o_intras[h] + o_inters[h], gain_ref[...], zg).astype(o_ref.dtype)

    x = zpr_ref[:, 0:POOL_DIM]
    x_hi, x_lo = _split_bf16(x)
    pbh_ref[BLK:2 * BLK, :] = x_hi
    pbl_ref[BLK:2 * BLK, :] = x_lo
    t_glob = t0 + lax.broadcasted_iota(jnp.int32, (BLK, 1), 0)
    groups = [slice(g * POOL_GDIM, (g + 1) * POOL_GDIM) for g in range(len(POOL_WINDOWS))]
    sums = [_dot(band_ref[g], jnp.concatenate([pbh_ref[:, groups[g]], pbl_ref[:, groups[g]]], axis=0))
            for g in range(len(POOL_WINDOWS))]
    ys = []
    for g, w in enumerate(POOL_WINDOWS):
        inv_cnt = 1.0 / jnp.minimum(w, t_glob + 1).astype(F32)
        d = sums[g] * inv_cnt - x[:, groups[g]]
        ys.append(_dot(d.astype(BF16), wp_ref[g]))
    for g in range(len(POOL_WINDOWS)):
        o_ref[:, GLA_WIDTH + g * POOL_GDIM:GLA_WIDTH + (g + 1) * POOL_GDIM] = (
            ys[g] * ps_ref[:, groups[g]]).astype(o_ref.dtype)
    pbh_ref[0:BLK, :] = x_hi
    pbl_ref[0:BLK, :] = x_lo
    return x


def _odd_block_mats():
    r = jnp.arange(BLK)[:, None]
    c = jnp.arange(BLK)[None, :]
    tri = ((r >= c) & (r // GLA_CHUNK == c // GLA_CHUNK)).astype(BF16)
    back = r + BLK - jnp.arange(2 * BLK)[None, :]
    bands = [((back >= 0) & (back < w)).astype(BF16) for w in POOL_WINDOWS]
    return (jnp.concatenate([tri, tri], axis=1),
            jnp.stack([jnp.concatenate([bd, bd], axis=1) for bd in bands]))


def _odd_prompt(zm_p, zpr_p, wa, ba, gain, wp, ps, w_out, layer):
    n_step = NBLK // MIX_SUB
    step_rows = MIX_SUB * BLK
    blk = lambda b, i: (b * n_step + i, 0)
    fixed2 = lambda b, i: (0, 0)
    wo_in, wo_out, wo_shape = _w_slab_specs(w_out, layer, n_step)
    tri2, bands2 = _odd_block_mats()
    return pl.pallas_call(
        _odd_prompt_kernel,
        grid=(BATCH, n_step),
        in_specs=[pl.BlockSpec((step_rows, ODD_MAIN), blk),
                  pl.BlockSpec((step_rows, POOL_RANK_W), blk),
                  pl.BlockSpec((LANES, GLA_K_WIDTH), fixed2),
                  pl.BlockSpec((1, GLA_K_WIDTH), fixed2),
                  pl.BlockSpec((1, GLA_DV), fixed2),
                  pl.BlockSpec((4, POOL_GDIM, POOL_GDIM), lambda b, i: (0, 0, 0)),
                  pl.BlockSpec((1, POOL_DIM), fixed2),
                  pl.BlockSpec(tri2.shape, fixed2),
                  pl.BlockSpec(bands2.shape, lambda b, i: (0, 0, 0)),
                  wo_in],
        out_specs=[pl.BlockSpec((step_rows, D_MODEL), blk),
                   pl.BlockSpec((None, GLA_HEADS, GLA_DK, GLA_DV), lambda b, i: (b, 0, 0, 0)),
                   pl.BlockSpec((None, POOL_BUF, POOL_DIM), lambda b, i: (b, 0, 0)),
                   wo_out],
        out_shape=[jax.ShapeDtypeStruct((M_PROMPT, D_MODEL), BF16),
                   jax.ShapeDtypeStruct((BATCH, GLA_HEADS, GLA_DK, GLA_DV), F32),
                   jax.ShapeDtypeStruct((BATCH, POOL_BUF, POOL_DIM), F32),
                   wo_shape],
        scratch_shapes=[pltpu.VMEM((2 * BLK, POOL_DIM), BF16), pltpu.VMEM((2 * BLK, POOL_DIM), BF16)],
        compiler_params=_cparams("arbitrary", "arbitrary"),
        name="odd_prompt",
    )(zm_p, zpr_p, wa, ba, gain, wp, ps, tri2, bands2, w_out)


def _odd_sample_prep_kernel(zm_ref, zr_ref, wa_ref, ba_ref, dec_ref, q_ref):
    dec_ref[...] = jnp.exp(_log_decay(zr_ref[...], wa_ref, ba_ref))
    q_ref[...] = zm_ref[:, 0:GLA_K_WIDTH] * (GLA_DK ** -0.5)


def _odd_sample_prep(zm_s, zr_s, wa, ba):
    return pl.pallas_call(
        _odd_sample_prep_kernel,
        out_shape=[jax.ShapeDtypeStruct((DEC_BATCH, GLA_K_WIDTH), F32),
                   jax.ShapeDtypeStruct((DEC_BATCH, GLA_K_WIDTH), F32)],
        compiler_params=pltpu.CompilerParams(vmem_limit_bytes=VMEM_LIMIT),
        name="odd_sample_prep",
    )(zm_s, zr_s, wa, ba)


STATE_BB = 8


def _odd_sample_state_kernel(*refs):
    dec_ref, q_ref, k_ref, v_ref, s_ref = refs[:5]
    ns_ref, o_ref = refs[-2:]
    for bb in range(STATE_BB):
        for h in range(GLA_HEADS):
            ks = slice(h * GLA_DK, (h + 1) * GLA_DK)
            vs = slice(h * GLA_DV, (h + 1) * GLA_DV)
            s_new = (_row_to_col(dec_ref[bb][:, ks]) * s_ref[bb, h]
                     + _row_to_col(k_ref[bb][:, ks]) * v_ref[bb][:, vs])
            ns_ref[bb, h] = s_new
            o_ref[bb, :, vs] = jnp.sum(_row_to_col(q_ref[bb][:, ks]) * s_new, axis=0, keepdims=True)


def _odd_sample_state(dec, q, k, v, state, li, prev):
    vec = lambda w: pl.BlockSpec((STATE_BB, 1, w), lambda b: (b, 0, 0))
    st = pl.BlockSpec((None, STATE_BB, GLA_HEADS, GLA_DK, GLA_DV), lambda b: (li, b, 0, 0, 0))
    r3 = lambda a: a.reshape(DEC_BATCH, 1, a.shape[-1])
    in_specs = [vec(GLA_K_WIDTH), vec(GLA_K_WIDTH), vec(GLA_K_WIDTH), vec(GLA_WIDTH), st]
    args = [r3(dec), r3(q), r3(k), r3(v), state]
    aliases = {}
    if prev is not None:
        aliases = {len(args): 0}
        in_specs.append(pl.BlockSpec(memory_space=pl.ANY))
        args.append(prev)
    return pl.pallas_call(
        _odd_sample_state_kernel,
        grid=(DEC_BATCH // STATE_BB,),
        in_specs=in_specs,
        out_specs=[st, vec(GLA_WIDTH)],
        out_shape=[jax.ShapeDtypeStruct(state.shape, F32),
                   jax.ShapeDtypeStruct((DEC_BATCH, 1, GLA_WIDTH), F32)],
        input_output_aliases=aliases,
        compiler_params=_cparams("arbitrary"),
        name="odd_sample_state",
    )(*args)


def _odd_sample_post_kernel(o_in_ref, zm_ref, zp_ref, hist_ref, gain_ref, wp_ref, ps_ref, o_ref):
    o_g = 2 * GLA_K_WIDTH + GLA_WIDTH
    for h in range(GLA_HEADS):
        vs = slice(h * GLA_DV, (h + 1) * GLA_DV)
        zg = zm_ref[:, o_g + h * GLA_DV:o_g + (h + 1) * GLA_DV]
        o_ref[:, vs] = _head_rmsnorm_gate(o_in_ref[:, vs], gain_ref[...], zg).astype(o_ref.dtype)
    n_prev = hist_ref.shape[0]
    for g, w in enumerate(POOL_WINDOWS):
        gs = slice(g * POOL_GDIM, (g + 1) * POOL_GDIM)
        x = zp_ref[:, gs]
        s = hist_ref[n_prev - (w - 1)][:, gs]
        for jj in range(n_prev - (w - 1) + 1, n_prev):
            s = s + hist_ref[jj][:, gs]
        s = s + x
        d = s / float(min(w, n_prev + 1)) - x
        y = _dot(d.astype(BF16), wp_ref[g]) * ps_ref[:, gs]
        o_ref[:, GLA_WIDTH + g * POOL_GDIM:GLA_WIDTH + (g + 1) * POOL_GDIM] = y.astype(o_ref.dtype)


def _odd_sample_post(o_raw, zm_s, zp_s, hist_t, gain, wp, ps):
    return pl.pallas_call(
        _odd_sample_post_kernel,
        out_shape=jax.ShapeDtypeStruct((DEC_BATCH, D_MODEL), BF16),
        compiler_params=pltpu.CompilerParams(vmem_limit_bytes=VMEM_LIMIT),
        name="odd_sample_post",
    )(o_raw, zm_s, zp_s, hist_t, gain, wp, ps)


def _rope_tables(pos):
    half = HEAD_DIM // 2
    inv = jnp.power(ROPE_THETA, -jnp.arange(half, dtype=F32) / half)
    ang = pos.astype(F32)[:, None] * inv[None, :]
    c, s = jnp.cos(ang), jnp.sin(ang)
    return jnp.tile(c, (1, 4)), jnp.tile(jnp.concatenate([-s, s], axis=1), (1, 2))


def kernel(x_prompt, x_sample, cache_swa_k, cache_swa_v, state_conv, state_gla, state_pool, norm_mix, norm_ffn, w_in_even, w_out_even, q_norm, k_norm, attn_sinks, conv_w, w_in_odd, w_out_odd, w_alpha_up, b_alpha, gla_out_norm, w_pool, pool_scale, w_gate, w_up, w_down):
    lc = cache_swa_k.shape[2]
    assert x_prompt.shape == (BATCH, SEQ, D_MODEL) and x_sample.shape == (DEC_BATCH, 1, D_MODEL)
    assert lc == BLK == WINDOW, "the prompt's new cache is its last attention block"
    assert state_pool.shape[2] == POOL_BUF and state_conv.shape[2] == CONV_W - 1
    x_p = x_prompt.reshape(M_PROMPT, D_MODEL)
    x_s = x_sample.reshape(DEC_BATCH, D_MODEL)
    cos_p, sin_p = _rope_tables(jnp.arange(SEQ))
    cos_s, sin_s = _rope_tables(PAST_LEN + jnp.arange(1))
    mats = _rope_mats()
    kc_all = cache_swa_k.reshape(N_EVEN, DEC_BATCH, lc, A_KV_WIDTH)
    vc_all = cache_swa_v.reshape(N_EVEN, DEC_BATCH, lc, A_KV_WIDTH)
    o_r = ODD_MAIN
    w_odd_t = jnp.swapaxes(w_in_odd, 1, 2)
    w_odd_pr_t = jnp.concatenate([w_odd_t[:, o_r + GLA_RANK:], w_odd_t[:, o_r:o_r + GLA_RANK],
                                  jnp.zeros((N_ODD, LANES - GLA_RANK, D_MODEL), F32)], axis=1)
    pk, pv, pc, pg, pp, sc, sp = ([] for _ in range(7))
    sk_all = sv_all = sg_all = None

    h_p, h_s = _rmsnorm(x_p, x_s, norm_mix[0])
    for layer in range(DEPTH):
        li = layer // 2
        if layer % 2 == 0:
            z_p, z_s = _dense(h_p, h_s, [(w_in_even, li)], n_cols=EVEN_IN, **TILES["in_even"], name="in_even")
            qg = jnp.tile(q_norm[li], 2).reshape(1, LANES)
            kg = jnp.tile(k_norm[li], 2).reshape(1, LANES)
            m_p, nk, nv, nc, w_out_bf = _even_prompt(z_p, cos_p, sin_p, qg, kg, conv_w[li], attn_sinks[li], mats,
                                                     w_out_even, li)
            pk.append(nk.reshape(BATCH, lc, A_KV_HEADS, HEAD_DIM))
            pv.append(nv.reshape(BATCH, lc, A_KV_HEADS, HEAD_DIM))
            pc.append(nc)
            m_s, nc_s, sk_all, sv_all = _even_sample(
                attn_sinks[li], z_s, cos_s, sin_s, qg, kg, state_conv[li], conv_w[li], mats, kc_all, vc_all, li,
                None if sk_all is None else (sk_all, sv_all))
            sc.append(nc_s)
        else:
            zm_p, zm_s = _dense(h_p, h_s, [(w_odd_t, li)], n_cols=ODD_MAIN, **TILES["in_odd_main"],
                                transposed=True, name="in_odd_main")
            zpr_p, zpr_s = _dense(h_p, h_s, [(w_odd_pr_t, li)], n_cols=POOL_RANK_W, **TILES["in_odd_pool_rank"],
                                  transposed=True, name="in_odd_pool_rank")
            zp_s, zr_s = zpr_s[:, :POOL_DIM], zpr_s[:, POOL_DIM:]
            wa = jnp.pad(w_alpha_up[li], ((0, LANES - GLA_RANK), (0, 0))).astype(BF16)
            ba = b_alpha[li].reshape(1, GLA_K_WIDTH)
            gain = gla_out_norm[li].reshape(1, GLA_DV)
            wp = w_pool[li].astype(BF16)
            ps = pool_scale[li].reshape(1, POOL_DIM)
            m_p, ng, npool, w_out_bf = _odd_prompt(zm_p, zpr_p, wa, ba, gain, wp, ps, w_out_odd, li)
            pg.append(ng)
            pp.append(npool)
            dec_s, q_s = _odd_sample_prep(zm_s, zr_s, wa, ba)
            sg_all, o_raw = _odd_sample_state(
                dec_s, q_s, zm_s[:, GLA_K_WIDTH:2 * GLA_K_WIDTH],
                zm_s[:, 2 * GLA_K_WIDTH:2 * GLA_K_WIDTH + GLA_WIDTH], state_gla, li, sg_all)
            m_s = _odd_sample_post(o_raw.reshape(DEC_BATCH, GLA_WIDTH), zm_s, zp_s,
                                   jnp.swapaxes(state_pool[li], 0, 1), gain, wp, ps)
            sp.append(jnp.concatenate([state_pool[li, :, 1:], zp_s[:, None, :]], axis=1))
        x_p, x_s, h_p, h_s = _proj_res(m_p, m_s, (w_out_bf, None), (x_p, x_s), norm_ffn[layer],
                                       **TILES["out_proj"], name="out_proj")
        a_p, a_s, w_down_bf = _dense(h_p, h_s, [(w_gate, layer), (w_up, layer)], n_cols=D_FF, **TILES["ffn_up"],
                                     mode="swiglu", out_dtype=BF16, side_cast=(w_down, layer), name="ffn_up")
        if layer + 1 < DEPTH:
            x_p, x_s, h_p, h_s = _proj_res(a_p, a_s, (w_down_bf, None), (x_p, x_s),
                                           norm_mix[layer + 1], **TILES["ffn_down"], name="ffn_down")
        else:
            x_p, x_s = _proj_res(a_p, a_s, (w_down_bf, None), (x_p, x_s), None,
                                 **TILES["ffn_down"], name="ffn_down_last")

    st = lambda parts: jnp.stack(parts)
    cache5 = lambda a: a.reshape(N_EVEN, DEC_BATCH, lc, A_KV_HEADS, HEAD_DIM)
    return (x_p.reshape(BATCH, SEQ, D_MODEL), x_s.reshape(DEC_BATCH, 1, D_MODEL),
            st(pk), st(pv), st(pc), st(pg), st(pp), cache5(sk_all), cache5(sv_all), st(sc), sg_all, st(sp))
```

```python
import functools

import jax
import jax.numpy as jnp
from jax import lax
from jax.experimental import pallas as pl
from jax.experimental.pallas import tpu as pltpu

F32 = jnp.float32
BF16 = jnp.bfloat16

D_MODEL = 2048
BATCH = 4
SEQ = 2048
DEPTH = 4
DEC_BATCH = 32
PAST_LEN = 16384
N_EVEN = 2
N_ODD = 2
EPS = 1e-6
NEG_INF = -1e30
A_HEADS = 16
A_KV_HEADS = 4
HEAD_DIM = 64
A_WIDTH = A_HEADS * HEAD_DIM
A_KV_WIDTH = A_KV_HEADS * HEAD_DIM
WINDOW = 128
ROPE_THETA = 10000.0
CONV_DIM = D_MODEL // 2
CONV_W = 3
GLA_HEADS = 4
GLA_WIDTH = D_MODEL // 2
GLA_DV = GLA_WIDTH // GLA_HEADS
GLA_DK = GLA_DV // 2
GLA_K_WIDTH = GLA_HEADS * GLA_DK
GLA_RANK = 16
GLA_TAU = 16.0
GLA_CHUNK = 64
POOL_DIM = D_MODEL // 2
POOL_WINDOWS = (2, 4, 8, 16)
POOL_GDIM = POOL_DIM // 4
POOL_BUF = 15
D_FF = 5632
EVEN_IN = A_WIDTH + 2 * A_KV_WIDTH + 3 * CONV_DIM
ODD_MAIN = 2 * GLA_K_WIDTH + 2 * GLA_WIDTH

M_PROMPT = BATCH * SEQ
LANES = 128
BLK = 128
NBLK = SEQ // BLK
MIX_SUB = 4
POOL_RANK_W = POOL_DIM + LANES
VMEM_LIMIT = 58 * 1024 * 1024

TILES = {
    "in_even": dict(tm=1024, tn=1536),
    "in_odd_main": dict(tm=1024, tn=ODD_MAIN // 2),
    "in_odd_pool_rank": dict(tm=1024, tn=POOL_RANK_W),
    "ffn_up": dict(tm=2048, tn=512, sub=2),
    "out_proj": dict(tm=512),
    "ffn_down": dict(tm=512),
}


def _cparams(*sem):
    return pltpu.CompilerParams(dimension_semantics=sem, vmem_limit_bytes=VMEM_LIMIT)


def _dot(a, b):
    return jnp.dot(a, b, preferred_element_type=F32)


def _dot_nt(a, b):
    return lax.dot_general(a, b, (((1,), (1,)), ((), ())), preferred_element_type=F32)


def _dot_tn(a, b):
    return lax.dot_general(a, b, (((0,), (0,)), ((), ())), preferred_element_type=F32)


def _silu(x):
    return x * (1.0 / (1.0 + jnp.exp(-x)))


def _rmsnorm_rows(x, gain):
    ms = jnp.mean(x * x, axis=-1, keepdims=True)
    return (x * lax.rsqrt(ms + EPS)) * gain


def _rmsnorm_kernel(xp_ref, xs_ref, g_ref, op_ref, os_ref, *, n_i):
    op_ref[...] = _rmsnorm_rows(xp_ref[...], g_ref[...]).astype(op_ref.dtype)

    @pl.when(pl.program_id(0) == n_i - 1)
    def _():
        os_ref[...] = _rmsnorm_rows(xs_ref[...], g_ref[...]).astype(os_ref.dtype)


def _rmsnorm(x_p, x_s, gain, *, tm=1024):
    n_i = M_PROMPT // tm
    row = lambda i: (i, 0)
    fixed = lambda i: (0, 0)
    return pl.pallas_call(
        functools.partial(_rmsnorm_kernel, n_i=n_i),
        grid=(n_i,),
        in_specs=[pl.BlockSpec((tm, D_MODEL), row),
                  pl.BlockSpec((DEC_BATCH, D_MODEL), fixed),
                  pl.BlockSpec((1, D_MODEL), fixed)],
        out_specs=[pl.BlockSpec((tm, D_MODEL), row),
                   pl.BlockSpec((DEC_BATCH, D_MODEL), fixed)],
        out_shape=[jax.ShapeDtypeStruct((M_PROMPT, D_MODEL), BF16),
                   jax.ShapeDtypeStruct((DEC_BATCH, D_MODEL), BF16)],
        compiler_params=_cparams("arbitrary"),
        name="rmsnorm",
    )(x_p, x_s, gain.reshape(1, D_MODEL))


XPOSE_COLS = 128


def _dense_kernel(*refs, n_w, mode, n_i, transposed, side_cast, sub):
    a_p, a_s = refs[0], refs[1]
    w = refs[2:2 + n_w]
    pos = 2 + n_w
    side_in = side_out = None
    if side_cast:
        side_in = refs[pos]
        pos += 1
    o_p, o_s = refs[pos], refs[pos + 1]
    pos += 2
    if side_cast:
        side_out = refs[pos]
        pos += 1
    wbf = refs[pos:pos + n_w]
    i = pl.program_id(1)

    @pl.when(i == 0)
    def _():
        for k in range(n_w):
            if transposed:
                tn = wbf[k].shape[1]
                for c in range(0, tn, XPOSE_COLS):
                    wbf[k][:, c:c + XPOSE_COLS] = w[k][c:c + XPOSE_COLS, :].T.astype(BF16)
            else:
                wbf[k][...] = w[k][...].astype(BF16)
        if side_cast:
            side_out[...] = side_in[...].astype(BF16)

    def run(a_ref, o_ref, n_sub):
        rows = a_ref.shape[0] // n_sub
        for r in range(n_sub):
            a = a_ref[r * rows:(r + 1) * rows, :]
            if mode == "swiglu":
                y = _silu(_dot(a, wbf[0][...])) * _dot(a, wbf[1][...])
            else:
                y = _dot(a, wbf[0][...])
            o_ref[r * rows:(r + 1) * rows, :] = y.astype(o_ref.dtype)

    run(a_p, o_p, sub)

    @pl.when(i == n_i - 1)
    def _():
        run(a_s, o_s, 1)


def _dense(a_p, a_s, weights, *, n_cols, tm, tn, mode="plain", out_dtype=F32, transposed=False,
           side_cast=None, sub=1, name):
    k_dim = a_p.shape[1]
    n_i = M_PROMPT // tm
    n_j = n_cols // tn
    row = lambda j, i: (i, 0)
    fixed = lambda j, i: (0, 0)
    tile = lambda j, i: (i, j)
    panel = lambda j, i: (0, j)
    in_specs = [pl.BlockSpec((tm, k_dim), row), pl.BlockSpec((DEC_BATCH, k_dim), fixed)]
    args = [a_p, a_s]
    for arr, layer in weights:
        if transposed:
            in_specs.append(pl.BlockSpec((None, tn, k_dim), lambda j, i, layer=layer: (layer, j, 0)))
        else:
            in_specs.append(pl.BlockSpec((None, k_dim, tn), lambda j, i, layer=layer: (layer, 0, j)))
        args.append(arr)
    out_specs = [pl.BlockSpec((tm, tn), tile), pl.BlockSpec((DEC_BATCH, tn), panel)]
    out_shape = [jax.ShapeDtypeStruct((M_PROMPT, n_cols), out_dtype),
                 jax.ShapeDtypeStruct((DEC_BATCH, n_cols), out_dtype)]
    if side_cast is not None:
        s_arr, s_layer = side_cast
        slab = s_arr.shape[1] // n_j
        in_specs.append(pl.BlockSpec((None, slab, s_arr.shape[2]), lambda j, i: (s_layer, j, 0)))
        args.append(s_arr)
        out_specs.append(pl.BlockSpec((slab, s_arr.shape[2]), lambda j, i: (j, 0)))
        out_shape.append(jax.ShapeDtypeStruct(s_arr.shape[1:], BF16))
    n_w = len(weights)
    return pl.pallas_call(
        functools.partial(_dense_kernel, n_w=n_w, mode=mode, n_i=n_i, transposed=transposed,
                          side_cast=side_cast is not None, sub=sub),
        grid=(n_j, n_i),
        in_specs=in_specs,
        out_specs=out_specs,
        out_shape=out_shape,
        scratch_shapes=[pltpu.VMEM((k_dim, tn), BF16) for _ in range(n_w)],
        compiler_params=_cparams("arbitrary", "arbitrary"),
        name=name,
    )(*args)


def _proj_res_kernel(*refs, n_i, with_norm):
    a_p, a_s, w_ref, r_p, r_s = refs[:5]
    if with_norm:
        g_ref, x_p, x_s, h_p, h_s = refs[5:]
    else:
        x_p, x_s = refs[5:]
        g_ref = h_p = h_s = None

    def run(a_ref, r_ref, x_ref, h_ref):
        a = a_ref[...]
        if a.dtype != BF16:
            a = a.astype(BF16)
        x = r_ref[...] + _dot(a, w_ref[...])
        x_ref[...] = x
        if with_norm:
            h_ref[...] = _rmsnorm_rows(x, g_ref[...]).astype(h_ref.dtype)

    run(a_p, r_p, x_p, h_p)

    @pl.when(pl.program_id(0) == n_i - 1)
    def _():
        run(a_s, r_s, x_s, h_s)


def _proj_res(a_p, a_s, weight, res, gain, *, tm, name):
    w_bf, layer = weight
    k_dim = a_p.shape[1]
    if layer is None:
        w_spec = pl.BlockSpec((k_dim, D_MODEL), lambda i: (0, 0), pipeline_mode=pl.Buffered(1))
    else:
        w_spec = pl.BlockSpec((None, k_dim, D_MODEL), lambda i: (layer, 0, 0), pipeline_mode=pl.Buffered(1))
    n_i = M_PROMPT // tm
    with_norm = gain is not None
    row = lambda i: (i, 0)
    fixed = lambda i: (0, 0)
    rows_p = pl.BlockSpec((tm, D_MODEL), row)
    rows_s = pl.BlockSpec((DEC_BATCH, D_MODEL), fixed)
    in_specs = [pl.BlockSpec((tm, k_dim), row), pl.BlockSpec((DEC_BATCH, k_dim), fixed),
                w_spec, rows_p, rows_s]
    args = [a_p, a_s, w_bf, res[0], res[1]]
    out_specs = [rows_p, rows_s]
    out_shape = [jax.ShapeDtypeStruct((M_PROMPT, D_MODEL), F32), jax.ShapeDtypeStruct((DEC_BATCH, D_MODEL), F32)]
    if with_norm:
        in_specs.append(pl.BlockSpec((1, D_MODEL), fixed))
        args.append(gain.reshape(1, D_MODEL))
        out_specs += [rows_p, rows_s]
        out_shape += [jax.ShapeDtypeStruct((M_PROMPT, D_MODEL), BF16),
                      jax.ShapeDtypeStruct((DEC_BATCH, D_MODEL), BF16)]
    return pl.pallas_call(
        functools.partial(_proj_res_kernel, n_i=n_i, with_norm=with_norm),
        grid=(n_i,),
        in_specs=in_specs,
        out_specs=out_specs,
        out_shape=out_shape,
        compiler_params=_cparams("arbitrary"),
        name=name,
    )(*args)


def _split_cat(x):
    hi = x.astype(BF16)
    lo = (x - hi.astype(F32)).astype(BF16)
    return jnp.concatenate([hi, lo], axis=1)


def _rope_mats():
    j = jnp.arange(2 * LANES)[:, None] % LANES
    l = jnp.arange(LANES)[None, :]
    head_sum = (j // HEAD_DIM == l // HEAD_DIM).astype(BF16)
    src = jnp.where(l % HEAD_DIM < HEAD_DIM // 2, l + HEAD_DIM // 2, l - HEAD_DIM // 2)
    half_swap = (j == src).astype(BF16)
    return head_sum, half_swap


def _norm_rope_chunk(xc, gain, cos, sin, hs_ref, sw_ref):
    ms = _dot(_split_cat(xc * xc), hs_ref[...]) * (1.0 / HEAD_DIM)
    y = (xc * lax.rsqrt(ms + EPS)) * gain
    swapped = _dot(_split_cat(y), sw_ref[...])
    return y * cos + swapped * sin


def _spread_heads(xc, own_lo):
    lane = lax.broadcasted_iota(jnp.int32, xc.shape, 1)
    keep = (lane < HEAD_DIM) if own_lo else (lane >= HEAD_DIM)
    nat = jnp.where(keep, xc, 0.0)
    rol = pltpu.roll(nat, HEAD_DIM, axis=1)
    parts = (nat, rol) if own_lo else (rol, nat)
    return jnp.concatenate(parts, axis=0).astype(BF16)


def _attn_core(qs, kfull, vfull, mask_t, sink_ref):
    rows = qs[0].shape[0]
    nkeys = kfull.shape[0]
    outs = [None] * 8
    for kh in range(A_KV_HEADS):
        c0 = (kh // 2) * LANES
        kk = _spread_heads(kfull[:, c0:c0 + LANES], kh % 2 == 0)
        vv = _spread_heads(vfull[:, c0:c0 + LANES], kh % 2 == 0)
        lhs = jnp.concatenate([qs[2 * kh], qs[2 * kh + 1]], axis=0)
        s = _dot_nt(kk, lhs)
        prow = []
        for half in range(2):
            pcol = []
            for cc in range(2):
                sb = s[half * nkeys:(half + 1) * nkeys, cc * rows:(cc + 1) * rows]
                sb = jnp.where(mask_t, sb, NEG_INF)
                sink = sink_ref[kh * 4 + 2 * cc + half]
                m = jnp.maximum(jnp.max(sb, axis=0, keepdims=True), sink)
                e = jnp.exp(sb - m)
                den = jnp.sum(e, axis=0, keepdims=True) + jnp.exp(sink - m)
                pcol.append((e / den).astype(BF16))
            prow.append(jnp.concatenate(pcol, axis=1))
        p = jnp.concatenate(prow, axis=0)
        o = _dot_tn(p, vv)
        outs[2 * kh] = o[0:rows]
        outs[2 * kh + 1] = o[rows:2 * rows]
    return outs


def _even_prompt_kernel(sink_ref, z_ref, cos_ref, sin_ref, qg_ref, kg_ref, cw_ref, hs_ref, sw_ref, wo_ref,
                        o_ref, nk_ref, nv_ref, nc_ref, wo_bf_ref, kf_ref, vf_ref, ub_ref):
    i = pl.program_id(1)
    wo_bf_ref[...] = wo_ref[...].astype(BF16)

    @pl.when(i == 0)
    def _():
        kf_ref[...] = jnp.zeros_like(kf_ref)
        vf_ref[...] = jnp.zeros_like(vf_ref)
        ub_ref[0:8, :] = jnp.zeros((8, CONV_DIM), F32)

    kk = lax.broadcasted_iota(jnp.int32, (2 * BLK, BLK), 0)
    r = lax.broadcasted_iota(jnp.int32, (2 * BLK, BLK), 1)
    d = kk - r
    band = (d >= 0) & (d <= WINDOW)
    o0 = A_WIDTH + 2 * A_KV_WIDTH
    k = v = None
    for t in range(MIX_SUB):
        rows = slice(t * BLK, (t + 1) * BLK)
        kf_ref[0:BLK, :] = kf_ref[BLK:2 * BLK, :]
        vf_ref[0:BLK, :] = vf_ref[BLK:2 * BLK, :]
        cos = cos_ref[rows, :]
        sin = sin_ref[rows, :]
        k = jnp.concatenate(
            [_norm_rope_chunk(z_ref[rows, A_WIDTH + c * LANES:A_WIDTH + (c + 1) * LANES], kg_ref[...], cos, sin,
                              hs_ref, sw_ref)
             for c in range(A_KV_WIDTH // LANES)], axis=1)
        v = z_ref[rows, A_WIDTH + A_KV_WIDTH:A_WIDTH + 2 * A_KV_WIDTH]
        kf_ref[BLK:2 * BLK, :] = k
        vf_ref[BLK:2 * BLK, :] = v
        qs = [(_norm_rope_chunk(z_ref[rows, c * LANES:(c + 1) * LANES], qg_ref[...], cos, sin, hs_ref, sw_ref)
               * (HEAD_DIM ** -0.5)).astype(BF16) for c in range(A_WIDTH // LANES)]
        mask_t = band & ((kk >= BLK) | (i > 0)) if t == 0 else band
        outs = _attn_core(qs, kf_ref[...], vf_ref[...], mask_t, sink_ref)
        for c in range(A_WIDTH // LANES):
            o_ref[rows, c * LANES:(c + 1) * LANES] = outs[c].astype(o_ref.dtype)

        u = z_ref[rows, o0 + CONV_DIM:o0 + 2 * CONV_DIM] * z_ref[rows, o0 + 2 * CONV_DIM:o0 + 3 * CONV_DIM]
        ub_ref[8:8 + BLK, :] = u
        y = ub_ref[6:6 + BLK, :] * cw_ref[0:1, :]
        y = y + ub_ref[7:7 + BLK, :] * cw_ref[1:2, :]
        y = y + ub_ref[8:8 + BLK, :] * cw_ref[2:3, :]
        o_ref[rows, A_WIDTH:A_WIDTH + CONV_DIM] = (z_ref[rows, o0:o0 + CONV_DIM] * y).astype(o_ref.dtype)
        ub_ref[0:8, :] = ub_ref[BLK:BLK + 8, :]

    @pl.when(i == pl.num_programs(1) - 1)
    def _():
        nk_ref[...] = k
        nv_ref[...] = v
        nc_ref[...] = ub_ref[BLK + 6:BLK + 8, :]


def _w_slab_specs(w_out, layer, n_step):
    slab = w_out.shape[1] // (BATCH * n_step)
    return (pl.BlockSpec((None, slab, w_out.shape[2]), lambda b, i: (layer, b * n_step + i, 0)),
            pl.BlockSpec((slab, w_out.shape[2]), lambda b, i: (b * n_step + i, 0)),
            jax.ShapeDtypeStruct(w_out.shape[1:], BF16))


def _even_prompt(z_p, cos, sin, qg, kg, cw, sinks, mats, w_out, layer):
    n_step = NBLK // MIX_SUB
    step_rows = MIX_SUB * BLK
    blk = lambda b, i: (b * n_step + i, 0)
    fixed = lambda b, i: (0, 0)
    per_b = lambda b, i: (b, 0, 0)
    wo_in, wo_out, wo_shape = _w_slab_specs(w_out, layer, n_step)
    return pl.pallas_call(
        _even_prompt_kernel,
        grid=(BATCH, n_step),
        in_specs=[pl.BlockSpec(memory_space=pltpu.SMEM),
                  pl.BlockSpec((step_rows, EVEN_IN), blk),
                  pl.BlockSpec((step_rows, LANES), lambda b, i: (i, 0)),
                  pl.BlockSpec((step_rows, LANES), lambda b, i: (i, 0)),
                  pl.BlockSpec((1, LANES), fixed),
                  pl.BlockSpec((1, LANES), fixed),
                  pl.BlockSpec((CONV_W, CONV_DIM), fixed),
                  pl.BlockSpec((2 * LANES, LANES), fixed),
                  pl.BlockSpec((2 * LANES, LANES), fixed),
                  wo_in],
        out_specs=[pl.BlockSpec((step_rows, D_MODEL), blk),
                   pl.BlockSpec((None, BLK, A_KV_WIDTH), per_b),
                   pl.BlockSpec((None, BLK, A_KV_WIDTH), per_b),
                   pl.BlockSpec((None, CONV_W - 1, CONV_DIM), per_b),
                   wo_out],
        out_shape=[jax.ShapeDtypeStruct((M_PROMPT, D_MODEL), BF16),
                   jax.ShapeDtypeStruct((BATCH, BLK, A_KV_WIDTH), F32),
                   jax.ShapeDtypeStruct((BATCH, BLK, A_KV_WIDTH), F32),
                   jax.ShapeDtypeStruct((BATCH, CONV_W - 1, CONV_DIM), F32),
                   wo_shape],
        scratch_shapes=[pltpu.VMEM((2 * BLK, A_KV_WIDTH), F32),
                        pltpu.VMEM((2 * BLK, A_KV_WIDTH), F32),
                        pltpu.VMEM((BLK + 8, CONV_DIM), F32)],
        compiler_params=_cparams("arbitrary", "arbitrary"),
        name="even_prompt",
    )(sinks, z_p, cos, sin, qg, kg, cw, *mats, w_out)


SROWS = LANES
N_SAMPLE_IN = 12


def _even_sample_kernel(*refs):
    (sink_ref, z_ref, cos_ref, sin_ref, qg_ref, kg_ref, cs_ref, cw_ref, hs_ref, sw_ref,
     kc_ref, vc_ref) = refs[:N_SAMPLE_IN]
    m_ref, ncs_ref, nk_ref, nv_ref, q_all, k_all = refs[-6:]
    b = pl.program_id(0)
    lc = kc_ref.shape[0]

    @pl.when(b == 0)
    def _():
        cos = cos_ref[...]
        sin = sin_ref[...]
        for c in range(A_WIDTH // LANES):
            q = _norm_rope_chunk(z_ref[:, c * LANES:(c + 1) * LANES], qg_ref[...], cos, sin, hs_ref, sw_ref)
            q_all[:, c * LANES:(c + 1) * LANES] = q * (HEAD_DIM ** -0.5)
        for c in range(A_KV_WIDTH // LANES):
            k_all[:, c * LANES:(c + 1) * LANES] = _norm_rope_chunk(
                z_ref[:, A_WIDTH + c * LANES:A_WIDTH + (c + 1) * LANES], kg_ref[...], cos, sin, hs_ref, sw_ref)
        o0 = A_WIDTH + 2 * A_KV_WIDTH
        u = z_ref[:, o0 + CONV_DIM:o0 + 2 * CONV_DIM] * z_ref[:, o0 + 2 * CONV_DIM:o0 + 3 * CONV_DIM]
        y = cs_ref[:, 0:CONV_DIM] * cw_ref[0:1, :]
        y = y + cs_ref[:, CONV_DIM:2 * CONV_DIM] * cw_ref[1:2, :]
        y = y + u * cw_ref[2:3, :]
        m_ref[:, A_WIDTH:A_WIDTH + CONV_DIM] = z_ref[:, o0:o0 + CONV_DIM] * y
        m_ref[:, 0:A_WIDTH] = jnp.zeros((DEC_BATCH, A_WIDTH), F32)
        ncs_ref[:, 0:CONV_DIM] = cs_ref[:, CONV_DIM:2 * CONV_DIM]
        ncs_ref[:, CONV_DIM:2 * CONV_DIM] = u

    grp = pl.ds(pl.multiple_of((b // 8) * 8, 8), 8)
    mine = lax.broadcasted_iota(jnp.int32, (8, 1), 0) == b % 8

    def pick(tile):
        return jnp.sum(jnp.where(mine, tile, 0.0), axis=0, keepdims=True)

    kn = pick(k_all[grp, :])
    vn = pick(z_ref[grp, A_WIDTH + A_KV_WIDTH:A_WIDTH + 2 * A_KV_WIDTH])
    row = lax.broadcasted_iota(jnp.int32, (lc, A_KV_WIDTH), 0)
    kfull = jnp.concatenate([kc_ref[...], jnp.where(row == 0, kn, 0.0)], axis=0)
    vfull = jnp.concatenate([vc_ref[...], jnp.where(row == 0, vn, 0.0)], axis=0)
    qs = [jnp.broadcast_to(pick(q_all[grp, c * LANES:(c + 1) * LANES]), (SROWS, LANES)).astype(BF16)
          for c in range(A_WIDTH // LANES)]
    kk = lax.broadcasted_iota(jnp.int32, (2 * lc, SROWS), 0)
    mask_t = (kk <= lc) & (lc - kk <= WINDOW)
    outs = _attn_core(qs, kfull, vfull, mask_t, sink_ref)
    for c in range(A_WIDTH // LANES):
        cols = slice(c * LANES, (c + 1) * LANES)
        m_ref[grp, cols] = jnp.where(mine, outs[c][0:8], m_ref[grp, cols])
    nk_ref[0:lc - 1, :] = kc_ref[1:lc, :]
    nk_ref[lc - 1:lc, :] = kn
    nv_ref[0:lc - 1, :] = vc_ref[1:lc, :]
    nv_ref[lc - 1:lc, :] = vn


def _even_sample(sinks, z_s, cos, sin, qg, kg, conv_state, cw, mats, k_cache, v_cache, li, prev):
    lc = k_cache.shape[2]
    whole = lambda a: pl.BlockSpec(a.shape, lambda b: (0,) * a.ndim)
    cache = pl.BlockSpec((None, None, lc, A_KV_WIDTH), lambda b: (li, b, 0, 0))
    cs = conv_state.reshape(DEC_BATCH, (CONV_W - 1) * CONV_DIM)
    args = [sinks, z_s, cos, sin, qg, kg, cs, cw, *mats, k_cache, v_cache]
    assert len(args) == N_SAMPLE_IN
    in_specs = [pl.BlockSpec(memory_space=pltpu.SMEM)] + [whole(a) for a in args[1:N_SAMPLE_IN - 2]] + [cache, cache]
    aliases = {}
    if prev is not None:
        aliases = {len(args): 2, len(args) + 1: 3}
        in_specs += [pl.BlockSpec(memory_space=pl.ANY)] * 2
        args += list(prev)
    rows = lambda w: pl.BlockSpec((DEC_BATCH, w), lambda b: (0, 0))
    m_s, ncs, nk, nv = pl.pallas_call(
        _even_sample_kernel,
        grid=(DEC_BATCH,),
        in_specs=in_specs,
        out_specs=[rows(D_MODEL), rows((CONV_W - 1) * CONV_DIM), cache, cache],
        out_shape=[jax.ShapeDtypeStruct((DEC_BATCH, D_MODEL), F32),
                   jax.ShapeDtypeStruct((DEC_BATCH, (CONV_W - 1) * CONV_DIM), F32),
                   jax.ShapeDtypeStruct(k_cache.shape, F32),
                   jax.ShapeDtypeStruct(v_cache.shape, F32)],
        scratch_shapes=[pltpu.VMEM((DEC_BATCH, A_WIDTH), F32), pltpu.VMEM((DEC_BATCH, A_KV_WIDTH), F32)],
        input_output_aliases=aliases,
        compiler_params=_cparams("arbitrary"),
        name="even_sample",
    )(*args)
    return m_s, ncs.reshape(DEC_BATCH, CONV_W - 1, CONV_DIM), nk, nv


def _log_decay(zr, wa_ref, ba_ref):
    pre = _dot(zr.astype(BF16), wa_ref[...]) + ba_ref[...]
    return (jnp.minimum(pre, 0.0) - jnp.log1p(jnp.exp(-jnp.abs(pre)))) * (1.0 / GLA_TAU)


def _split_bf16(x):
    hi = x.astype(BF16)
    lo = (x - hi.astype(F32)).astype(BF16)
    return hi, lo


def _row_to_col(row):
    n = row.shape[1]
    r = lax.broadcasted_iota(jnp.int32, (n, n), 0)
    c = lax.broadcasted_iota(jnp.int32, (n, n), 1)
    return jnp.sum(jnp.where(r == c, jnp.broadcast_to(row, (n, n)), 0.0), axis=-1, keepdims=True)


def _head_rmsnorm_gate(o, gain, zg):
    ms = jnp.mean(o * o, axis=-1, keepdims=True)
    return ((o * lax.rsqrt(ms + EPS)) * gain) * _silu(zg)


def _odd_prompt_kernel(zm_ref, zpr_ref, wa_ref, ba_ref, gain_ref, wp_ref, ps_ref, tri_ref, band_ref, wo_ref,
                       o_ref, s_ref, np_ref, wo_bf_ref, pbh_ref, pbl_ref):
    i = pl.program_id(1)
    wo_bf_ref[...] = wo_ref[...].astype(BF16)

    @pl.when(i == 0)
    def _():
        s_ref[...] = jnp.zeros_like(s_ref)
        pbh_ref[0:BLK, :] = jnp.zeros((BLK, POOL_DIM), BF16)
        pbl_ref[0:BLK, :] = jnp.zeros((BLK, POOL_DIM), BF16)

    x = None
    for t in range(MIX_SUB):
        rows = pl.ds(t * BLK, BLK)
        x = _odd_block(zm_ref.at[rows], zpr_ref.at[rows], wa_ref, ba_ref, gain_ref, wp_ref, ps_ref,
                       tri_ref, band_ref, o_ref.at[rows], s_ref, pbh_ref, pbl_ref, (i * MIX_SUB + t) * BLK)

    @pl.when(i == pl.num_programs(1) - 1)
    def _():
        np_ref[...] = x[BLK - POOL_BUF:BLK, :]


def _odd_block(zm_ref, zpr_ref, wa_ref, ba_ref, gain_ref, wp_ref, ps_ref, tri_ref, band_ref,
               o_ref, s_ref, pbh_ref, pbl_ref, t0):
    n_ch = BLK // GLA_CHUNK
    gk = _log_decay(zpr_ref[:, POOL_DIM:POOL_RANK_W], wa_ref, ba_ref)
    r = lax.broadcasted_iota(jnp.int32, (BLK, BLK), 0)
    c = lax.broadcasted_iota(jnp.int32, (BLK, BLK), 1)
    causal = (r >= c) & (r // GLA_CHUNK == c // GLA_CHUNK)
    g_hi, g_lo = _split_bf16(gk)
    b = _dot(tri_ref[...], jnp.concatenate([g_hi, g_lo], axis=0))
    b_last = [b[(ch + 1) * GLA_CHUNK - 1:(ch + 1) * GLA_CHUNK, :] for ch in range(n_ch)]
    b_end = jnp.concatenate([jnp.broadcast_to(bl, (GLA_CHUNK, GLA_K_WIDTH)) for bl in b_last], axis=0)
    o_k, o_v, o_g = GLA_K_WIDTH, 2 * GLA_K_WIDTH, 2 * GLA_K_WIDTH + GLA_WIDTH
    zq = zm_ref[:, 0:GLA_K_WIDTH]
    zk = zm_ref[:, o_k:o_k + GLA_K_WIDTH]
    qd = ((zq * (GLA_DK ** -0.5)) * jnp.exp(b)).astype(BF16)
    kd = (zk * jnp.exp(-b)).astype(BF16)
    k2 = (zk * jnp.exp(b_end - b)).astype(BF16)
    sub = lax.broadcasted_iota(jnp.int32, (8, LANES), 0)
    sel = jnp.where(sub < 2, 1.0, 0.0).astype(BF16)
    hk = [slice(h * GLA_DK, (h + 1) * GLA_DK) for h in range(GLA_HEADS)]
    chunk = [slice(ch * GLA_CHUNK, (ch + 1) * GLA_CHUNK) for ch in range(n_ch)]
    v_hs = [zm_ref[:, o_v + h * GLA_DV:o_v + (h + 1) * GLA_DV].astype(BF16) for h in range(GLA_HEADS)]
    atts = [_dot_nt(qd[:, hk[h]], kd[:, hk[h]]) for h in range(GLA_HEADS)]
    log_decs = []
    for h in range(GLA_HEADS):
        for ch in range(n_ch):
            bl = b_last[ch][:, hk[h]]
            bl_hi = bl.astype(BF16).astype(F32)
            rows8 = jnp.where(sub == 0, bl_hi, jnp.where(sub == 1, bl - bl_hi, 0.0)).astype(BF16)
            log_decs.append(_dot_tn(rows8, sel))
    deltas = [[_dot_tn(k2[chunk[ch], hk[h]], v_hs[h][chunk[ch]]) for ch in range(n_ch)]
              for h in range(GLA_HEADS)]
    o_intras = [_dot(jnp.where(causal, atts[h], 0.0).astype(BF16), v_hs[h]) for h in range(GLA_HEADS)]
    o_inters = []
    for h in range(GLA_HEADS):
        s_h = s_ref[h]
        parts = []
        for ch in range(n_ch):
            parts.append(_dot(qd[chunk[ch], hk[h]], s_h.astype(BF16)))
            dec = jnp.exp(log_decs[h * n_ch + ch])
            s_h = jnp.concatenate([dec, dec], axis=1) * s_h + deltas[h][ch]
        s_ref[h] = s_h
        o_inters.append(jnp.concatenate(parts, axis=0))
    for h in range(GLA_HEADS):
        vs = slice(h * GLA_DV, (h + 1) * GLA_DV)
        zg = zm_ref[:, o_g + h * GLA_DV:o_g + (h + 1) * GLA_DV]
        o_ref[:, vs] = _head_rmsnorm_gate(o_intras[h] + o_inters[h], gain_ref[...], zg).astype(o_ref.dtype)

    x = zpr_ref[:, 0:POOL_DIM]
    x_hi, x_lo = _split_bf16(x)
    pbh_ref[BLK:2 * BLK, :] = x_hi
    pbl_ref[BLK:2 * BLK, :] = x_lo
    t_glob = t0 + lax.broadcasted_iota(jnp.int32, (BLK, 1), 0)
    groups = [slice(g * POOL_GDIM, (g + 1) * POOL_GDIM) for g in range(len(POOL_WINDOWS))]
    sums = [_dot(band_ref[g], jnp.concatenate([pbh_ref[:, groups[g]], pbl_ref[:, groups[g]]], axis=0))
            for g in range(len(POOL_WINDOWS))]
    ys = []
    for g, w in enumerate(POOL_WINDOWS):
        inv_cnt = 1.0 / jnp.minimum(w, t_glob + 1).astype(F32)
        d = sums[g] * inv_cnt - x[:, groups[g]]
        ys.append(_dot(d.astype(BF16), wp_ref[g]))
    for g in range(len(POOL_WINDOWS)):
        o_ref[:, GLA_WIDTH + g * POOL_GDIM:GLA_WIDTH + (g + 1) * POOL_GDIM] = (
            ys[g] * ps_ref[:, groups[g]]).astype(o_ref.dtype)
    pbh_ref[0:BLK, :] = x_hi
    pbl_ref[0:BLK, :] = x_lo
    return x


def _odd_block_mats():
    r = jnp.arange(BLK)[:, None]
    c = jnp.arange(BLK)[None, :]
    tri = ((r >= c) & (r // GLA_CHUNK == c // GLA_CHUNK)).astype(BF16)
    back = r + BLK - jnp.arange(2 * BLK)[None, :]
    bands = [((back >= 0) & (back < w)).astype(BF16) for w in POOL_WINDOWS]
    return (jnp.concatenate([tri, tri], axis=1),
            jnp.stack([jnp.concatenate([bd, bd], axis=1) for bd in bands]))


def _odd_prompt(zm_p, zpr_p, wa, ba, gain, wp, ps, w_out, layer):
    n_step = NBLK // MIX_SUB
    step_rows = MIX_SUB * BLK
    blk = lambda b, i: (b * n_step + i, 0)
    fixed2 = lambda b, i: (0, 0)
    wo_in, wo_out, wo_shape = _w_slab_specs(w_out, layer, n_step)
    tri2, bands2 = _odd_block_mats()
    return pl.pallas_call(
        _odd_prompt_kernel,
        grid=(BATCH, n_step),
        in_specs=[pl.BlockSpec((step_rows, ODD_MAIN), blk),
                  pl.BlockSpec((step_rows, POOL_RANK_W), blk),
                  pl.BlockSpec((LANES, GLA_K_WIDTH), fixed2),
                  pl.BlockSpec((1, GLA_K_WIDTH), fixed2),
                  pl.BlockSpec((1, GLA_DV), fixed2),
                  pl.BlockSpec((4, POOL_GDIM, POOL_GDIM), lambda b, i: (0, 0, 0)),
                  pl.BlockSpec((1, POOL_DIM), fixed2),
                  pl.BlockSpec(tri2.shape, fixed2),
                  pl.BlockSpec(bands2.shape, lambda b, i: (0, 0, 0)),
                  wo_in],
        out_specs=[pl.BlockSpec((step_rows, D_MODEL), blk),
                   pl.BlockSpec((None, GLA_HEADS, GLA_DK, GLA_DV), lambda b, i: (b, 0, 0, 0)),
                   pl.BlockSpec((None, POOL_BUF, POOL_DIM), lambda b, i: (b, 0, 0)),
                   wo_out],
        out_shape=[jax.ShapeDtypeStruct((M_PROMPT, D_MODEL), BF16),
                   jax.ShapeDtypeStruct((BATCH, GLA_HEADS, GLA_DK, GLA_DV), F32),
                   jax.ShapeDtypeStruct((BATCH, POOL_BUF, POOL_DIM), F32),
                   wo_shape],
        scratch_shapes=[pltpu.VMEM((2 * BLK, POOL_DIM), BF16), pltpu.VMEM((2 * BLK, POOL_DIM), BF16)],
        compiler_params=_cparams("arbitrary", "arbitrary"),
        name="odd_prompt",
    )(zm_p, zpr_p, wa, ba, gain, wp, ps, tri2, bands2, w_out)


STATE_BB = 8
N_ODD_SAMPLE_IN = 9


def _odd_sample_kernel(*refs):
    zm_ref, zpr_ref, wa_ref, ba_ref, gain_ref, wp_ref, ps_ref, pool_ref, s_ref = refs[:N_ODD_SAMPLE_IN]
    m_ref, np_ref, ns_ref, dec_all, q_all, o_rows = refs[-6:]
    step = pl.program_id(0)
    n_prev = pool_ref.shape[1] // POOL_DIM

    @pl.when(step == 0)
    def _():
        dec_all[...] = jnp.exp(_log_decay(zpr_ref[:, POOL_DIM:POOL_RANK_W], wa_ref, ba_ref))
        q_all[...] = zm_ref[:, 0:GLA_K_WIDTH] * (GLA_DK ** -0.5)
        for g, w in enumerate(POOL_WINDOWS):
            gs = slice(g * POOL_GDIM, (g + 1) * POOL_GDIM)
            x = zpr_ref[:, gs]
            first = n_prev - (w - 1)
            s = pool_ref[:, first * POOL_DIM + g * POOL_GDIM:first * POOL_DIM + (g + 1) * POOL_GDIM]
            for jj in range(first + 1, n_prev):
                s = s + pool_ref[:, jj * POOL_DIM + g * POOL_GDIM:jj * POOL_DIM + (g + 1) * POOL_GDIM]
            s = s + x
            d = s / float(min(w, n_prev + 1)) - x
            m_ref[:, GLA_WIDTH + g * POOL_GDIM:GLA_WIDTH + (g + 1) * POOL_GDIM] = (
                _dot(d.astype(BF16), wp_ref[g]) * ps_ref[:, gs])
        np_ref[:, 0:(n_prev - 1) * POOL_DIM] = pool_ref[:, POOL_DIM:n_prev * POOL_DIM]
        np_ref[:, (n_prev - 1) * POOL_DIM:n_prev * POOL_DIM] = zpr_ref[:, 0:POOL_DIM]

    grp = pl.ds(pl.multiple_of(step * STATE_BB, STATE_BB), STATE_BB)
    dec = dec_all[grp, :]
    q = q_all[grp, :]
    k = zm_ref[grp, GLA_K_WIDTH:2 * GLA_K_WIDTH]
    v = zm_ref[grp, 2 * GLA_K_WIDTH:2 * GLA_K_WIDTH + GLA_WIDTH]
    for bb in range(STATE_BB):
        for h in range(GLA_HEADS):
            ks = slice(h * GLA_DK, (h + 1) * GLA_DK)
            vs = slice(h * GLA_DV, (h + 1) * GLA_DV)
            s_new = (_row_to_col(dec[bb:bb + 1, ks]) * s_ref[bb, h]
                     + _row_to_col(k[bb:bb + 1, ks]) * v[bb:bb + 1, vs])
            ns_ref[bb, h] = s_new
            o_rows[bb:bb + 1, vs] = jnp.sum(_row_to_col(q[bb:bb + 1, ks]) * s_new, axis=0, keepdims=True)
    o_g = 2 * GLA_K_WIDTH + GLA_WIDTH
    for h in range(GLA_HEADS):
        vs = slice(h * GLA_DV, (h + 1) * GLA_DV)
        zg = zm_ref[grp, o_g + h * GLA_DV:o_g + (h + 1) * GLA_DV]
        m_ref[grp, vs] = _head_rmsnorm_gate(o_rows[:, vs], gain_ref[...], zg)


def _odd_sample(zm_s, zpr_s, wa, ba, gain, wp, ps, pool_state, state, li, prev):
    whole = lambda a: pl.BlockSpec(a.shape, lambda b: (0,) * a.ndim)
    st = pl.BlockSpec((None, STATE_BB, GLA_HEADS, GLA_DK, GLA_DV), lambda b: (li, b, 0, 0, 0))
    pool2d = pool_state.reshape(DEC_BATCH, POOL_BUF * POOL_DIM)
    args = [zm_s, zpr_s, wa, ba, gain, wp, ps, pool2d, state]
    assert len(args) == N_ODD_SAMPLE_IN
    in_specs = [whole(a) for a in args[:-1]] + [st]
    aliases = {}
    if prev is not None:
        aliases = {len(args): 2}
        in_specs.append(pl.BlockSpec(memory_space=pl.ANY))
        args.append(prev)
    rows = lambda w: pl.BlockSpec((DEC_BATCH, w), lambda b: (0, 0))
    m_s, npool, ns = pl.pallas_call(
        _odd_sample_kernel,
        grid=(DEC_BATCH // STATE_BB,),
        in_specs=in_specs,
        out_specs=[rows(D_MODEL), rows(POOL_BUF * POOL_DIM), st],
        out_shape=[jax.ShapeDtypeStruct((DEC_BATCH, D_MODEL), F32),
                   jax.ShapeDtypeStruct((DEC_BATCH, POOL_BUF * POOL_DIM), F32),
                   jax.ShapeDtypeStruct(state.shape, F32)],
        scratch_shapes=[pltpu.VMEM((DEC_BATCH, GLA_K_WIDTH), F32), pltpu.VMEM((DEC_BATCH, GLA_K_WIDTH), F32),
                        pltpu.VMEM((STATE_BB, GLA_WIDTH), F32)],
        input_output_aliases=aliases,
        compiler_params=_cparams("arbitrary"),
        name="odd_sample",
    )(*args)
    return m_s, npool.reshape(DEC_BATCH, POOL_BUF, POOL_DIM), ns


def _rope_tables(pos):
    half = HEAD_DIM // 2
    inv = jnp.power(ROPE_THETA, -jnp.arange(half, dtype=F32) / half)
    ang = pos.astype(F32)[:, None] * inv[None, :]
    c, s = jnp.cos(ang), jnp.sin(ang)
    return jnp.tile(c, (1, 4)), jnp.tile(jnp.concatenate([-s, s], axis=1), (1, 2))


def kernel(x_prompt, x_sample, cache_swa_k, cache_swa_v, state_conv, state_gla, state_pool, norm_mix, norm_ffn, w_in_even, w_out_even, q_norm, k_norm, attn_sinks, conv_w, w_in_odd, w_out_odd, w_alpha_up, b_alpha, gla_out_norm, w_pool, pool_scale, w_gate, w_up, w_down):
    lc = cache_swa_k.shape[2]
    assert x_prompt.shape == (BATCH, SEQ, D_MODEL) and x_sample.shape == (DEC_BATCH, 1, D_MODEL)
    assert lc == BLK == WINDOW, "the prompt's new cache is its last attention block"
    assert state_pool.shape[2] == POOL_BUF and state_conv.shape[2] == CONV_W - 1
    x_p = x_prompt.reshape(M_PROMPT, D_MODEL)
    x_s = x_sample.reshape(DEC_BATCH, D_MODEL)
    cos_p, sin_p = _rope_tables(jnp.arange(SEQ))
    cos_s, sin_s = _rope_tables(PAST_LEN + jnp.arange(1))
    mats = _rope_mats()
    kc_all = cache_swa_k.reshape(N_EVEN, DEC_BATCH, lc, A_KV_WIDTH)
    vc_all = cache_swa_v.reshape(N_EVEN, DEC_BATCH, lc, A_KV_WIDTH)
    o_r = ODD_MAIN
    w_odd_t = jnp.swapaxes(w_in_odd, 1, 2)
    w_odd_pr_t = jnp.concatenate([w_odd_t[:, o_r + GLA_RANK:], w_odd_t[:, o_r:o_r + GLA_RANK],
                                  jnp.zeros((N_ODD, LANES - GLA_RANK, D_MODEL), F32)], axis=1)
    pk, pv, pc, pg, pp, sc, sp = ([] for _ in range(7))
    sk_all = sv_all = sg_all = None

    h_p, h_s = _rmsnorm(x_p, x_s, norm_mix[0])
    for layer in range(DEPTH):
        li = layer // 2
        if layer % 2 == 0:
            z_p, z_s = _dense(h_p, h_s, [(w_in_even, li)], n_cols=EVEN_IN, **TILES["in_even"], name="in_even")
            qg = jnp.tile(q_norm[li], 2).reshape(1, LANES)
            kg = jnp.tile(k_norm[li], 2).reshape(1, LANES)
            m_p, nk, nv, nc, w_out_bf = _even_prompt(z_p, cos_p, sin_p, qg, kg, conv_w[li], attn_sinks[li], mats,
                                                     w_out_even, li)
            pk.append(nk.reshape(BATCH, lc, A_KV_HEADS, HEAD_DIM))
            pv.append(nv.reshape(BATCH, lc, A_KV_HEADS, HEAD_DIM))
            pc.append(nc)
            m_s, nc_s, sk_all, sv_all = _even_sample(
                attn_sinks[li], z_s, cos_s, sin_s, qg, kg, state_conv[li], conv_w[li], mats, kc_all, vc_all, li,
                None if sk_all is None else (sk_all, sv_all))
            sc.append(nc_s)
        else:
            zm_p, zm_s = _dense(h_p, h_s, [(w_odd_t, li)], n_cols=ODD_MAIN, **TILES["in_odd_main"],
                                transposed=True, name="in_odd_main")
            zpr_p, zpr_s = _dense(h_p, h_s, [(w_odd_pr_t, li)], n_cols=POOL_RANK_W, **TILES["in_odd_pool_rank"],
                                  transposed=True, name="in_odd_pool_rank")
            wa = jnp.pad(w_alpha_up[li], ((0, LANES - GLA_RANK), (0, 0))).astype(BF16)
            ba = b_alpha[li].reshape(1, GLA_K_WIDTH)
            gain = gla_out_norm[li].reshape(1, GLA_DV)
            wp = w_pool[li].astype(BF16)
            ps = pool_scale[li].reshape(1, POOL_DIM)
            m_p, ng, npool, w_out_bf = _odd_prompt(zm_p, zpr_p, wa, ba, gain, wp, ps, w_out_odd, li)
            pg.append(ng)
            pp.append(npool)
            m_s, np_s, sg_all = _odd_sample(zm_s, zpr_s, wa, ba, gain, wp, ps, state_pool[li], state_gla, li,
                                            sg_all)
            sp.append(np_s)
        x_p, x_s, h_p, h_s = _proj_res(m_p, m_s, (w_out_bf, None), (x_p, x_s), norm_ffn[layer],
                                       **TILES["out_proj"], name="out_proj")
        a_p, a_s, w_down_bf = _dense(h_p, h_s, [(w_gate, layer), (w_up, layer)], n_cols=D_FF, **TILES["ffn_up"],
                                     mode="swiglu", out_dtype=BF16, side_cast=(w_down, layer), name="ffn_up")
        if layer + 1 < DEPTH:
            x_p, x_s, h_p, h_s = _proj_res(a_p, a_s, (w_down_bf, None), (x_p, x_s),
                                           norm_mix[layer + 1], **TILES["ffn_down"], name="ffn_down")
        else:
            x_p, x_s = _proj_res(a_p, a_s, (w_down_bf, None), (x_p, x_s), None,
                                 **TILES["ffn_down"], name="ffn_down_last")

    st = lambda parts: jnp.stack(parts)
    cache5 = lambda a: a.reshape(N_EVEN, DEC_BATCH, lc, A_KV_HEADS, HEAD_DIM)
    return (x_p.reshape(BATCH, SEQ, D_MODEL), x_s.reshape(DEC_BATCH, 1, D_MODEL),
            st(pk), st(pv), st(pc), st(pg), st(pp), cache5(sk_all), cache5(sv_all), st(sc), sg_all, st(sp))
```

```python
import functools

import jax
import jax.numpy as jnp
from jax import lax
from jax.experimental import pallas as pl
from jax.experimental.pallas import tpu as pltpu

F32 = jnp.float32
BF16 = jnp.bfloat16

D_MODEL = 2048
BATCH = 4
SEQ = 2048
DEPTH = 4
DEC_BATCH = 32
PAST_LEN = 16384
N_EVEN = 2
N_ODD = 2
EPS = 1e-6
NEG_INF = -1e30
A_HEADS = 16
A_KV_HEADS = 4
HEAD_DIM = 64
A_WIDTH = A_HEADS * HEAD_DIM
A_KV_WIDTH = A_KV_HEADS * HEAD_DIM
WINDOW = 128
ROPE_THETA = 10000.0
CONV_DIM = D_MODEL // 2
CONV_W = 3
GLA_HEADS = 4
GLA_WIDTH = D_MODEL // 2
GLA_DV = GLA_WIDTH // GLA_HEADS
GLA_DK = GLA_DV // 2
GLA_K_WIDTH = GLA_HEADS * GLA_DK
GLA_RANK = 16
GLA_TAU = 16.0
GLA_CHUNK = 64
POOL_DIM = D_MODEL // 2
POOL_WINDOWS = (2, 4, 8, 16)
POOL_GDIM = POOL_DIM // 4
POOL_BUF = 15
D_FF = 5632
EVEN_IN = A_WIDTH + 2 * A_KV_WIDTH + 3 * CONV_DIM
ODD_MAIN = 2 * GLA_K_WIDTH + 2 * GLA_WIDTH

M_PROMPT = BATCH * SEQ
LANES = 128
BLK = 128
NBLK = SEQ // BLK
MIX_SUB = 4
POOL_RANK_W = POOL_DIM + LANES
VMEM_LIMIT = 58 * 1024 * 1024

TILES = {
    "in_even": dict(tm=1024, tn=1536),
    "in_odd_main": dict(tm=1024, tn=ODD_MAIN // 2),
    "in_odd_pool_rank": dict(tm=1024, tn=POOL_RANK_W),
    "ffn_up": dict(tm=2048, tn=512, sub=2),
    "out_proj": dict(tm=512),
    "ffn_down": dict(tm=512),
}


def _cparams(*sem):
    return pltpu.CompilerParams(dimension_semantics=sem, vmem_limit_bytes=VMEM_LIMIT)


def _dot(a, b):
    return jnp.dot(a, b, preferred_element_type=F32)


def _dot_nt(a, b):
    return lax.dot_general(a, b, (((1,), (1,)), ((), ())), preferred_element_type=F32)


def _dot_tn(a, b):
    return lax.dot_general(a, b, (((0,), (0,)), ((), ())), preferred_element_type=F32)


def _silu(x):
    return x * (1.0 / (1.0 + jnp.exp(-x)))


def _rmsnorm_rows(x, gain):
    ms = jnp.mean(x * x, axis=-1, keepdims=True)
    return (x * lax.rsqrt(ms + EPS)) * gain


def _rmsnorm_kernel(xp_ref, xs_ref, g_ref, op_ref, os_ref, *, n_i):
    op_ref[...] = _rmsnorm_rows(xp_ref[...], g_ref[...]).astype(op_ref.dtype)

    @pl.when(pl.program_id(0) == n_i - 1)
    def _():
        os_ref[...] = _rmsnorm_rows(xs_ref[...], g_ref[...]).astype(os_ref.dtype)


def _rmsnorm(x_p, x_s, gain, *, tm=1024):
    n_i = M_PROMPT // tm
    row = lambda i: (i, 0)
    fixed = lambda i: (0, 0)
    return pl.pallas_call(
        functools.partial(_rmsnorm_kernel, n_i=n_i),
        grid=(n_i,),
        in_specs=[pl.BlockSpec((tm, D_MODEL), row),
                  pl.BlockSpec((DEC_BATCH, D_MODEL), fixed),
                  pl.BlockSpec((1, D_MODEL), fixed)],
        out_specs=[pl.BlockSpec((tm, D_MODEL), row),
                   pl.BlockSpec((DEC_BATCH, D_MODEL), fixed)],
        out_shape=[jax.ShapeDtypeStruct((M_PROMPT, D_MODEL), BF16),
                   jax.ShapeDtypeStruct((DEC_BATCH, D_MODEL), BF16)],
        compiler_params=_cparams("arbitrary"),
        name="rmsnorm",
    )(x_p, x_s, gain.reshape(1, D_MODEL))


XPOSE_COLS = 128


def _dense_kernel(*refs, n_w, mode, n_i, transposed, side_cast, sub):
    a_p, a_s = refs[0], refs[1]
    w = refs[2:2 + n_w]
    pos = 2 + n_w
    side_in = side_out = None
    if side_cast:
        side_in = refs[pos]
        pos += 1
    o_p, o_s = refs[pos], refs[pos + 1]
    pos += 2
    if side_cast:
        side_out = refs[pos]
        pos += 1
    wbf = refs[pos:pos + n_w]
    i = pl.program_id(1)

    @pl.when(i == 0)
    def _():
        for k in range(n_w):
            if transposed:
                tn = wbf[k].shape[1]
                for c in range(0, tn, XPOSE_COLS):
                    wbf[k][:, c:c + XPOSE_COLS] = w[k][c:c + XPOSE_COLS, :].T.astype(BF16)
            else:
                wbf[k][...] = w[k][...].astype(BF16)
        if side_cast:
            side_out[...] = side_in[...].astype(BF16)

    def run(a_ref, o_ref, n_sub):
        rows = a_ref.shape[0] // n_sub
        for r in range(n_sub):
            a = a_ref[r * rows:(r + 1) * rows, :]
            if mode == "swiglu":
                y = _silu(_dot(a, wbf[0][...])) * _dot(a, wbf[1][...])
            else:
                y = _dot(a, wbf[0][...])
            o_ref[r * rows:(r + 1) * rows, :] = y.astype(o_ref.dtype)

    run(a_p, o_p, sub)

    @pl.when(i == n_i - 1)
    def _():
        run(a_s, o_s, 1)


def _dense(a_p, a_s, weights, *, n_cols, tm, tn, mode="plain", out_dtype=F32, transposed=False,
           side_cast=None, sub=1, name):
    k_dim = a_p.shape[1]
    n_i = M_PROMPT // tm
    n_j = n_cols // tn
    row = lambda j, i: (i, 0)
    fixed = lambda j, i: (0, 0)
    tile = lambda j, i: (i, j)
    panel = lambda j, i: (0, j)
    in_specs = [pl.BlockSpec((tm, k_dim), row), pl.BlockSpec((DEC_BATCH, k_dim), fixed)]
    args = [a_p, a_s]
    for arr, layer in weights:
        if transposed:
            in_specs.append(pl.BlockSpec((None, tn, k_dim), lambda j, i, layer=layer: (layer, j, 0)))
        else:
            in_specs.append(pl.BlockSpec((None, k_dim, tn), lambda j, i, layer=layer: (layer, 0, j)))
        args.append(arr)
    out_specs = [pl.BlockSpec((tm, tn), tile), pl.BlockSpec((DEC_BATCH, tn), panel)]
    out_shape = [jax.ShapeDtypeStruct((M_PROMPT, n_cols), out_dtype),
                 jax.ShapeDtypeStruct((DEC_BATCH, n_cols), out_dtype)]
    if side_cast is not None:
        s_arr, s_layer = side_cast
        slab = s_arr.shape[1] // n_j
        in_specs.append(pl.BlockSpec((None, slab, s_arr.shape[2]), lambda j, i: (s_layer, j, 0)))
        args.append(s_arr)
        out_specs.append(pl.BlockSpec((slab, s_arr.shape[2]), lambda j, i: (j, 0)))
        out_shape.append(jax.ShapeDtypeStruct(s_arr.shape[1:], BF16))
    n_w = len(weights)
    return pl.pallas_call(
        functools.partial(_dense_kernel, n_w=n_w, mode=mode, n_i=n_i, transposed=transposed,
                          side_cast=side_cast is not None, sub=sub),
        grid=(n_j, n_i),
        in_specs=in_specs,
        out_specs=out_specs,
        out_shape=out_shape,
        scratch_shapes=[pltpu.VMEM((k_dim, tn), BF16) for _ in range(n_w)],
        compiler_params=_cparams("arbitrary", "arbitrary"),
        name=name,
    )(*args)


def _proj_res_kernel(*refs, n_i, with_norm):
    a_p, a_s, w_ref, r_p, r_s = refs[:5]
    if with_norm:
        g_ref, x_p, x_s, h_p, h_s = refs[5:]
    else:
        x_p, x_s = refs[5:]
        g_ref = h_p = h_s = None

    def run(a_ref, r_ref, x_ref, h_ref):
        a = a_ref[...]
        if a.dtype != BF16:
            a = a.astype(BF16)
        x = r_ref[...] + _dot(a, w_ref[...])
        x_ref[...] = x
        if with_norm:
            h_ref[...] = _rmsnorm_rows(x, g_ref[...]).astype(h_ref.dtype)

    run(a_p, r_p, x_p, h_p)

    @pl.when(pl.program_id(0) == n_i - 1)
    def _():
        run(a_s, r_s, x_s, h_s)


def _proj_res(a_p, a_s, weight, res, gain, *, tm, name):
    w_bf, layer = weight
    k_dim = a_p.shape[1]
    if layer is None:
        w_spec = pl.BlockSpec((k_dim, D_MODEL), lambda i: (0, 0), pipeline_mode=pl.Buffered(1))
    else:
        w_spec = pl.BlockSpec((None, k_dim, D_MODEL), lambda i: (layer, 0, 0), pipeline_mode=pl.Buffered(1))
    n_i = M_PROMPT // tm
    with_norm = gain is not None
    row = lambda i: (i, 0)
    fixed = lambda i: (0, 0)
    rows_p = pl.BlockSpec((tm, D_MODEL), row)
    rows_s = pl.BlockSpec((DEC_BATCH, D_MODEL), fixed)
    in_specs = [pl.BlockSpec((tm, k_dim), row), pl.BlockSpec((DEC_BATCH, k_dim), fixed),
                w_spec, rows_p, rows_s]
    args = [a_p, a_s, w_bf, res[0], res[1]]
    out_specs = [rows_p, rows_s]
    out_shape = [jax.ShapeDtypeStruct((M_PROMPT, D_MODEL), F32), jax.ShapeDtypeStruct((DEC_BATCH, D_MODEL), F32)]
    if with_norm:
        in_specs.append(pl.BlockSpec((1, D_MODEL), fixed))
        args.append(gain.reshape(1, D_MODEL))
        out_specs += [rows_p, rows_s]
        out_shape += [jax.ShapeDtypeStruct((M_PROMPT, D_MODEL), BF16),
                      jax.ShapeDtypeStruct((DEC_BATCH, D_MODEL), BF16)]
    return pl.pallas_call(
        functools.partial(_proj_res_kernel, n_i=n_i, with_norm=with_norm),
        grid=(n_i,),
        in_specs=in_specs,
        out_specs=out_specs,
        out_shape=out_shape,
        compiler_params=_cparams("arbitrary"),
        name=name,
    )(*args)


def _split_cat(x):
    hi = x.astype(BF16)
    lo = (x - hi.astype(F32)).astype(BF16)
    return jnp.concatenate([hi, lo], axis=1)


def _rope_mats():
    j = jnp.arange(2 * LANES)[:, None] % LANES
    l = jnp.arange(LANES)[None, :]
    head_sum = (j // HEAD_DIM == l // HEAD_DIM).astype(BF16)
    src = jnp.where(l % HEAD_DIM < HEAD_DIM // 2, l + HEAD_DIM // 2, l - HEAD_DIM // 2)
    half_swap = (j == src).astype(BF16)
    return head_sum, half_swap


def _norm_rope_chunk(xc, gain, cos, sin, hs_ref, sw_ref):
    ms = _dot(_split_cat(xc * xc), hs_ref[...]) * (1.0 / HEAD_DIM)
    y = (xc * lax.rsqrt(ms + EPS)) * gain
    swapped = _dot(_split_cat(y), sw_ref[...])
    return y * cos + swapped * sin


def _spread_heads(xc, own_lo):
    lane = lax.broadcasted_iota(jnp.int32, xc.shape, 1)
    keep = (lane < HEAD_DIM) if own_lo else (lane >= HEAD_DIM)
    nat = jnp.where(keep, xc, 0.0)
    rol = pltpu.roll(nat, HEAD_DIM, axis=1)
    parts = (nat, rol) if own_lo else (rol, nat)
    return jnp.concatenate(parts, axis=0).astype(BF16)


def _attn_core(qs, kfull, vfull, mask_t, sink_ref):
    rows = qs[0].shape[0]
    nkeys = kfull.shape[0]
    outs = [None] * 8
    for kh in range(A_KV_HEADS):
        c0 = (kh // 2) * LANES
        kk = _spread_heads(kfull[:, c0:c0 + LANES], kh % 2 == 0)
        vv = _spread_heads(vfull[:, c0:c0 + LANES], kh % 2 == 0)
        lhs = jnp.concatenate([qs[2 * kh], qs[2 * kh + 1]], axis=0)
        s = _dot_nt(kk, lhs)
        prow = []
        for half in range(2):
            pcol = []
            for cc in range(2):
                sb = s[half * nkeys:(half + 1) * nkeys, cc * rows:(cc + 1) * rows]
                sb = jnp.where(mask_t, sb, NEG_INF)
                sink = sink_ref[kh * 4 + 2 * cc + half]
                m = jnp.maximum(jnp.max(sb, axis=0, keepdims=True), sink)
                e = jnp.exp(sb - m)
                den = jnp.sum(e, axis=0, keepdims=True) + jnp.exp(sink - m)
                pcol.append((e / den).astype(BF16))
            prow.append(jnp.concatenate(pcol, axis=1))
        p = jnp.concatenate(prow, axis=0)
        o = _dot_tn(p, vv)
        outs[2 * kh] = o[0:rows]
        outs[2 * kh + 1] = o[rows:2 * rows]
    return outs


def _even_prompt_kernel(sink_ref, z_ref, cos_ref, sin_ref, qg_ref, kg_ref, cw_ref, hs_ref, sw_ref, wo_ref,
                        o_ref, nk_ref, nv_ref, nc_ref, wo_bf_ref, kf_ref, vf_ref, ub_ref):
    i = pl.program_id(1)
    wo_bf_ref[...] = wo_ref[...].astype(BF16)

    @pl.when(i == 0)
    def _():
        kf_ref[...] = jnp.zeros_like(kf_ref)
        vf_ref[...] = jnp.zeros_like(vf_ref)
        ub_ref[0:8, :] = jnp.zeros((8, CONV_DIM), F32)

    kk = lax.broadcasted_iota(jnp.int32, (2 * BLK, BLK), 0)
    r = lax.broadcasted_iota(jnp.int32, (2 * BLK, BLK), 1)
    d = kk - r
    band = (d >= 0) & (d <= WINDOW)
    o0 = A_WIDTH + 2 * A_KV_WIDTH
    k = v = None
    for t in range(MIX_SUB):
        rows = slice(t * BLK, (t + 1) * BLK)
        kf_ref[0:BLK, :] = kf_ref[BLK:2 * BLK, :]
        vf_ref[0:BLK, :] = vf_ref[BLK:2 * BLK, :]
        cos = cos_ref[rows, :]
        sin = sin_ref[rows, :]
        k = jnp.concatenate(
            [_norm_rope_chunk(z_ref[rows, A_WIDTH + c * LANES:A_WIDTH + (c + 1) * LANES], kg_ref[...], cos, sin,
                              hs_ref, sw_ref)
             for c in range(A_KV_WIDTH // LANES)], axis=1)
        v = z_ref[rows, A_WIDTH + A_KV_WIDTH:A_WIDTH + 2 * A_KV_WIDTH]
        kf_ref[BLK:2 * BLK, :] = k
        vf_ref[BLK:2 * BLK, :] = v
        qs = [(_norm_rope_chunk(z_ref[rows, c * LANES:(c + 1) * LANES], qg_ref[...], cos, sin, hs_ref, sw_ref)
               * (HEAD_DIM ** -0.5)).astype(BF16) for c in range(A_WIDTH // LANES)]
        mask_t = band & ((kk >= BLK) | (i > 0)) if t == 0 else band
        outs = _attn_core(qs, kf_ref[...], vf_ref[...], mask_t, sink_ref)
        for c in range(A_WIDTH // LANES):
            o_ref[rows, c * LANES:(c + 1) * LANES] = outs[c].astype(o_ref.dtype)

        u = z_ref[rows, o0 + CONV_DIM:o0 + 2 * CONV_DIM] * z_ref[rows, o0 + 2 * CONV_DIM:o0 + 3 * CONV_DIM]
        ub_ref[8:8 + BLK, :] = u
        y = ub_ref[6:6 + BLK, :] * cw_ref[0:1, :]
        y = y + ub_ref[7:7 + BLK, :] * cw_ref[1:2, :]
        y = y + ub_ref[8:8 + BLK, :] * cw_ref[2:3, :]
        o_ref[rows, A_WIDTH:A_WIDTH + CONV_DIM] = (z_ref[rows, o0:o0 + CONV_DIM] * y).astype(o_ref.dtype)
        ub_ref[0:8, :] = ub_ref[BLK:BLK + 8, :]

    @pl.when(i == pl.num_programs(1) - 1)
    def _():
        nk_ref[...] = k
        nv_ref[...] = v
        nc_ref[...] = ub_ref[BLK + 6:BLK + 8, :]


def _w_slab_specs(w_out, layer, n_step):
    slab = w_out.shape[1] // (BATCH * n_step)
    return (pl.BlockSpec((None, slab, w_out.shape[2]), lambda b, i: (layer, b * n_step + i, 0)),
            pl.BlockSpec((slab, w_out.shape[2]), lambda b, i: (b * n_step + i, 0)),
            jax.ShapeDtypeStruct(w_out.shape[1:], BF16))


def _even_prompt(z_p, cos, sin, qg, kg, cw, sinks, mats, w_out, layer):
    n_step = NBLK // MIX_SUB
    step_rows = MIX_SUB * BLK
    blk = lambda b, i: (b * n_step + i, 0)
    fixed = lambda b, i: (0, 0)
    per_b = lambda b, i: (b, 0, 0)
    wo_in, wo_out, wo_shape = _w_slab_specs(w_out, layer, n_step)
    return pl.pallas_call(
        _even_prompt_kernel,
        grid=(BATCH, n_step),
        in_specs=[pl.BlockSpec(memory_space=pltpu.SMEM),
                  pl.BlockSpec((step_rows, EVEN_IN), blk),
                  pl.BlockSpec((step_rows, LANES), lambda b, i: (i, 0)),
                  pl.BlockSpec((step_rows, LANES), lambda b, i: (i, 0)),
                  pl.BlockSpec((1, LANES), fixed),
                  pl.BlockSpec((1, LANES), fixed),
                  pl.BlockSpec((CONV_W, CONV_DIM), fixed),
                  pl.BlockSpec((2 * LANES, LANES), fixed),
                  pl.BlockSpec((2 * LANES, LANES), fixed),
                  wo_in],
        out_specs=[pl.BlockSpec((step_rows, D_MODEL), blk),
                   pl.BlockSpec((None, BLK, A_KV_WIDTH), per_b),
                   pl.BlockSpec((None, BLK, A_KV_WIDTH), per_b),
                   pl.BlockSpec((None, CONV_W - 1, CONV_DIM), per_b),
                   wo_out],
        out_shape=[jax.ShapeDtypeStruct((M_PROMPT, D_MODEL), BF16),
                   jax.ShapeDtypeStruct((BATCH, BLK, A_KV_WIDTH), F32),
                   jax.ShapeDtypeStruct((BATCH, BLK, A_KV_WIDTH), F32),
                   jax.ShapeDtypeStruct((BATCH, CONV_W - 1, CONV_DIM), F32),
                   wo_shape],
        scratch_shapes=[pltpu.VMEM((2 * BLK, A_KV_WIDTH), F32),
                        pltpu.VMEM((2 * BLK, A_KV_WIDTH), F32),
                        pltpu.VMEM((BLK + 8, CONV_DIM), F32)],
        compiler_params=_cparams("arbitrary", "arbitrary"),
        name="even_prompt",
    )(sinks, z_p, cos, sin, qg, kg, cw, *mats, w_out)


SROWS = LANES
N_SAMPLE_IN = 12
SAMPLE_BB = 2


def _even_sample_kernel(*refs):
    (sink_ref, z_ref, cos_ref, sin_ref, qg_ref, kg_ref, cs_ref, cw_ref, hs_ref, sw_ref,
     kc_ref, vc_ref) = refs[:N_SAMPLE_IN]
    m_ref, ncs_ref, nk_ref, nv_ref, q_all, k_all = refs[-6:]
    step = pl.program_id(0)
    lc = kc_ref.shape[1]

    @pl.when(step == 0)
    def _():
        cos = cos_ref[...]
        sin = sin_ref[...]
        for c in range(A_WIDTH // LANES):
            q = _norm_rope_chunk(z_ref[:, c * LANES:(c + 1) * LANES], qg_ref[...], cos, sin, hs_ref, sw_ref)
            q_all[:, c * LANES:(c + 1) * LANES] = q * (HEAD_DIM ** -0.5)
        for c in range(A_KV_WIDTH // LANES):
            k_all[:, c * LANES:(c + 1) * LANES] = _norm_rope_chunk(
                z_ref[:, A_WIDTH + c * LANES:A_WIDTH + (c + 1) * LANES], kg_ref[...], cos, sin, hs_ref, sw_ref)
        o0 = A_WIDTH + 2 * A_KV_WIDTH
        u = z_ref[:, o0 + CONV_DIM:o0 + 2 * CONV_DIM] * z_ref[:, o0 + 2 * CONV_DIM:o0 + 3 * CONV_DIM]
        y = cs_ref[:, 0:CONV_DIM] * cw_ref[0:1, :]
        y = y + cs_ref[:, CONV_DIM:2 * CONV_DIM] * cw_ref[1:2, :]
        y = y + u * cw_ref[2:3, :]
        m_ref[:, A_WIDTH:A_WIDTH + CONV_DIM] = z_ref[:, o0:o0 + CONV_DIM] * y
        m_ref[:, 0:A_WIDTH] = jnp.zeros((DEC_BATCH, A_WIDTH), F32)
        ncs_ref[:, 0:CONV_DIM] = cs_ref[:, CONV_DIM:2 * CONV_DIM]
        ncs_ref[:, CONV_DIM:2 * CONV_DIM] = u

    kk = lax.broadcasted_iota(jnp.int32, (2 * lc, SROWS), 0)
    mask_t = (kk <= lc) & (lc - kk <= WINDOW)
    row = lax.broadcasted_iota(jnp.int32, (lc, A_KV_WIDTH), 0)
    grp = pl.ds(pl.multiple_of((step * SAMPLE_BB // 8) * 8, 8), 8)
    for u in range(SAMPLE_BB):
        mine = lax.broadcasted_iota(jnp.int32, (8, 1), 0) == (step * SAMPLE_BB + u) % 8

        def pick(tile, mine=mine):
            return jnp.sum(jnp.where(mine, tile, 0.0), axis=0, keepdims=True)

        kn = pick(k_all[grp, :])
        vn = pick(z_ref[grp, A_WIDTH + A_KV_WIDTH:A_WIDTH + 2 * A_KV_WIDTH])
        kfull = jnp.concatenate([kc_ref[u], jnp.where(row == 0, kn, 0.0)], axis=0)
        vfull = jnp.concatenate([vc_ref[u], jnp.where(row == 0, vn, 0.0)], axis=0)
        qs = [jnp.broadcast_to(pick(q_all[grp, c * LANES:(c + 1) * LANES]), (SROWS, LANES)).astype(BF16)
              for c in range(A_WIDTH // LANES)]
        outs = _attn_core(qs, kfull, vfull, mask_t, sink_ref)
        for c in range(A_WIDTH // LANES):
            cols = slice(c * LANES, (c + 1) * LANES)
            m_ref[grp, cols] = jnp.where(mine, outs[c][0:8], m_ref[grp, cols])
        nk_ref[u, 0:lc - 1, :] = kc_ref[u, 1:lc, :]
        nk_ref[u, lc - 1:lc, :] = kn
        nv_ref[u, 0:lc - 1, :] = vc_ref[u, 1:lc, :]
        nv_ref[u, lc - 1:lc, :] = vn


def _even_sample(sinks, z_s, cos, sin, qg, kg, conv_state, cw, mats, k_cache, v_cache, li, prev):
    lc = k_cache.shape[2]
    whole = lambda a: pl.BlockSpec(a.shape, lambda b: (0,) * a.ndim)
    cache = pl.BlockSpec((None, SAMPLE_BB, lc, A_KV_WIDTH), lambda b: (li, b, 0, 0))
    cs = conv_state.reshape(DEC_BATCH, (CONV_W - 1) * CONV_DIM)
    args = [sinks, z_s, cos, sin, qg, kg, cs, cw, *mats, k_cache, v_cache]
    assert len(args) == N_SAMPLE_IN
    in_specs = [pl.BlockSpec(memory_space=pltpu.SMEM)] + [whole(a) for a in args[1:N_SAMPLE_IN - 2]] + [cache, cache]
    aliases = {}
    if prev is not None:
        aliases = {len(args): 2, len(args) + 1: 3}
        in_specs += [pl.BlockSpec(memory_space=pl.ANY)] * 2
        args += list(prev)
    rows = lambda w: pl.BlockSpec((DEC_BATCH, w), lambda b: (0, 0))
    m_s, ncs, nk, nv = pl.pallas_call(
        _even_sample_kernel,
        grid=(DEC_BATCH // SAMPLE_BB,),
        in_specs=in_specs,
        out_specs=[rows(D_MODEL), rows((CONV_W - 1) * CONV_DIM), cache, cache],
        out_shape=[jax.ShapeDtypeStruct((DEC_BATCH, D_MODEL), F32),
                   jax.ShapeDtypeStruct((DEC_BATCH, (CONV_W - 1) * CONV_DIM), F32),
                   jax.ShapeDtypeStruct(k_cache.shape, F32),
                   jax.ShapeDtypeStruct(v_cache.shape, F32)],
        scratch_shapes=[pltpu.VMEM((DEC_BATCH, A_WIDTH), F32), pltpu.VMEM((DEC_BATCH, A_KV_WIDTH), F32)],
        input_output_aliases=aliases,
        compiler_params=_cparams("arbitrary"),
        name="even_sample",
    )(*args)
    return m_s, ncs.reshape(DEC_BATCH, CONV_W - 1, CONV_DIM), nk, nv


def _log_decay(zr, wa_ref, ba_ref):
    pre = _dot(zr.astype(BF16), wa_ref[...]) + ba_ref[...]
    return (jnp.minimum(pre, 0.0) - jnp.log1p(jnp.exp(-jnp.abs(pre)))) * (1.0 / GLA_TAU)


def _split_bf16(x):
    hi = x.astype(BF16)
    lo = (x - hi.astype(F32)).astype(BF16)
    return hi, lo


def _row_to_col(row):
    n = row.shape[1]
    r = lax.broadcasted_iota(jnp.int32, (n, n), 0)
    c = lax.broadcasted_iota(jnp.int32, (n, n), 1)
    return jnp.sum(jnp.where(r == c, jnp.broadcast_to(row, (n, n)), 0.0), axis=-1, keepdims=True)


def _head_rmsnorm_gate(o, gain, zg):
    ms = jnp.mean(o * o, axis=-1, keepdims=True)
    return ((o * lax.rsqrt(ms + EPS)) * gain) * _silu(zg)


def _odd_prompt_kernel(zm_ref, zpr_ref, wa_ref, ba_ref, gain_ref, wp_ref, ps_ref, tri_ref, band_ref, wo_ref,
                       o_ref, s_ref, np_ref, wo_bf_ref, pbh_ref, pbl_ref):
    i = pl.program_id(1)
    wo_bf_ref[...] = wo_ref[...].astype(BF16)

    @pl.when(i == 0)
    def _():
        s_ref[...] = jnp.zeros_like(s_ref)
        pbh_ref[0:BLK, :] = jnp.zeros((BLK, POOL_DIM), BF16)
        pbl_ref[0:BLK, :] = jnp.zeros((BLK, POOL_DIM), BF16)

    x = None
    for t in range(MIX_SUB):
        rows = pl.ds(t * BLK, BLK)
        x = _odd_block(zm_ref.at[rows], zpr_ref.at[rows], wa_ref, ba_ref, gain_ref, wp_ref, ps_ref,
                       tri_ref, band_ref, o_ref.at[rows], s_ref, pbh_ref, pbl_ref, (i * MIX_SUB + t) * BLK)

    @pl.when(i == pl.num_programs(1) - 1)
    def _():
        np_ref[...] = x[BLK - POOL_BUF:BLK, :]


def _odd_block(zm_ref, zpr_ref, wa_ref, ba_ref, gain_ref, wp_ref, ps_ref, tri_ref, band_ref,
               o_ref, s_ref, pbh_ref, pbl_ref, t0):
    n_ch = BLK // GLA_CHUNK
    gk = _log_decay(zpr_ref[:, POOL_DIM:POOL_RANK_W], wa_ref, ba_ref)
    r = lax.broadcasted_iota(jnp.int32, (BLK, BLK), 0)
    c = lax.broadcasted_iota(jnp.int32, (BLK, BLK), 1)
    causal = (r >= c) & (r // GLA_CHUNK == c // GLA_CHUNK)
    g_hi, g_lo = _split_bf16(gk)
    b = _dot(tri_ref[...], jnp.concatenate([g_hi, g_lo], axis=0))
    b_last = [b[(ch + 1) * GLA_CHUNK - 1:(ch + 1) * GLA_CHUNK, :] for ch in range(n_ch)]
    b_end = jnp.concatenate([jnp.broadcast_to(bl, (GLA_CHUNK, GLA_K_WIDTH)) for bl in b_last], axis=0)
    o_k, o_v, o_g = GLA_K_WIDTH, 2 * GLA_K_WIDTH, 2 * GLA_K_WIDTH + GLA_WIDTH
    zq = zm_ref[:, 0:GLA_K_WIDTH]
    zk = zm_ref[:, o_k:o_k + GLA_K_WIDTH]
    qd = ((zq * (GLA_DK ** -0.5)) * jnp.exp(b)).astype(BF16)
    kd = (zk * jnp.exp(-b)).astype(BF16)
    k2 = (zk * jnp.exp(b_end - b)).astype(BF16)
    sub = lax.broadcasted_iota(jnp.int32, (8, LANES), 0)
    sel = jnp.where(sub < 2, 1.0, 0.0).astype(BF16)
    hk = [slice(h * GLA_DK, (h + 1) * GLA_DK) for h in range(GLA_HEADS)]
    chunk = [slice(ch * GLA_CHUNK, (ch + 1) * GLA_CHUNK) for ch in range(n_ch)]
    v_hs = [zm_ref[:, o_v + h * GLA_DV:o_v + (h + 1) * GLA_DV].astype(BF16) for h in range(GLA_HEADS)]
    atts = [_dot_nt(qd[:, hk[h]], kd[:, hk[h]]) for h in range(GLA_HEADS)]
    log_decs = []
    for h in range(GLA_HEADS):
        for ch in range(n_ch):
            bl = b_last[ch][:, hk[h]]
            bl_hi = bl.astype(BF16).astype(F32)
            rows8 = jnp.where(sub == 0, bl_hi, jnp.where(sub == 1, bl - bl_hi, 0.0)).astype(BF16)
            log_decs.append(_dot_tn(rows8, sel))
    deltas = [[_dot_tn(k2[chunk[ch], hk[h]], v_hs[h][chunk[ch]]) for ch in range(n_ch)]
              for h in range(GLA_HEADS)]
    o_intras = [_dot(jnp.where(causal, atts[h], 0.0).astype(BF16), v_hs[h]) for h in range(GLA_HEADS)]
    o_inters = []
    for h in range(GLA_HEADS):
        s_h = s_ref[h]
        parts = []
        for ch in range(n_ch):
            parts.append(_dot(qd[chunk[ch], hk[h]], s_h.astype(BF16)))
            dec = jnp.exp(log_decs[h * n_ch + ch])
            s_h = jnp.concatenate([dec, dec], axis=1) * s_h + deltas[h][ch]
        s_ref[h] = s_h
        o_inters.append(jnp.concatenate(parts, axis=0))
    for h in range(GLA_HEADS):
        vs = slice(h * GLA_DV, (h + 1) * GLA_DV)
        zg = zm_ref[:, o_g + h * GLA_DV:o_g + (h + 1) * GLA_DV]
        o_ref[:, vs] = _head_rmsnorm_gate(o_intras[h] + o_inters[h], gain_ref[...], zg).astype(o_ref.dtype)

    x = zpr_ref[:, 0:POOL_DIM]
    x_hi, x_lo = _split_bf16(x)
    pbh_ref[BLK:2 * BLK, :] = x_hi
    pbl_ref[BLK:2 * BLK, :] = x_lo
    t_glob = t0 + lax.broadcasted_iota(jnp.int32, (BLK, 1), 0)
    groups = [slice(g * POOL_GDIM, (g + 1) * POOL_GDIM) for g in range(len(POOL_WINDOWS))]
    sums = [_dot(band_ref[g], jnp.concatenate([pbh_ref[:, groups[g]], pbl_ref[:, groups[g]]], axis=0))
            for g in range(len(POOL_WINDOWS))]
    ys = []
    for g, w in enumerate(POOL_WINDOWS):
        inv_cnt = 1.0 / jnp.minimum(w, t_glob + 1).astype(F32)
        d = sums[g] * inv_cnt - x[:, groups[g]]
        ys.append(_dot(d.astype(BF16), wp_ref[g]))
    for g in range(len(POOL_WINDOWS)):
        o_ref[:, GLA_WIDTH + g * POOL_GDIM:GLA_WIDTH + (g + 1) * POOL_GDIM] = (
            ys[g] * ps_ref[:, groups[g]]).astype(o_ref.dtype)
    pbh_ref[0:BLK, :] = x_hi
    pbl_ref[0:BLK, :] = x_lo
    return x


def _odd_block_mats():
    r = jnp.arange(BLK)[:, None]
    c = jnp.arange(BLK)[None, :]
    tri = ((r >= c) & (r // GLA_CHUNK == c // GLA_CHUNK)).astype(BF16)
    back = r + BLK - jnp.arange(2 * BLK)[None, :]
    bands = [((back >= 0) & (back < w)).astype(BF16) for w in POOL_WINDOWS]
    return (jnp.concatenate([tri, tri], axis=1),
            jnp.stack([jnp.concatenate([bd, bd], axis=1) for bd in bands]))


def _odd_prompt(zm_p, zpr_p, wa, ba, gain, wp, ps, w_out, layer):
    n_step = NBLK // MIX_SUB
    step_rows = MIX_SUB * BLK
    blk = lambda b, i: (b * n_step + i, 0)
    fixed2 = lambda b, i: (0, 0)
    wo_in, wo_out, wo_shape = _w_slab_specs(w_out, layer, n_step)
    tri2, bands2 = _odd_block_mats()
    return pl.pallas_call(
        _odd_prompt_kernel,
        grid=(BATCH, n_step),
        in_specs=[pl.BlockSpec((step_rows, ODD_MAIN), blk),
                  pl.BlockSpec((step_rows, POOL_RANK_W), blk),
                  pl.BlockSpec((LANES, GLA_K_WIDTH), fixed2),
                  pl.BlockSpec((1, GLA_K_WIDTH), fixed2),
                  pl.BlockSpec((1, GLA_DV), fixed2),
                  pl.BlockSpec((4, POOL_GDIM, POOL_GDIM), lambda b, i: (0, 0, 0)),
                  pl.BlockSpec((1, POOL_DIM), fixed2),
                  pl.BlockSpec(tri2.shape, fixed2),
                  pl.BlockSpec(bands2.shape, lambda b, i: (0, 0, 0)),
                  wo_in],
        out_specs=[pl.BlockSpec((step_rows, D_MODEL), blk),
                   pl.BlockSpec((None, GLA_HEADS, GLA_DK, GLA_DV), lambda b, i: (b, 0, 0, 0)),
                   pl.BlockSpec((None, POOL_BUF, POOL_DIM), lambda b, i: (b, 0, 0)),
                   wo_out],
        out_shape=[jax.ShapeDtypeStruct((M_PROMPT, D_MODEL), BF16),
                   jax.ShapeDtypeStruct((BATCH, GLA_HEADS, GLA_DK, GLA_DV), F32),
                   jax.ShapeDtypeStruct((BATCH, POOL_BUF, POOL_DIM), F32),
                   wo_shape],
        scratch_shapes=[pltpu.VMEM((2 * BLK, POOL_DIM), BF16), pltpu.VMEM((2 * BLK, POOL_DIM), BF16)],
        compiler_params=_cparams("arbitrary", "arbitrary"),
        name="odd_prompt",
    )(zm_p, zpr_p, wa, ba, gain, wp, ps, tri2, bands2, w_out)


STATE_BB = 8
N_ODD_SAMPLE_IN = 9


def _odd_sample_kernel(*refs):
    zm_ref, zpr_ref, wa_ref, ba_ref, gain_ref, wp_ref, ps_ref, pool_ref, s_ref = refs[:N_ODD_SAMPLE_IN]
    m_ref, np_ref, ns_ref, dec_all, q_all, o_rows = refs[-6:]
    step = pl.program_id(0)
    n_prev = pool_ref.shape[1] // POOL_DIM

    @pl.when(step == 0)
    def _():
        dec_all[...] = jnp.exp(_log_decay(zpr_ref[:, POOL_DIM:POOL_RANK_W], wa_ref, ba_ref))
        q_all[...] = zm_ref[:, 0:GLA_K_WIDTH] * (GLA_DK ** -0.5)
        for g, w in enumerate(POOL_WINDOWS):
            gs = slice(g * POOL_GDIM, (g + 1) * POOL_GDIM)
            x = zpr_ref[:, gs]
            first = n_prev - (w - 1)
            s = pool_ref[:, first * POOL_DIM + g * POOL_GDIM:first * POOL_DIM + (g + 1) * POOL_GDIM]
            for jj in range(first + 1, n_prev):
                s = s + pool_ref[:, jj * POOL_DIM + g * POOL_GDIM:jj * POOL_DIM + (g + 1) * POOL_GDIM]
            s = s + x
            d = s / float(min(w, n_prev + 1)) - x
            m_ref[:, GLA_WIDTH + g * POOL_GDIM:GLA_WIDTH + (g + 1) * POOL_GDIM] = (
                _dot(d.astype(BF16), wp_ref[g]) * ps_ref[:, gs])
        np_ref[:, 0:(n_prev - 1) * POOL_DIM] = pool_ref[:, POOL_DIM:n_prev * POOL_DIM]
        np_ref[:, (n_prev - 1) * POOL_DIM:n_prev * POOL_DIM] = zpr_ref[:, 0:POOL_DIM]

    grp = pl.ds(pl.multiple_of(step * STATE_BB, STATE_BB), STATE_BB)
    dec = dec_all[grp, :]
    q = q_all[grp, :]
    k = zm_ref[grp, GLA_K_WIDTH:2 * GLA_K_WIDTH]
    v = zm_ref[grp, 2 * GLA_K_WIDTH:2 * GLA_K_WIDTH + GLA_WIDTH]
    for bb in range(STATE_BB):
        for h in range(GLA_HEADS):
            ks = slice(h * GLA_DK, (h + 1) * GLA_DK)
            vs = slice(h * GLA_DV, (h + 1) * GLA_DV)
            s_new = (_row_to_col(dec[bb:bb + 1, ks]) * s_ref[bb, h]
                     + _row_to_col(k[bb:bb + 1, ks]) * v[bb:bb + 1, vs])
            ns_ref[bb, h] = s_new
            o_rows[bb:bb + 1, vs] = jnp.sum(_row_to_col(q[bb:bb + 1, ks]) * s_new, axis=0, keepdims=True)
    o_g = 2 * GLA_K_WIDTH + GLA_WIDTH
    for h in range(GLA_HEADS):
        vs = slice(h * GLA_DV, (h + 1) * GLA_DV)
        zg = zm_ref[grp, o_g + h * GLA_DV:o_g + (h + 1) * GLA_DV]
        m_ref[grp, vs] = _head_rmsnorm_gate(o_rows[:, vs], gain_ref[...], zg)


def _odd_sample(zm_s, zpr_s, wa, ba, gain, wp, ps, pool_state, state, li, prev):
    whole = lambda a: pl.BlockSpec(a.shape, lambda b: (0,) * a.ndim)
    st = pl.BlockSpec((None, STATE_BB, GLA_HEADS, GLA_DK, GLA_DV), lambda b: (li, b, 0, 0, 0))
    pool2d = pool_state.reshape(DEC_BATCH, POOL_BUF * POOL_DIM)
    args = [zm_s, zpr_s, wa, ba, gain, wp, ps, pool2d, state]
    assert len(args) == N_ODD_SAMPLE_IN
    in_specs = [whole(a) for a in args[:-1]] + [st]
    aliases = {}
    if prev is not None:
        aliases = {len(args): 2}
        in_specs.append(pl.BlockSpec(memory_space=pl.ANY))
        args.append(prev)
    rows = lambda w: pl.BlockSpec((DEC_BATCH, w), lambda b: (0, 0))
    m_s, npool, ns = pl.pallas_call(
        _odd_sample_kernel,
        grid=(DEC_BATCH // STATE_BB,),
        in_specs=in_specs,
        out_specs=[rows(D_MODEL), rows(POOL_BUF * POOL_DIM), st],
        out_shape=[jax.ShapeDtypeStruct((DEC_BATCH, D_MODEL), F32),
                   jax.ShapeDtypeStruct((DEC_BATCH, POOL_BUF * POOL_DIM), F32),
                   jax.ShapeDtypeStruct(state.shape, F32)],
        scratch_shapes=[pltpu.VMEM((DEC_BATCH, GLA_K_WIDTH), F32), pltpu.VMEM((DEC_BATCH, GLA_K_WIDTH), F32),
                        pltpu.VMEM((STATE_BB, GLA_WIDTH), F32)],
        input_output_aliases=aliases,
        compiler_params=_cparams("arbitrary"),
        name="odd_sample",
    )(*args)
    return m_s, npool.reshape(DEC_BATCH, POOL_BUF, POOL_DIM), ns


def _rope_tables(pos):
    half = HEAD_DIM // 2
    inv = jnp.power(ROPE_THETA, -jnp.arange(half, dtype=F32) / half)
    ang = pos.astype(F32)[:, None] * inv[None, :]
    c, s = jnp.cos(ang), jnp.sin(ang)
    return jnp.tile(c, (1, 4)), jnp.tile(jnp.concatenate([-s, s], axis=1), (1, 2))


def kernel(x_prompt, x_sample, cache_swa_k, cache_swa_v, state_conv, state_gla, state_pool, norm_mix, norm_ffn, w_in_even, w_out_even, q_norm, k_norm, attn_sinks, conv_w, w_in_odd, w_out_odd, w_alpha_up, b_alpha, gla_out_norm, w_pool, pool_scale, w_gate, w_up, w_down):
    lc = cache_swa_k.shape[2]
    assert x_prompt.shape == (BATCH, SEQ, D_MODEL) and x_sample.shape == (DEC_BATCH, 1, D_MODEL)
    assert lc == BLK == WINDOW, "the prompt's new cache is its last attention block"
    assert state_pool.shape[2] == POOL_BUF and state_conv.shape[2] == CONV_W - 1
    x_p = x_prompt.reshape(M_PROMPT, D_MODEL)
    x_s = x_sample.reshape(DEC_BATCH, D_MODEL)
    cos_p, sin_p = _rope_tables(jnp.arange(SEQ))
    cos_s, sin_s = _rope_tables(PAST_LEN + jnp.arange(1))
    mats = _rope_mats()
    kc_all = cache_swa_k.reshape(N_EVEN, DEC_BATCH, lc, A_KV_WIDTH)
    vc_all = cache_swa_v.reshape(N_EVEN, DEC_BATCH, lc, A_KV_WIDTH)
    o_r = ODD_MAIN
    w_odd_t = jnp.swapaxes(w_in_odd, 1, 2)
    w_odd_pr_t = jnp.concatenate([w_odd_t[:, o_r + GLA_RANK:], w_odd_t[:, o_r:o_r + GLA_RANK],
                                  jnp.zeros((N_ODD, LANES - GLA_RANK, D_MODEL), F32)], axis=1)
    pk, pv, pc, pg, pp, sc, sp = ([] for _ in range(7))
    sk_all = sv_all = sg_all = None

    h_p, h_s = _rmsnorm(x_p, x_s, norm_mix[0])
    for layer in range(DEPTH):
        li = layer // 2
        if layer % 2 == 0:
            z_p, z_s = _dense(h_p, h_s, [(w_in_even, li)], n_cols=EVEN_IN, **TILES["in_even"], name="in_even")
            qg = jnp.tile(q_norm[li], 2).reshape(1, LANES)
            kg = jnp.tile(k_norm[li], 2).reshape(1, LANES)
            m_p, nk, nv, nc, w_out_bf = _even_prompt(z_p, cos_p, sin_p, qg, kg, conv_w[li], attn_sinks[li], mats,
                                                     w_out_even, li)
            pk.append(nk.reshape(BATCH, lc, A_KV_HEADS, HEAD_DIM))
            pv.append(nv.reshape(BATCH, lc, A_KV_HEADS, HEAD_DIM))
            pc.append(nc)
            m_s, nc_s, sk_all, sv_all = _even_sample(
                attn_sinks[li], z_s, cos_s, sin_s, qg, kg, state_conv[li], conv_w[li], mats, kc_all, vc_all, li,
                None if sk_all is None else (sk_all, sv_all))
            sc.append(nc_s)
        else:
            zm_p, zm_s = _dense(h_p, h_s, [(w_odd_t, li)], n_cols=ODD_MAIN, **TILES["in_odd_main"],
                                transposed=True, name="in_odd_main")
            zpr_p, zpr_s = _dense(h_p, h_s, [(w_odd_pr_t, li)], n_cols=POOL_RANK_W, **TILES["in_odd_pool_rank"],
                                  transposed=True, name="in_odd_pool_rank")
            wa = jnp.pad(w_alpha_up[li], ((0, LANES - GLA_RANK), (0, 0))).astype(BF16)
            ba = b_alpha[li].reshape(1, GLA_K_WIDTH)
            gain = gla_out_norm[li].reshape(1, GLA_DV)
            wp = w_pool[li].astype(BF16)
            ps = pool_scale[li].reshape(1, POOL_DIM)
            m_p, ng, npool, w_out_bf = _odd_prompt(zm_p, zpr_p, wa, ba, gain, wp, ps, w_out_odd, li)
            pg.append(ng)
            pp.append(npool)
            m_s, np_s, sg_all = _odd_sample(zm_s, zpr_s, wa, ba, gain, wp, ps, state_pool[li], state_gla, li,
                                            sg_all)
            sp.append(np_s)
        x_p, x_s, h_p, h_s = _proj_res(m_p, m_s, (w_out_bf, None), (x_p, x_s), norm_ffn[layer],
                                       **TILES["out_proj"], name="out_proj")
        a_p, a_s, w_down_bf = _dense(h_p, h_s, [(w_gate, layer), (w_up, layer)], n_cols=D_FF, **TILES["ffn_up"],
                                     mode="swiglu", out_dtype=BF16, side_cast=(w_down, layer), name="ffn_up")
        if layer + 1 < DEPTH:
            x_p, x_s, h_p, h_s = _proj_res(a_p, a_s, (w_down_bf, None), (x_p, x_s),
                                           norm_mix[layer + 1], **TILES["ffn_down"], name="ffn_down")
        else:
            x_p, x_s = _proj_res(a_p, a_s, (w_down_bf, None), (x_p, x_s), None,
                                 **TILES["ffn_down"], name="ffn_down_last")

    st = lambda parts: jnp.stack(parts)
    cache5 = lambda a: a.reshape(N_EVEN, DEC_BATCH, lc, A_KV_HEADS, HEAD_DIM)
    return (x_p.reshape(BATCH, SEQ, D_MODEL), x_s.reshape(DEC_BATCH, 1, D_MODEL),
            st(pk), st(pv), st(pc), st(pg), st(pp), cache5(sk_all), cache5(sv_all), st(sc), sg_all, st(sp))
```

```python
import functools

import jax
import jax.numpy as jnp
from jax import lax
from jax.experimental import pallas as pl
from jax.experimental.pallas import tpu as pltpu

F32 = jnp.float32
BF16 = jnp.bfloat16

D_MODEL = 2048
BATCH = 4
SEQ = 2048
DEPTH = 4
DEC_BATCH = 32
PAST_LEN = 16384
N_EVEN = 2
N_ODD = 2
EPS = 1e-6
NEG_INF = -1e30
A_HEADS = 16
A_KV_HEADS = 4
HEAD_DIM = 64
A_WIDTH = A_HEADS * HEAD_DIM
A_KV_WIDTH = A_KV_HEADS * HEAD_DIM
WINDOW = 128
ROPE_THETA = 10000.0
CONV_DIM = D_MODEL // 2
CONV_W = 3
GLA_HEADS = 4
GLA_WIDTH = D_MODEL // 2
GLA_DV = GLA_WIDTH // GLA_HEADS
GLA_DK = GLA_DV // 2
GLA_K_WIDTH = GLA_HEADS * GLA_DK
GLA_RANK = 16
GLA_TAU = 16.0
GLA_CHUNK = 64
POOL_DIM = D_MODEL // 2
POOL_WINDOWS = (2, 4, 8, 16)
POOL_GDIM = POOL_DIM // 4
POOL_BUF = 15
D_FF = 5632
EVEN_IN = A_WIDTH + 2 * A_KV_WIDTH + 3 * CONV_DIM
ODD_MAIN = 2 * GLA_K_WIDTH + 2 * GLA_WIDTH

M_PROMPT = BATCH * SEQ
LANES = 128
BLK = 128
NBLK = SEQ // BLK
MIX_SUB = 4
POOL_RANK_W = POOL_DIM + LANES
VMEM_LIMIT = 58 * 1024 * 1024

TILES = {
    "in_even": dict(tm=1024, tn=1536),
    "in_odd_main": dict(tm=1024, tn=ODD_MAIN // 2),
    "in_odd_pool_rank": dict(tm=1024, tn=POOL_RANK_W),
    "ffn_up": dict(tm=2048, tn=512, sub=2),
    "out_proj": dict(tm=512),
    "ffn_down": dict(tm=512),
}


def _cparams(*sem):
    return pltpu.CompilerParams(dimension_semantics=sem, vmem_limit_bytes=VMEM_LIMIT)


def _dot(a, b):
    return jnp.dot(a, b, preferred_element_type=F32)


def _dot_nt(a, b):
    return lax.dot_general(a, b, (((1,), (1,)), ((), ())), preferred_element_type=F32)


def _dot_tn(a, b):
    return lax.dot_general(a, b, (((0,), (0,)), ((), ())), preferred_element_type=F32)


def _silu(x):
    return x * (1.0 / (1.0 + jnp.exp(-x)))


def _rmsnorm_rows(x, gain):
    ms = jnp.mean(x * x, axis=-1, keepdims=True)
    return (x * lax.rsqrt(ms + EPS)) * gain


def _rmsnorm_kernel(xp_ref, xs_ref, g_ref, op_ref, os_ref, *, n_i):
    op_ref[...] = _rmsnorm_rows(xp_ref[...], g_ref[...]).astype(op_ref.dtype)

    @pl.when(pl.program_id(0) == n_i - 1)
    def _():
        os_ref[...] = _rmsnorm_rows(xs_ref[...], g_ref[...]).astype(os_ref.dtype)


def _rmsnorm(x_p, x_s, gain, *, tm=1024):
    n_i = M_PROMPT // tm
    row = lambda i: (i, 0)
    fixed = lambda i: (0, 0)
    return pl.pallas_call(
        functools.partial(_rmsnorm_kernel, n_i=n_i),
        grid=(n_i,),
        in_specs=[pl.BlockSpec((tm, D_MODEL), row),
                  pl.BlockSpec((DEC_BATCH, D_MODEL), fixed),
                  pl.BlockSpec((1, D_MODEL), fixed)],
        out_specs=[pl.BlockSpec((tm, D_MODEL), row),
                   pl.BlockSpec((DEC_BATCH, D_MODEL), fixed)],
        out_shape=[jax.ShapeDtypeStruct((M_PROMPT, D_MODEL), BF16),
                   jax.ShapeDtypeStruct((DEC_BATCH, D_MODEL), BF16)],
        compiler_params=_cparams("arbitrary"),
        name="rmsnorm",
    )(x_p, x_s, gain.reshape(1, D_MODEL))


XPOSE_COLS = 128


def _dense_kernel(*refs, n_w, mode, n_i, transposed, side_cast, sub):
    a_p, a_s = refs[0], refs[1]
    w = refs[2:2 + n_w]
    pos = 2 + n_w
    side_in = side_out = None
    if side_cast:
        side_in = refs[pos]
        pos += 1
    o_p, o_s = refs[pos], refs[pos + 1]
    pos += 2
    if side_cast:
        side_out = refs[pos]
        pos += 1
    wbf = refs[pos:pos + n_w]
    i = pl.program_id(1)

    @pl.when(i == 0)
    def _():
        for k in range(n_w):
            if transposed:
                tn = wbf[k].shape[1]
                for c in range(0, tn, XPOSE_COLS):
                    wbf[k][:, c:c + XPOSE_COLS] = w[k][c:c + XPOSE_COLS, :].T.astype(BF16)
            else:
                wbf[k][...] = w[k][...].astype(BF16)
        if side_cast:
            side_out[...] = side_in[...].astype(BF16)

    def run(a_ref, o_ref, n_sub):
        rows = a_ref.shape[0] // n_sub
        for r in range(n_sub):
            a = a_ref[r * rows:(r + 1) * rows, :]
            if mode == "swiglu":
                y = _silu(_dot(a, wbf[0][...])) * _dot(a, wbf[1][...])
            else:
                y = _dot(a, wbf[0][...])
            o_ref[r * rows:(r + 1) * rows, :] = y.astype(o_ref.dtype)

    run(a_p, o_p, sub)

    @pl.when(i == n_i - 1)
    def _():
        run(a_s, o_s, 1)


def _dense(a_p, a_s, weights, *, n_cols, tm, tn, mode="plain", out_dtype=F32, transposed=False,
           side_cast=None, sub=1, name):
    k_dim = a_p.shape[1]
    n_i = M_PROMPT // tm
    n_j = n_cols // tn
    row = lambda j, i: (i, 0)
    fixed = lambda j, i: (0, 0)
    tile = lambda j, i: (i, j)
    panel = lambda j, i: (0, j)
    in_specs = [pl.BlockSpec((tm, k_dim), row), pl.BlockSpec((DEC_BATCH, k_dim), fixed)]
    args = [a_p, a_s]
    for arr, layer in weights:
        if transposed:
            in_specs.append(pl.BlockSpec((None, tn, k_dim), lambda j, i, layer=layer: (layer, j, 0)))
        else:
            in_specs.append(pl.BlockSpec((None, k_dim, tn), lambda j, i, layer=layer: (layer, 0, j)))
        args.append(arr)
    out_specs = [pl.BlockSpec((tm, tn), tile), pl.BlockSpec((DEC_BATCH, tn), panel)]
    out_shape = [jax.ShapeDtypeStruct((M_PROMPT, n_cols), out_dtype),
                 jax.ShapeDtypeStruct((DEC_BATCH, n_cols), out_dtype)]
    if side_cast is not None:
        s_arr, s_layer = side_cast
        slab = s_arr.shape[1] // n_j
        in_specs.append(pl.BlockSpec((None, slab, s_arr.shape[2]), lambda j, i: (s_layer, j, 0)))
        args.append(s_arr)
        out_specs.append(pl.BlockSpec((slab, s_arr.shape[2]), lambda j, i: (j, 0)))
        out_shape.append(jax.ShapeDtypeStruct(s_arr.shape[1:], BF16))
    n_w = len(weights)
    return pl.pallas_call(
        functools.partial(_dense_kernel, n_w=n_w, mode=mode, n_i=n_i, transposed=transposed,
                          side_cast=side_cast is not None, sub=sub),
        grid=(n_j, n_i),
        in_specs=in_specs,
        out_specs=out_specs,
        out_shape=out_shape,
        scratch_shapes=[pltpu.VMEM((k_dim, tn), BF16) for _ in range(n_w)],
        compiler_params=_cparams("arbitrary", "arbitrary"),
        name=name,
    )(*args)


def _proj_res_kernel(*refs, n_i, with_norm):
    a_p, a_s, w_ref, r_p, r_s = refs[:5]
    if with_norm:
        g_ref, x_p, x_s, h_p, h_s = refs[5:]
    else:
        x_p, x_s = refs[5:]
        g_ref = h_p = h_s = None

    def run(a_ref, r_ref, x_ref, h_ref):
        a = a_ref[...]
        if a.dtype != BF16:
            a = a.astype(BF16)
        x = r_ref[...] + _dot(a, w_ref[...])
        x_ref[...] = x
        if with_norm:
            h_ref[...] = _rmsnorm_rows(x, g_ref[...]).astype(h_ref.dtype)

    run(a_p, r_p, x_p, h_p)

    @pl.when(pl.program_id(0) == n_i - 1)
    def _():
        run(a_s, r_s, x_s, h_s)


def _proj_res(a_p, a_s, weight, res, gain, *, tm, name):
    w_bf, layer = weight
    k_dim = a_p.shape[1]
    if layer is None:
        w_spec = pl.BlockSpec((k_dim, D_MODEL), lambda i: (0, 0), pipeline_mode=pl.Buffered(1))
    else:
        w_spec = pl.BlockSpec((None, k_dim, D_MODEL), lambda i: (layer, 0, 0), pipeline_mode=pl.Buffered(1))
    n_i = M_PROMPT // tm
    with_norm = gain is not None
    row = lambda i: (i, 0)
    fixed = lambda i: (0, 0)
    rows_p = pl.BlockSpec((tm, D_MODEL), row)
    rows_s = pl.BlockSpec((DEC_BATCH, D_MODEL), fixed)
    in_specs = [pl.BlockSpec((tm, k_dim), row), pl.BlockSpec((DEC_BATCH, k_dim), fixed),
                w_spec, rows_p, rows_s]
    args = [a_p, a_s, w_bf, res[0], res[1]]
    out_specs = [rows_p, rows_s]
    out_shape = [jax.ShapeDtypeStruct((M_PROMPT, D_MODEL), F32), jax.ShapeDtypeStruct((DEC_BATCH, D_MODEL), F32)]
    if with_norm:
        in_specs.append(pl.BlockSpec((1, D_MODEL), fixed))
        args.append(gain.reshape(1, D_MODEL))
        out_specs += [rows_p, rows_s]
        out_shape += [jax.ShapeDtypeStruct((M_PROMPT, D_MODEL), BF16),
                      jax.ShapeDtypeStruct((DEC_BATCH, D_MODEL), BF16)]
    return pl.pallas_call(
        functools.partial(_proj_res_kernel, n_i=n_i, with_norm=with_norm),
        grid=(n_i,),
        in_specs=in_specs,
        out_specs=out_specs,
        out_shape=out_shape,
        compiler_params=_cparams("arbitrary"),
        name=name,
    )(*args)


def _split_cat(x):
    hi = x.astype(BF16)
    lo = (x - hi.astype(F32)).astype(BF16)
    return jnp.concatenate([hi, lo], axis=1)


def _rope_mats():
    j = jnp.arange(2 * LANES)[:, None] % LANES
    l = jnp.arange(LANES)[None, :]
    head_sum = (j // HEAD_DIM == l // HEAD_DIM).astype(BF16)
    src = jnp.where(l % HEAD_DIM < HEAD_DIM // 2, l + HEAD_DIM // 2, l - HEAD_DIM // 2)
    half_swap = (j == src).astype(BF16)
    return head_sum, half_swap


def _norm_rope_chunk(xc, gain, cos, sin, hs_ref, sw_ref):
    ms = _dot(_split_cat(xc * xc), hs_ref[...]) * (1.0 / HEAD_DIM)
    y = (xc * lax.rsqrt(ms + EPS)) * gain
    swapped = _dot(_split_cat(y), sw_ref[...])
    return y * cos + swapped * sin


def _spread_heads(xc, own_lo):
    lane = lax.broadcasted_iota(jnp.int32, xc.shape, 1)
    keep = (lane < HEAD_DIM) if own_lo else (lane >= HEAD_DIM)
    nat = jnp.where(keep, xc, 0.0)
    rol = pltpu.roll(nat, HEAD_DIM, axis=1)
    parts = (nat, rol) if own_lo else (rol, nat)
    return jnp.concatenate(parts, axis=0).astype(BF16)


def _attn_core(qs, kfull, vfull, mask_t, sink_ref):
    rows = qs[0].shape[0]
    nkeys = kfull.shape[0]
    outs = [None] * 8
    for kh in range(A_KV_HEADS):
        c0 = (kh // 2) * LANES
        kk = _spread_heads(kfull[:, c0:c0 + LANES], kh % 2 == 0)
        vv = _spread_heads(vfull[:, c0:c0 + LANES], kh % 2 == 0)
        lhs = jnp.concatenate([qs[2 * kh], qs[2 * kh + 1]], axis=0)
        s = _dot_nt(kk, lhs)
        prow = []
        for half in range(2):
            pcol = []
            for cc in range(2):
                sb = s[half * nkeys:(half + 1) * nkeys, cc * rows:(cc + 1) * rows]
                sb = jnp.where(mask_t, sb, NEG_INF)
                sink = sink_ref[kh * 4 + 2 * cc + half]
                m = jnp.maximum(jnp.max(sb, axis=0, keepdims=True), sink)
                e = jnp.exp(sb - m)
                den = jnp.sum(e, axis=0, keepdims=True) + jnp.exp(sink - m)
                pcol.append((e / den).astype(BF16))
            prow.append(jnp.concatenate(pcol, axis=1))
        p = jnp.concatenate(prow, axis=0)
        o = _dot_tn(p, vv)
        outs[2 * kh] = o[0:rows]
        outs[2 * kh + 1] = o[rows:2 * rows]
    return outs


def _even_prompt_kernel(sink_ref, z_ref, cos_ref, sin_ref, qg_ref, kg_ref, cw_ref, hs_ref, sw_ref, wo_ref,
                        o_ref, nk_ref, nv_ref, nc_ref, wo_bf_ref, kf_ref, vf_ref, ub_ref):
    i = pl.program_id(1)
    wo_bf_ref[...] = wo_ref[...].astype(BF16)

    @pl.when(i == 0)
    def _():
        kf_ref[...] = jnp.zeros_like(kf_ref)
        vf_ref[...] = jnp.zeros_like(vf_ref)
        ub_ref[0:8, :] = jnp.zeros((8, CONV_DIM), F32)

    kk = lax.broadcasted_iota(jnp.int32, (2 * BLK, BLK), 0)
    r = lax.broadcasted_iota(jnp.int32, (2 * BLK, BLK), 1)
    d = kk - r
    band = (d >= 0) & (d <= WINDOW)
    o0 = A_WIDTH + 2 * A_KV_WIDTH
    k = v = None
    for t in range(MIX_SUB):
        rows = slice(t * BLK, (t + 1) * BLK)
        kf_ref[0:BLK, :] = kf_ref[BLK:2 * BLK, :]
        vf_ref[0:BLK, :] = vf_ref[BLK:2 * BLK, :]
        cos = cos_ref[rows, :]
        sin = sin_ref[rows, :]
        k = jnp.concatenate(
            [_norm_rope_chunk(z_ref[rows, A_WIDTH + c * LANES:A_WIDTH + (c + 1) * LANES], kg_ref[...], cos, sin,
                              hs_ref, sw_ref)
             for c in range(A_KV_WIDTH // LANES)], axis=1)
        v = z_ref[rows, A_WIDTH + A_KV_WIDTH:A_WIDTH + 2 * A_KV_WIDTH]
        kf_ref[BLK:2 * BLK, :] = k
        vf_ref[BLK:2 * BLK, :] = v
        qs = [(_norm_rope_chunk(z_ref[rows, c * LANES:(c + 1) * LANES], qg_ref[...], cos, sin, hs_ref, sw_ref)
               * (HEAD_DIM ** -0.5)).astype(BF16) for c in range(A_WIDTH // LANES)]
        mask_t = band & ((kk >= BLK) | (i > 0)) if t == 0 else band
        outs = _attn_core(qs, kf_ref[...], vf_ref[...], mask_t, sink_ref)
        for c in range(A_WIDTH // LANES):
            o_ref[rows, c * LANES:(c + 1) * LANES] = outs[c].astype(o_ref.dtype)

        u = z_ref[rows, o0 + CONV_DIM:o0 + 2 * CONV_DIM] * z_ref[rows, o0 + 2 * CONV_DIM:o0 + 3 * CONV_DIM]
        ub_ref[8:8 + BLK, :] = u
        y = ub_ref[6:6 + BLK, :] * cw_ref[0:1, :]
        y = y + ub_ref[7:7 + BLK, :] * cw_ref[1:2, :]
        y = y + ub_ref[8:8 + BLK, :] * cw_ref[2:3, :]
        o_ref[rows, A_WIDTH:A_WIDTH + CONV_DIM] = (z_ref[rows, o0:o0 + CONV_DIM] * y).astype(o_ref.dtype)
        ub_ref[0:8, :] = ub_ref[BLK:BLK + 8, :]

    @pl.when(i == pl.num_programs(1) - 1)
    def _():
        nk_ref[...] = k
        nv_ref[...] = v
        nc_ref[...] = ub_ref[BLK + 6:BLK + 8, :]


def _w_slab_specs(w_out, layer, n_step):
    slab = w_out.shape[1] // (BATCH * n_step)
    return (pl.BlockSpec((None, slab, w_out.shape[2]), lambda b, i: (layer, b * n_step + i, 0)),
            pl.BlockSpec((slab, w_out.shape[2]), lambda b, i: (b * n_step + i, 0)),
            jax.ShapeDtypeStruct(w_out.shape[1:], BF16))


def _even_prompt(z_p, cos, sin, qg, kg, cw, sinks, mats, w_out, layer):
    n_step = NBLK // MIX_SUB
    step_rows = MIX_SUB * BLK
    blk = lambda b, i: (b * n_step + i, 0)
    fixed = lambda b, i: (0, 0)
    per_b = lambda b, i: (b, 0, 0)
    wo_in, wo_out, wo_shape = _w_slab_specs(w_out, layer, n_step)
    return pl.pallas_call(
        _even_prompt_kernel,
        grid=(BATCH, n_step),
        in_specs=[pl.BlockSpec(memory_space=pltpu.SMEM),
                  pl.BlockSpec((step_rows, EVEN_IN), blk),
                  pl.BlockSpec((step_rows, LANES), lambda b, i: (i, 0)),
                  pl.BlockSpec((step_rows, LANES), lambda b, i: (i, 0)),
                  pl.BlockSpec((1, LANES), fixed),
                  pl.BlockSpec((1, LANES), fixed),
                  pl.BlockSpec((CONV_W, CONV_DIM), fixed),
                  pl.BlockSpec((2 * LANES, LANES), fixed),
                  pl.BlockSpec((2 * LANES, LANES), fixed),
                  wo_in],
        out_specs=[pl.BlockSpec((step_rows, D_MODEL), blk),
                   pl.BlockSpec((None, BLK, A_KV_WIDTH), per_b),
                   pl.BlockSpec((None, BLK, A_KV_WIDTH), per_b),
                   pl.BlockSpec((None, CONV_W - 1, CONV_DIM), per_b),
                   wo_out],
        out_shape=[jax.ShapeDtypeStruct((M_PROMPT, D_MODEL), BF16),
                   jax.ShapeDtypeStruct((BATCH, BLK, A_KV_WIDTH), F32),
                   jax.ShapeDtypeStruct((BATCH, BLK, A_KV_WIDTH), F32),
                   jax.ShapeDtypeStruct((BATCH, CONV_W - 1, CONV_DIM), F32),
                   wo_shape],
        scratch_shapes=[pltpu.VMEM((2 * BLK, A_KV_WIDTH), F32),
                        pltpu.VMEM((2 * BLK, A_KV_WIDTH), F32),
                        pltpu.VMEM((BLK + 8, CONV_DIM), F32)],
        compiler_params=_cparams("arbitrary", "arbitrary"),
        name="even_prompt",
    )(sinks, z_p, cos, sin, qg, kg, cw, *mats, w_out)


SROWS = LANES
N_SAMPLE_IN = 12
SAMPLE_BB = 8


def _even_sample_kernel(*refs):
    (sink_ref, z_ref, cos_ref, sin_ref, qg_ref, kg_ref, cs_ref, cw_ref, hs_ref, sw_ref,
     kc_ref, vc_ref) = refs[:N_SAMPLE_IN]
    m_ref, ncs_ref, nk_ref, nv_ref, q_all, k_all = refs[-6:]
    step = pl.program_id(0)
    lc = kc_ref.shape[1]

    @pl.when(step == 0)
    def _():
        cos = cos_ref[...]
        sin = sin_ref[...]
        for c in range(A_WIDTH // LANES):
            q = _norm_rope_chunk(z_ref[:, c * LANES:(c + 1) * LANES], qg_ref[...], cos, sin, hs_ref, sw_ref)
            q_all[:, c * LANES:(c + 1) * LANES] = q * (HEAD_DIM ** -0.5)
        for c in range(A_KV_WIDTH // LANES):
            k_all[:, c * LANES:(c + 1) * LANES] = _norm_rope_chunk(
                z_ref[:, A_WIDTH + c * LANES:A_WIDTH + (c + 1) * LANES], kg_ref[...], cos, sin, hs_ref, sw_ref)
        o0 = A_WIDTH + 2 * A_KV_WIDTH
        u = z_ref[:, o0 + CONV_DIM:o0 + 2 * CONV_DIM] * z_ref[:, o0 + 2 * CONV_DIM:o0 + 3 * CONV_DIM]
        y = cs_ref[:, 0:CONV_DIM] * cw_ref[0:1, :]
        y = y + cs_ref[:, CONV_DIM:2 * CONV_DIM] * cw_ref[1:2, :]
        y = y + u * cw_ref[2:3, :]
        m_ref[:, A_WIDTH:A_WIDTH + CONV_DIM] = z_ref[:, o0:o0 + CONV_DIM] * y
        m_ref[:, 0:A_WIDTH] = jnp.zeros((DEC_BATCH, A_WIDTH), F32)
        ncs_ref[:, 0:CONV_DIM] = cs_ref[:, CONV_DIM:2 * CONV_DIM]
        ncs_ref[:, CONV_DIM:2 * CONV_DIM] = u

    kk = lax.broadcasted_iota(jnp.int32, (2 * lc, SROWS), 0)
    mask_t = (kk <= lc) & (lc - kk <= WINDOW)
    row = lax.broadcasted_iota(jnp.int32, (lc, A_KV_WIDTH), 0)
    grp = pl.ds(pl.multiple_of((step * SAMPLE_BB // 8) * 8, 8), 8)
    for u in range(SAMPLE_BB):
        mine = lax.broadcasted_iota(jnp.int32, (8, 1), 0) == (step * SAMPLE_BB + u) % 8

        def pick(tile, mine=mine):
            return jnp.sum(jnp.where(mine, tile, 0.0), axis=0, keepdims=True)

        kn = pick(k_all[grp, :])
        vn = pick(z_ref[grp, A_WIDTH + A_KV_WIDTH:A_WIDTH + 2 * A_KV_WIDTH])
        kfull = jnp.concatenate([kc_ref[u], jnp.where(row == 0, kn, 0.0)], axis=0)
        vfull = jnp.concatenate([vc_ref[u], jnp.where(row == 0, vn, 0.0)], axis=0)
        qs = [jnp.broadcast_to(pick(q_all[grp, c * LANES:(c + 1) * LANES]), (SROWS, LANES)).astype(BF16)
              for c in range(A_WIDTH // LANES)]
        outs = _attn_core(qs, kfull, vfull, mask_t, sink_ref)
        for c in range(A_WIDTH // LANES):
            cols = slice(c * LANES, (c + 1) * LANES)
            m_ref[grp, cols] = jnp.where(mine, outs[c][0:8], m_ref[grp, cols])
        nk_ref[u, 0:lc - 1, :] = kc_ref[u, 1:lc, :]
        nk_ref[u, lc - 1:lc, :] = kn
        nv_ref[u, 0:lc - 1, :] = vc_ref[u, 1:lc, :]
        nv_ref[u, lc - 1:lc, :] = vn


def _even_sample(sinks, z_s, cos, sin, qg, kg, conv_state, cw, mats, k_cache, v_cache, li, prev):
    lc = k_cache.shape[2]
    whole = lambda a: pl.BlockSpec(a.shape, lambda b: (0,) * a.ndim)
    cache = pl.BlockSpec((None, SAMPLE_BB, lc, A_KV_WIDTH), lambda b: (li, b, 0, 0))
    cs = conv_state.reshape(DEC_BATCH, (CONV_W - 1) * CONV_DIM)
    args = [sinks, z_s, cos, sin, qg, kg, cs, cw, *mats, k_cache, v_cache]
    assert len(args) == N_SAMPLE_IN
    in_specs = [pl.BlockSpec(memory_space=pltpu.SMEM)] + [whole(a) for a in args[1:N_SAMPLE_IN - 2]] + [cache, cache]
    aliases = {}
    if prev is not None:
        aliases = {len(args): 2, len(args) + 1: 3}
        in_specs += [pl.BlockSpec(memory_space=pl.ANY)] * 2
        args += list(prev)
    rows = lambda w: pl.BlockSpec((DEC_BATCH, w), lambda b: (0, 0))
    m_s, ncs, nk, nv = pl.pallas_call(
        _even_sample_kernel,
        grid=(DEC_BATCH // SAMPLE_BB,),
        in_specs=in_specs,
        out_specs=[rows(D_MODEL), rows((CONV_W - 1) * CONV_DIM), cache, cache],
        out_shape=[jax.ShapeDtypeStruct((DEC_BATCH, D_MODEL), F32),
                   jax.ShapeDtypeStruct((DEC_BATCH, (CONV_W - 1) * CONV_DIM), F32),
                   jax.ShapeDtypeStruct(k_cache.shape, F32),
                   jax.ShapeDtypeStruct(v_cache.shape, F32)],
        scratch_shapes=[pltpu.VMEM((DEC_BATCH, A_WIDTH), F32), pltpu.VMEM((DEC_BATCH, A_KV_WIDTH), F32)],
        input_output_aliases=aliases,
        compiler_params=_cparams("arbitrary"),
        name="even_sample",
    )(*args)
    return m_s, ncs.reshape(DEC_BATCH, CONV_W - 1, CONV_DIM), nk, nv


def _log_decay(zr, wa_ref, ba_ref):
    pre = _dot(zr.astype(BF16), wa_ref[...]) + ba_ref[...]
    return (jnp.minimum(pre, 0.0) - jnp.log1p(jnp.exp(-jnp.abs(pre)))) * (1.0 / GLA_TAU)


def _split_bf16(x):
    hi = x.astype(BF16)
    lo = (x - hi.astype(F32)).astype(BF16)
    return hi, lo


def _row_to_col(row):
    n = row.shape[1]
    r = lax.broadcasted_iota(jnp.int32, (n, n), 0)
    c = lax.broadcasted_iota(jnp.int32, (n, n), 1)
    return jnp.sum(jnp.where(r == c, jnp.broadcast_to(row, (n, n)), 0.0), axis=-1, keepdims=True)


def _head_rmsnorm_gate(o, gain, zg):
    ms = jnp.mean(o * o, axis=-1, keepdims=True)
    return ((o * lax.rsqrt(ms + EPS)) * gain) * _silu(zg)


def _odd_prompt_kernel(zm_ref, zpr_ref, wa_ref, ba_ref, gain_ref, wp_ref, ps_ref, tri_ref, band_ref, wo_ref,
                       o_ref, s_ref, np_ref, wo_bf_ref, pbh_ref, pbl_ref):
    i = pl.program_id(1)
    wo_bf_ref[...] = wo_ref[...].astype(BF16)

    @pl.when(i == 0)
    def _():
        s_ref[...] = jnp.zeros_like(s_ref)
        pbh_ref[0:BLK, :] = jnp.zeros((BLK, POOL_DIM), BF16)
        pbl_ref[0:BLK, :] = jnp.zeros((BLK, POOL_DIM), BF16)

    x = None
    for t in range(MIX_SUB):
        rows = pl.ds(t * BLK, BLK)
        x = _odd_block(zm_ref.at[rows], zpr_ref.at[rows], wa_ref, ba_ref, gain_ref, wp_ref, ps_ref,
                       tri_ref, band_ref, o_ref.at[rows], s_ref, pbh_ref, pbl_ref, (i * MIX_SUB + t) * BLK)

    @pl.when(i == pl.num_programs(1) - 1)
    def _():
        np_ref[...] = x[BLK - POOL_BUF:BLK, :]


def _odd_block(zm_ref, zpr_ref, wa_ref, ba_ref, gain_ref, wp_ref, ps_ref, tri_ref, band_ref,
               o_ref, s_ref, pbh_ref, pbl_ref, t0):
    n_ch = BLK // GLA_CHUNK
    gk = _log_decay(zpr_ref[:, POOL_DIM:POOL_RANK_W], wa_ref, ba_ref)
    r = lax.broadcasted_iota(jnp.int32, (BLK, BLK), 0)
    c = lax.broadcasted_iota(jnp.int32, (BLK, BLK), 1)
    causal = (r >= c) & (r // GLA_CHUNK == c // GLA_CHUNK)
    g_hi, g_lo = _split_bf16(gk)
    b = _dot(tri_ref[...], jnp.concatenate([g_hi, g_lo], axis=0))
    b_last = [b[(ch + 1) * GLA_CHUNK - 1:(ch + 1) * GLA_CHUNK, :] for ch in range(n_ch)]
    b_end = jnp.concatenate([jnp.broadcast_to(bl, (GLA_CHUNK, GLA_K_WIDTH)) for bl in b_last], axis=0)
    o_k, o_v, o_g = GLA_K_WIDTH, 2 * GLA_K_WIDTH, 2 * GLA_K_WIDTH + GLA_WIDTH
    zq = zm_ref[:, 0:GLA_K_WIDTH]
    zk = zm_ref[:, o_k:o_k + GLA_K_WIDTH]
    qd = ((zq * (GLA_DK ** -0.5)) * jnp.exp(b)).astype(BF16)
    kd = (zk * jnp.exp(-b)).astype(BF16)
    k2 = (zk * jnp.exp(b_end - b)).astype(BF16)
    sub = lax.broadcasted_iota(jnp.int32, (8, LANES), 0)
    sel = jnp.where(sub < 2, 1.0, 0.0).astype(BF16)
    hk = [slice(h * GLA_DK, (h + 1) * GLA_DK) for h in range(GLA_HEADS)]
    chunk = [slice(ch * GLA_CHUNK, (ch + 1) * GLA_CHUNK) for ch in range(n_ch)]
    v_hs = [zm_ref[:, o_v + h * GLA_DV:o_v + (h + 1) * GLA_DV].astype(BF16) for h in range(GLA_HEADS)]
    atts = [_dot_nt(qd[:, hk[h]], kd[:, hk[h]]) for h in range(GLA_HEADS)]
    log_decs = []
    for h in range(GLA_HEADS):
        for ch in range(n_ch):
            bl = b_last[ch][:, hk[h]]
            bl_hi = bl.astype(BF16).astype(F32)
            rows8 = jnp.where(sub == 0, bl_hi, jnp.where(sub == 1, bl - bl_hi, 0.0)).astype(BF16)
            log_decs.append(_dot_tn(rows8, sel))
    deltas = [[_dot_tn(k2[chunk[ch], hk[h]], v_hs[h][chunk[ch]]) for ch in range(n_ch)]
              for h in range(GLA_HEADS)]
    o_intras = [_dot(jnp.where(causal, atts[h], 0.0).astype(BF16), v_hs[h]) for h in range(GLA_HEADS)]
    o_inters = []
    for h in range(GLA_HEADS):
        s_h = s_ref[h]
        parts = []
        for ch in range(n_ch):
            parts.append(_dot(qd[chunk[ch], hk[h]], s_h.astype(BF16)))
            dec = jnp.exp(log_decs[h * n_ch + ch])
            s_h = jnp.concatenate([dec, dec], axis=1) * s_h + deltas[h][ch]
        s_ref[h] = s_h
        o_inters.append(jnp.concatenate(parts, axis=0))
    for h in range(GLA_HEADS):
        vs = slice(h * GLA_DV, (h + 1) * GLA_DV)
        zg = zm_ref[:, o_g + h * GLA_DV:o_g + (h + 1) * GLA_DV]
        o_ref[:, vs] = _head_rmsnorm_gate(o_intras[h] + o_inters[h], gain_ref[...], zg).astype(o_ref.dtype)

    x = zpr_ref[:, 0:POOL_DIM]
    x_hi, x_lo = _split_bf16(x)
    pbh_ref[BLK:2 * BLK, :] = x_hi
    pbl_ref[BLK:2 * BLK, :] = x_lo
    t_glob = t0 + lax.broadcasted_iota(jnp.int32, (BLK, 1), 0)
    groups = [slice(g * POOL_GDIM, (g + 1) * POOL_GDIM) for g in range(len(POOL_WINDOWS))]
    sums = [_dot(band_ref[g], jnp.concatenate([pbh_ref[:, groups[g]], pbl_ref[:, groups[g]]], axis=0))
            for g in range(len(POOL_WINDOWS))]
    ys = []
    for g, w in enumerate(POOL_WINDOWS):
        inv_cnt = 1.0 / jnp.minimum(w, t_glob + 1).astype(F32)
        d = sums[g] * inv_cnt - x[:, groups[g]]
        ys.append(_dot(d.astype(BF16), wp_ref[g]))
    for g in range(len(POOL_WINDOWS)):
        o_ref[:, GLA_WIDTH + g * POOL_GDIM:GLA_WIDTH + (g + 1) * POOL_GDIM] = (
            ys[g] * ps_ref[:, groups[g]]).astype(o_ref.dtype)
    pbh_ref[0:BLK, :] = x_hi
    pbl_ref[0:BLK, :] = x_lo
    return x


def _odd_block_mats():
    r = jnp.arange(BLK)[:, None]
    c = jnp.arange(BLK)[None, :]
    tri = ((r >= c) & (r // GLA_CHUNK == c // GLA_CHUNK)).astype(BF16)
    back = r + BLK - jnp.arange(2 * BLK)[None, :]
    bands = [((back >= 0) & (back < w)).astype(BF16) for w in POOL_WINDOWS]
    return (jnp.concatenate([tri, tri], axis=1),
            jnp.stack([jnp.concatenate([bd, bd], axis=1) for bd in bands]))


def _odd_prompt(zm_p, zpr_p, wa, ba, gain, wp, ps, w_out, layer):
    n_step = NBLK // MIX_SUB
    step_rows = MIX_SUB * BLK
    blk = lambda b, i: (b * n_step + i, 0)
    fixed2 = lambda b, i: (0, 0)
    wo_in, wo_out, wo_shape = _w_slab_specs(w_out, layer, n_step)
    tri2, bands2 = _odd_block_mats()
    return pl.pallas_call(
        _odd_prompt_kernel,
        grid=(BATCH, n_step),
        in_specs=[pl.BlockSpec((step_rows, ODD_MAIN), blk),
                  pl.BlockSpec((step_rows, POOL_RANK_W), blk),
                  pl.BlockSpec((LANES, GLA_K_WIDTH), fixed2),
                  pl.BlockSpec((1, GLA_K_WIDTH), fixed2),
                  pl.BlockSpec((1, GLA_DV), fixed2),
                  pl.BlockSpec((4, POOL_GDIM, POOL_GDIM), lambda b, i: (0, 0, 0)),
                  pl.BlockSpec((1, POOL_DIM), fixed2),
                  pl.BlockSpec(tri2.shape, fixed2),
                  pl.BlockSpec(bands2.shape, lambda b, i: (0, 0, 0)),
                  wo_in],
        out_specs=[pl.BlockSpec((step_rows, D_MODEL), blk),
                   pl.BlockSpec((None, GLA_HEADS, GLA_DK, GLA_DV), lambda b, i: (b, 0, 0, 0)),
                   pl.BlockSpec((None, POOL_BUF, POOL_DIM), lambda b, i: (b, 0, 0)),
                   wo_out],
        out_shape=[jax.ShapeDtypeStruct((M_PROMPT, D_MODEL), BF16),
                   jax.ShapeDtypeStruct((BATCH, GLA_HEADS, GLA_DK, GLA_DV), F32),
                   jax.ShapeDtypeStruct((BATCH, POOL_BUF, POOL_DIM), F32),
                   wo_shape],
        scratch_shapes=[pltpu.VMEM((2 * BLK, POOL_DIM), BF16), pltpu.VMEM((2 * BLK, POOL_DIM), BF16)],
        compiler_params=_cparams("arbitrary", "arbitrary"),
        name="odd_prompt",
    )(zm_p, zpr_p, wa, ba, gain, wp, ps, tri2, bands2, w_out)


STATE_BB = 8
N_ODD_SAMPLE_IN = 9


def _odd_sample_kernel(*refs):
    zm_ref, zpr_ref, wa_ref, ba_ref, gain_ref, wp_ref, ps_ref, pool_ref, s_ref = refs[:N_ODD_SAMPLE_IN]
    m_ref, np_ref, ns_ref, dec_all, q_all, o_rows = refs[-6:]
    step = pl.program_id(0)
    n_prev = pool_ref.shape[1] // POOL_DIM

    @pl.when(step == 0)
    def _():
        dec_all[...] = jnp.exp(_log_decay(zpr_ref[:, POOL_DIM:POOL_RANK_W], wa_ref, ba_ref))
        q_all[...] = zm_ref[:, 0:GLA_K_WIDTH] * (GLA_DK ** -0.5)
        for g, w in enumerate(POOL_WINDOWS):
            gs = slice(g * POOL_GDIM, (g + 1) * POOL_GDIM)
            x = zpr_ref[:, gs]
            first = n_prev - (w - 1)
            s = pool_ref[:, first * POOL_DIM + g * POOL_GDIM:first * POOL_DIM + (g + 1) * POOL_GDIM]
            for jj in range(first + 1, n_prev):
                s = s + pool_ref[:, jj * POOL_DIM + g * POOL_GDIM:jj * POOL_DIM + (g + 1) * POOL_GDIM]
            s = s + x
            d = s / float(min(w, n_prev + 1)) - x
            m_ref[:, GLA_WIDTH + g * POOL_GDIM:GLA_WIDTH + (g + 1) * POOL_GDIM] = (
                _dot(d.astype(BF16), wp_ref[g]) * ps_ref[:, gs])
        np_ref[:, 0:(n_prev - 1) * POOL_DIM] = pool_ref[:, POOL_DIM:n_prev * POOL_DIM]
        np_ref[:, (n_prev - 1) * POOL_DIM:n_prev * POOL_DIM] = zpr_ref[:, 0:POOL_DIM]

    grp = pl.ds(pl.multiple_of(step * STATE_BB, STATE_BB), STATE_BB)
    dec = dec_all[grp, :]
    q = q_all[grp, :]
    k = zm_ref[grp, GLA_K_WIDTH:2 * GLA_K_WIDTH]
    v = zm_ref[grp, 2 * GLA_K_WIDTH:2 * GLA_K_WIDTH + GLA_WIDTH]
    for bb in range(STATE_BB):
        for h in range(GLA_HEADS):
            ks = slice(h * GLA_DK, (h + 1) * GLA_DK)
            vs = slice(h * GLA_DV, (h + 1) * GLA_DV)
            s_new = (_row_to_col(dec[bb:bb + 1, ks]) * s_ref[bb, h]
                     + _row_to_col(k[bb:bb + 1, ks]) * v[bb:bb + 1, vs])
            ns_ref[bb, h] = s_new
            o_rows[bb:bb + 1, vs] = jnp.sum(_row_to_col(q[bb:bb + 1, ks]) * s_new, axis=0, keepdims=True)
    o_g = 2 * GLA_K_WIDTH + GLA_WIDTH
    for h in range(GLA_HEADS):
        vs = slice(h * GLA_DV, (h + 1) * GLA_DV)
        zg = zm_ref[grp, o_g + h * GLA_DV:o_g + (h + 1) * GLA_DV]
        m_ref[grp, vs] = _head_rmsnorm_gate(o_rows[:, vs], gain_ref[...], zg)


def _odd_sample(zm_s, zpr_s, wa, ba, gain, wp, ps, pool_state, state, li, prev):
    whole = lambda a: pl.BlockSpec(a.shape, lambda b: (0,) * a.ndim)
    st = pl.BlockSpec((None, STATE_BB, GLA_HEADS, GLA_DK, GLA_DV), lambda b: (li, b, 0, 0, 0))
    pool2d = pool_state.reshape(DEC_BATCH, POOL_BUF * POOL_DIM)
    args = [zm_s, zpr_s, wa, ba, gain, wp, ps, pool2d, state]
    assert len(args) == N_ODD_SAMPLE_IN
    in_specs = [whole(a) for a in args[:-1]] + [st]
    aliases = {}
    if prev is not None:
        aliases = {len(args): 2}
        in_specs.append(pl.BlockSpec(memory_space=pl.ANY))
        args.append(prev)
    rows = lambda w: pl.BlockSpec((DEC_BATCH, w), lambda b: (0, 0))
    m_s, npool, ns = pl.pallas_call(
        _odd_sample_kernel,
        grid=(DEC_BATCH // STATE_BB,),
        in_specs=in_specs,
        out_specs=[rows(D_MODEL), rows(POOL_BUF * POOL_DIM), st],
        out_shape=[jax.ShapeDtypeStruct((DEC_BATCH, D_MODEL), F32),
                   jax.ShapeDtypeStruct((DEC_BATCH, POOL_BUF * POOL_DIM), F32),
                   jax.ShapeDtypeStruct(state.shape, F32)],
        scratch_shapes=[pltpu.VMEM((DEC_BATCH, GLA_K_WIDTH), F32), pltpu.VMEM((DEC_BATCH, GLA_K_WIDTH), F32),
                        pltpu.VMEM((STATE_BB, GLA_WIDTH), F32)],
        input_output_aliases=aliases,
        compiler_params=_cparams("arbitrary"),
        name="odd_sample",
    )(*args)
    return m_s, npool.reshape(DEC_BATCH, POOL_BUF, POOL_DIM), ns


def _rope_tables(pos):
    half = HEAD_DIM // 2
    inv = jnp.power(ROPE_THETA, -jnp.arange(half, dtype=F32) / half)
    ang = pos.astype(F32)[:, None] * inv[None, :]
    c, s = jnp.cos(ang), jnp.sin(ang)
    return jnp.tile(c, (1, 4)), jnp.tile(jnp.concatenate([-s, s], axis=1), (1, 2))


def kernel(x_prompt, x_sample, cache_swa_k, cache_swa_v, state_conv, state_gla, state_pool, norm_mix, norm_ffn, w_in_even, w_out_even, q_norm, k_norm, attn_sinks, conv_w, w_in_odd, w_out_odd, w_alpha_up, b_alpha, gla_out_norm, w_pool, pool_scale, w_gate, w_up, w_down):
    lc = cache_swa_k.shape[2]
    assert x_prompt.shape == (BATCH, SEQ, D_MODEL) and x_sample.shape == (DEC_BATCH, 1, D_MODEL)
    assert lc == BLK == WINDOW, "the prompt's new cache is its last attention block"
    assert state_pool.shape[2] == POOL_BUF and state_conv.shape[2] == CONV_W - 1
    x_p = x_prompt.reshape(M_PROMPT, D_MODEL)
    x_s = x_sample.reshape(DEC_BATCH, D_MODEL)
    cos_p, sin_p = _rope_tables(jnp.arange(SEQ))
    cos_s, sin_s = _rope_tables(PAST_LEN + jnp.arange(1))
    mats = _rope_mats()
    kc_all = cache_swa_k.reshape(N_EVEN, DEC_BATCH, lc, A_KV_WIDTH)
    vc_all = cache_swa_v.reshape(N_EVEN, DEC_BATCH, lc, A_KV_WIDTH)
    o_r = ODD_MAIN
    w_odd_t = jnp.swapaxes(w_in_odd, 1, 2)
    w_odd_pr_t = jnp.concatenate([w_odd_t[:, o_r + GLA_RANK:], w_odd_t[:, o_r:o_r + GLA_RANK],
                                  jnp.zeros((N_ODD, LANES - GLA_RANK, D_MODEL), F32)], axis=1)
    pk, pv, pc, pg, pp, sc, sp = ([] for _ in range(7))
    sk_all = sv_all = sg_all = None

    h_p, h_s = _rmsnorm(x_p, x_s, norm_mix[0])
    for layer in range(DEPTH):
        li = layer // 2
        if layer % 2 == 0:
            z_p, z_s = _dense(h_p, h_s, [(w_in_even, li)], n_cols=EVEN_IN, **TILES["in_even"], name="in_even")
            qg = jnp.tile(q_norm[li], 2).reshape(1, LANES)
            kg = jnp.tile(k_norm[li], 2).reshape(1, LANES)
            m_p, nk, nv, nc, w_out_bf = _even_prompt(z_p, cos_p, sin_p, qg, kg, conv_w[li], attn_sinks[li], mats,
                                                     w_out_even, li)
            pk.append(nk.reshape(BATCH, lc, A_KV_HEADS, HEAD_DIM))
            pv.append(nv.reshape(BATCH, lc, A_KV_HEADS, HEAD_DIM))
            pc.append(nc)
            m_s, nc_s, sk_all, sv_all = _even_sample(
                attn_sinks[li], z_s, cos_s, sin_s, qg, kg, state_conv[li], conv_w[li], mats, kc_all, vc_all, li,
                None if sk_all is None else (sk_all, sv_all))
            sc.append(nc_s)
        else:
            zm_p, zm_s = _dense(h_p, h_s, [(w_odd_t, li)], n_cols=ODD_MAIN, **TILES["in_odd_main"],
                                transposed=True, name="in_odd_main")
            zpr_p, zpr_s = _dense(h_p, h_s, [(w_odd_pr_t, li)], n_cols=POOL_RANK_W, **TILES["in_odd_pool_rank"],
                                  transposed=True, name="in_odd_pool_rank")
            wa = jnp.pad(w_alpha_up[li], ((0, LANES - GLA_RANK), (0, 0))).astype(BF16)
            ba = b_alpha[li].reshape(1, GLA_K_WIDTH)
            gain = gla_out_norm[li].reshape(1, GLA_DV)
            wp = w_pool[li].astype(BF16)
            ps = pool_scale[li].reshape(1, POOL_DIM)
            m_p, ng, npool, w_out_bf = _odd_prompt(zm_p, zpr_p, wa, ba, gain, wp, ps, w_out_odd, li)
            pg.append(ng)
            pp.append(npool)
            m_s, np_s, sg_all = _odd_sample(zm_s, zpr_s, wa, ba, gain, wp, ps, state_pool[li], state_gla, li,
                                            sg_all)
            sp.append(np_s)
        x_p, x_s, h_p, h_s = _proj_res(m_p, m_s, (w_out_bf, None), (x_p, x_s), norm_ffn[layer],
                                       **TILES["out_proj"], name="out_proj")
        a_p, a_s, w_down_bf = _dense(h_p, h_s, [(w_gate, layer), (w_up, layer)], n_cols=D_FF, **TILES["ffn_up"],
                                     mode="swiglu", out_dtype=BF16, side_cast=(w_down, layer), name="ffn_up")
        if layer + 1 < DEPTH:
            x_p, x_s, h_p, h_s = _proj_res(a_p, a_s, (w_down_bf, None), (x_p, x_s),
                                           norm_mix[layer + 1], **TILES["ffn_down"], name="ffn_down")
        else:
            x_p, x_s = _proj_res(a_p, a_s, (w_down_bf, None), (x_p, x_s), None,
                                 **TILES["ffn_down"], name="ffn_down_last")

    st = lambda parts: jnp.stack(parts)
    cache5 = lambda a: a.reshape(N_EVEN, DEC_BATCH, lc, A_KV_HEADS, HEAD_DIM)
    return (x_p.reshape(BATCH, SEQ, D_MODEL), x_s.reshape(DEC_BATCH, 1, D_MODEL),
            st(pk), st(pv), st(pc), st(pg), st(pp), cache5(sk_all), cache5(sv_all), st(sc), sg_all, st(sp))
```

```python
import functools

import jax
import jax.numpy as jnp
from jax import lax
from jax.experimental import pallas as pl
from jax.experimental.pallas import tpu as pltpu

F32 = jnp.float32
BF16 = jnp.bfloat16

D_MODEL = 2048
BATCH = 4
SEQ = 2048
DEPTH = 4
DEC_BATCH = 32
PAST_LEN = 16384
N_EVEN = 2
N_ODD = 2
EPS = 1e-6
NEG_INF = -1e30
A_HEADS = 16
A_KV_HEADS = 4
HEAD_DIM = 64
A_WIDTH = A_HEADS * HEAD_DIM
A_KV_WIDTH = A_KV_HEADS * HEAD_DIM
WINDOW = 128
ROPE_THETA = 10000.0
CONV_DIM = D_MODEL // 2
CONV_W = 3
GLA_HEADS = 4
GLA_WIDTH = D_MODEL // 2
GLA_DV = GLA_WIDTH // GLA_HEADS
GLA_DK = GLA_DV // 2
GLA_K_WIDTH = GLA_HEADS * GLA_DK
GLA_RANK = 16
GLA_TAU = 16.0
GLA_CHUNK = 64
POOL_DIM = D_MODEL // 2
POOL_WINDOWS = (2, 4, 8, 16)
POOL_GDIM = POOL_DIM // 4
POOL_BUF = 15
D_FF = 5632
EVEN_IN = A_WIDTH + 2 * A_KV_WIDTH + 3 * CONV_DIM
ODD_MAIN = 2 * GLA_K_WIDTH + 2 * GLA_WIDTH

M_PROMPT = BATCH * SEQ
LANES = 128
BLK = 128
NBLK = SEQ // BLK
MIX_SUB = 4
POOL_RANK_W = POOL_DIM + LANES
VMEM_LIMIT = 58 * 1024 * 1024

TILES = {
    "in_even": dict(tm=1024, tn=1536),
    "in_odd_main": dict(tm=1024, tn=ODD_MAIN // 2),
    "in_odd_pool_rank": dict(tm=1024, tn=POOL_RANK_W),
    "ffn_up": dict(tm=2048, tn=512, sub=2),
    "out_proj": dict(tm=512),
    "ffn_down": dict(tm=512),
}


def _cparams(*sem):
    return pltpu.CompilerParams(dimension_semantics=sem, vmem_limit_bytes=VMEM_LIMIT)


def _dot(a, b):
    return jnp.dot(a, b, preferred_element_type=F32)


def _dot_nt(a, b):
    return lax.dot_general(a, b, (((1,), (1,)), ((), ())), preferred_element_type=F32)


def _dot_tn(a, b):
    return lax.dot_general(a, b, (((0,), (0,)), ((), ())), preferred_element_type=F32)


def _silu(x):
    return x * (1.0 / (1.0 + jnp.exp(-x)))


def _rmsnorm_rows(x, gain):
    ms = jnp.mean(x * x, axis=-1, keepdims=True)
    return (x * lax.rsqrt(ms + EPS)) * gain


def _rmsnorm_kernel(xp_ref, xs_ref, g_ref, op_ref, os_ref, *, n_i):
    op_ref[...] = _rmsnorm_rows(xp_ref[...], g_ref[...]).astype(op_ref.dtype)

    @pl.when(pl.program_id(0) == n_i - 1)
    def _():
        os_ref[...] = _rmsnorm_rows(xs_ref[...], g_ref[...]).astype(os_ref.dtype)


def _rmsnorm(x_p, x_s, gain, *, tm=1024):
    n_i = M_PROMPT // tm
    row = lambda i: (i, 0)
    fixed = lambda i: (0, 0)
    return pl.pallas_call(
        functools.partial(_rmsnorm_kernel, n_i=n_i),
        grid=(n_i,),
        in_specs=[pl.BlockSpec((tm, D_MODEL), row),
                  pl.BlockSpec((DEC_BATCH, D_MODEL), fixed),
                  pl.BlockSpec((1, D_MODEL), fixed)],
        out_specs=[pl.BlockSpec((tm, D_MODEL), row),
                   pl.BlockSpec((DEC_BATCH, D_MODEL), fixed)],
        out_shape=[jax.ShapeDtypeStruct((M_PROMPT, D_MODEL), BF16),
                   jax.ShapeDtypeStruct((DEC_BATCH, D_MODEL), BF16)],
        compiler_params=_cparams("arbitrary"),
        name="rmsnorm",
    )(x_p, x_s, gain.reshape(1, D_MODEL))


XPOSE_COLS = 128


def _dense_kernel(*refs, n_w, mode, n_i, transposed, sub):
    a_p, a_s = refs[0], refs[1]
    w = refs[2:2 + n_w]
    o_p, o_s = refs[2 + n_w], refs[3 + n_w]
    wbf = refs[4 + n_w:4 + 2 * n_w]
    i = pl.program_id(1)

    @pl.when(i == 0)
    def _():
        for k in range(n_w):
            if transposed:
                tn = wbf[k].shape[1]
                for c in range(0, tn, XPOSE_COLS):
                    wbf[k][:, c:c + XPOSE_COLS] = w[k][c:c + XPOSE_COLS, :].T.astype(BF16)
            else:
                wbf[k][...] = w[k][...].astype(BF16)

    def run(a_ref, o_ref, n_sub):
        rows = a_ref.shape[0] // n_sub
        for r in range(n_sub):
            a = a_ref[r * rows:(r + 1) * rows, :]
            if mode == "swiglu":
                y = _silu(_dot(a, wbf[0][...])) * _dot(a, wbf[1][...])
            else:
                y = _dot(a, wbf[0][...])
            o_ref[r * rows:(r + 1) * rows, :] = y.astype(o_ref.dtype)

    run(a_p, o_p, sub)

    @pl.when(i == n_i - 1)
    def _():
        run(a_s, o_s, 1)


def _dense(a_p, a_s, weights, *, n_cols, tm, tn, mode="plain", out_dtype=F32, transposed=False, sub=1, name):
    k_dim = a_p.shape[1]
    n_i = M_PROMPT // tm
    n_j = n_cols // tn
    row = lambda j, i: (i, 0)
    fixed = lambda j, i: (0, 0)
    tile = lambda j, i: (i, j)
    panel = lambda j, i: (0, j)
    in_specs = [pl.BlockSpec((tm, k_dim), row), pl.BlockSpec((DEC_BATCH, k_dim), fixed)]
    args = [a_p, a_s]
    for arr, layer in weights:
        if transposed:
            in_specs.append(pl.BlockSpec((None, tn, k_dim), lambda j, i, layer=layer: (layer, j, 0)))
        else:
            in_specs.append(pl.BlockSpec((None, k_dim, tn), lambda j, i, layer=layer: (layer, 0, j)))
        args.append(arr)
    out_specs = [pl.BlockSpec((tm, tn), tile), pl.BlockSpec((DEC_BATCH, tn), panel)]
    out_shape = [jax.ShapeDtypeStruct((M_PROMPT, n_cols), out_dtype),
                 jax.ShapeDtypeStruct((DEC_BATCH, n_cols), out_dtype)]
    n_w = len(weights)
    return pl.pallas_call(
        functools.partial(_dense_kernel, n_w=n_w, mode=mode, n_i=n_i, transposed=transposed, sub=sub),
        grid=(n_j, n_i),
        in_specs=in_specs,
        out_specs=out_specs,
        out_shape=out_shape,
        scratch_shapes=[pltpu.VMEM((k_dim, tn), BF16) for _ in range(n_w)],
        compiler_params=_cparams("arbitrary", "arbitrary"),
        name=name,
    )(*args)


def _proj_res_kernel(*refs, n_i, with_norm):
    a_p, a_s, w_ref, r_p, r_s = refs[:5]
    if with_norm:
        g_ref, x_p, x_s, h_p, h_s = refs[5:]
    else:
        x_p, x_s = refs[5:]
        g_ref = h_p = h_s = None

    def run(a_ref, r_ref, x_ref, h_ref):
        a = a_ref[...]
        if a.dtype != BF16:
            a = a.astype(BF16)
        x = r_ref[...] + _dot(a, w_ref[...])
        x_ref[...] = x
        if with_norm:
            h_ref[...] = _rmsnorm_rows(x, g_ref[...]).astype(h_ref.dtype)

    run(a_p, r_p, x_p, h_p)

    @pl.when(pl.program_id(0) == n_i - 1)
    def _():
        run(a_s, r_s, x_s, h_s)


def _proj_res(a_p, a_s, weight, res, gain, *, tm, name):
    w_bf, layer = weight
    k_dim = a_p.shape[1]
    if layer is None:
        w_spec = pl.BlockSpec((k_dim, D_MODEL), lambda i: (0, 0), pipeline_mode=pl.Buffered(1))
    else:
        w_spec = pl.BlockSpec((None, k_dim, D_MODEL), lambda i: (layer, 0, 0), pipeline_mode=pl.Buffered(1))
    n_i = M_PROMPT // tm
    with_norm = gain is not None
    row = lambda i: (i, 0)
    fixed = lambda i: (0, 0)
    rows_p = pl.BlockSpec((tm, D_MODEL), row)
    rows_s = pl.BlockSpec((DEC_BATCH, D_MODEL), fixed)
    in_specs = [pl.BlockSpec((tm, k_dim), row), pl.BlockSpec((DEC_BATCH, k_dim), fixed),
                w_spec, rows_p, rows_s]
    args = [a_p, a_s, w_bf, res[0], res[1]]
    out_specs = [rows_p, rows_s]
    out_shape = [jax.ShapeDtypeStruct((M_PROMPT, D_MODEL), F32), jax.ShapeDtypeStruct((DEC_BATCH, D_MODEL), F32)]
    if with_norm:
        in_specs.append(pl.BlockSpec((1, D_MODEL), fixed))
        args.append(gain.reshape(1, D_MODEL))
        out_specs += [rows_p, rows_s]
        out_shape += [jax.ShapeDtypeStruct((M_PROMPT, D_MODEL), BF16),
                      jax.ShapeDtypeStruct((DEC_BATCH, D_MODEL), BF16)]
    return pl.pallas_call(
        functools.partial(_proj_res_kernel, n_i=n_i, with_norm=with_norm),
        grid=(n_i,),
        in_specs=in_specs,
        out_specs=out_specs,
        out_shape=out_shape,
        compiler_params=_cparams("arbitrary"),
        name=name,
    )(*args)


def _split_cat(x):
    hi = x.astype(BF16)
    lo = (x - hi.astype(F32)).astype(BF16)
    return jnp.concatenate([hi, lo], axis=1)


def _rope_mats():
    j = jnp.arange(2 * LANES)[:, None] % LANES
    l = jnp.arange(LANES)[None, :]
    head_sum = (j // HEAD_DIM == l // HEAD_DIM).astype(BF16)
    src = jnp.where(l % HEAD_DIM < HEAD_DIM // 2, l + HEAD_DIM // 2, l - HEAD_DIM // 2)
    half_swap = (j == src).astype(BF16)
    return head_sum, half_swap


def _norm_rope_chunk(xc, gain, cos, sin, hs_ref, sw_ref):
    ms = _dot(_split_cat(xc * xc), hs_ref[...]) * (1.0 / HEAD_DIM)
    y = (xc * lax.rsqrt(ms + EPS)) * gain
    swapped = _dot(_split_cat(y), sw_ref[...])
    return y * cos + swapped * sin


def _spread_heads(xc, own_lo):
    lane = lax.broadcasted_iota(jnp.int32, xc.shape, 1)
    keep = (lane < HEAD_DIM) if own_lo else (lane >= HEAD_DIM)
    nat = jnp.where(keep, xc, 0.0)
    rol = pltpu.roll(nat, HEAD_DIM, axis=1)
    parts = (nat, rol) if own_lo else (rol, nat)
    return jnp.concatenate(parts, axis=0).astype(BF16)


def _attn_core(qs, kfull, vfull, mask_t, sink_ref):
    rows = qs[0].shape[0]
    nkeys = kfull.shape[0]
    outs = [None] * 8
    for kh in range(A_KV_HEADS):
        c0 = (kh // 2) * LANES
        kk = _spread_heads(kfull[:, c0:c0 + LANES], kh % 2 == 0)
        vv = _spread_heads(vfull[:, c0:c0 + LANES], kh % 2 == 0)
        lhs = jnp.concatenate([qs[2 * kh], qs[2 * kh + 1]], axis=0)
        s = _dot_nt(kk, lhs)
        prow = []
        for half in range(2):
            pcol = []
            for cc in range(2):
                sb = s[half * nkeys:(half + 1) * nkeys, cc * rows:(cc + 1) * rows]
                sb = jnp.where(mask_t, sb, NEG_INF)
                sink = sink_ref[kh * 4 + 2 * cc + half]
                m = jnp.maximum(jnp.max(sb, axis=0, keepdims=True), sink)
                e = jnp.exp(sb - m)
                den = jnp.sum(e, axis=0, keepdims=True) + jnp.exp(sink - m)
                pcol.append((e / den).astype(BF16))
            prow.append(jnp.concatenate(pcol, axis=1))
        p = jnp.concatenate(prow, axis=0)
        o = _dot_tn(p, vv)
        outs[2 * kh] = o[0:rows]
        outs[2 * kh + 1] = o[rows:2 * rows]
    return outs


def _even_prompt_kernel(sink_ref, z_ref, cos_ref, sin_ref, qg_ref, kg_ref, cw_ref, hs_ref, sw_ref, wo_ref, wd_ref,
                        o_ref, nk_ref, nv_ref, nc_ref, wo_bf_ref, wd_bf_ref, kf_ref, vf_ref, ub_ref):
    i = pl.program_id(1)
    wo_bf_ref[...] = wo_ref[...].astype(BF16)
    wd_bf_ref[...] = wd_ref[...].astype(BF16)

    @pl.when(i == 0)
    def _():
        kf_ref[...] = jnp.zeros_like(kf_ref)
        vf_ref[...] = jnp.zeros_like(vf_ref)
        ub_ref[0:8, :] = jnp.zeros((8, CONV_DIM), F32)

    kk = lax.broadcasted_iota(jnp.int32, (2 * BLK, BLK), 0)
    r = lax.broadcasted_iota(jnp.int32, (2 * BLK, BLK), 1)
    d = kk - r
    band = (d >= 0) & (d <= WINDOW)
    o0 = A_WIDTH + 2 * A_KV_WIDTH
    k = v = None
    for t in range(MIX_SUB):
        rows = slice(t * BLK, (t + 1) * BLK)
        kf_ref[0:BLK, :] = kf_ref[BLK:2 * BLK, :]
        vf_ref[0:BLK, :] = vf_ref[BLK:2 * BLK, :]
        cos = cos_ref[rows, :]
        sin = sin_ref[rows, :]
        k = jnp.concatenate(
            [_norm_rope_chunk(z_ref[rows, A_WIDTH + c * LANES:A_WIDTH + (c + 1) * LANES], kg_ref[...], cos, sin,
                              hs_ref, sw_ref)
             for c in range(A_KV_WIDTH // LANES)], axis=1)
        v = z_ref[rows, A_WIDTH + A_KV_WIDTH:A_WIDTH + 2 * A_KV_WIDTH]
        kf_ref[BLK:2 * BLK, :] = k
        vf_ref[BLK:2 * BLK, :] = v
        qs = [(_norm_rope_chunk(z_ref[rows, c * LANES:(c + 1) * LANES], qg_ref[...], cos, sin, hs_ref, sw_ref)
               * (HEAD_DIM ** -0.5)).astype(BF16) for c in range(A_WIDTH // LANES)]
        mask_t = band & ((kk >= BLK) | (i > 0)) if t == 0 else band
        outs = _attn_core(qs, kf_ref[...], vf_ref[...], mask_t, sink_ref)
        for c in range(A_WIDTH // LANES):
            o_ref[rows, c * LANES:(c + 1) * LANES] = outs[c].astype(o_ref.dtype)

        u = z_ref[rows, o0 + CONV_DIM:o0 + 2 * CONV_DIM] * z_ref[rows, o0 + 2 * CONV_DIM:o0 + 3 * CONV_DIM]
        ub_ref[8:8 + BLK, :] = u
        y = ub_ref[6:6 + BLK, :] * cw_ref[0:1, :]
        y = y + ub_ref[7:7 + BLK, :] * cw_ref[1:2, :]
        y = y + ub_ref[8:8 + BLK, :] * cw_ref[2:3, :]
        o_ref[rows, A_WIDTH:A_WIDTH + CONV_DIM] = (z_ref[rows, o0:o0 + CONV_DIM] * y).astype(o_ref.dtype)
        ub_ref[0:8, :] = ub_ref[BLK:BLK + 8, :]

    @pl.when(i == pl.num_programs(1) - 1)
    def _():
        nk_ref[...] = k
        nv_ref[...] = v
        nc_ref[...] = ub_ref[BLK + 6:BLK + 8, :]


def _w_slab_specs(w_out, layer, n_step):
    slab = w_out.shape[1] // (BATCH * n_step)
    return (pl.BlockSpec((None, slab, w_out.shape[2]), lambda b, i: (layer, b * n_step + i, 0)),
            pl.BlockSpec((slab, w_out.shape[2]), lambda b, i: (b * n_step + i, 0)),
            jax.ShapeDtypeStruct(w_out.shape[1:], BF16))


def _even_prompt(z_p, cos, sin, qg, kg, cw, sinks, mats, w_out, layer, w_dn, dn_layer):
    n_step = NBLK // MIX_SUB
    step_rows = MIX_SUB * BLK
    blk = lambda b, i: (b * n_step + i, 0)
    fixed = lambda b, i: (0, 0)
    per_b = lambda b, i: (b, 0, 0)
    wo_in, wo_out, wo_shape = _w_slab_specs(w_out, layer, n_step)
    wd_in, wd_out, wd_shape = _w_slab_specs(w_dn, dn_layer, n_step)
    return pl.pallas_call(
        _even_prompt_kernel,
        grid=(BATCH, n_step),
        in_specs=[pl.BlockSpec(memory_space=pltpu.SMEM),
                  pl.BlockSpec((step_rows, EVEN_IN), blk),
                  pl.BlockSpec((step_rows, LANES), lambda b, i: (i, 0)),
                  pl.BlockSpec((step_rows, LANES), lambda b, i: (i, 0)),
                  pl.BlockSpec((1, LANES), fixed),
                  pl.BlockSpec((1, LANES), fixed),
                  pl.BlockSpec((CONV_W, CONV_DIM), fixed),
                  pl.BlockSpec((2 * LANES, LANES), fixed),
                  pl.BlockSpec((2 * LANES, LANES), fixed),
                  wo_in, wd_in],
        out_specs=[pl.BlockSpec((step_rows, D_MODEL), blk),
                   pl.BlockSpec((None, BLK, A_KV_WIDTH), per_b),
                   pl.BlockSpec((None, BLK, A_KV_WIDTH), per_b),
                   pl.BlockSpec((None, CONV_W - 1, CONV_DIM), per_b),
                   wo_out, wd_out],
        out_shape=[jax.ShapeDtypeStruct((M_PROMPT, D_MODEL), BF16),
                   jax.ShapeDtypeStruct((BATCH, BLK, A_KV_WIDTH), F32),
                   jax.ShapeDtypeStruct((BATCH, BLK, A_KV_WIDTH), F32),
                   jax.ShapeDtypeStruct((BATCH, CONV_W - 1, CONV_DIM), F32),
                   wo_shape, wd_shape],
        scratch_shapes=[pltpu.VMEM((2 * BLK, A_KV_WIDTH), F32),
                        pltpu.VMEM((2 * BLK, A_KV_WIDTH), F32),
                        pltpu.VMEM((BLK + 8, CONV_DIM), F32)],
        compiler_params=_cparams("arbitrary", "arbitrary"),
        name="even_prompt",
    )(sinks, z_p, cos, sin, qg, kg, cw, *mats, w_out, w_dn)


SROWS = LANES
N_SAMPLE_IN = 12
SAMPLE_BB = 8


def _even_sample_kernel(*refs):
    (sink_ref, z_ref, cos_ref, sin_ref, qg_ref, kg_ref, cs_ref, cw_ref, hs_ref, sw_ref,
     kc_ref, vc_ref) = refs[:N_SAMPLE_IN]
    m_ref, ncs_ref, nk_ref, nv_ref, q_all, k_all = refs[-6:]
    step = pl.program_id(0)
    lc = kc_ref.shape[1]

    @pl.when(step == 0)
    def _():
        cos = cos_ref[...]
        sin = sin_ref[...]
        for c in range(A_WIDTH // LANES):
            q = _norm_rope_chunk(z_ref[:, c * LANES:(c + 1) * LANES], qg_ref[...], cos, sin, hs_ref, sw_ref)
            q_all[:, c * LANES:(c + 1) * LANES] = q * (HEAD_DIM ** -0.5)
        for c in range(A_KV_WIDTH // LANES):
            k_all[:, c * LANES:(c + 1) * LANES] = _norm_rope_chunk(
                z_ref[:, A_WIDTH + c * LANES:A_WIDTH + (c + 1) * LANES], kg_ref[...], cos, sin, hs_ref, sw_ref)
        o0 = A_WIDTH + 2 * A_KV_WIDTH
        u = z_ref[:, o0 + CONV_DIM:o0 + 2 * CONV_DIM] * z_ref[:, o0 + 2 * CONV_DIM:o0 + 3 * CONV_DIM]
        y = cs_ref[:, 0:CONV_DIM] * cw_ref[0:1, :]
        y = y + cs_ref[:, CONV_DIM:2 * CONV_DIM] * cw_ref[1:2, :]
        y = y + u * cw_ref[2:3, :]
        m_ref[:, A_WIDTH:A_WIDTH + CONV_DIM] = z_ref[:, o0:o0 + CONV_DIM] * y
        m_ref[:, 0:A_WIDTH] = jnp.zeros((DEC_BATCH, A_WIDTH), F32)
        ncs_ref[:, 0:CONV_DIM] = cs_ref[:, CONV_DIM:2 * CONV_DIM]
        ncs_ref[:, CONV_DIM:2 * CONV_DIM] = u

    kk = lax.broadcasted_iota(jnp.int32, (2 * lc, SROWS), 0)
    mask_t = (kk <= lc) & (lc - kk <= WINDOW)
    row = lax.broadcasted_iota(jnp.int32, (lc, A_KV_WIDTH), 0)
    grp = pl.ds(pl.multiple_of((step * SAMPLE_BB // 8) * 8, 8), 8)
    for u in range(SAMPLE_BB):
        mine = lax.broadcasted_iota(jnp.int32, (8, 1), 0) == (step * SAMPLE_BB + u) % 8

        def pick(tile, mine=mine):
            return jnp.sum(jnp.where(mine, tile, 0.0), axis=0, keepdims=True)

        kn = pick(k_all[grp, :])
        vn = pick(z_ref[grp, A_WIDTH + A_KV_WIDTH:A_WIDTH + 2 * A_KV_WIDTH])
        kfull = jnp.concatenate([kc_ref[u], jnp.where(row == 0, kn, 0.0)], axis=0)
        vfull = jnp.concatenate([vc_ref[u], jnp.where(row == 0, vn, 0.0)], axis=0)
        qs = [jnp.broadcast_to(pick(q_all[grp, c * LANES:(c + 1) * LANES]), (SROWS, LANES)).astype(BF16)
              for c in range(A_WIDTH // LANES)]
        outs = _attn_core(qs, kfull, vfull, mask_t, sink_ref)
        for c in range(A_WIDTH // LANES):
            cols = slice(c * LANES, (c + 1) * LANES)
            m_ref[grp, cols] = jnp.where(mine, outs[c][0:8], m_ref[grp, cols])
        nk_ref[u, 0:lc - 1, :] = kc_ref[u, 1:lc, :]
        nk_ref[u, lc - 1:lc, :] = kn
        nv_ref[u, 0:lc - 1, :] = vc_ref[u, 1:lc, :]
        nv_ref[u, lc - 1:lc, :] = vn


def _even_sample(sinks, z_s, cos, sin, qg, kg, conv_state, cw, mats, k_cache, v_cache, li, prev):
    lc = k_cache.shape[2]
    whole = lambda a: pl.BlockSpec(a.shape, lambda b: (0,) * a.ndim)
    cache = pl.BlockSpec((None, SAMPLE_BB, lc, A_KV_WIDTH), lambda b: (li, b, 0, 0))
    cs = conv_state.reshape(DEC_BATCH, (CONV_W - 1) * CONV_DIM)
    args = [sinks, z_s, cos, sin, qg, kg, cs, cw, *mats, k_cache, v_cache]
    assert len(args) == N_SAMPLE_IN
    in_specs = [pl.BlockSpec(memory_space=pltpu.SMEM)] + [whole(a) for a in args[1:N_SAMPLE_IN - 2]] + [cache, cache]
    aliases = {len(args): 2, len(args) + 1: 3}
    in_specs += [pl.BlockSpec(memory_space=pl.ANY)] * 2
    args += list(prev)
    rows = lambda w: pl.BlockSpec((DEC_BATCH, w), lambda b: (0, 0))
    m_s, ncs, nk, nv = pl.pallas_call(
        _even_sample_kernel,
        grid=(DEC_BATCH // SAMPLE_BB,),
        in_specs=in_specs,
        out_specs=[rows(D_MODEL), rows((CONV_W - 1) * CONV_DIM), cache, cache],
        out_shape=[jax.ShapeDtypeStruct((DEC_BATCH, D_MODEL), F32),
                   jax.ShapeDtypeStruct((DEC_BATCH, (CONV_W - 1) * CONV_DIM), F32),
                   jax.ShapeDtypeStruct(k_cache.shape, F32),
                   jax.ShapeDtypeStruct(v_cache.shape, F32)],
        scratch_shapes=[pltpu.VMEM((DEC_BATCH, A_WIDTH), F32), pltpu.VMEM((DEC_BATCH, A_KV_WIDTH), F32)],
        input_output_aliases=aliases,
        compiler_params=_cparams("arbitrary"),
        name="even_sample",
    )(*args)
    return m_s, ncs.reshape(DEC_BATCH, CONV_W - 1, CONV_DIM), nk, nv


def _log_decay(zr, wa_ref, ba_ref):
    pre = _dot(zr.astype(BF16), wa_ref[...]) + ba_ref[...]
    return (jnp.minimum(pre, 0.0) - jnp.log1p(jnp.exp(-jnp.abs(pre)))) * (1.0 / GLA_TAU)


def _split_bf16(x):
    hi = x.astype(BF16)
    lo = (x - hi.astype(F32)).astype(BF16)
    return hi, lo


N_PIECES = 3
COL_ROWS = 16


def _rows_to_cols(rows):
    n = rows[0].shape[1]
    assert N_PIECES * len(rows) <= COL_ROWS
    sub = lax.broadcasted_iota(jnp.int32, (COL_ROWS, n), 0)
    stacked = jnp.zeros((COL_ROWS, n), F32)
    for v, x in enumerate(rows):
        rest = x
        for p in range(N_PIECES):
            piece = rest.astype(BF16).astype(F32)
            stacked = jnp.where(sub == v * N_PIECES + p, piece, stacked)
            rest = rest - piece
    r = lax.broadcasted_iota(jnp.int32, (COL_ROWS, len(rows) * n), 0)
    c = lax.broadcasted_iota(jnp.int32, (COL_ROWS, len(rows) * n), 1)
    sel = jnp.where(r // N_PIECES == c // n, 1.0, 0.0).astype(BF16)
    return _dot_tn(stacked.astype(BF16), sel)


def _head_rmsnorm_gate(o, gain, zg):
    ms = jnp.mean(o * o, axis=-1, keepdims=True)
    return ((o * lax.rsqrt(ms + EPS)) * gain) * _silu(zg)


def _odd_prompt_kernel(zm_ref, zpr_ref, wa_ref, ba_ref, gain_ref, wp_ref, ps_ref, tri_ref, band_ref, wo_ref,
                       wd_ref, o_ref, s_ref, np_ref, wo_bf_ref, wd_bf_ref, pbh_ref, pbl_ref):
    i = pl.program_id(1)
    wo_bf_ref[...] = wo_ref[...].astype(BF16)
    wd_bf_ref[...] = wd_ref[...].astype(BF16)

    @pl.when(i == 0)
    def _():
        s_ref[...] = jnp.zeros_like(s_ref)
        pbh_ref[0:BLK, :] = jnp.zeros((BLK, POOL_DIM), BF16)
        pbl_ref[0:BLK, :] = jnp.zeros((BLK, POOL_DIM), BF16)

    x = None
    for t in range(MIX_SUB):
        rows = pl.ds(t * BLK, BLK)
        x = _odd_block(zm_ref.at[rows], zpr_ref.at[rows], wa_ref, ba_ref, gain_ref, wp_ref, ps_ref,
                       tri_ref, band_ref, o_ref.at[rows], s_ref, pbh_ref, pbl_ref, (i * MIX_SUB + t) * BLK)

    @pl.when(i == pl.num_programs(1) - 1)
    def _():
        np_ref[...] = x[BLK - POOL_BUF:BLK, :]


def _odd_block(zm_ref, zpr_ref, wa_ref, ba_ref, gain_ref, wp_ref, ps_ref, tri_ref, band_ref,
               o_ref, s_ref, pbh_ref, pbl_ref, t0):
    n_ch = BLK // GLA_CHUNK
    gk = _log_decay(zpr_ref[:, POOL_DIM:POOL_RANK_W], wa_ref, ba_ref)
    r = lax.broadcasted_iota(jnp.int32, (BLK, BLK), 0)
    c = lax.broadcasted_iota(jnp.int32, (BLK, BLK), 1)
    causal = (r >= c) & (r // GLA_CHUNK == c // GLA_CHUNK)
    g_hi, g_lo = _split_bf16(gk)
    b = _dot(tri_ref[...], jnp.concatenate([g_hi, g_lo], axis=0))
    b_last = [b[(ch + 1) * GLA_CHUNK - 1:(ch + 1) * GLA_CHUNK, :] for ch in range(n_ch)]
    b_end = jnp.concatenate([jnp.broadcast_to(bl, (GLA_CHUNK, GLA_K_WIDTH)) for bl in b_last], axis=0)
    o_k, o_v, o_g = GLA_K_WIDTH, 2 * GLA_K_WIDTH, 2 * GLA_K_WIDTH + GLA_WIDTH
    zq = zm_ref[:, 0:GLA_K_WIDTH]
    zk = zm_ref[:, o_k:o_k + GLA_K_WIDTH]
    qd = ((zq * (GLA_DK ** -0.5)) * jnp.exp(b)).astype(BF16)
    kd = (zk * jnp.exp(-b)).astype(BF16)
    k2 = (zk * jnp.exp(b_end - b)).astype(BF16)
    sub = lax.broadcasted_iota(jnp.int32, (8, LANES), 0)
    sel = jnp.where(sub < 2, 1.0, 0.0).astype(BF16)
    hk = [slice(h * GLA_DK, (h + 1) * GLA_DK) for h in range(GLA_HEADS)]
    chunk = [slice(ch * GLA_CHUNK, (ch + 1) * GLA_CHUNK) for ch in range(n_ch)]
    v_hs = [zm_ref[:, o_v + h * GLA_DV:o_v + (h + 1) * GLA_DV].astype(BF16) for h in range(GLA_HEADS)]
    atts = [_dot_nt(qd[:, hk[h]], kd[:, hk[h]]) for h in range(GLA_HEADS)]
    log_decs = []
    for h in range(GLA_HEADS):
        for ch in range(n_ch):
            bl = b_last[ch][:, hk[h]]
            bl_hi = bl.astype(BF16).astype(F32)
            rows8 = jnp.where(sub == 0, bl_hi, jnp.where(sub == 1, bl - bl_hi, 0.0)).astype(BF16)
            log_decs.append(_dot_tn(rows8, sel))
    deltas = [[_dot_tn(k2[chunk[ch], hk[h]], v_hs[h][chunk[ch]]) for ch in range(n_ch)]
              for h in range(GLA_HEADS)]
    o_intras = [_dot(jnp.where(causal, atts[h], 0.0).astype(BF16), v_hs[h]) for h in range(GLA_HEADS)]
    o_inters = []
    for h in range(GLA_HEADS):
        s_h = s_ref[h]
        parts = []
        for ch in range(n_ch):
            parts.append(_dot(qd[chunk[ch], hk[h]], s_h.astype(BF16)))
            dec = jnp.exp(log_decs[h * n_ch + ch])
            s_h = jnp.concatenate([dec, dec], axis=1) * s_h + deltas[h][ch]
        s_ref[h] = s_h
        o_inters.append(jnp.concatenate(parts, axis=0))
    for h in range(GLA_HEADS):
        vs = slice(h * GLA_DV, (h + 1) * GLA_DV)
        zg = zm_ref[:, o_g + h * GLA_DV:o_g + (h + 1) * GLA_DV]
        o_ref[:, vs] = _head_rmsnorm_gate(o_intras[h] + o_inters[h], gain_ref[...], zg).astype(o_ref.dtype)

    x = zpr_ref[:, 0:POOL_DIM]
    x_hi, x_lo = _split_bf16(x)
    pbh_ref[BLK:2 * BLK, :] = x_hi
    pbl_ref[BLK:2 * BLK, :] = x_lo
    t_glob = t0 + lax.broadcasted_iota(jnp.int32, (BLK, 1), 0)
    groups = [slice(g * POOL_GDIM, (g + 1) * POOL_GDIM) for g in range(len(POOL_WINDOWS))]
    sums = [_dot(band_ref[g], jnp.concatenate([pbh_ref[:, groups[g]], pbl_ref[:, groups[g]]], axis=0))
            for g in range(len(POOL_WINDOWS))]
    ys = []
    for g, w in enumerate(POOL_WINDOWS):
        inv_cnt = 1.0 / jnp.minimum(w, t_glob + 1).astype(F32)
        d = sums[g] * inv_cnt - x[:, groups[g]]
        ys.append(_dot(d.astype(BF16), wp_ref[g]))
    for g in range(len(POOL_WINDOWS)):
        o_ref[:, GLA_WIDTH + g * POOL_GDIM:GLA_WIDTH + (g + 1) * POOL_GDIM] = (
            ys[g] * ps_ref[:, groups[g]]).astype(o_ref.dtype)
    pbh_ref[0:BLK, :] = x_hi
    pbl_ref[0:BLK, :] = x_lo
    return x


def _odd_block_mats():
    r = jnp.arange(BLK)[:, None]
    c = jnp.arange(BLK)[None, :]
    tri = ((r >= c) & (r // GLA_CHUNK == c // GLA_CHUNK)).astype(BF16)
    back = r + BLK - jnp.arange(2 * BLK)[None, :]
    bands = [((back >= 0) & (back < w)).astype(BF16) for w in POOL_WINDOWS]
    return (jnp.concatenate([tri, tri], axis=1),
            jnp.stack([jnp.concatenate([bd, bd], axis=1) for bd in bands]))


def _odd_prompt(zm_p, zpr_p, wa, ba, gain, wp, ps, w_out, layer, w_dn, dn_layer):
    n_step = NBLK // MIX_SUB
    step_rows = MIX_SUB * BLK
    blk = lambda b, i: (b * n_step + i, 0)
    fixed2 = lambda b, i: (0, 0)
    wo_in, wo_out, wo_shape = _w_slab_specs(w_out, layer, n_step)
    wd_in, wd_out, wd_shape = _w_slab_specs(w_dn, dn_layer, n_step)
    tri2, bands2 = _odd_block_mats()
    return pl.pallas_call(
        _odd_prompt_kernel,
        grid=(BATCH, n_step),
        in_specs=[pl.BlockSpec((step_rows, ODD_MAIN), blk),
                  pl.BlockSpec((step_rows, POOL_RANK_W), blk),
                  pl.BlockSpec((LANES, GLA_K_WIDTH), fixed2),
                  pl.BlockSpec((1, GLA_K_WIDTH), fixed2),
                  pl.BlockSpec((1, GLA_DV), fixed2),
                  pl.BlockSpec((4, POOL_GDIM, POOL_GDIM), lambda b, i: (0, 0, 0)),
                  pl.BlockSpec((1, POOL_DIM), fixed2),
                  pl.BlockSpec(tri2.shape, fixed2),
                  pl.BlockSpec(bands2.shape, lambda b, i: (0, 0, 0)),
                  wo_in, wd_in],
        out_specs=[pl.BlockSpec((step_rows, D_MODEL), blk),
                   pl.BlockSpec((None, GLA_HEADS, GLA_DK, GLA_DV), lambda b, i: (b, 0, 0, 0)),
                   pl.BlockSpec((None, POOL_BUF, POOL_DIM), lambda b, i: (b, 0, 0)),
                   wo_out, wd_out],
        out_shape=[jax.ShapeDtypeStruct((M_PROMPT, D_MODEL), BF16),
                   jax.ShapeDtypeStruct((BATCH, GLA_HEADS, GLA_DK, GLA_DV), F32),
                   jax.ShapeDtypeStruct((BATCH, POOL_BUF, POOL_DIM), F32),
                   wo_shape, wd_shape],
        scratch_shapes=[pltpu.VMEM((2 * BLK, POOL_DIM), BF16), pltpu.VMEM((2 * BLK, POOL_DIM), BF16)],
        compiler_params=_cparams("arbitrary", "arbitrary"),
        name="odd_prompt",
    )(zm_p, zpr_p, wa, ba, gain, wp, ps, tri2, bands2, w_out, w_dn)


STATE_BB = 8
N_ODD_SAMPLE_IN = 9


def _odd_sample_kernel(*refs):
    zm_ref, zpr_ref, wa_ref, ba_ref, gain_ref, wp_ref, ps_ref, pool_ref, s_ref = refs[:N_ODD_SAMPLE_IN]
    m_ref, np_ref, ns_ref, dec_all, q_all, o_rows = refs[-6:]
    step = pl.program_id(0)
    n_prev = pool_ref.shape[1] // POOL_DIM

    @pl.when(step == 0)
    def _():
        dec_all[...] = jnp.exp(_log_decay(zpr_ref[:, POOL_DIM:POOL_RANK_W], wa_ref, ba_ref))
        q_all[...] = zm_ref[:, 0:GLA_K_WIDTH] * (GLA_DK ** -0.5)
        for g, w in enumerate(POOL_WINDOWS):
            gs = slice(g * POOL_GDIM, (g + 1) * POOL_GDIM)
            x = zpr_ref[:, gs]
            first = n_prev - (w - 1)
            s = pool_ref[:, first * POOL_DIM + g * POOL_GDIM:first * POOL_DIM + (g + 1) * POOL_GDIM]
            for jj in range(first + 1, n_prev):
                s = s + pool_ref[:, jj * POOL_DIM + g * POOL_GDIM:jj * POOL_DIM + (g + 1) * POOL_GDIM]
            s = s + x
            d = s / float(min(w, n_prev + 1)) - x
            m_ref[:, GLA_WIDTH + g * POOL_GDIM:GLA_WIDTH + (g + 1) * POOL_GDIM] = (
                _dot(d.astype(BF16), wp_ref[g]) * ps_ref[:, gs])
        np_ref[:, 0:(n_prev - 1) * POOL_DIM] = pool_ref[:, POOL_DIM:n_prev * POOL_DIM]
        np_ref[:, (n_prev - 1) * POOL_DIM:n_prev * POOL_DIM] = zpr_ref[:, 0:POOL_DIM]

    grp = pl.ds(pl.multiple_of(step * STATE_BB, STATE_BB), STATE_BB)
    dec = dec_all[grp, :]
    q = q_all[grp, :]
    k = zm_ref[grp, GLA_K_WIDTH:2 * GLA_K_WIDTH]
    v = zm_ref[grp, 2 * GLA_K_WIDTH:2 * GLA_K_WIDTH + GLA_WIDTH]
    for bb in range(STATE_BB):
        for h in range(GLA_HEADS):
            ks = slice(h * GLA_DK, (h + 1) * GLA_DK)
            vs = slice(h * GLA_DV, (h + 1) * GLA_DV)
            cols = _rows_to_cols([dec[bb:bb + 1, ks], k[bb:bb + 1, ks], q[bb:bb + 1, ks]])
            wide = lambda blk: jnp.concatenate([cols[:, blk * GLA_DK:(blk + 1) * GLA_DK]] * 2, axis=1)
            s_new = wide(0) * s_ref[bb, h] + wide(1) * v[bb:bb + 1, vs]
            ns_ref[bb, h] = s_new
            o_rows[bb:bb + 1, vs] = jnp.sum(wide(2) * s_new, axis=0, keepdims=True)
    o_g = 2 * GLA_K_WIDTH + GLA_WIDTH
    for h in range(GLA_HEADS):
        vs = slice(h * GLA_DV, (h + 1) * GLA_DV)
        zg = zm_ref[grp, o_g + h * GLA_DV:o_g + (h + 1) * GLA_DV]
        m_ref[grp, vs] = _head_rmsnorm_gate(o_rows[:, vs], gain_ref[...], zg)


def _odd_sample(zm_s, zpr_s, wa, ba, gain, wp, ps, pool_state, state, li, prev):
    whole = lambda a: pl.BlockSpec(a.shape, lambda b: (0,) * a.ndim)
    st = pl.BlockSpec((None, STATE_BB, GLA_HEADS, GLA_DK, GLA_DV), lambda b: (li, b, 0, 0, 0))
    pool2d = pool_state.reshape(DEC_BATCH, POOL_BUF * POOL_DIM)
    args = [zm_s, zpr_s, wa, ba, gain, wp, ps, pool2d, state]
    assert len(args) == N_ODD_SAMPLE_IN
    in_specs = [whole(a) for a in args[:-1]] + [st]
    aliases = {len(args): 2}
    in_specs.append(pl.BlockSpec(memory_space=pl.ANY))
    args.append(prev)
    rows = lambda w: pl.BlockSpec((DEC_BATCH, w), lambda b: (0, 0))
    m_s, npool, ns = pl.pallas_call(
        _odd_sample_kernel,
        grid=(DEC_BATCH // STATE_BB,),
        in_specs=in_specs,
        out_specs=[rows(D_MODEL), rows(POOL_BUF * POOL_DIM), st],
        out_shape=[jax.ShapeDtypeStruct((DEC_BATCH, D_MODEL), F32),
                   jax.ShapeDtypeStruct((DEC_BATCH, POOL_BUF * POOL_DIM), F32),
                   jax.ShapeDtypeStruct(state.shape, F32)],
        scratch_shapes=[pltpu.VMEM((DEC_BATCH, GLA_K_WIDTH), F32), pltpu.VMEM((DEC_BATCH, GLA_K_WIDTH), F32),
                        pltpu.VMEM((STATE_BB, GLA_WIDTH), F32)],
        input_output_aliases=aliases,
        compiler_params=_cparams("arbitrary"),
        name="odd_sample",
    )(*args)
    return m_s, npool.reshape(DEC_BATCH, POOL_BUF, POOL_DIM), ns


def _rope_tables(pos):
    half = HEAD_DIM // 2
    inv = jnp.power(ROPE_THETA, -jnp.arange(half, dtype=F32) / half)
    ang = pos.astype(F32)[:, None] * inv[None, :]
    c, s = jnp.cos(ang), jnp.sin(ang)
    return jnp.tile(c, (1, 4)), jnp.tile(jnp.concatenate([-s, s], axis=1), (1, 2))


def kernel(x_prompt, x_sample, cache_swa_k, cache_swa_v, state_conv, state_gla, state_pool, norm_mix, norm_ffn, w_in_even, w_out_even, q_norm, k_norm, attn_sinks, conv_w, w_in_odd, w_out_odd, w_alpha_up, b_alpha, gla_out_norm, w_pool, pool_scale, w_gate, w_up, w_down):
    lc = cache_swa_k.shape[2]
    assert x_prompt.shape == (BATCH, SEQ, D_MODEL) and x_sample.shape == (DEC_BATCH, 1, D_MODEL)
    assert lc == BLK == WINDOW, "the prompt's new cache is its last attention block"
    assert state_pool.shape[2] == POOL_BUF and state_conv.shape[2] == CONV_W - 1
    x_p = x_prompt.reshape(M_PROMPT, D_MODEL)
    x_s = x_sample.reshape(DEC_BATCH, D_MODEL)
    cos_p, sin_p = _rope_tables(jnp.arange(SEQ))
    cos_s, sin_s = _rope_tables(PAST_LEN + jnp.arange(1))
    mats = _rope_mats()
    kc_all = cache_swa_k.reshape(N_EVEN, DEC_BATCH, lc, A_KV_WIDTH)
    vc_all = cache_swa_v.reshape(N_EVEN, DEC_BATCH, lc, A_KV_WIDTH)
    o_r = ODD_MAIN
    w_odd_t = jnp.swapaxes(w_in_odd, 1, 2)
    w_odd_pr_t = jnp.concatenate([w_odd_t[:, o_r + GLA_RANK:], w_odd_t[:, o_r:o_r + GLA_RANK],
                                  jnp.zeros((N_ODD, LANES - GLA_RANK, D_MODEL), F32)], axis=1)
    pk, pv, pc, pg, pp, sc, sp = ([] for _ in range(7))
    sk_all = jnp.zeros(kc_all.shape, F32)
    sv_all = jnp.zeros(vc_all.shape, F32)
    sg_all = jnp.zeros(state_gla.shape, F32)

    h_p, h_s = _rmsnorm(x_p, x_s, norm_mix[0])
    for layer in range(DEPTH):
        li = layer // 2
        if layer % 2 == 0:
            z_p, z_s = _dense(h_p, h_s, [(w_in_even, li)], n_cols=EVEN_IN, **TILES["in_even"], name="in_even")
            qg = jnp.tile(q_norm[li], 2).reshape(1, LANES)
            kg = jnp.tile(k_norm[li], 2).reshape(1, LANES)
            m_p, nk, nv, nc, w_out_bf, w_down_bf = _even_prompt(
                z_p, cos_p, sin_p, qg, kg, conv_w[li], attn_sinks[li], mats, w_out_even, li, w_down, layer)
            pk.append(nk.reshape(BATCH, lc, A_KV_HEADS, HEAD_DIM))
            pv.append(nv.reshape(BATCH, lc, A_KV_HEADS, HEAD_DIM))
            pc.append(nc)
            m_s, nc_s, sk_all, sv_all = _even_sample(
                attn_sinks[li], z_s, cos_s, sin_s, qg, kg, state_conv[li], conv_w[li], mats, kc_all, vc_all, li,
                (sk_all, sv_all))
            sc.append(nc_s)
        else:
            zm_p, zm_s = _dense(h_p, h_s, [(w_odd_t, li)], n_cols=ODD_MAIN, **TILES["in_odd_main"],
                                transposed=True, name="in_odd_main")
            zpr_p, zpr_s = _dense(h_p, h_s, [(w_odd_pr_t, li)], n_cols=POOL_RANK_W, **TILES["in_odd_pool_rank"],
                                  transposed=True, name="in_odd_pool_rank")
            wa = jnp.pad(w_alpha_up[li], ((0, LANES - GLA_RANK), (0, 0))).astype(BF16)
            ba = b_alpha[li].reshape(1, GLA_K_WIDTH)
            gain = gla_out_norm[li].reshape(1, GLA_DV)
            wp = w_pool[li].astype(BF16)
            ps = pool_scale[li].reshape(1, POOL_DIM)
            m_p, ng, npool, w_out_bf, w_down_bf = _odd_prompt(zm_p, zpr_p, wa, ba, gain, wp, ps, w_out_odd, li,
                                                              w_down, layer)
            pg.append(ng)
            pp.append(npool)
            m_s, np_s, sg_all = _odd_sample(zm_s, zpr_s, wa, ba, gain, wp, ps, state_pool[li], state_gla, li,
                                            sg_all)
            sp.append(np_s)
        x_p, x_s, h_p, h_s = _proj_res(m_p, m_s, (w_out_bf, None), (x_p, x_s), norm_ffn[layer],
                                       **TILES["out_proj"], name="out_proj")
        a_p, a_s = _dense(h_p, h_s, [(w_gate, layer), (w_up, layer)], n_cols=D_FF, **TILES["ffn_up"],
                          mode="swiglu", out_dtype=BF16, name="ffn_up")
        if layer + 1 < DEPTH:
            x_p, x_s, h_p, h_s = _proj_res(a_p, a_s, (w_down_bf, None), (x_p, x_s),
                                           norm_mix[layer + 1], **TILES["ffn_down"], name="ffn_down")
        else:
            x_p, x_s = _proj_res(a_p, a_s, (w_down_bf, None), (x_p, x_s), None,
                                 **TILES["ffn_down"], name="ffn_down_last")

    st = lambda parts: jnp.stack(parts)
    cache5 = lambda a: a.reshape(N_EVEN, DEC_BATCH, lc, A_KV_HEADS, HEAD_DIM)
    return (x_p.reshape(BATCH, SEQ, D_MODEL), x_s.reshape(DEC_BATCH, 1, D_MODEL),
            st(pk), st(pv), st(pc), st(pg), st(pp), cache5(sk_all), cache5(sv_all), st(sc), sg_all, st(sp))
```

```python
import functools

import jax
import jax.numpy as jnp
from jax import lax
from jax.experimental import pallas as pl
from jax.experimental.pallas import tpu as pltpu

F32 = jnp.float32
BF16 = jnp.bfloat16

D_MODEL = 2048
BATCH = 4
SEQ = 2048
DEPTH = 4
DEC_BATCH = 32
PAST_LEN = 16384
N_EVEN = 2
N_ODD = 2
EPS = 1e-6
NEG_INF = -1e30
A_HEADS = 16
A_KV_HEADS = 4
HEAD_DIM = 64
A_WIDTH = A_HEADS * HEAD_DIM
A_KV_WIDTH = A_KV_HEADS * HEAD_DIM
WINDOW = 128
ROPE_THETA = 10000.0
CONV_DIM = D_MODEL // 2
CONV_W = 3
GLA_HEADS = 4
GLA_WIDTH = D_MODEL // 2
GLA_DV = GLA_WIDTH // GLA_HEADS
GLA_DK = GLA_DV // 2
GLA_K_WIDTH = GLA_HEADS * GLA_DK
GLA_RANK = 16
GLA_TAU = 16.0
GLA_CHUNK = 64
POOL_DIM = D_MODEL // 2
POOL_WINDOWS = (2, 4, 8, 16)
POOL_GDIM = POOL_DIM // 4
POOL_BUF = 15
D_FF = 5632
EVEN_IN = A_WIDTH + 2 * A_KV_WIDTH + 3 * CONV_DIM
ODD_MAIN = 2 * GLA_K_WIDTH + 2 * GLA_WIDTH

M_PROMPT = BATCH * SEQ
LANES = 128
BLK = 128
NBLK = SEQ // BLK
MIX_SUB = 4
POOL_RANK_W = POOL_DIM + LANES
VMEM_LIMIT = 58 * 1024 * 1024

TILES = {
    "in_even": dict(tm=1024, tn=1536),
    "in_odd_main": dict(tm=1024, tn=ODD_MAIN // 2),
    "in_odd_pool_rank": dict(tm=1024, tn=POOL_RANK_W),
    "ffn_up": dict(tm=2048, tn=512, sub=2),
    "out_proj": dict(tm=512),
    "ffn_down": dict(tm=512),
}


def _cparams(*sem):
    return pltpu.CompilerParams(dimension_semantics=sem, vmem_limit_bytes=VMEM_LIMIT)


def _dot(a, b):
    return jnp.dot(a, b, preferred_element_type=F32)


def _dot_nt(a, b):
    return lax.dot_general(a, b, (((1,), (1,)), ((), ())), preferred_element_type=F32)


def _dot_tn(a, b):
    return lax.dot_general(a, b, (((0,), (0,)), ((), ())), preferred_element_type=F32)


def _silu(x):
    return x * (1.0 / (1.0 + jnp.exp(-x)))


def _rmsnorm_rows(x, gain):
    ms = jnp.mean(x * x, axis=-1, keepdims=True)
    return (x * lax.rsqrt(ms + EPS)) * gain


def _rmsnorm_kernel(xp_ref, xs_ref, g_ref, op_ref, os_ref, *, n_i):
    op_ref[...] = _rmsnorm_rows(xp_ref[...], g_ref[...]).astype(op_ref.dtype)

    @pl.when(pl.program_id(0) == n_i - 1)
    def _():
        os_ref[...] = _rmsnorm_rows(xs_ref[...], g_ref[...]).astype(os_ref.dtype)


def _rmsnorm(x_p, x_s, gain, *, tm=1024):
    n_i = M_PROMPT // tm
    row = lambda i: (i, 0)
    fixed = lambda i: (0, 0)
    return pl.pallas_call(
        functools.partial(_rmsnorm_kernel, n_i=n_i),
        grid=(n_i,),
        in_specs=[pl.BlockSpec((tm, D_MODEL), row),
                  pl.BlockSpec((DEC_BATCH, D_MODEL), fixed),
                  pl.BlockSpec((1, D_MODEL), fixed)],
        out_specs=[pl.BlockSpec((tm, D_MODEL), row),
                   pl.BlockSpec((DEC_BATCH, D_MODEL), fixed)],
        out_shape=[jax.ShapeDtypeStruct((M_PROMPT, D_MODEL), BF16),
                   jax.ShapeDtypeStruct((DEC_BATCH, D_MODEL), BF16)],
        compiler_params=_cparams("arbitrary"),
        name="rmsnorm",
    )(x_p, x_s, gain.reshape(1, D_MODEL))


XPOSE_COLS = 128


def _dense_kernel(*refs, n_w, mode, n_i, transposed, sub):
    a_p, a_s = refs[0], refs[1]
    w = refs[2:2 + n_w]
    o_p, o_s = refs[2 + n_w], refs[3 + n_w]
    wbf = refs[4 + n_w:4 + 2 * n_w]
    i = pl.program_id(1)

    @pl.when(i == 0)
    def _():
        for k in range(n_w):
            if transposed:
                tn = wbf[k].shape[1]
                for c in range(0, tn, XPOSE_COLS):
                    wbf[k][:, c:c + XPOSE_COLS] = w[k][c:c + XPOSE_COLS, :].T.astype(BF16)
            else:
                wbf[k][...] = w[k][...].astype(BF16)

    def run(a_ref, o_ref, n_sub):
        rows = a_ref.shape[0] // n_sub
        for r in range(n_sub):
            a = a_ref[r * rows:(r + 1) * rows, :]
            if mode == "swiglu":
                y = _silu(_dot(a, wbf[0][...])) * _dot(a, wbf[1][...])
            else:
                y = _dot(a, wbf[0][...])
            o_ref[r * rows:(r + 1) * rows, :] = y.astype(o_ref.dtype)

    run(a_p, o_p, sub)

    @pl.when(i == n_i - 1)
    def _():
        run(a_s, o_s, 1)


def _dense(a_p, a_s, weights, *, n_cols, tm, tn, mode="plain", out_dtype=F32, transposed=False, sub=1, name):
    k_dim = a_p.shape[1]
    n_i = M_PROMPT // tm
    n_j = n_cols // tn
    row = lambda j, i: (i, 0)
    fixed = lambda j, i: (0, 0)
    tile = lambda j, i: (i, j)
    panel = lambda j, i: (0, j)
    in_specs = [pl.BlockSpec((tm, k_dim), row), pl.BlockSpec((DEC_BATCH, k_dim), fixed)]
    args = [a_p, a_s]
    for arr, layer in weights:
        if transposed:
            in_specs.append(pl.BlockSpec((None, tn, k_dim), lambda j, i, layer=layer: (layer, j, 0)))
        else:
            in_specs.append(pl.BlockSpec((None, k_dim, tn), lambda j, i, layer=layer: (layer, 0, j)))
        args.append(arr)
    out_specs = [pl.BlockSpec((tm, tn), tile), pl.BlockSpec((DEC_BATCH, tn), panel)]
    out_shape = [jax.ShapeDtypeStruct((M_PROMPT, n_cols), out_dtype),
                 jax.ShapeDtypeStruct((DEC_BATCH, n_cols), out_dtype)]
    n_w = len(weights)
    return pl.pallas_call(
        functools.partial(_dense_kernel, n_w=n_w, mode=mode, n_i=n_i, transposed=transposed, sub=sub),
        grid=(n_j, n_i),
        in_specs=in_specs,
        out_specs=out_specs,
        out_shape=out_shape,
        scratch_shapes=[pltpu.VMEM((k_dim, tn), BF16) for _ in range(n_w)],
        compiler_params=_cparams("arbitrary", "arbitrary"),
        name=name,
    )(*args)


W_CHUNKS = 4


def _proj_res_kernel(*refs, n_i, with_norm):
    a_p, a_s, w_hbm, r_p, r_s = refs[:5]
    w_ref, sem = refs[-2:]
    if with_norm:
        g_ref, x_p, x_s, h_p, h_s = refs[5:-2]
    else:
        x_p, x_s = refs[5:-2]
        g_ref = h_p = h_s = None
    i = pl.program_id(0)
    kc = w_ref.shape[0] // W_CHUNKS

    def chunk_copy(c):
        rows = pl.ds(c * kc, kc)
        return pltpu.make_async_copy(w_hbm.at[rows], w_ref.at[rows], sem.at[c])

    def finish(x, x_ref, h_ref):
        x_ref[...] = x
        if with_norm:
            h_ref[...] = _rmsnorm_rows(x, g_ref[...]).astype(h_ref.dtype)

    def run(a_ref, r_ref, x_ref, h_ref):
        a = a_ref[...]
        if a.dtype != BF16:
            a = a.astype(BF16)
        finish(r_ref[...] + _dot(a, w_ref[...]), x_ref, h_ref)

    @pl.when(i == 0)
    def _():
        for c in range(W_CHUNKS):
            chunk_copy(c).start()
        x_p[...] = r_p[...]
        for c in range(W_CHUNKS):
            chunk_copy(c).wait()
            x_p[...] += _dot(a_p[:, c * kc:(c + 1) * kc], w_ref[c * kc:(c + 1) * kc, :])
        if with_norm:
            h_p[...] = _rmsnorm_rows(x_p[...], g_ref[...]).astype(h_p.dtype)

    @pl.when(i > 0)
    def _():
        run(a_p, r_p, x_p, h_p)

    @pl.when(i == n_i - 1)
    def _():
        run(a_s, r_s, x_s, h_s)


def _proj_res(a_p, a_s, weight, res, gain, *, tm, name):
    w_bf, layer = weight
    assert layer is None and w_bf.ndim == 2
    k_dim = a_p.shape[1]
    assert k_dim % (W_CHUNKS * 16) == 0
    w_spec = pl.BlockSpec(memory_space=pl.ANY)
    n_i = M_PROMPT // tm
    assert n_i > 1
    with_norm = gain is not None
    row = lambda i: (i, 0)
    fixed = lambda i: (0, 0)
    rows_p = pl.BlockSpec((tm, D_MODEL), row)
    rows_s = pl.BlockSpec((DEC_BATCH, D_MODEL), fixed)
    in_specs = [pl.BlockSpec((tm, k_dim), row), pl.BlockSpec((DEC_BATCH, k_dim), fixed),
                w_spec, rows_p, rows_s]
    args = [a_p, a_s, w_bf, res[0], res[1]]
    out_specs = [rows_p, rows_s]
    out_shape = [jax.ShapeDtypeStruct((M_PROMPT, D_MODEL), F32), jax.ShapeDtypeStruct((DEC_BATCH, D_MODEL), F32)]
    if with_norm:
        in_specs.append(pl.BlockSpec((1, D_MODEL), fixed))
        args.append(gain.reshape(1, D_MODEL))
        out_specs += [rows_p, rows_s]
        out_shape += [jax.ShapeDtypeStruct((M_PROMPT, D_MODEL), BF16),
                      jax.ShapeDtypeStruct((DEC_BATCH, D_MODEL), BF16)]
    return pl.pallas_call(
        functools.partial(_proj_res_kernel, n_i=n_i, with_norm=with_norm),
        grid=(n_i,),
        in_specs=in_specs,
        out_specs=out_specs,
        out_shape=out_shape,
        scratch_shapes=[pltpu.VMEM((k_dim, D_MODEL), BF16), pltpu.SemaphoreType.DMA((W_CHUNKS,))],
        compiler_params=_cparams("arbitrary"),
        name=name,
    )(*args)


def _split_cat(x):
    hi = x.astype(BF16)
    lo = (x - hi.astype(F32)).astype(BF16)
    return jnp.concatenate([hi, lo], axis=1)


def _rope_mats():
    j = jnp.arange(2 * LANES)[:, None] % LANES
    l = jnp.arange(LANES)[None, :]
    head_sum = (j // HEAD_DIM == l // HEAD_DIM).astype(BF16)
    src = jnp.where(l % HEAD_DIM < HEAD_DIM // 2, l + HEAD_DIM // 2, l - HEAD_DIM // 2)
    half_swap = (j == src).astype(BF16)
    return head_sum, half_swap


def _norm_rope_chunk(xc, gain, cos, sin, hs_ref, sw_ref):
    ms = _dot(_split_cat(xc * xc), hs_ref[...]) * (1.0 / HEAD_DIM)
    y = (xc * lax.rsqrt(ms + EPS)) * gain
    swapped = _dot(_split_cat(y), sw_ref[...])
    return y * cos + swapped * sin


def _spread_heads(xc, own_lo):
    lane = lax.broadcasted_iota(jnp.int32, xc.shape, 1)
    keep = (lane < HEAD_DIM) if own_lo else (lane >= HEAD_DIM)
    nat = jnp.where(keep, xc, 0.0)
    rol = pltpu.roll(nat, HEAD_DIM, axis=1)
    parts = (nat, rol) if own_lo else (rol, nat)
    return jnp.concatenate(parts, axis=0).astype(BF16)


def _attn_core(qs, kfull, vfull, mask_t, sink_ref):
    rows = qs[0].shape[0]
    nkeys = kfull.shape[0]
    outs = [None] * 8
    for kh in range(A_KV_HEADS):
        c0 = (kh // 2) * LANES
        kk = _spread_heads(kfull[:, c0:c0 + LANES], kh % 2 == 0)
        vv = _spread_heads(vfull[:, c0:c0 + LANES], kh % 2 == 0)
        lhs = jnp.concatenate([qs[2 * kh], qs[2 * kh + 1]], axis=0)
        s = _dot_nt(kk, lhs)
        prow = []
        for half in range(2):
            pcol = []
            for cc in range(2):
                sb = s[half * nkeys:(half + 1) * nkeys, cc * rows:(cc + 1) * rows]
                sb = jnp.where(mask_t, sb, NEG_INF)
                sink = sink_ref[kh * 4 + 2 * cc + half]
                m = jnp.maximum(jnp.max(sb, axis=0, keepdims=True), sink)
                e = jnp.exp(sb - m)
                den = jnp.sum(e, axis=0, keepdims=True) + jnp.exp(sink - m)
                pcol.append((e / den).astype(BF16))
            prow.append(jnp.concatenate(pcol, axis=1))
        p = jnp.concatenate(prow, axis=0)
        o = _dot_tn(p, vv)
        outs[2 * kh] = o[0:rows]
        outs[2 * kh + 1] = o[rows:2 * rows]
    return outs


def _even_prompt_kernel(sink_ref, z_ref, cos_ref, sin_ref, qg_ref, kg_ref, cw_ref, hs_ref, sw_ref, wo_ref, wd_ref,
                        o_ref, nk_ref, nv_ref, nc_ref, wo_bf_ref, wd_bf_ref, kf_ref, vf_ref, ub_ref):
    i = pl.program_id(1)
    wo_bf_ref[...] = wo_ref[...].astype(BF16)
    wd_bf_ref[...] = wd_ref[...].astype(BF16)

    @pl.when(i == 0)
    def _():
        kf_ref[...] = jnp.zeros_like(kf_ref)
        vf_ref[...] = jnp.zeros_like(vf_ref)
        ub_ref[0:8, :] = jnp.zeros((8, CONV_DIM), F32)

    kk = lax.broadcasted_iota(jnp.int32, (2 * BLK, BLK), 0)
    r = lax.broadcasted_iota(jnp.int32, (2 * BLK, BLK), 1)
    d = kk - r
    band = (d >= 0) & (d <= WINDOW)
    o0 = A_WIDTH + 2 * A_KV_WIDTH
    k = v = None
    for t in range(MIX_SUB):
        rows = slice(t * BLK, (t + 1) * BLK)
        kf_ref[0:BLK, :] = kf_ref[BLK:2 * BLK, :]
        vf_ref[0:BLK, :] = vf_ref[BLK:2 * BLK, :]
        cos = cos_ref[rows, :]
        sin = sin_ref[rows, :]
        k = jnp.concatenate(
            [_norm_rope_chunk(z_ref[rows, A_WIDTH + c * LANES:A_WIDTH + (c + 1) * LANES], kg_ref[...], cos, sin,
                              hs_ref, sw_ref)
             for c in range(A_KV_WIDTH // LANES)], axis=1)
        v = z_ref[rows, A_WIDTH + A_KV_WIDTH:A_WIDTH + 2 * A_KV_WIDTH]
        kf_ref[BLK:2 * BLK, :] = k
        vf_ref[BLK:2 * BLK, :] = v
        qs = [(_norm_rope_chunk(z_ref[rows, c * LANES:(c + 1) * LANES], qg_ref[...], cos, sin, hs_ref, sw_ref)
               * (HEAD_DIM ** -0.5)).astype(BF16) for c in range(A_WIDTH // LANES)]
        mask_t = band & ((kk >= BLK) | (i > 0)) if t == 0 else band
        outs = _attn_core(qs, kf_ref[...], vf_ref[...], mask_t, sink_ref)
        for c in range(A_WIDTH // LANES):
            o_ref[rows, c * LANES:(c + 1) * LANES] = outs[c].astype(o_ref.dtype)

        u = z_ref[rows, o0 + CONV_DIM:o0 + 2 * CONV_DIM] * z_ref[rows, o0 + 2 * CONV_DIM:o0 + 3 * CONV_DIM]
        ub_ref[8:8 + BLK, :] = u
        y = ub_ref[6:6 + BLK, :] * cw_ref[0:1, :]
        y = y + ub_ref[7:7 + BLK, :] * cw_ref[1:2, :]
        y = y + ub_ref[8:8 + BLK, :] * cw_ref[2:3, :]
        o_ref[rows, A_WIDTH:A_WIDTH + CONV_DIM] = (z_ref[rows, o0:o0 + CONV_DIM] * y).astype(o_ref.dtype)
        ub_ref[0:8, :] = ub_ref[BLK:BLK + 8, :]

    @pl.when(i == pl.num_programs(1) - 1)
    def _():
        nk_ref[...] = k
        nv_ref[...] = v
        nc_ref[...] = ub_ref[BLK + 6:BLK + 8, :]


def _w_slab_specs(w_out, layer, n_step):
    slab = w_out.shape[1] // (BATCH * n_step)
    return (pl.BlockSpec((None, slab, w_out.shape[2]), lambda b, i: (layer, b * n_step + i, 0)),
            pl.BlockSpec((slab, w_out.shape[2]), lambda b, i: (b * n_step + i, 0)),
            jax.ShapeDtypeStruct(w_out.shape[1:], BF16))


def _even_prompt(z_p, cos, sin, qg, kg, cw, sinks, mats, w_out, layer, w_dn, dn_layer):
    n_step = NBLK // MIX_SUB
    step_rows = MIX_SUB * BLK
    blk = lambda b, i: (b * n_step + i, 0)
    fixed = lambda b, i: (0, 0)
    per_b = lambda b, i: (b, 0, 0)
    wo_in, wo_out, wo_shape = _w_slab_specs(w_out, layer, n_step)
    wd_in, wd_out, wd_shape = _w_slab_specs(w_dn, dn_layer, n_step)
    return pl.pallas_call(
        _even_prompt_kernel,
        grid=(BATCH, n_step),
        in_specs=[pl.BlockSpec(memory_space=pltpu.SMEM),
                  pl.BlockSpec((step_rows, EVEN_IN), blk),
                  pl.BlockSpec((step_rows, LANES), lambda b, i: (i, 0)),
                  pl.BlockSpec((step_rows, LANES), lambda b, i: (i, 0)),
                  pl.BlockSpec((1, LANES), fixed),
                  pl.BlockSpec((1, LANES), fixed),
                  pl.BlockSpec((CONV_W, CONV_DIM), fixed),
                  pl.BlockSpec((2 * LANES, LANES), fixed),
                  pl.BlockSpec((2 * LANES, LANES), fixed),
                  wo_in, wd_in],
        out_specs=[pl.BlockSpec((step_rows, D_MODEL), blk),
                   pl.BlockSpec((None, BLK, A_KV_WIDTH), per_b),
                   pl.BlockSpec((None, BLK, A_KV_WIDTH), per_b),
                   pl.BlockSpec((None, CONV_W - 1, CONV_DIM), per_b),
                   wo_out, wd_out],
        out_shape=[jax.ShapeDtypeStruct((M_PROMPT, D_MODEL), BF16),
                   jax.ShapeDtypeStruct((BATCH, BLK, A_KV_WIDTH), F32),
                   jax.ShapeDtypeStruct((BATCH, BLK, A_KV_WIDTH), F32),
                   jax.ShapeDtypeStruct((BATCH, CONV_W - 1, CONV_DIM), F32),
                   wo_shape, wd_shape],
        scratch_shapes=[pltpu.VMEM((2 * BLK, A_KV_WIDTH), F32),
                        pltpu.VMEM((2 * BLK, A_KV_WIDTH), F32),
                        pltpu.VMEM((BLK + 8, CONV_DIM), F32)],
        compiler_params=_cparams("arbitrary", "arbitrary"),
        name="even_prompt",
    )(sinks, z_p, cos, sin, qg, kg, cw, *mats, w_out, w_dn)


SROWS = LANES
N_SAMPLE_IN = 12
SAMPLE_BB = 8


def _even_sample_kernel(*refs):
    (sink_ref, z_ref, cos_ref, sin_ref, qg_ref, kg_ref, cs_ref, cw_ref, hs_ref, sw_ref,
     kc_ref, vc_ref) = refs[:N_SAMPLE_IN]
    m_ref, ncs_ref, nk_ref, nv_ref, q_all, k_all = refs[-6:]
    step = pl.program_id(0)
    lc = kc_ref.shape[1]

    @pl.when(step == 0)
    def _():
        cos = cos_ref[...]
        sin = sin_ref[...]
        for c in range(A_WIDTH // LANES):
            q = _norm_rope_chunk(z_ref[:, c * LANES:(c + 1) * LANES], qg_ref[...], cos, sin, hs_ref, sw_ref)
            q_all[:, c * LANES:(c + 1) * LANES] = q * (HEAD_DIM ** -0.5)
        for c in range(A_KV_WIDTH // LANES):
            k_all[:, c * LANES:(c + 1) * LANES] = _norm_rope_chunk(
                z_ref[:, A_WIDTH + c * LANES:A_WIDTH + (c + 1) * LANES], kg_ref[...], cos, sin, hs_ref, sw_ref)
        o0 = A_WIDTH + 2 * A_KV_WIDTH
        u = z_ref[:, o0 + CONV_DIM:o0 + 2 * CONV_DIM] * z_ref[:, o0 + 2 * CONV_DIM:o0 + 3 * CONV_DIM]
        y = cs_ref[:, 0:CONV_DIM] * cw_ref[0:1, :]
        y = y + cs_ref[:, CONV_DIM:2 * CONV_DIM] * cw_ref[1:2, :]
        y = y + u * cw_ref[2:3, :]
        m_ref[:, A_WIDTH:A_WIDTH + CONV_DIM] = z_ref[:, o0:o0 + CONV_DIM] * y
        m_ref[:, 0:A_WIDTH] = jnp.zeros((DEC_BATCH, A_WIDTH), F32)
        ncs_ref[:, 0:CONV_DIM] = cs_ref[:, CONV_DIM:2 * CONV_DIM]
        ncs_ref[:, CONV_DIM:2 * CONV_DIM] = u

    kk = lax.broadcasted_iota(jnp.int32, (2 * lc, SROWS), 0)
    mask_t = (kk <= lc) & (lc - kk <= WINDOW)
    row = lax.broadcasted_iota(jnp.int32, (lc, A_KV_WIDTH), 0)
    grp = pl.ds(pl.multiple_of((step * SAMPLE_BB // 8) * 8, 8), 8)
    for u in range(SAMPLE_BB):
        mine = lax.broadcasted_iota(jnp.int32, (8, 1), 0) == (step * SAMPLE_BB + u) % 8

        def pick(tile, mine=mine):
            return jnp.sum(jnp.where(mine, tile, 0.0), axis=0, keepdims=True)

        kn = pick(k_all[grp, :])
        vn = pick(z_ref[grp, A_WIDTH + A_KV_WIDTH:A_WIDTH + 2 * A_KV_WIDTH])
        kfull = jnp.concatenate([kc_ref[u], jnp.where(row == 0, kn, 0.0)], axis=0)
        vfull = jnp.concatenate([vc_ref[u], jnp.where(row == 0, vn, 0.0)], axis=0)
        qs = [jnp.broadcast_to(pick(q_all[grp, c * LANES:(c + 1) * LANES]), (SROWS, LANES)).astype(BF16)
              for c in range(A_WIDTH // LANES)]
        outs = _attn_core(qs, kfull, vfull, mask_t, sink_ref)
        for c in range(A_WIDTH // LANES):
            cols = slice(c * LANES, (c + 1) * LANES)
            m_ref[grp, cols] = jnp.where(mine, outs[c][0:8], m_ref[grp, cols])
        nk_ref[u, 0:lc - 1, :] = kc_ref[u, 1:lc, :]
        nk_ref[u, lc - 1:lc, :] = kn
        nv_ref[u, 0:lc - 1, :] = vc_ref[u, 1:lc, :]
        nv_ref[u, lc - 1:lc, :] = vn


def _even_sample(sinks, z_s, cos, sin, qg, kg, conv_state, cw, mats, k_cache, v_cache, li, prev):
    lc = k_cache.shape[2]
    whole = lambda a: pl.BlockSpec(a.shape, lambda b: (0,) * a.ndim)
    cache = pl.BlockSpec((None, SAMPLE_BB, lc, A_KV_WIDTH), lambda b: (li, b, 0, 0))
    cs = conv_state.reshape(DEC_BATCH, (CONV_W - 1) * CONV_DIM)
    args = [sinks, z_s, cos, sin, qg, kg, cs, cw, *mats, k_cache, v_cache]
    assert len(args) == N_SAMPLE_IN
    in_specs = [pl.BlockSpec(memory_space=pltpu.SMEM)] + [whole(a) for a in args[1:N_SAMPLE_IN - 2]] + [cache, cache]
    aliases = {len(args): 2, len(args) + 1: 3}
    in_specs += [pl.BlockSpec(memory_space=pl.ANY)] * 2
    args += list(prev)
    rows = lambda w: pl.BlockSpec((DEC_BATCH, w), lambda b: (0, 0))
    m_s, ncs, nk, nv = pl.pallas_call(
        _even_sample_kernel,
        grid=(DEC_BATCH // SAMPLE_BB,),
        in_specs=in_specs,
        out_specs=[rows(D_MODEL), rows((CONV_W - 1) * CONV_DIM), cache, cache],
        out_shape=[jax.ShapeDtypeStruct((DEC_BATCH, D_MODEL), F32),
                   jax.ShapeDtypeStruct((DEC_BATCH, (CONV_W - 1) * CONV_DIM), F32),
                   jax.ShapeDtypeStruct(k_cache.shape, F32),
                   jax.ShapeDtypeStruct(v_cache.shape, F32)],
        scratch_shapes=[pltpu.VMEM((DEC_BATCH, A_WIDTH), F32), pltpu.VMEM((DEC_BATCH, A_KV_WIDTH), F32)],
        input_output_aliases=aliases,
        compiler_params=_cparams("arbitrary"),
        name="even_sample",
    )(*args)
    return m_s, ncs.reshape(DEC_BATCH, CONV_W - 1, CONV_DIM), nk, nv


def _log_decay(zr, wa_ref, ba_ref):
    pre = _dot(zr.astype(BF16), wa_ref[...]) + ba_ref[...]
    return (jnp.minimum(pre, 0.0) - jnp.log1p(jnp.exp(-jnp.abs(pre)))) * (1.0 / GLA_TAU)


def _split_bf16(x):
    hi = x.astype(BF16)
    lo = (x - hi.astype(F32)).astype(BF16)
    return hi, lo


N_PIECES = 3
COL_ROWS = 16


def _rows_to_cols(rows):
    n = rows[0].shape[1]
    assert N_PIECES * len(rows) <= COL_ROWS
    sub = lax.broadcasted_iota(jnp.int32, (COL_ROWS, n), 0)
    stacked = jnp.zeros((COL_ROWS, n), F32)
    for v, x in enumerate(rows):
        rest = x
        for p in range(N_PIECES):
            piece = rest.astype(BF16).astype(F32)
            stacked = jnp.where(sub == v * N_PIECES + p, piece, stacked)
            rest = rest - piece
    r = lax.broadcasted_iota(jnp.int32, (COL_ROWS, len(rows) * n), 0)
    c = lax.broadcasted_iota(jnp.int32, (COL_ROWS, len(rows) * n), 1)
    sel = jnp.where(r // N_PIECES == c // n, 1.0, 0.0).astype(BF16)
    return _dot_tn(stacked.astype(BF16), sel)


def _head_rmsnorm_gate(o, gain, zg):
    ms = jnp.mean(o * o, axis=-1, keepdims=True)
    return ((o * lax.rsqrt(ms + EPS)) * gain) * _silu(zg)


def _odd_prompt_kernel(zm_ref, zpr_ref, wa_ref, ba_ref, gain_ref, wp_ref, ps_ref, tri_ref, band_ref, wo_ref,
                       wd_ref, o_ref, s_ref, np_ref, wo_bf_ref, wd_bf_ref, pbh_ref, pbl_ref):
    i = pl.program_id(1)
    wo_bf_ref[...] = wo_ref[...].astype(BF16)
    wd_bf_ref[...] = wd_ref[...].astype(BF16)

    @pl.when(i == 0)
    def _():
        s_ref[...] = jnp.zeros_like(s_ref)
        pbh_ref[0:BLK, :] = jnp.zeros((BLK, POOL_DIM), BF16)
        pbl_ref[0:BLK, :] = jnp.zeros((BLK, POOL_DIM), BF16)

    x = None
    for t in range(MIX_SUB):
        rows = pl.ds(t * BLK, BLK)
        x = _odd_block(zm_ref.at[rows], zpr_ref.at[rows], wa_ref, ba_ref, gain_ref, wp_ref, ps_ref,
                       tri_ref, band_ref, o_ref.at[rows], s_ref, pbh_ref, pbl_ref, (i * MIX_SUB + t) * BLK)

    @pl.when(i == pl.num_programs(1) - 1)
    def _():
        np_ref[...] = x[BLK - POOL_BUF:BLK, :]


def _odd_block(zm_ref, zpr_ref, wa_ref, ba_ref, gain_ref, wp_ref, ps_ref, tri_ref, band_ref,
               o_ref, s_ref, pbh_ref, pbl_ref, t0):
    n_ch = BLK // GLA_CHUNK
    gk = _log_decay(zpr_ref[:, POOL_DIM:POOL_RANK_W], wa_ref, ba_ref)
    r = lax.broadcasted_iota(jnp.int32, (BLK, BLK), 0)
    c = lax.broadcasted_iota(jnp.int32, (BLK, BLK), 1)
    causal = (r >= c) & (r // GLA_CHUNK == c // GLA_CHUNK)
    g_hi, g_lo = _split_bf16(gk)
    b = _dot(tri_ref[...], jnp.concatenate([g_hi, g_lo], axis=0))
    b_last = [b[(ch + 1) * GLA_CHUNK - 1:(ch + 1) * GLA_CHUNK, :] for ch in range(n_ch)]
    b_end = jnp.concatenate([jnp.broadcast_to(bl, (GLA_CHUNK, GLA_K_WIDTH)) for bl in b_last], axis=0)
    o_k, o_v, o_g = GLA_K_WIDTH, 2 * GLA_K_WIDTH, 2 * GLA_K_WIDTH + GLA_WIDTH
    zq = zm_ref[:, 0:GLA_K_WIDTH]
    zk = zm_ref[:, o_k:o_k + GLA_K_WIDTH]
    qd = ((zq * (GLA_DK ** -0.5)) * jnp.exp(b)).astype(BF16)
    kd = (zk * jnp.exp(-b)).astype(BF16)
    k2 = (zk * jnp.exp(b_end - b)).astype(BF16)
    sub = lax.broadcasted_iota(jnp.int32, (8, LANES), 0)
    sel = jnp.where(sub < 2, 1.0, 0.0).astype(BF16)
    hk = [slice(h * GLA_DK, (h + 1) * GLA_DK) for h in range(GLA_HEADS)]
    chunk = [slice(ch * GLA_CHUNK, (ch + 1) * GLA_CHUNK) for ch in range(n_ch)]
    v_hs = [zm_ref[:, o_v + h * GLA_DV:o_v + (h + 1) * GLA_DV].astype(BF16) for h in range(GLA_HEADS)]
    atts = [_dot_nt(qd[:, hk[h]], kd[:, hk[h]]) for h in range(GLA_HEADS)]
    log_decs = []
    for h in range(GLA_HEADS):
        for ch in range(n_ch):
            bl = b_last[ch][:, hk[h]]
            bl_hi = bl.astype(BF16).astype(F32)
            rows8 = jnp.where(sub == 0, bl_hi, jnp.where(sub == 1, bl - bl_hi, 0.0)).astype(BF16)
            log_decs.append(_dot_tn(rows8, sel))
    deltas = [[_dot_tn(k2[chunk[ch], hk[h]], v_hs[h][chunk[ch]]) for ch in range(n_ch)]
              for h in range(GLA_HEADS)]
    o_intras = [_dot(jnp.where(causal, atts[h], 0.0).astype(BF16), v_hs[h]) for h in range(GLA_HEADS)]
    o_inters = []
    for h in range(GLA_HEADS):
        s_h = s_ref[h]
        parts = []
        for ch in range(n_ch):
            parts.append(_dot(qd[chunk[ch], hk[h]], s_h.astype(BF16)))
            dec = jnp.exp(log_decs[h * n_ch + ch])
            s_h = jnp.concatenate([dec, dec], axis=1) * s_h + deltas[h][ch]
        s_ref[h] = s_h
        o_inters.append(jnp.concatenate(parts, axis=0))
    for h in range(GLA_HEADS):
        vs = slice(h * GLA_DV, (h + 1) * GLA_DV)
        zg = zm_ref[:, o_g + h * GLA_DV:o_g + (h + 1) * GLA_DV]
        o_ref[:, vs] = _head_rmsnorm_gate(o_intras[h] + o_inters[h], gain_ref[...], zg).astype(o_ref.dtype)

    x = zpr_ref[:, 0:POOL_DIM]
    x_hi, x_lo = _split_bf16(x)
    pbh_ref[BLK:2 * BLK, :] = x_hi
    pbl_ref[BLK:2 * BLK, :] = x_lo
    t_glob = t0 + lax.broadcasted_iota(jnp.int32, (BLK, 1), 0)
    groups = [slice(g * POOL_GDIM, (g + 1) * POOL_GDIM) for g in range(len(POOL_WINDOWS))]
    sums = [_dot(band_ref[g], jnp.concatenate([pbh_ref[:, groups[g]], pbl_ref[:, groups[g]]], axis=0))
            for g in range(len(POOL_WINDOWS))]
    ys = []
    for g, w in enumerate(POOL_WINDOWS):
        inv_cnt = 1.0 / jnp.minimum(w, t_glob + 1).astype(F32)
        d = sums[g] * inv_cnt - x[:, groups[g]]
        ys.append(_dot(d.astype(BF16), wp_ref[g]))
    for g in range(len(POOL_WINDOWS)):
        o_ref[:, GLA_WIDTH + g * POOL_GDIM:GLA_WIDTH + (g + 1) * POOL_GDIM] = (
            ys[g] * ps_ref[:, groups[g]]).astype(o_ref.dtype)
    pbh_ref[0:BLK, :] = x_hi
    pbl_ref[0:BLK, :] = x_lo
    return x


def _odd_block_mats():
    r = jnp.arange(BLK)[:, None]
    c = jnp.arange(BLK)[None, :]
    tri = ((r >= c) & (r // GLA_CHUNK == c // GLA_CHUNK)).astype(BF16)
    back = r + BLK - jnp.arange(2 * BLK)[None, :]
    bands = [((back >= 0) & (back < w)).astype(BF16) for w in POOL_WINDOWS]
    return (jnp.concatenate([tri, tri], axis=1),
            jnp.stack([jnp.concatenate([bd, bd], axis=1) for bd in bands]))


def _odd_prompt(zm_p, zpr_p, wa, ba, gain, wp, ps, w_out, layer, w_dn, dn_layer):
    n_step = NBLK // MIX_SUB
    step_rows = MIX_SUB * BLK
    blk = lambda b, i: (b * n_step + i, 0)
    fixed2 = lambda b, i: (0, 0)
    wo_in, wo_out, wo_shape = _w_slab_specs(w_out, layer, n_step)
    wd_in, wd_out, wd_shape = _w_slab_specs(w_dn, dn_layer, n_step)
    tri2, bands2 = _odd_block_mats()
    return pl.pallas_call(
        _odd_prompt_kernel,
        grid=(BATCH, n_step),
        in_specs=[pl.BlockSpec((step_rows, ODD_MAIN), blk),
                  pl.BlockSpec((step_rows, POOL_RANK_W), blk),
                  pl.BlockSpec((LANES, GLA_K_WIDTH), fixed2),
                  pl.BlockSpec((1, GLA_K_WIDTH), fixed2),
                  pl.BlockSpec((1, GLA_DV), fixed2),
                  pl.BlockSpec((4, POOL_GDIM, POOL_GDIM), lambda b, i: (0, 0, 0)),
                  pl.BlockSpec((1, POOL_DIM), fixed2),
                  pl.BlockSpec(tri2.shape, fixed2),
                  pl.BlockSpec(bands2.shape, lambda b, i: (0, 0, 0)),
                  wo_in, wd_in],
        out_specs=[pl.BlockSpec((step_rows, D_MODEL), blk),
                   pl.BlockSpec((None, GLA_HEADS, GLA_DK, GLA_DV), lambda b, i: (b, 0, 0, 0)),
                   pl.BlockSpec((None, POOL_BUF, POOL_DIM), lambda b, i: (b, 0, 0)),
                   wo_out, wd_out],
        out_shape=[jax.ShapeDtypeStruct((M_PROMPT, D_MODEL), BF16),
                   jax.ShapeDtypeStruct((BATCH, GLA_HEADS, GLA_DK, GLA_DV), F32),
                   jax.ShapeDtypeStruct((BATCH, POOL_BUF, POOL_DIM), F32),
                   wo_shape, wd_shape],
        scratch_shapes=[pltpu.VMEM((2 * BLK, POOL_DIM), BF16), pltpu.VMEM((2 * BLK, POOL_DIM), BF16)],
        compiler_params=_cparams("arbitrary", "arbitrary"),
        name="odd_prompt",
    )(zm_p, zpr_p, wa, ba, gain, wp, ps, tri2, bands2, w_out, w_dn)


STATE_BB = 8
N_ODD_SAMPLE_IN = 9


def _odd_sample_kernel(*refs):
    zm_ref, zpr_ref, wa_ref, ba_ref, gain_ref, wp_ref, ps_ref, pool_ref, s_ref = refs[:N_ODD_SAMPLE_IN]
    m_ref, np_ref, ns_ref, dec_all, q_all, o_rows = refs[-6:]
    step = pl.program_id(0)
    n_prev = pool_ref.shape[1] // POOL_DIM

    @pl.when(step == 0)
    def _():
        dec_all[...] = jnp.exp(_log_decay(zpr_ref[:, POOL_DIM:POOL_RANK_W], wa_ref, ba_ref))
        q_all[...] = zm_ref[:, 0:GLA_K_WIDTH] * (GLA_DK ** -0.5)
        for g, w in enumerate(POOL_WINDOWS):
            gs = slice(g * POOL_GDIM, (g + 1) * POOL_GDIM)
            x = zpr_ref[:, gs]
            first = n_prev - (w - 1)
            s = pool_ref[:, first * POOL_DIM + g * POOL_GDIM:first * POOL_DIM + (g + 1) * POOL_GDIM]
            for jj in range(first + 1, n_prev):
                s = s + pool_ref[:, jj * POOL_DIM + g * POOL_GDIM:jj * POOL_DIM + (g + 1) * POOL_GDIM]
            s = s + x
            d = s / float(min(w, n_prev + 1)) - x
            m_ref[:, GLA_WIDTH + g * POOL_GDIM:GLA_WIDTH + (g + 1) * POOL_GDIM] = (
                _dot(d.astype(BF16), wp_ref[g]) * ps_ref[:, gs])
        np_ref[:, 0:(n_prev - 1) * POOL_DIM] = pool_ref[:, POOL_DIM:n_prev * POOL_DIM]
        np_ref[:, (n_prev - 1) * POOL_DIM:n_prev * POOL_DIM] = zpr_ref[:, 0:POOL_DIM]

    grp = pl.ds(pl.multiple_of(step * STATE_BB, STATE_BB), STATE_BB)
    dec = dec_all[grp, :]
    q = q_all[grp, :]
    k = zm_ref[grp, GLA_K_WIDTH:2 * GLA_K_WIDTH]
    v = zm_ref[grp, 2 * GLA_K_WIDTH:2 * GLA_K_WIDTH + GLA_WIDTH]
    for bb in range(STATE_BB):
        for h in range(GLA_HEADS):
            ks = slice(h * GLA_DK, (h + 1) * GLA_DK)
            vs = slice(h * GLA_DV, (h + 1) * GLA_DV)
            cols = _rows_to_cols([dec[bb:bb + 1, ks], k[bb:bb + 1, ks], q[bb:bb + 1, ks]])
            wide = lambda blk: jnp.concatenate([cols[:, blk * GLA_DK:(blk + 1) * GLA_DK]] * 2, axis=1)
            s_new = wide(0) * s_ref[bb, h] + wide(1) * v[bb:bb + 1, vs]
            ns_ref[bb, h] = s_new
            o_rows[bb:bb + 1, vs] = jnp.sum(wide(2) * s_new, axis=0, keepdims=True)
    o_g = 2 * GLA_K_WIDTH + GLA_WIDTH
    for h in range(GLA_HEADS):
        vs = slice(h * GLA_DV, (h + 1) * GLA_DV)
        zg = zm_ref[grp, o_g + h * GLA_DV:o_g + (h + 1) * GLA_DV]
        m_ref[grp, vs] = _head_rmsnorm_gate(o_rows[:, vs], gain_ref[...], zg)


def _odd_sample(zm_s, zpr_s, wa, ba, gain, wp, ps, pool_state, state, li, prev):
    whole = lambda a: pl.BlockSpec(a.shape, lambda b: (0,) * a.ndim)
    st = pl.BlockSpec((None, STATE_BB, GLA_HEADS, GLA_DK, GLA_DV), lambda b: (li, b, 0, 0, 0))
    pool2d = pool_state.reshape(DEC_BATCH, POOL_BUF * POOL_DIM)
    args = [zm_s, zpr_s, wa, ba, gain, wp, ps, pool2d, state]
    assert len(args) == N_ODD_SAMPLE_IN
    in_specs = [whole(a) for a in args[:-1]] + [st]
    aliases = {len(args): 2}
    in_specs.append(pl.BlockSpec(memory_space=pl.ANY))
    args.append(prev)
    rows = lambda w: pl.BlockSpec((DEC_BATCH, w), lambda b: (0, 0))
    m_s, npool, ns = pl.pallas_call(
        _odd_sample_kernel,
        grid=(DEC_BATCH // STATE_BB,),
        in_specs=in_specs,
        out_specs=[rows(D_MODEL), rows(POOL_BUF * POOL_DIM), st],
        out_shape=[jax.ShapeDtypeStruct((DEC_BATCH, D_MODEL), F32),
                   jax.ShapeDtypeStruct((DEC_BATCH, POOL_BUF * POOL_DIM), F32),
                   jax.ShapeDtypeStruct(state.shape, F32)],
        scratch_shapes=[pltpu.VMEM((DEC_BATCH, GLA_K_WIDTH), F32), pltpu.VMEM((DEC_BATCH, GLA_K_WIDTH), F32),
                        pltpu.VMEM((STATE_BB, GLA_WIDTH), F32)],
        input_output_aliases=aliases,
        compiler_params=_cparams("arbitrary"),
        name="odd_sample",
    )(*args)
    return m_s, npool.reshape(DEC_BATCH, POOL_BUF, POOL_DIM), ns


def _rope_tables(pos):
    half = HEAD_DIM // 2
    inv = jnp.power(ROPE_THETA, -jnp.arange(half, dtype=F32) / half)
    ang = pos.astype(F32)[:, None] * inv[None, :]
    c, s = jnp.cos(ang), jnp.sin(ang)
    return jnp.tile(c, (1, 4)), jnp.tile(jnp.concatenate([-s, s], axis=1), (1, 2))


def kernel(x_prompt, x_sample, cache_swa_k, cache_swa_v, state_conv, state_gla, state_pool, norm_mix, norm_ffn, w_in_even, w_out_even, q_norm, k_norm, attn_sinks, conv_w, w_in_odd, w_out_odd, w_alpha_up, b_alpha, gla_out_norm, w_pool, pool_scale, w_gate, w_up, w_down):
    lc = cache_swa_k.shape[2]
    assert x_prompt.shape == (BATCH, SEQ, D_MODEL) and x_sample.shape == (DEC_BATCH, 1, D_MODEL)
    assert lc == BLK == WINDOW, "the prompt's new cache is its last attention block"
    assert state_pool.shape[2] == POOL_BUF and state_conv.shape[2] == CONV_W - 1
    x_p = x_prompt.reshape(M_PROMPT, D_MODEL)
    x_s = x_sample.reshape(DEC_BATCH, D_MODEL)
    cos_p, sin_p = _rope_tables(jnp.arange(SEQ))
    cos_s, sin_s = _rope_tables(PAST_LEN + jnp.arange(1))
    mats = _rope_mats()
    kc_all = cache_swa_k.reshape(N_EVEN, DEC_BATCH, lc, A_KV_WIDTH)
    vc_all = cache_swa_v.reshape(N_EVEN, DEC_BATCH, lc, A_KV_WIDTH)
    o_r = ODD_MAIN
    w_odd_t = jnp.swapaxes(w_in_odd, 1, 2)
    w_odd_pr_t = jnp.concatenate([w_odd_t[:, o_r + GLA_RANK:], w_odd_t[:, o_r:o_r + GLA_RANK],
                                  jnp.zeros((N_ODD, LANES - GLA_RANK, D_MODEL), F32)], axis=1)
    pk, pv, pc, pg, pp, sc, sp = ([] for _ in range(7))
    sk_all = jnp.zeros(kc_all.shape, F32)
    sv_all = jnp.zeros(vc_all.shape, F32)
    sg_all = jnp.zeros(state_gla.shape, F32)

    h_p, h_s = _rmsnorm(x_p, x_s, norm_mix[0])
    for layer in range(DEPTH):
        li = layer // 2
        if layer % 2 == 0:
            z_p, z_s = _dense(h_p, h_s, [(w_in_even, li)], n_cols=EVEN_IN, **TILES["in_even"], name="in_even")
            qg = jnp.tile(q_norm[li], 2).reshape(1, LANES)
            kg = jnp.tile(k_norm[li], 2).reshape(1, LANES)
            m_p, nk, nv, nc, w_out_bf, w_down_bf = _even_prompt(
                z_p, cos_p, sin_p, qg, kg, conv_w[li], attn_sinks[li], mats, w_out_even, li, w_down, layer)
            pk.append(nk.reshape(BATCH, lc, A_KV_HEADS, HEAD_DIM))
            pv.append(nv.reshape(BATCH, lc, A_KV_HEADS, HEAD_DIM))
            pc.append(nc)
            m_s, nc_s, sk_all, sv_all = _even_sample(
                attn_sinks[li], z_s, cos_s, sin_s, qg, kg, state_conv[li], conv_w[li], mats, kc_all, vc_all, li,
                (sk_all, sv_all))
            sc.append(nc_s)
        else:
            zm_p, zm_s = _dense(h_p, h_s, [(w_odd_t, li)], n_cols=ODD_MAIN, **TILES["in_odd_main"],
                                transposed=True, name="in_odd_main")
            zpr_p, zpr_s = _dense(h_p, h_s, [(w_odd_pr_t, li)], n_cols=POOL_RANK_W, **TILES["in_odd_pool_rank"],
                                  transposed=True, name="in_odd_pool_rank")
            wa = jnp.pad(w_alpha_up[li], ((0, LANES - GLA_RANK), (0, 0))).astype(BF16)
            ba = b_alpha[li].reshape(1, GLA_K_WIDTH)
            gain = gla_out_norm[li].reshape(1, GLA_DV)
            wp = w_pool[li].astype(BF16)
            ps = pool_scale[li].reshape(1, POOL_DIM)
            m_p, ng, npool, w_out_bf, w_down_bf = _odd_prompt(zm_p, zpr_p, wa, ba, gain, wp, ps, w_out_odd, li,
                                                              w_down, layer)
            pg.append(ng)
            pp.append(npool)
            m_s, np_s, sg_all = _odd_sample(zm_s, zpr_s, wa, ba, gain, wp, ps, state_pool[li], state_gla, li,
                                            sg_all)
            sp.append(np_s)
        x_p, x_s, h_p, h_s = _proj_res(m_p, m_s, (w_out_bf, None), (x_p, x_s), norm_ffn[layer],
                                       **TILES["out_proj"], name="out_proj")
        a_p, a_s = _dense(h_p, h_s, [(w_gate, layer), (w_up, layer)], n_cols=D_FF, **TILES["ffn_up"],
                          mode="swiglu", out_dtype=BF16, name="ffn_up")
        if layer + 1 < DEPTH:
            x_p, x_s, h_p, h_s = _proj_res(a_p, a_s, (w_down_bf, None), (x_p, x_s),
                                           norm_mix[layer + 1], **TILES["ffn_down"], name="ffn_down")
        else:
            x_p, x_s = _proj_res(a_p, a_s, (w_down_bf, None), (x_p, x_s), None,
                                 **TILES["ffn_down"], name="ffn_down_last")

    st = lambda parts: jnp.stack(parts)
    cache5 = lambda a: a.reshape(N_EVEN, DEC_BATCH, lc, A_KV_HEADS, HEAD_DIM)
    return (x_p.reshape(BATCH, SEQ, D_MODEL), x_s.reshape(DEC_BATCH, 1, D_MODEL),
            st(pk), st(pv), st(pc), st(pg), st(pp), cache5(sk_all), cache5(sv_all), st(sc), sg_all, st(sp))
```

```python
import functools

import jax
import jax.numpy as jnp
from jax import lax
from jax.experimental import pallas as pl
from jax.experimental.pallas import tpu as pltpu

F32 = jnp.float32
BF16 = jnp.bfloat16

D_MODEL = 2048
BATCH = 4
SEQ = 2048
DEPTH = 4
DEC_BATCH = 32
PAST_LEN = 16384
N_EVEN = 2
N_ODD = 2
EPS = 1e-6
NEG_INF = -1e30
A_HEADS = 16
A_KV_HEADS = 4
HEAD_DIM = 64
A_WIDTH = A_HEADS * HEAD_DIM
A_KV_WIDTH = A_KV_HEADS * HEAD_DIM
WINDOW = 128
ROPE_THETA = 10000.0
CONV_DIM = D_MODEL // 2
CONV_W = 3
GLA_HEADS = 4
GLA_WIDTH = D_MODEL // 2
GLA_DV = GLA_WIDTH // GLA_HEADS
GLA_DK = GLA_DV // 2
GLA_K_WIDTH = GLA_HEADS * GLA_DK
GLA_RANK = 16
GLA_TAU = 16.0
GLA_CHUNK = 64
POOL_DIM = D_MODEL // 2
POOL_WINDOWS = (2, 4, 8, 16)
POOL_GDIM = POOL_DIM // 4
POOL_BUF = 15
D_FF = 5632
EVEN_IN = A_WIDTH + 2 * A_KV_WIDTH + 3 * CONV_DIM
ODD_MAIN = 2 * GLA_K_WIDTH + 2 * GLA_WIDTH

M_PROMPT = BATCH * SEQ
LANES = 128
BLK = 128
NBLK = SEQ // BLK
MIX_SUB = 4
POOL_RANK_W = POOL_DIM + LANES
VMEM_LIMIT = 58 * 1024 * 1024

TILES = {
    "in_even": dict(tm=1024, tn=1536),
    "in_odd_main": dict(tm=1024, tn=ODD_MAIN // 2),
    "in_odd_pool_rank": dict(tm=1024, tn=POOL_RANK_W),
    "ffn_up": dict(tm=2048, tn=512, sub=2),
    "out_proj": dict(tm=512),
    "ffn_down": dict(tm=512),
}


def _cparams(*sem):
    return pltpu.CompilerParams(dimension_semantics=sem, vmem_limit_bytes=VMEM_LIMIT)


def _dot(a, b):
    return jnp.dot(a, b, preferred_element_type=F32)


def _dot_nt(a, b):
    return lax.dot_general(a, b, (((1,), (1,)), ((), ())), preferred_element_type=F32)


def _dot_tn(a, b):
    return lax.dot_general(a, b, (((0,), (0,)), ((), ())), preferred_element_type=F32)


def _silu(x):
    return x * (1.0 / (1.0 + jnp.exp(-x)))


def _rmsnorm_rows(x, gain):
    ms = jnp.mean(x * x, axis=-1, keepdims=True)
    return (x * lax.rsqrt(ms + EPS)) * gain


def _rmsnorm_kernel(xp_ref, xs_ref, g_ref, op_ref, os_ref, *, n_i):
    op_ref[...] = _rmsnorm_rows(xp_ref[...], g_ref[...]).astype(op_ref.dtype)

    @pl.when(pl.program_id(0) == n_i - 1)
    def _():
        os_ref[...] = _rmsnorm_rows(xs_ref[...], g_ref[...]).astype(os_ref.dtype)


def _rmsnorm(x_p, x_s, gain, *, tm=1024):
    n_i = M_PROMPT // tm
    row = lambda i: (i, 0)
    fixed = lambda i: (0, 0)
    return pl.pallas_call(
        functools.partial(_rmsnorm_kernel, n_i=n_i),
        grid=(n_i,),
        in_specs=[pl.BlockSpec((tm, D_MODEL), row),
                  pl.BlockSpec((DEC_BATCH, D_MODEL), fixed),
                  pl.BlockSpec((1, D_MODEL), fixed)],
        out_specs=[pl.BlockSpec((tm, D_MODEL), row),
                   pl.BlockSpec((DEC_BATCH, D_MODEL), fixed)],
        out_shape=[jax.ShapeDtypeStruct((M_PROMPT, D_MODEL), BF16),
                   jax.ShapeDtypeStruct((DEC_BATCH, D_MODEL), BF16)],
        compiler_params=_cparams("arbitrary"),
        name="rmsnorm",
    )(x_p, x_s, gain.reshape(1, D_MODEL))


XPOSE_COLS = 128


def _dense_kernel(*refs, n_w, mode, n_i, transposed, sub):
    a_p, a_s = refs[0], refs[1]
    w = refs[2:2 + n_w]
    o_p, o_s = refs[2 + n_w], refs[3 + n_w]
    wbf = refs[4 + n_w:4 + 2 * n_w]
    i = pl.program_id(1)

    @pl.when(i == 0)
    def _():
        for k in range(n_w):
            if transposed:
                tn = wbf[k].shape[1]
                for c in range(0, tn, XPOSE_COLS):
                    wbf[k][:, c:c + XPOSE_COLS] = w[k][c:c + XPOSE_COLS, :].T.astype(BF16)
            else:
                wbf[k][...] = w[k][...].astype(BF16)

    def run(a_ref, o_ref, n_sub):
        rows = a_ref.shape[0] // n_sub
        for r in range(n_sub):
            a = a_ref[r * rows:(r + 1) * rows, :]
            if mode == "swiglu":
                y = _silu(_dot(a, wbf[0][...])) * _dot(a, wbf[1][...])
            else:
                y = _dot(a, wbf[0][...])
            o_ref[r * rows:(r + 1) * rows, :] = y.astype(o_ref.dtype)

    run(a_p, o_p, sub)

    @pl.when(i == n_i - 1)
    def _():
        run(a_s, o_s, 1)


def _dense(a_p, a_s, weights, *, n_cols, tm, tn, mode="plain", out_dtype=F32, transposed=False, sub=1, name):
    k_dim = a_p.shape[1]
    n_i = M_PROMPT // tm
    n_j = n_cols // tn
    row = lambda j, i: (i, 0)
    fixed = lambda j, i: (0, 0)
    tile = lambda j, i: (i, j)
    panel = lambda j, i: (0, j)
    in_specs = [pl.BlockSpec((tm, k_dim), row), pl.BlockSpec((DEC_BATCH, k_dim), fixed)]
    args = [a_p, a_s]
    for arr, layer in weights:
        if transposed:
            in_specs.append(pl.BlockSpec((None, tn, k_dim), lambda j, i, layer=layer: (layer, j, 0)))
        else:
            in_specs.append(pl.BlockSpec((None, k_dim, tn), lambda j, i, layer=layer: (layer, 0, j)))
        args.append(arr)
    out_specs = [pl.BlockSpec((tm, tn), tile), pl.BlockSpec((DEC_BATCH, tn), panel)]
    out_shape = [jax.ShapeDtypeStruct((M_PROMPT, n_cols), out_dtype),
                 jax.ShapeDtypeStruct((DEC_BATCH, n_cols), out_dtype)]
    n_w = len(weights)
    return pl.pallas_call(
        functools.partial(_dense_kernel, n_w=n_w, mode=mode, n_i=n_i, transposed=transposed, sub=sub),
        grid=(n_j, n_i),
        in_specs=in_specs,
        out_specs=out_specs,
        out_shape=out_shape,
        scratch_shapes=[pltpu.VMEM((k_dim, tn), BF16) for _ in range(n_w)],
        compiler_params=_cparams("arbitrary", "arbitrary"),
        name=name,
    )(*args)


def _proj_res_kernel(*refs, n_i, with_norm):
    a_p, a_s, w_ref, r_p, r_s = refs[:5]
    if with_norm:
        g_ref, x_p, x_s, h_p, h_s = refs[5:]
    else:
        x_p, x_s = refs[5:]
        g_ref = h_p = h_s = None

    def run(a_ref, r_ref, x_ref, h_ref):
        a = a_ref[...]
        if a.dtype != BF16:
            a = a.astype(BF16)
        x = r_ref[...] + _dot(a, w_ref[...])
        x_ref[...] = x
        if with_norm:
            h_ref[...] = _rmsnorm_rows(x, g_ref[...]).astype(h_ref.dtype)

    run(a_p, r_p, x_p, h_p)

    @pl.when(pl.program_id(0) == n_i - 1)
    def _():
        run(a_s, r_s, x_s, h_s)


def _proj_res(a_p, a_s, weight, res, gain, *, tm, name):
    w_bf, layer = weight
    k_dim = a_p.shape[1]
    if layer is None:
        w_spec = pl.BlockSpec((k_dim, D_MODEL), lambda i: (0, 0), pipeline_mode=pl.Buffered(1))
    else:
        w_spec = pl.BlockSpec((None, k_dim, D_MODEL), lambda i: (layer, 0, 0), pipeline_mode=pl.Buffered(1))
    n_i = M_PROMPT // tm
    with_norm = gain is not None
    row = lambda i: (i, 0)
    fixed = lambda i: (0, 0)
    rows_p = pl.BlockSpec((tm, D_MODEL), row)
    rows_s = pl.BlockSpec((DEC_BATCH, D_MODEL), fixed)
    in_specs = [pl.BlockSpec((tm, k_dim), row), pl.BlockSpec((DEC_BATCH, k_dim), fixed),
                w_spec, rows_p, rows_s]
    args = [a_p, a_s, w_bf, res[0], res[1]]
    out_specs = [rows_p, rows_s]
    out_shape = [jax.ShapeDtypeStruct((M_PROMPT, D_MODEL), F32), jax.ShapeDtypeStruct((DEC_BATCH, D_MODEL), F32)]
    if with_norm:
        in_specs.append(pl.BlockSpec((1, D_MODEL), fixed))
        args.append(gain.reshape(1, D_MODEL))
        out_specs += [rows_p, rows_s]
        out_shape += [jax.ShapeDtypeStruct((M_PROMPT, D_MODEL), BF16),
                      jax.ShapeDtypeStruct((DEC_BATCH, D_MODEL), BF16)]
    return pl.pallas_call(
        functools.partial(_proj_res_kernel, n_i=n_i, with_norm=with_norm),
        grid=(n_i,),
        in_specs=in_specs,
        out_specs=out_specs,
        out_shape=out_shape,
        compiler_params=_cparams("arbitrary"),
        name=name,
    )(*args)


def _split_cat(x):
    hi = x.astype(BF16)
    lo = (x - hi.astype(F32)).astype(BF16)
    return jnp.concatenate([hi, lo], axis=1)


def _rope_mats():
    j = jnp.arange(2 * LANES)[:, None] % LANES
    l = jnp.arange(LANES)[None, :]
    head_sum = (j // HEAD_DIM == l // HEAD_DIM).astype(BF16)
    src = jnp.where(l % HEAD_DIM < HEAD_DIM // 2, l + HEAD_DIM // 2, l - HEAD_DIM // 2)
    half_swap = (j == src).astype(BF16)
    return head_sum, half_swap


def _norm_rope_chunk(xc, gain, cos, sin, hs_ref, sw_ref):
    ms = _dot(_split_cat(xc * xc), hs_ref[...]) * (1.0 / HEAD_DIM)
    y = (xc * lax.rsqrt(ms + EPS)) * gain
    swapped = _dot(_split_cat(y), sw_ref[...])
    return y * cos + swapped * sin


def _spread_heads(xc, own_lo):
    lane = lax.broadcasted_iota(jnp.int32, xc.shape, 1)
    keep = (lane < HEAD_DIM) if own_lo else (lane >= HEAD_DIM)
    nat = jnp.where(keep, xc, 0.0)
    rol = pltpu.roll(nat, HEAD_DIM, axis=1)
    parts = (nat, rol) if own_lo else (rol, nat)
    return jnp.concatenate(parts, axis=0).astype(BF16)


def _attn_core(qs, kfull, vfull, mask_t, sink_ref):
    rows = qs[0].shape[0]
    nkeys = kfull.shape[0]
    outs = [None] * 8
    for kh in range(A_KV_HEADS):
        c0 = (kh // 2) * LANES
        kk = _spread_heads(kfull[:, c0:c0 + LANES], kh % 2 == 0)
        vv = _spread_heads(vfull[:, c0:c0 + LANES], kh % 2 == 0)
        lhs = jnp.concatenate([qs[2 * kh], qs[2 * kh + 1]], axis=0)
        s = _dot_nt(kk, lhs)
        prow = []
        for half in range(2):
            pcol = []
            for cc in range(2):
                sb = s[half * nkeys:(half + 1) * nkeys, cc * rows:(cc + 1) * rows]
                sb = jnp.where(mask_t, sb, NEG_INF)
                sink = sink_ref[kh * 4 + 2 * cc + half]
                m = jnp.maximum(jnp.max(sb, axis=0, keepdims=True), sink)
                e = jnp.exp(sb - m)
                den = jnp.sum(e, axis=0, keepdims=True) + jnp.exp(sink - m)
                pcol.append((e / den).astype(BF16))
            prow.append(jnp.concatenate(pcol, axis=1))
        p = jnp.concatenate(prow, axis=0)
        o = _dot_tn(p, vv)
        outs[2 * kh] = o[0:rows]
        outs[2 * kh + 1] = o[rows:2 * rows]
    return outs


def _even_prompt_kernel(sink_ref, z_ref, cos_ref, sin_ref, qg_ref, kg_ref, cw_ref, hs_ref, sw_ref, wo_ref, wd_ref,
                        o_ref, nk_ref, nv_ref, nc_ref, wo_bf_ref, wd_bf_ref, kf_ref, vf_ref, ub_ref):
    i = pl.program_id(1)
    wo_bf_ref[...] = wo_ref[...].astype(BF16)
    wd_bf_ref[...] = wd_ref[...].astype(BF16)

    @pl.when(i == 0)
    def _():
        kf_ref[...] = jnp.zeros_like(kf_ref)
        vf_ref[...] = jnp.zeros_like(vf_ref)
        ub_ref[0:8, :] = jnp.zeros((8, CONV_DIM), F32)

    kk = lax.broadcasted_iota(jnp.int32, (2 * BLK, BLK), 0)
    r = lax.broadcasted_iota(jnp.int32, (2 * BLK, BLK), 1)
    d = kk - r
    band = (d >= 0) & (d <= WINDOW)
    o0 = A_WIDTH + 2 * A_KV_WIDTH
    k = v = None
    for t in range(MIX_SUB):
        rows = slice(t * BLK, (t + 1) * BLK)
        kf_ref[0:BLK, :] = kf_ref[BLK:2 * BLK, :]
        vf_ref[0:BLK, :] = vf_ref[BLK:2 * BLK, :]
        cos = cos_ref[rows, :]
        sin = sin_ref[rows, :]
        k = jnp.concatenate(
            [_norm_rope_chunk(z_ref[rows, A_WIDTH + c * LANES:A_WIDTH + (c + 1) * LANES], kg_ref[...], cos, sin,
                              hs_ref, sw_ref)
             for c in range(A_KV_WIDTH // LANES)], axis=1)
        v = z_ref[rows, A_WIDTH + A_KV_WIDTH:A_WIDTH + 2 * A_KV_WIDTH]
        kf_ref[BLK:2 * BLK, :] = k
        vf_ref[BLK:2 * BLK, :] = v
        qs = [(_norm_rope_chunk(z_ref[rows, c * LANES:(c + 1) * LANES], qg_ref[...], cos, sin, hs_ref, sw_ref)
               * (HEAD_DIM ** -0.5)).astype(BF16) for c in range(A_WIDTH // LANES)]
        mask_t = band & ((kk >= BLK) | (i > 0)) if t == 0 else band
        outs = _attn_core(qs, kf_ref[...], vf_ref[...], mask_t, sink_ref)
        for c in range(A_WIDTH // LANES):
            o_ref[rows, c * LANES:(c + 1) * LANES] = outs[c].astype(o_ref.dtype)

        u = z_ref[rows, o0 + CONV_DIM:o0 + 2 * CONV_DIM] * z_ref[rows, o0 + 2 * CONV_DIM:o0 + 3 * CONV_DIM]
        ub_ref[8:8 + BLK, :] = u
        y = ub_ref[6:6 + BLK, :] * cw_ref[0:1, :]
        y = y + ub_ref[7:7 + BLK, :] * cw_ref[1:2, :]
        y = y + ub_ref[8:8 + BLK, :] * cw_ref[2:3, :]
        o_ref[rows, A_WIDTH:A_WIDTH + CONV_DIM] = (z_ref[rows, o0:o0 + CONV_DIM] * y).astype(o_ref.dtype)
        ub_ref[0:8, :] = ub_ref[BLK:BLK + 8, :]

    @pl.when(i == pl.num_programs(1) - 1)
    def _():
        nk_ref[...] = k
        nv_ref[...] = v
        nc_ref[...] = ub_ref[BLK + 6:BLK + 8, :]


def _w_slab_specs(w_out, layer, n_step):
    slab = w_out.shape[1] // (BATCH * n_step)
    return (pl.BlockSpec((None, slab, w_out.shape[2]), lambda b, i: (layer, b * n_step + i, 0)),
            pl.BlockSpec((slab, w_out.shape[2]), lambda b, i: (b * n_step + i, 0)),
            jax.ShapeDtypeStruct(w_out.shape[1:], BF16))


def _even_prompt(z_p, cos, sin, qg, kg, cw, sinks, mats, w_out, layer, w_dn, dn_layer):
    n_step = NBLK // MIX_SUB
    step_rows = MIX_SUB * BLK
    blk = lambda b, i: (b * n_step + i, 0)
    fixed = lambda b, i: (0, 0)
    per_b = lambda b, i: (b, 0, 0)
    wo_in, wo_out, wo_shape = _w_slab_specs(w_out, layer, n_step)
    wd_in, wd_out, wd_shape = _w_slab_specs(w_dn, dn_layer, n_step)
    return pl.pallas_call(
        _even_prompt_kernel,
        grid=(BATCH, n_step),
        in_specs=[pl.BlockSpec(memory_space=pltpu.SMEM),
                  pl.BlockSpec((step_rows, EVEN_IN), blk),
                  pl.BlockSpec((step_rows, LANES), lambda b, i: (i, 0)),
                  pl.BlockSpec((step_rows, LANES), lambda b, i: (i, 0)),
                  pl.BlockSpec((1, LANES), fixed),
                  pl.BlockSpec((1, LANES), fixed),
                  pl.BlockSpec((CONV_W, CONV_DIM), fixed),
                  pl.BlockSpec((2 * LANES, LANES), fixed),
                  pl.BlockSpec((2 * LANES, LANES), fixed),
                  wo_in, wd_in],
        out_specs=[pl.BlockSpec((step_rows, D_MODEL), blk),
                   pl.BlockSpec((None, BLK, A_KV_WIDTH), per_b),
                   pl.BlockSpec((None, BLK, A_KV_WIDTH), per_b),
                   pl.BlockSpec((None, CONV_W - 1, CONV_DIM), per_b),
                   wo_out, wd_out],
        out_shape=[jax.ShapeDtypeStruct((M_PROMPT, D_MODEL), BF16),
                   jax.ShapeDtypeStruct((BATCH, BLK, A_KV_WIDTH), F32),
                   jax.ShapeDtypeStruct((BATCH, BLK, A_KV_WIDTH), F32),
                   jax.ShapeDtypeStruct((BATCH, CONV_W - 1, CONV_DIM), F32),
                   wo_shape, wd_shape],
        scratch_shapes=[pltpu.VMEM((2 * BLK, A_KV_WIDTH), F32),
                        pltpu.VMEM((2 * BLK, A_KV_WIDTH), F32),
                        pltpu.VMEM((BLK + 8, CONV_DIM), F32)],
        compiler_params=_cparams("arbitrary", "arbitrary"),
        name="even_prompt",
    )(sinks, z_p, cos, sin, qg, kg, cw, *mats, w_out, w_dn)


SROWS = LANES
N_SAMPLE_IN = 12
SAMPLE_BB = 8


def _even_sample_kernel(*refs):
    (sink_ref, z_ref, cos_ref, sin_ref, qg_ref, kg_ref, cs_ref, cw_ref, hs_ref, sw_ref,
     kc_ref, vc_ref) = refs[:N_SAMPLE_IN]
    m_ref, ncs_ref, nk_ref, nv_ref, q_all, k_all = refs[-6:]
    step = pl.program_id(0)
    lc = kc_ref.shape[1]

    @pl.when(step == 0)
    def _():
        cos = cos_ref[...]
        sin = sin_ref[...]
        for c in range(A_WIDTH // LANES):
            q = _norm_rope_chunk(z_ref[:, c * LANES:(c + 1) * LANES], qg_ref[...], cos, sin, hs_ref, sw_ref)
            q_all[:, c * LANES:(c + 1) * LANES] = q * (HEAD_DIM ** -0.5)
        for c in range(A_KV_WIDTH // LANES):
            k_all[:, c * LANES:(c + 1) * LANES] = _norm_rope_chunk(
                z_ref[:, A_WIDTH + c * LANES:A_WIDTH + (c + 1) * LANES], kg_ref[...], cos, sin, hs_ref, sw_ref)
        o0 = A_WIDTH + 2 * A_KV_WIDTH
        u = z_ref[:, o0 + CONV_DIM:o0 + 2 * CONV_DIM] * z_ref[:, o0 + 2 * CONV_DIM:o0 + 3 * CONV_DIM]
        y = cs_ref[:, 0:CONV_DIM] * cw_ref[0:1, :]
        y = y + cs_ref[:, CONV_DIM:2 * CONV_DIM] * cw_ref[1:2, :]
        y = y + u * cw_ref[2:3, :]
        m_ref[:, A_WIDTH:A_WIDTH + CONV_DIM] = z_ref[:, o0:o0 + CONV_DIM] * y
        m_ref[:, 0:A_WIDTH] = jnp.zeros((DEC_BATCH, A_WIDTH), F32)
        ncs_ref[:, 0:CONV_DIM] = cs_ref[:, CONV_DIM:2 * CONV_DIM]
        ncs_ref[:, CONV_DIM:2 * CONV_DIM] = u

    kk = lax.broadcasted_iota(jnp.int32, (2 * lc, SROWS), 0)
    mask_t = (kk <= lc) & (lc - kk <= WINDOW)
    row = lax.broadcasted_iota(jnp.int32, (lc, A_KV_WIDTH), 0)
    grp = pl.ds(pl.multiple_of((step * SAMPLE_BB // 8) * 8, 8), 8)
    for u in range(SAMPLE_BB):
        mine = lax.broadcasted_iota(jnp.int32, (8, 1), 0) == (step * SAMPLE_BB + u) % 8

        def pick(tile, mine=mine):
            return jnp.sum(jnp.where(mine, tile, 0.0), axis=0, keepdims=True)

        kn = pick(k_all[grp, :])
        vn = pick(z_ref[grp, A_WIDTH + A_KV_WIDTH:A_WIDTH + 2 * A_KV_WIDTH])
        kfull = jnp.concatenate([kc_ref[u], jnp.where(row == 0, kn, 0.0)], axis=0)
        vfull = jnp.concatenate([vc_ref[u], jnp.where(row == 0, vn, 0.0)], axis=0)
        qs = [jnp.broadcast_to(pick(q_all[grp, c * LANES:(c + 1) * LANES]), (SROWS, LANES)).astype(BF16)
              for c in range(A_WIDTH // LANES)]
        outs = _attn_core(qs, kfull, vfull, mask_t, sink_ref)
        for c in range(A_WIDTH // LANES):
            cols = slice(c * LANES, (c + 1) * LANES)
            m_ref[grp, cols] = jnp.where(mine, outs[c][0:8], m_ref[grp, cols])
        nk_ref[u, 0:lc - 1, :] = kc_ref[u, 1:lc, :]
        nk_ref[u, lc - 1:lc, :] = kn
        nv_ref[u, 0:lc - 1, :] = vc_ref[u, 1:lc, :]
        nv_ref[u, lc - 1:lc, :] = vn


def _even_sample(sinks, z_s, cos, sin, qg, kg, conv_state, cw, mats, k_cache, v_cache, li, prev):
    lc = k_cache.shape[2]
    whole = lambda a: pl.BlockSpec(a.shape, lambda b: (0,) * a.ndim)
    cache = pl.BlockSpec((None, SAMPLE_BB, lc, A_KV_WIDTH), lambda b: (li, b, 0, 0))
    cs = conv_state.reshape(DEC_BATCH, (CONV_W - 1) * CONV_DIM)
    args = [sinks, z_s, cos, sin, qg, kg, cs, cw, *mats, k_cache, v_cache]
    assert len(args) == N_SAMPLE_IN
    in_specs = [pl.BlockSpec(memory_space=pltpu.SMEM)] + [whole(a) for a in args[1:N_SAMPLE_IN - 2]] + [cache, cache]
    aliases = {len(args): 2, len(args) + 1: 3}
    in_specs += [pl.BlockSpec(memory_space=pl.ANY)] * 2
    args += list(prev)
    rows = lambda w: pl.BlockSpec((DEC_BATCH, w), lambda b: (0, 0))
    m_s, ncs, nk, nv = pl.pallas_call(
        _even_sample_kernel,
        grid=(DEC_BATCH // SAMPLE_BB,),
        in_specs=in_specs,
        out_specs=[rows(D_MODEL), rows((CONV_W - 1) * CONV_DIM), cache, cache],
        out_shape=[jax.ShapeDtypeStruct((DEC_BATCH, D_MODEL), F32),
                   jax.ShapeDtypeStruct((DEC_BATCH, (CONV_W - 1) * CONV_DIM), F32),
                   jax.ShapeDtypeStruct(k_cache.shape, F32),
                   jax.ShapeDtypeStruct(v_cache.shape, F32)],
        scratch_shapes=[pltpu.VMEM((DEC_BATCH, A_WIDTH), F32), pltpu.VMEM((DEC_BATCH, A_KV_WIDTH), F32)],
        input_output_aliases=aliases,
        compiler_params=_cparams("arbitrary"),
        name="even_sample",
    )(*args)
    return m_s, ncs.reshape(DEC_BATCH, CONV_W - 1, CONV_DIM), nk, nv


def _log_decay(zr, wa_ref, ba_ref):
    pre = _dot(zr.astype(BF16), wa_ref[...]) + ba_ref[...]
    return (jnp.minimum(pre, 0.0) - jnp.log1p(jnp.exp(-jnp.abs(pre)))) * (1.0 / GLA_TAU)


def _split_bf16(x):
    hi = x.astype(BF16)
    lo = (x - hi.astype(F32)).astype(BF16)
    return hi, lo


N_PIECES = 3
COL_ROWS = 16


def _rows_to_cols(rows):
    n = rows[0].shape[1]
    assert N_PIECES * len(rows) <= COL_ROWS
    sub = lax.broadcasted_iota(jnp.int32, (COL_ROWS, n), 0)
    stacked = jnp.zeros((COL_ROWS, n), F32)
    for v, x in enumerate(rows):
        rest = x
        for p in range(N_PIECES):
            piece = rest.astype(BF16).astype(F32)
            stacked = jnp.where(sub == v * N_PIECES + p, piece, stacked)
            rest = rest - piece
    r = lax.broadcasted_iota(jnp.int32, (COL_ROWS, len(rows) * n), 0)
    c = lax.broadcasted_iota(jnp.int32, (COL_ROWS, len(rows) * n), 1)
    sel = jnp.where(r // N_PIECES == c // n, 1.0, 0.0).astype(BF16)
    return _dot_tn(stacked.astype(BF16), sel)


def _head_rmsnorm_gate(o, gain, zg):
    ms = jnp.mean(o * o, axis=-1, keepdims=True)
    return ((o * lax.rsqrt(ms + EPS)) * gain) * _silu(zg)


def _odd_prompt_kernel(zm_ref, zpr_ref, wa_ref, ba_ref, gain_ref, wp_ref, ps_ref, tri_ref, band_ref, wo_ref,
                       wd_ref, o_ref, s_ref, np_ref, wo_bf_ref, wd_bf_ref, pbh_ref, pbl_ref):
    i = pl.program_id(1)
    wo_bf_ref[...] = wo_ref[...].astype(BF16)
    wd_bf_ref[...] = wd_ref[...].astype(BF16)

    @pl.when(i == 0)
    def _():
        s_ref[...] = jnp.zeros_like(s_ref)
        pbh_ref[0:BLK, :] = jnp.zeros((BLK, POOL_DIM), BF16)
        pbl_ref[0:BLK, :] = jnp.zeros((BLK, POOL_DIM), BF16)

    x = None
    for t in range(MIX_SUB):
        rows = pl.ds(t * BLK, BLK)
        x = _odd_block(zm_ref.at[rows], zpr_ref.at[rows], wa_ref, ba_ref, gain_ref, wp_ref, ps_ref,
                       tri_ref, band_ref, o_ref.at[rows], s_ref, pbh_ref, pbl_ref, (i * MIX_SUB + t) * BLK)

    @pl.when(i == pl.num_programs(1) - 1)
    def _():
        np_ref[...] = x[BLK - POOL_BUF:BLK, :]


def _odd_block(zm_ref, zpr_ref, wa_ref, ba_ref, gain_ref, wp_ref, ps_ref, tri_ref, band_ref,
               o_ref, s_ref, pbh_ref, pbl_ref, t0):
    n_ch = BLK // GLA_CHUNK
    gk = _log_decay(zpr_ref[:, POOL_DIM:POOL_RANK_W], wa_ref, ba_ref)
    r = lax.broadcasted_iota(jnp.int32, (BLK, BLK), 0)
    c = lax.broadcasted_iota(jnp.int32, (BLK, BLK), 1)
    causal = (r >= c) & (r // GLA_CHUNK == c // GLA_CHUNK)
    g_hi, g_lo = _split_bf16(gk)
    b = _dot(tri_ref[...], jnp.concatenate([g_hi, g_lo], axis=0))
    b_last = [b[(ch + 1) * GLA_CHUNK - 1:(ch + 1) * GLA_CHUNK, :] for ch in range(n_ch)]
    b_end = jnp.concatenate([jnp.broadcast_to(bl, (GLA_CHUNK, GLA_K_WIDTH)) for bl in b_last], axis=0)
    o_k, o_v, o_g = GLA_K_WIDTH, 2 * GLA_K_WIDTH, 2 * GLA_K_WIDTH + GLA_WIDTH
    zq = zm_ref[:, 0:GLA_K_WIDTH]
    zk = zm_ref[:, o_k:o_k + GLA_K_WIDTH]
    qd = ((zq * (GLA_DK ** -0.5)) * jnp.exp(b)).astype(BF16)
    kd = (zk * jnp.exp(-b)).astype(BF16)
    k2 = (zk * jnp.exp(b_end - b)).astype(BF16)
    sub = lax.broadcasted_iota(jnp.int32, (8, LANES), 0)
    sel = jnp.where(sub < 2, 1.0, 0.0).astype(BF16)
    hk = [slice(h * GLA_DK, (h + 1) * GLA_DK) for h in range(GLA_HEADS)]
    chunk = [slice(ch * GLA_CHUNK, (ch + 1) * GLA_CHUNK) for ch in range(n_ch)]
    v_hs = [zm_ref[:, o_v + h * GLA_DV:o_v + (h + 1) * GLA_DV].astype(BF16) for h in range(GLA_HEADS)]
    atts = [_dot_nt(qd[:, hk[h]], kd[:, hk[h]]) for h in range(GLA_HEADS)]
    log_decs = []
    for h in range(GLA_HEADS):
        for ch in range(n_ch):
            bl = b_last[ch][:, hk[h]]
            bl_hi = bl.astype(BF16).astype(F32)
            rows8 = jnp.where(sub == 0, bl_hi, jnp.where(sub == 1, bl - bl_hi, 0.0)).astype(BF16)
            log_decs.append(_dot_tn(rows8, sel))
    deltas = [[_dot_tn(k2[chunk[ch], hk[h]], v_hs[h][chunk[ch]]) for ch in range(n_ch)]
              for h in range(GLA_HEADS)]
    o_intras = [_dot(jnp.where(causal, atts[h], 0.0).astype(BF16), v_hs[h]) for h in range(GLA_HEADS)]
    o_inters = []
    for h in range(GLA_HEADS):
        s_h = s_ref[h]
        parts = []
        for ch in range(n_ch):
            parts.append(_dot(qd[chunk[ch], hk[h]], s_h.astype(BF16)))
            dec = jnp.exp(log_decs[h * n_ch + ch])
            s_h = jnp.concatenate([dec, dec], axis=1) * s_h + deltas[h][ch]
        s_ref[h] = s_h
        o_inters.append(jnp.concatenate(parts, axis=0))
    for h in range(GLA_HEADS):
        vs = slice(h * GLA_DV, (h + 1) * GLA_DV)
        zg = zm_ref[:, o_g + h * GLA_DV:o_g + (h + 1) * GLA_DV]
        o_ref[:, vs] = _head_rmsnorm_gate(o_intras[h] + o_inters[h], gain_ref[...], zg).astype(o_ref.dtype)

    x = zpr_ref[:, 0:POOL_DIM]
    x_hi, x_lo = _split_bf16(x)
    pbh_ref[BLK:2 * BLK, :] = x_hi
    pbl_ref[BLK:2 * BLK, :] = x_lo
    t_glob = t0 + lax.broadcasted_iota(jnp.int32, (BLK, 1), 0)
    groups = [slice(g * POOL_GDIM, (g + 1) * POOL_GDIM) for g in range(len(POOL_WINDOWS))]
    sums = [_dot(band_ref[g], jnp.concatenate([pbh_ref[:, groups[g]], pbl_ref[:, groups[g]]], axis=0))
            for g in range(len(POOL_WINDOWS))]
    ys = []
    for g, w in enumerate(POOL_WINDOWS):
        inv_cnt = 1.0 / jnp.minimum(w, t_glob + 1).astype(F32)
        d = sums[g] * inv_cnt - x[:, groups[g]]
        ys.append(_dot(d.astype(BF16), wp_ref[g]))
    for g in range(len(POOL_WINDOWS)):
        o_ref[:, GLA_WIDTH + g * POOL_GDIM:GLA_WIDTH + (g + 1) * POOL_GDIM] = (
            ys[g] * ps_ref[:, groups[g]]).astype(o_ref.dtype)
    pbh_ref[0:BLK, :] = x_hi
    pbl_ref[0:BLK, :] = x_lo
    return x


def _odd_block_mats():
    r = jnp.arange(BLK)[:, None]
    c = jnp.arange(BLK)[None, :]
    tri = ((r >= c) & (r // GLA_CHUNK == c // GLA_CHUNK)).astype(BF16)
    back = r + BLK - jnp.arange(2 * BLK)[None, :]
    bands = [((back >= 0) & (back < w)).astype(BF16) for w in POOL_WINDOWS]
    return (jnp.concatenate([tri, tri], axis=1),
            jnp.stack([jnp.concatenate([bd, bd], axis=1) for bd in bands]))


def _odd_prompt(zm_p, zpr_p, wa, ba, gain, wp, ps, w_out, layer, w_dn, dn_layer):
    n_step = NBLK // MIX_SUB
    step_rows = MIX_SUB * BLK
    blk = lambda b, i: (b * n_step + i, 0)
    fixed2 = lambda b, i: (0, 0)
    wo_in, wo_out, wo_shape = _w_slab_specs(w_out, layer, n_step)
    wd_in, wd_out, wd_shape = _w_slab_specs(w_dn, dn_layer, n_step)
    tri2, bands2 = _odd_block_mats()
    return pl.pallas_call(
        _odd_prompt_kernel,
        grid=(BATCH, n_step),
        in_specs=[pl.BlockSpec((step_rows, ODD_MAIN), blk),
                  pl.BlockSpec((step_rows, POOL_RANK_W), blk),
                  pl.BlockSpec((LANES, GLA_K_WIDTH), fixed2),
                  pl.BlockSpec((1, GLA_K_WIDTH), fixed2),
                  pl.BlockSpec((1, GLA_DV), fixed2),
                  pl.BlockSpec((4, POOL_GDIM, POOL_GDIM), lambda b, i: (0, 0, 0)),
                  pl.BlockSpec((1, POOL_DIM), fixed2),
                  pl.BlockSpec(tri2.shape, fixed2),
                  pl.BlockSpec(bands2.shape, lambda b, i: (0, 0, 0)),
                  wo_in, wd_in],
        out_specs=[pl.BlockSpec((step_rows, D_MODEL), blk),
                   pl.BlockSpec((None, GLA_HEADS, GLA_DK, GLA_DV), lambda b, i: (b, 0, 0, 0)),
                   pl.BlockSpec((None, POOL_BUF, POOL_DIM), lambda b, i: (b, 0, 0)),
                   wo_out, wd_out],
        out_shape=[jax.ShapeDtypeStruct((M_PROMPT, D_MODEL), BF16),
                   jax.ShapeDtypeStruct((BATCH, GLA_HEADS, GLA_DK, GLA_DV), F32),
                   jax.ShapeDtypeStruct((BATCH, POOL_BUF, POOL_DIM), F32),
                   wo_shape, wd_shape],
        scratch_shapes=[pltpu.VMEM((2 * BLK, POOL_DIM), BF16), pltpu.VMEM((2 * BLK, POOL_DIM), BF16)],
        compiler_params=_cparams("arbitrary", "arbitrary"),
        name="odd_prompt",
    )(zm_p, zpr_p, wa, ba, gain, wp, ps, tri2, bands2, w_out, w_dn)


STATE_BB = 8
N_ODD_SAMPLE_IN = 9


def _odd_sample_kernel(*refs, li, fill):
    zm_ref, zpr_ref, wa_ref, ba_ref, gain_ref, wp_ref, ps_ref, pool_ref, s_ref = refs[:N_ODD_SAMPLE_IN]
    m_ref, np_ref, ns_ref, dec_all, q_all, o_rows = refs[-6:]
    if fill:
        for other in range(N_ODD):
            if other != li:
                ns_ref[other] = jnp.zeros(ns_ref.shape[1:], F32)
        ns_ref = ns_ref.at[li]
    step = pl.program_id(0)
    n_prev = pool_ref.shape[1] // POOL_DIM

    @pl.when(step == 0)
    def _():
        dec_all[...] = jnp.exp(_log_decay(zpr_ref[:, POOL_DIM:POOL_RANK_W], wa_ref, ba_ref))
        q_all[...] = zm_ref[:, 0:GLA_K_WIDTH] * (GLA_DK ** -0.5)
        for g, w in enumerate(POOL_WINDOWS):
            gs = slice(g * POOL_GDIM, (g + 1) * POOL_GDIM)
            x = zpr_ref[:, gs]
            first = n_prev - (w - 1)
            s = pool_ref[:, first * POOL_DIM + g * POOL_GDIM:first * POOL_DIM + (g + 1) * POOL_GDIM]
            for jj in range(first + 1, n_prev):
                s = s + pool_ref[:, jj * POOL_DIM + g * POOL_GDIM:jj * POOL_DIM + (g + 1) * POOL_GDIM]
            s = s + x
            d = s / float(min(w, n_prev + 1)) - x
            m_ref[:, GLA_WIDTH + g * POOL_GDIM:GLA_WIDTH + (g + 1) * POOL_GDIM] = (
                _dot(d.astype(BF16), wp_ref[g]) * ps_ref[:, gs])
        np_ref[:, 0:(n_prev - 1) * POOL_DIM] = pool_ref[:, POOL_DIM:n_prev * POOL_DIM]
        np_ref[:, (n_prev - 1) * POOL_DIM:n_prev * POOL_DIM] = zpr_ref[:, 0:POOL_DIM]

    grp = pl.ds(pl.multiple_of(step * STATE_BB, STATE_BB), STATE_BB)
    dec = dec_all[grp, :]
    q = q_all[grp, :]
    k = zm_ref[grp, GLA_K_WIDTH:2 * GLA_K_WIDTH]
    v = zm_ref[grp, 2 * GLA_K_WIDTH:2 * GLA_K_WIDTH + GLA_WIDTH]
    for bb in range(STATE_BB):
        for h in range(GLA_HEADS):
            ks = slice(h * GLA_DK, (h + 1) * GLA_DK)
            vs = slice(h * GLA_DV, (h + 1) * GLA_DV)
            cols = _rows_to_cols([dec[bb:bb + 1, ks], k[bb:bb + 1, ks], q[bb:bb + 1, ks]])
            wide = lambda blk: jnp.concatenate([cols[:, blk * GLA_DK:(blk + 1) * GLA_DK]] * 2, axis=1)
            s_new = wide(0) * s_ref[bb, h] + wide(1) * v[bb:bb + 1, vs]
            ns_ref[bb, h] = s_new
            o_rows[bb:bb + 1, vs] = jnp.sum(wide(2) * s_new, axis=0, keepdims=True)
    o_g = 2 * GLA_K_WIDTH + GLA_WIDTH
    for h in range(GLA_HEADS):
        vs = slice(h * GLA_DV, (h + 1) * GLA_DV)
        zg = zm_ref[grp, o_g + h * GLA_DV:o_g + (h + 1) * GLA_DV]
        m_ref[grp, vs] = _head_rmsnorm_gate(o_rows[:, vs], gain_ref[...], zg)


def _odd_sample(zm_s, zpr_s, wa, ba, gain, wp, ps, pool_state, state, li, prev):
    whole = lambda a: pl.BlockSpec(a.shape, lambda b: (0,) * a.ndim)
    st = pl.BlockSpec((None, STATE_BB, GLA_HEADS, GLA_DK, GLA_DV), lambda b: (li, b, 0, 0, 0))
    st_all = pl.BlockSpec((N_ODD, STATE_BB, GLA_HEADS, GLA_DK, GLA_DV), lambda b: (0, b, 0, 0, 0))
    pool2d = pool_state.reshape(DEC_BATCH, POOL_BUF * POOL_DIM)
    args = [zm_s, zpr_s, wa, ba, gain, wp, ps, pool2d, state]
    assert len(args) == N_ODD_SAMPLE_IN
    in_specs = [whole(a) for a in args[:-1]] + [st]
    aliases = {}
    if prev is not None:
        aliases = {len(args): 2}
        in_specs.append(pl.BlockSpec(memory_space=pl.ANY))
        args.append(prev)
    rows = lambda w: pl.BlockSpec((DEC_BATCH, w), lambda b: (0, 0))
    m_s, npool, ns = pl.pallas_call(
        functools.partial(_odd_sample_kernel, li=li, fill=prev is None),
        grid=(DEC_BATCH // STATE_BB,),
        in_specs=in_specs,
        out_specs=[rows(D_MODEL), rows(POOL_BUF * POOL_DIM), st_all if prev is None else st],
        out_shape=[jax.ShapeDtypeStruct((DEC_BATCH, D_MODEL), F32),
                   jax.ShapeDtypeStruct((DEC_BATCH, POOL_BUF * POOL_DIM), F32),
                   jax.ShapeDtypeStruct(state.shape, F32)],
        scratch_shapes=[pltpu.VMEM((DEC_BATCH, GLA_K_WIDTH), F32), pltpu.VMEM((DEC_BATCH, GLA_K_WIDTH), F32),
                        pltpu.VMEM((STATE_BB, GLA_WIDTH), F32)],
        input_output_aliases=aliases,
        compiler_params=_cparams("arbitrary"),
        name="odd_sample",
    )(*args)
    return m_s, npool.reshape(DEC_BATCH, POOL_BUF, POOL_DIM), ns


def _rope_tables(pos):
    half = HEAD_DIM // 2
    inv = jnp.power(ROPE_THETA, -jnp.arange(half, dtype=F32) / half)
    ang = pos.astype(F32)[:, None] * inv[None, :]
    c, s = jnp.cos(ang), jnp.sin(ang)
    return jnp.tile(c, (1, 4)), jnp.tile(jnp.concatenate([-s, s], axis=1), (1, 2))


def kernel(x_prompt, x_sample, cache_swa_k, cache_swa_v, state_conv, state_gla, state_pool, norm_mix, norm_ffn, w_in_even, w_out_even, q_norm, k_norm, attn_sinks, conv_w, w_in_odd, w_out_odd, w_alpha_up, b_alpha, gla_out_norm, w_pool, pool_scale, w_gate, w_up, w_down):
    lc = cache_swa_k.shape[2]
    assert x_prompt.shape == (BATCH, SEQ, D_MODEL) and x_sample.shape == (DEC_BATCH, 1, D_MODEL)
    assert lc == BLK == WINDOW, "the prompt's new cache is its last attention block"
    assert state_pool.shape[2] == POOL_BUF and state_conv.shape[2] == CONV_W - 1
    x_p = x_prompt.reshape(M_PROMPT, D_MODEL)
    x_s = x_sample.reshape(DEC_BATCH, D_MODEL)
    cos_p, sin_p = _rope_tables(jnp.arange(SEQ))
    cos_s, sin_s = _rope_tables(PAST_LEN + jnp.arange(1))
    mats = _rope_mats()
    kc_all = cache_swa_k.reshape(N_EVEN, DEC_BATCH, lc, A_KV_WIDTH)
    vc_all = cache_swa_v.reshape(N_EVEN, DEC_BATCH, lc, A_KV_WIDTH)
    o_r = ODD_MAIN
    w_odd_t = jnp.swapaxes(w_in_odd, 1, 2)
    w_odd_pr_t = jnp.concatenate([w_odd_t[:, o_r + GLA_RANK:], w_odd_t[:, o_r:o_r + GLA_RANK],
                                  jnp.zeros((N_ODD, LANES - GLA_RANK, D_MODEL), F32)], axis=1)
    pk, pv, pc, pg, pp, sc, sp = ([] for _ in range(7))
    sk_all = jnp.zeros(kc_all.shape, F32)
    sv_all = jnp.zeros(vc_all.shape, F32)
    sg_all = None

    h_p, h_s = _rmsnorm(x_p, x_s, norm_mix[0])
    for layer in range(DEPTH):
        li = layer // 2
        if layer % 2 == 0:
            z_p, z_s = _dense(h_p, h_s, [(w_in_even, li)], n_cols=EVEN_IN, **TILES["in_even"], name="in_even")
            qg = jnp.tile(q_norm[li], 2).reshape(1, LANES)
            kg = jnp.tile(k_norm[li], 2).reshape(1, LANES)
            m_p, nk, nv, nc, w_out_bf, w_down_bf = _even_prompt(
                z_p, cos_p, sin_p, qg, kg, conv_w[li], attn_sinks[li], mats, w_out_even, li, w_down, layer)
            pk.append(nk.reshape(BATCH, lc, A_KV_HEADS, HEAD_DIM))
            pv.append(nv.reshape(BATCH, lc, A_KV_HEADS, HEAD_DIM))
            pc.append(nc)
            m_s, nc_s, sk_all, sv_all = _even_sample(
                attn_sinks[li], z_s, cos_s, sin_s, qg, kg, state_conv[li], conv_w[li], mats, kc_all, vc_all, li,
                (sk_all, sv_all))
            sc.append(nc_s)
        else:
            zm_p, zm_s = _dense(h_p, h_s, [(w_odd_t, li)], n_cols=ODD_MAIN, **TILES["in_odd_main"],
                                transposed=True, name="in_odd_main")
            zpr_p, zpr_s = _dense(h_p, h_s, [(w_odd_pr_t, li)], n_cols=POOL_RANK_W, **TILES["in_odd_pool_rank"],
                                  transposed=True, name="in_odd_pool_rank")
            wa = jnp.pad(w_alpha_up[li], ((0, LANES - GLA_RANK), (0, 0))).astype(BF16)
            ba = b_alpha[li].reshape(1, GLA_K_WIDTH)
            gain = gla_out_norm[li].reshape(1, GLA_DV)
            wp = w_pool[li].astype(BF16)
            ps = pool_scale[li].reshape(1, POOL_DIM)
            m_p, ng, npool, w_out_bf, w_down_bf = _odd_prompt(zm_p, zpr_p, wa, ba, gain, wp, ps, w_out_odd, li,
                                                              w_down, layer)
            pg.append(ng)
            pp.append(npool)
            m_s, np_s, sg_all = _odd_sample(zm_s, zpr_s, wa, ba, gain, wp, ps, state_pool[li], state_gla, li,
                                            sg_all)
            sp.append(np_s)
        x_p, x_s, h_p, h_s = _proj_res(m_p, m_s, (w_out_bf, None), (x_p, x_s), norm_ffn[layer],
                                       **TILES["out_proj"], name="out_proj")
        a_p, a_s = _dense(h_p, h_s, [(w_gate, layer), (w_up, layer)], n_cols=D_FF, **TILES["ffn_up"],
                          mode="swiglu", out_dtype=BF16, name="ffn_up")
        if layer + 1 < DEPTH:
            x_p, x_s, h_p, h_s = _proj_res(a_p, a_s, (w_down_bf, None), (x_p, x_s),
                                           norm_mix[layer + 1], **TILES["ffn_down"], name="ffn_down")
        else:
            x_p, x_s = _proj_res(a_p, a_s, (w_down_bf, None), (x_p, x_s), None,
                                 **TILES["ffn_down"], name="ffn_down_last")

    st = lambda parts: jnp.stack(parts)
    cache5 = lambda a: a.reshape(N_EVEN, DEC_BATCH, lc, A_KV_HEADS, HEAD_DIM)
    return (x_p.reshape(BATCH, SEQ, D_MODEL), x_s.reshape(DEC_BATCH, 1, D_MODEL),
            st(pk), st(pv), st(pc), st(pg), st(pp), cache5(sk_all), cache5(sv_all), st(sc), sg_all, st(sp))
```
